```python
import functools
import jax, jax.numpy as jnp
from jax import lax
import numpy as np

D_MODEL = 1024
BATCH = 2
SEQ = 8192
DEPTH = 1
DEC_BATCH = 128
DEC_SEQ = 8
PAST_LEN = 16384
PAGE_SIZE = 128

HEAD_DIM = 64
ROT_DIM = HEAD_DIM // 4
ROPE_THETA = 500000.0
ATTN_SCALE = HEAD_DIM ** -0.5
BLOCK = 128
N_HEADS_A = 8
N_KV_A = 2
GROUP_A = N_HEADS_A // N_KV_A
WINDOW_A = 128
DIL_GROUPS = ((128, 1), (512, 4), (2048, 16))
N_DIL = 3
N_KV_B = 4
N_HEADS_B = N_DIL * N_KV_B
WINDOW_B = 2048
Q_A_W = N_HEADS_A * HEAD_DIM
KV_A_W = N_KV_A * HEAD_DIM
Q_B_W = N_HEADS_B * HEAD_DIM
KV_B_W = N_KV_B * HEAD_DIM
IN_COLS = Q_A_W + 2 * KV_A_W + Q_B_W + 2 * KV_B_W + 2 * D_MODEL
N_MEM = 256
MEM_HEADS = 4
MEM_HEAD_DIM = 128
MEM_WIDTH = MEM_HEADS * MEM_HEAD_DIM
MEM_SCALE = MEM_HEAD_DIM ** -0.5
D_FF = 2816
RMS_EPS = 1e-6

kernel_name = 'hybrid_swa_dilated_memory_decoder_step'


def rmsnorm(x, g):
    xf = x.astype(jnp.float32)
    xf = xf * lax.rsqrt(jnp.mean(xf * xf, axis=-1, keepdims=True) + RMS_EPS)
    return (xf * g.astype(jnp.float32)).astype(x.dtype)


def swiglu_half(x, norm_g, w_gate, w_up, w_down):
    u = rmsnorm(x, norm_g)
    return x + 0.5 * ((jax.nn.silu(u @ w_gate) * (u @ w_up)) @ w_down)


def rope(x, pos):
    half = ROT_DIM // 2
    inv_freq = jnp.power(jnp.float32(ROPE_THETA), -jnp.arange(half, dtype=jnp.float32) / half)
    ang = pos.astype(jnp.float32)[:, None] * inv_freq[None, :]
    cos = jnp.cos(ang)[:, None, :].astype(x.dtype)
    sin = jnp.sin(ang)[:, None, :].astype(x.dtype)
    x1, x2, rest = x[..., :half], x[..., half:ROT_DIM], x[..., ROT_DIM:]
    return jnp.concatenate([x1 * cos - x2 * sin, x2 * cos + x1 * sin, rest], axis=-1)


def band_attn(q, k, v, steps):
    n, s, kv, g, dh = q.shape
    pad = (-s) % BLOCK
    if pad:
        q = jnp.pad(q, ((0, 0), (0, pad), (0, 0), (0, 0), (0, 0)))
        k = jnp.pad(k, ((0, 0), (0, pad), (0, 0), (0, 0)))
        v = jnp.pad(v, ((0, 0), (0, pad), (0, 0), (0, 0)))
    nb = (s + pad) // BLOCK
    qb = q.reshape(n, nb, BLOCK, kv, g, dh)

    def with_prev(x):
        xb = x.reshape(n, nb, BLOCK, kv, dh)
        prev = jnp.concatenate([jnp.zeros_like(xb[:, :1]), xb[:, :-1]], axis=1)
        return jnp.concatenate([prev, xb], axis=2)

    kk, vv = with_prev(k), with_prev(v)
    sc = jnp.einsum('nbqkgd,nbjkd->nbkgqj', qb, kk, preferred_element_type=jnp.float32) * ATTN_SCALE
    qi = jnp.arange(BLOCK)[:, None]
    kj = jnp.arange(2 * BLOCK)[None, :]
    dist = qi + BLOCK - kj
    band = (dist >= 0) & (dist <= steps)
    has_prev = (jnp.arange(nb) > 0)[:, None, None] | (kj >= BLOCK)[None]
    mask = band[None] & has_prev
    sc = jnp.where(mask[None, :, None, None], sc, -jnp.inf)
    lse = jax.nn.logsumexp(sc, axis=-1)
    p = jnp.exp(sc - lse[..., None]).astype(v.dtype)
    o = jnp.einsum('nbkgqj,nbjkd->nbqkgd', p, vv, preferred_element_type=jnp.float32)
    o = o.reshape(n, nb * BLOCK, kv, g, dh)[:, :s]
    lse = jnp.transpose(lse, (0, 1, 4, 2, 3)).reshape(n, nb * BLOCK, kv, g)[:, :s]
    return o, lse


def dilated_band_attn(q, k, v, dilation, steps):
    n, s = q.shape[:2]
    sd = s // dilation

    def split(x):
        rest = x.shape[2:]
        x = jnp.moveaxis(x.reshape(n, sd, dilation, *rest), 2, 1)
        return x.reshape(n * dilation, sd, *rest)

    def merge(x):
        rest = x.shape[2:]
        x = jnp.moveaxis(x.reshape(n, dilation, sd, *rest), 1, 2)
        return x.reshape(n, s, *rest)

    o, lse = band_attn(split(q), split(k), split(v), steps)
    return merge(o), merge(lse)


def gather_attn(q, k_all, v_all, dilation, steps):
    t = q.shape[1]
    n_past = k_all.shape[1] - t
    idx = (n_past + jnp.arange(t))[:, None] - dilation * jnp.arange(steps + 1)[None, :]
    valid = idx >= 0
    idx = jnp.maximum(idx, 0)
    kg = jnp.take(k_all, idx, axis=1)
    vg = jnp.take(v_all, idx, axis=1)
    sc = jnp.einsum('ntkgd,ntjkd->ntkgj', q, kg, preferred_element_type=jnp.float32) * ATTN_SCALE
    sc = jnp.where(valid[None, :, None, None, :], sc, -jnp.inf)
    lse = jax.nn.logsumexp(sc, axis=-1)
    p = jnp.exp(sc - lse[..., None]).astype(v_all.dtype)
    o = jnp.einsum('ntkgj,ntjkd->ntkgd', p, vg, preferred_element_type=jnp.float32)
    return o, lse


def cached_attn(q, k, v, dilation, steps, k_past, v_past):
    return gather_attn(q, jnp.concatenate([k_past, k], axis=1), jnp.concatenate([v_past, v], axis=1), dilation, steps)


def project_mixers(h, pos, mix_norm, w_in):
    n, s, _ = h.shape
    z = rmsnorm(h, mix_norm) @ w_in
    sizes = (Q_A_W, KV_A_W, KV_A_W, Q_B_W, KV_B_W, KV_B_W, D_MODEL, D_MODEL)
    q_a, k_a, v_a, q_b, k_b, v_b, g_a, g_b = jnp.split(z, np.cumsum(sizes)[:-1].tolist(), axis=-1)
    q_a = rope(q_a.reshape(n, s, N_HEADS_A, HEAD_DIM), pos).reshape(n, s, N_KV_A, GROUP_A, HEAD_DIM)
    k_a = rope(k_a.reshape(n, s, N_KV_A, HEAD_DIM), pos)
    v_a = v_a.reshape(n, s, N_KV_A, HEAD_DIM)
    q_b = rope(q_b.reshape(n, s, N_HEADS_B, HEAD_DIM), pos).reshape(n, s, N_DIL, N_KV_B, HEAD_DIM)
    k_b = rope(k_b.reshape(n, s, N_KV_B, HEAD_DIM), pos)
    v_b = v_b.reshape(n, s, N_KV_B, HEAD_DIM)
    return q_a, k_a, v_a, q_b, k_b, v_b, g_a, g_b


def memory_kv(mem, norm_g, w_k, w_v):
    n, m, _ = mem.shape
    u = rmsnorm(mem, norm_g)
    return ((u @ w_k).reshape(n, m, MEM_HEADS, MEM_HEAD_DIM), (u @ w_v).reshape(n, m, MEM_HEADS, MEM_HEAD_DIM))


def cross_attn(h, mem_k, mem_v, norm_g, w_q, w_o):
    n, s, _ = h.shape
    q = (rmsnorm(h, norm_g) @ w_q).reshape(n, s, MEM_HEADS, MEM_HEAD_DIM)
    sc = jnp.einsum('nshd,nmhd->nhsm', q, mem_k, preferred_element_type=jnp.float32) * MEM_SCALE
    p = jax.nn.softmax(sc, axis=-1).astype(mem_v.dtype)
    o = jnp.einsum('nhsm,nmhd->nshd', p, mem_v, preferred_element_type=jnp.float32)
    return o.astype(h.dtype).reshape(n, s, MEM_WIDTH) @ w_o


def decoder_layer(x, pos, attn_a, attn_b, mem_k, mem_v, ffn1, mix, cross, ffn2):
    n, s, _ = x.shape
    h = swiglu_half(x, *ffn1)
    mix_norm, w_in, sink, w_branch_a, w_branch_b, w_out = mix
    q_a, k_a, v_a, q_b, k_b, v_b, g_a, g_b = project_mixers(h, pos, mix_norm, w_in)
    o_a, lse_a = attn_a(q_a, k_a, v_a)
    o_a = o_a * jax.nn.sigmoid(lse_a - sink.astype(jnp.float32).reshape(N_KV_A, GROUP_A))[..., None]
    outs, lses = [], []
    for g, (window, dilation) in enumerate(DIL_GROUPS):
        o, lse = attn_b(q_b[:, :, g, :, None], k_b, v_b, dilation, window // dilation)
        outs.append(o)
        lses.append(lse)
    wts = jax.nn.softmax(jnp.stack(lses), axis=0)
    o_b = jnp.sum(wts[..., None] * jnp.stack(outs), axis=0)
    branch_a = o_a.astype(h.dtype).reshape(n, s, Q_A_W) @ w_branch_a
    branch_b = o_b.astype(h.dtype).reshape(n, s, KV_B_W) @ w_branch_b
    h = h + (jax.nn.sigmoid(g_a) * branch_a + jax.nn.sigmoid(g_b) * branch_b) @ w_out
    h = h + cross_attn(h, mem_k, mem_v, *cross)
    h = swiglu_half(h, *ffn2)
    return h, k_a, v_a, k_b, v_b


def setup_inputs(seed: int = 0) -> dict:
    key = jax.random.key(seed)
    keys = iter(jax.random.split(key, 40))

    def normal(shape, scale=1.0):
        return scale * jax.random.normal(next(keys), shape, jnp.float32)

    def gain(shape):
        return 1.0 + 0.1 * normal(shape)

    l_a = min(WINDOW_A, PAST_LEN)
    l_b = min(WINDOW_B, PAST_LEN)
    d = D_MODEL
    return {
        'x_prompt': normal((BATCH, SEQ, d)),
        'x_sample': normal((DEC_BATCH, DEC_SEQ, d)),
        'cache_swa_k': normal((DEPTH, DEC_BATCH, l_a, N_KV_A, HEAD_DIM)),
        'cache_swa_v': normal((DEPTH, DEC_BATCH, l_a, N_KV_A, HEAD_DIM)),
        'cache_dil_k': normal((DEPTH, DEC_BATCH, l_b, N_KV_B, HEAD_DIM)),
        'cache_dil_v': normal((DEPTH, DEC_BATCH, l_b, N_KV_B, HEAD_DIM)),
        'cache_mem_k': normal((DEPTH, DEC_BATCH, N_MEM, MEM_HEADS, MEM_HEAD_DIM)),
        'cache_mem_v': normal((DEPTH, DEC_BATCH, N_MEM, MEM_HEADS, MEM_HEAD_DIM)),
        'mem_prompt': normal((BATCH, N_MEM, d)),
        'ffn1_norm': gain((DEPTH, d)),
        'ffn1_w_gate': normal((DEPTH, d, D_FF), d ** -0.5),
        'ffn1_w_up': normal((DEPTH, d, D_FF), d ** -0.5),
        'ffn1_w_down': normal((DEPTH, D_FF, d), D_FF ** -0.5),
        'mix_norm': gain((DEPTH, d)),
        'w_in': normal((DEPTH, d, IN_COLS), d ** -0.5),
        'attn_sink': normal((DEPTH, N_HEADS_A)),
        'w_branch_a': normal((DEPTH, Q_A_W, d), Q_A_W ** -0.5),
        'w_branch_b': normal((DEPTH, KV_B_W, d), KV_B_W ** -0.5),
        'w_out': normal((DEPTH, d, d), d ** -0.5),
        'mem_q_norm': gain((DEPTH, d)),
        'mem_kv_norm': gain((DEPTH, d)),
        'w_mem_q': normal((DEPTH, d, MEM_WIDTH), d ** -0.5),
        'w_mem_k': normal((DEPTH, d, MEM_WIDTH), d ** -0.5),
        'w_mem_v': normal((DEPTH, d, MEM_WIDTH), d ** -0.5),
        'w_mem_o': normal((DEPTH, MEM_WIDTH, d), MEM_WIDTH ** -0.5),
        'ffn2_norm': gain((DEPTH, d)),
        'ffn2_w_gate': normal((DEPTH, d, D_FF), d ** -0.5),
        'ffn2_w_up': normal((DEPTH, d, D_FF), d ** -0.5),
        'ffn2_w_down': normal((DEPTH, D_FF, d), D_FF ** -0.5),
        'final_norm': gain((d,)),
    }


def reference(x_prompt, x_sample, cache_swa_k, cache_swa_v, cache_dil_k, cache_dil_v, cache_mem_k, cache_mem_v,
              mem_prompt, ffn1_norm, ffn1_w_gate, ffn1_w_up, ffn1_w_down, mix_norm, w_in, attn_sink, w_branch_a,
              w_branch_b, w_out, mem_q_norm, mem_kv_norm, w_mem_q, w_mem_k, w_mem_v, w_mem_o, ffn2_norm,
              ffn2_w_gate, ffn2_w_up, ffn2_w_down, final_norm):
    s = x_prompt.shape[1]
    t = x_sample.shape[1]
    pos_p = jnp.arange(s, dtype=jnp.int32)
    pos_s = PAST_LEN + jnp.arange(t, dtype=jnp.int32)
    keep_a = min(WINDOW_A, s)
    keep_b = min(WINDOW_B, s)
    prompt_attn_a = functools.partial(dilated_band_attn, dilation=1, steps=WINDOW_A)
    hp, hs = x_prompt, x_sample
    rows = []
    for l in range(DEPTH):
        ffn1 = (ffn1_norm[l], ffn1_w_gate[l], ffn1_w_up[l], ffn1_w_down[l])
        mix = (mix_norm[l], w_in[l], attn_sink[l], w_branch_a[l], w_branch_b[l], w_out[l])
        cross = (mem_q_norm[l], w_mem_q[l], w_mem_o[l])
        ffn2 = (ffn2_norm[l], ffn2_w_gate[l], ffn2_w_up[l], ffn2_w_down[l])
        mk_p, mv_p = memory_kv(mem_prompt, mem_kv_norm[l], w_mem_k[l], w_mem_v[l])
        hp, ka_p, va_p, kb_p, vb_p = decoder_layer(hp, pos_p, prompt_attn_a, dilated_band_attn,
                                                   mk_p, mv_p, ffn1, mix, cross, ffn2)
        sample_attn_a = functools.partial(cached_attn, dilation=1, steps=WINDOW_A,
                                          k_past=cache_swa_k[l], v_past=cache_swa_v[l])
        sample_attn_b = functools.partial(cached_attn, k_past=cache_dil_k[l], v_past=cache_dil_v[l])
        hs, ka_s, va_s, kb_s, vb_s = decoder_layer(hs, pos_s, sample_attn_a, sample_attn_b,
                                                   cache_mem_k[l], cache_mem_v[l], ffn1, mix, cross, ffn2)
        rows.append((ka_p[:, s - keep_a:], va_p[:, s - keep_a:], kb_p[:, s - keep_b:], vb_p[:, s - keep_b:],
                     mk_p, mv_p, ka_s, va_s, kb_s, vb_s))
    (swa_k_p, swa_v_p, dil_k_p, dil_v_p, mem_k_p, mem_v_p,
     swa_k_s, swa_v_s, dil_k_s, dil_v_s) = [jnp.stack(c) for c in zip(*rows)]
    y_prompt = rmsnorm(hp, final_norm)
    y_sample = rmsnorm(hs, final_norm)
    return (y_prompt, y_sample, swa_k_p, swa_v_p, dil_k_p, dil_v_p, mem_k_p, mem_v_p,
            swa_k_s, swa_v_s, dil_k_s, dil_v_s)
```

```python
import functools

import jax
import jax.numpy as jnp
import numpy as np
from jax import lax
from jax.experimental import pallas as pl
from jax.experimental.pallas import tpu as pltpu

F32 = jnp.float32
BF16 = jnp.bfloat16

D_MODEL = 1024
D_FF = 2816
HEAD_DIM = 64
ROT_HALF = 8
ROPE_THETA = 500000.0
ATTN_SCALE = HEAD_DIM ** -0.5
RMS_EPS = 1e-6
PAST_LEN = 16384

N_HEADS_A = 8
N_KV_A = 2
N_KV_B = 4
DILATIONS = (1, 4, 16)
BAND_STEPS = 128
Q_A_W = N_HEADS_A * HEAD_DIM
KV_A_W = N_KV_A * HEAD_DIM
Q_B_W = len(DILATIONS) * N_KV_B * HEAD_DIM
KV_B_W = N_KV_B * HEAD_DIM
QKV_W = Q_A_W + 2 * KV_A_W + Q_B_W + 2 * KV_B_W
MEM_HEADS = 4
MEM_HEAD_DIM = 128
MEM_W = MEM_HEADS * MEM_HEAD_DIM
MEM_SCALE = MEM_HEAD_DIM ** -0.5
N_MEM = 256

HEAD_PERM_A = (0, 4, 1, 5, 2, 6, 3, 7)

LANES = 128
BLOCK = 128
TOKEN_TILE = 512
ATTN_CHUNK = 512
FF_CHUNKS = ((0, 1536), (1536, 2816))
VMEM_LIMIT = 56 * 1024 * 1024
NEG_INF = float("-inf")


def _rms(x, g):
    ms = jnp.mean(x * x, axis=-1, keepdims=True)
    return x * lax.rsqrt(ms + RMS_EPS) * g


def _dot(a, b):
    return jnp.dot(a, b, preferred_element_type=F32)


def _dot_nt(a, b):
    return lax.dot_general(a, b, (((1,), (1,)), ((), ())), preferred_element_type=F32)


def _ffn_half(x, g_ref, wg_ref, wu_ref, wd_ref):
    u = _rms(x, g_ref[...]).astype(BF16)
    acc = None
    for lo, hi in FF_CHUNKS:
        gate = _dot(u, wg_ref[:, lo:hi])
        up = _dot(u, wu_ref[:, lo:hi])
        act = (gate * jax.nn.sigmoid(gate) * up).astype(BF16)
        part = _dot(act, wd_ref[lo:hi, :])
        acc = part if acc is None else acc + part
    return x + 0.5 * acc


def _front_kernel(x_ref, cos_ref, sa_ref, sb_ref, g1_ref, wg_ref, wu_ref, wd_ref, gm_ref, wqkv_ref,
                  h_ref, *outs, sample):
    x = x_ref[...]
    h = _ffn_half(x, g1_ref, wg_ref, wu_ref, wd_ref)
    h_ref[...] = h
    n = _rms(h, gm_ref[...]).astype(BF16)
    z = _dot(n, wqkv_ref[...])
    cos, sa, sb = cos_ref[...], sa_ref[...], sb_ref[...]

    def chunk(c):
        return z[:, c * LANES:(c + 1) * LANES]

    def rope(c):
        zc = chunk(c)
        return zc * cos + pltpu.roll(zc, LANES - ROT_HALF, 1) * sa + pltpu.roll(zc, ROT_HALF, 1) * sb

    nqa, nka = Q_A_W // LANES, KV_A_W // LANES
    nqb, nkb = Q_B_W // LANES, KV_B_W // LANES
    c0 = 0
    qa = [rope(c0 + c) * ATTN_SCALE for c in range(nqa)]
    c0 += nqa
    ka = [rope(c0 + c) for c in range(nka)]
    c0 += nka
    va = [chunk(c0 + c) for c in range(nka)]
    c0 += nka
    qb = [rope(c0 + c) * ATTN_SCALE for c in range(nqb)]
    c0 += nqb
    kb = [rope(c0 + c) for c in range(nkb)]
    c0 += nkb
    vb = [chunk(c0 + c) for c in range(nkb)]

    def store(ref, parts, dtype):
        for c, p in enumerate(parts):
            ref[:, c * LANES:(c + 1) * LANES] = p.astype(dtype)

    if sample:
        q32_ref, ka32_ref, va32_ref, kb32_ref, vb32_ref = outs
        store(q32_ref, qa + qb, F32)
    else:
        qa_ref, ka_ref, va_ref, qb_ref, kb_ref, vb_ref, ka32_ref, va32_ref, kb32_ref, vb32_ref = outs
        store(qa_ref, qa, BF16)
        store(ka_ref, ka, BF16)
        store(va_ref, va, BF16)
        store(qb_ref, qb, BF16)
        store(kb_ref, kb, BF16)
        store(vb_ref, vb, BF16)
    store(ka32_ref, ka, F32)
    store(va32_ref, va, F32)
    store(kb32_ref, kb, F32)
    store(vb32_ref, vb, F32)


def _const_spec(shape):
    nd = len(shape)
    return pl.BlockSpec(shape, lambda *_: (0,) * nd, pipeline_mode=pl.Buffered(1))


def _front_call(x, tables, w, *, sample):
    t = x.shape[0]
    tm = TOKEN_TILE
    nsteps = t // tm
    cos_t, sa_t, sb_t = tables
    tab_blocks = cos_t.shape[0] // tm
    row = lambda i: (i, 0)
    tab = lambda i: (i % tab_blocks, 0)

    def tok(width, dtype):
        return jax.ShapeDtypeStruct((t, width), dtype), pl.BlockSpec((tm, width), row)

    outs = [tok(D_MODEL, F32)]
    if sample:
        outs += [tok(Q_A_W + Q_B_W, F32)]
    else:
        outs += [tok(Q_A_W, BF16), tok(KV_A_W, BF16), tok(KV_A_W, BF16),
                 tok(Q_B_W, BF16), tok(KV_B_W, BF16), tok(KV_B_W, BF16)]
    outs += [tok(KV_A_W, F32), tok(KV_A_W, F32), tok(KV_B_W, F32), tok(KV_B_W, F32)]
    out_shape, out_specs = zip(*outs)
    in_specs = [pl.BlockSpec((tm, D_MODEL), row),
                pl.BlockSpec((tm, LANES), tab), pl.BlockSpec((tm, LANES), tab), pl.BlockSpec((tm, LANES), tab),
                _const_spec((1, D_MODEL)), _const_spec((D_MODEL, D_FF)), _const_spec((D_MODEL, D_FF)),
                _const_spec((D_FF, D_MODEL)), _const_spec((1, D_MODEL)), _const_spec((D_MODEL, QKV_W))]
    return pl.pallas_call(
        functools.partial(_front_kernel, sample=sample),
        grid=(nsteps,),
        in_specs=in_specs,
        out_specs=list(out_specs),
        out_shape=list(out_shape),
        compiler_params=pltpu.CompilerParams(dimension_semantics=("parallel",), vmem_limit_bytes=VMEM_LIMIT),
        name="front_sample" if sample else "front_prompt",
    )(x, cos_t, sa_t, sb_t, w["ffn1_norm"], w["ffn1_wg"], w["ffn1_wu"], w["ffn1_wd"], w["mix_norm"], w["w_qkv"])


def _lane_block_masks(width, block, dtype):
    lane = lax.broadcasted_iota(jnp.int32, (1, width), 1)
    return [((lane >= i * block) & (lane < (i + 1) * block)).astype(dtype) for i in range(width // block)]


def _band_mask():
    qi = lax.broadcasted_iota(jnp.int32, (BLOCK, 2 * BLOCK), 0)
    kj = lax.broadcasted_iota(jnp.int32, (BLOCK, 2 * BLOCK), 1)
    dist = qi + BLOCK - kj
    return (dist >= 0) & (dist <= BAND_STEPS), kj >= BLOCK


def _band_softmax_pv(qbd, kcat, vcat, mask, nheads):
    s = _dot_nt(qbd, kcat)
    ps, ms, ls = [], [], []
    for p in range(nheads):
        sp = jnp.where(mask, s[p * BLOCK:(p + 1) * BLOCK], NEG_INF)
        m = jnp.max(sp, axis=-1, keepdims=True)
        e = jnp.exp(sp - m)
        ls.append(jnp.sum(e, axis=-1, keepdims=True))
        ms.append(m)
        ps.append(e.astype(BF16))
    o = _dot(jnp.concatenate(ps, axis=0), vcat)
    return o, ms, ls


def _kv_window(cur_ref, prev_ref, qb):
    if qb == 0:
        return jnp.concatenate([prev_ref[...], cur_ref[0:BLOCK]], axis=0)
    return cur_ref[(qb - 1) * BLOCK:(qb + 1) * BLOCK]


def _attn_a_kernel(sink_ref, q_ref, kc_ref, kp_ref, vc_ref, vp_ref, o_ref):
    first_chunk = pl.program_id(1) == 0
    band, in_cur = _band_mask()
    half_bf = _lane_block_masks(LANES, HEAD_DIM, BF16)
    low_half = lax.broadcasted_iota(jnp.int32, (1, LANES), 1) < HEAD_DIM
    for qb in range(ATTN_CHUNK // BLOCK):
        mask = band & (in_cur | jnp.logical_not(first_chunk)) if qb == 0 else band
        kcat = _kv_window(kc_ref, kp_ref, qb)
        vcat = _kv_window(vc_ref, vp_ref, qb)
        rows = slice(qb * BLOCK, (qb + 1) * BLOCK)
        qbd = jnp.concatenate(
            [q_ref[rows, (p // 2) * LANES:(p // 2 + 1) * LANES] * half_bf[p % 2] for p in range(N_HEADS_A)], axis=0)
        o, ms, ls = _band_softmax_pv(qbd, kcat, vcat, mask, N_HEADS_A)
        normed = []
        for p in range(N_HEADS_A):
            den = ls[p] + jnp.exp(sink_ref[HEAD_PERM_A[p]] - ms[p])
            normed.append(o[p * BLOCK:(p + 1) * BLOCK] * (1.0 / den))
        for c in range(N_HEADS_A // 2):
            o_ref[rows, c * LANES:(c + 1) * LANES] = jnp.where(low_half, normed[2 * c], normed[2 * c + 1]).astype(BF16)


def _attn_a_call(sink, qa, ka, va, batch, seq):
    t = qa.shape[0]
    cpb = seq // ATTN_CHUNK
    bpc = ATTN_CHUNK // BLOCK
    cur = lambda b, c: (b * cpb + c, 0)
    prev = lambda b, c: (b * cpb * bpc + jnp.maximum(c * bpc - 1, 0), 0)
    return pl.pallas_call(
        _attn_a_kernel,
        grid=(batch, cpb),
        in_specs=[pl.BlockSpec(memory_space=pltpu.SMEM),
                  pl.BlockSpec((ATTN_CHUNK, Q_A_W), cur),
                  pl.BlockSpec((ATTN_CHUNK, KV_A_W), cur), pl.BlockSpec((BLOCK, KV_A_W), prev),
                  pl.BlockSpec((ATTN_CHUNK, KV_A_W), cur), pl.BlockSpec((BLOCK, KV_A_W), prev)],
        out_specs=pl.BlockSpec((ATTN_CHUNK, Q_A_W), cur),
        out_shape=jax.ShapeDtypeStruct((t, Q_A_W), BF16),
        compiler_params=pltpu.CompilerParams(dimension_semantics=("parallel", "parallel"),
                                             vmem_limit_bytes=VMEM_LIMIT),
        name="attn_a_prompt",
    )(sink, qa, ka, ka, va, va)


def _attn_b_kernel(q_ref, kc_ref, kp_ref, vc_ref, vp_ref, o_ref, lse_ref):
    first_chunk = pl.program_id(2) == 0
    band, in_cur = _band_mask()
    head_bf = _lane_block_masks(KV_B_W, HEAD_DIM, BF16)
    lane = lax.broadcasted_iota(jnp.int32, (1, KV_B_W), 1)
    for qb in range(ATTN_CHUNK // BLOCK):
        mask = band & (in_cur | jnp.logical_not(first_chunk)) if qb == 0 else band
        kcat = _kv_window(kc_ref, kp_ref, qb)
        vcat = _kv_window(vc_ref, vp_ref, qb)
        rows = slice(qb * BLOCK, (qb + 1) * BLOCK)
        q = q_ref[rows, :]
        qbd = jnp.concatenate([q * head_bf[p] for p in range(N_KV_B)], axis=0)
        o, ms, ls = _band_softmax_pv(qbd, kcat, vcat, mask, N_KV_B)
        out = lse = None
        for p in reversed(range(N_KV_B)):
            op = o[p * BLOCK:(p + 1) * BLOCK] * (1.0 / ls[p])
            lp = jnp.broadcast_to(ms[p] + jnp.log(ls[p]), (BLOCK, KV_B_W))
            if out is None:
                out, lse = op, lp
            else:
                sel = lane < (p + 1) * HEAD_DIM
                out, lse = jnp.where(sel, op, out), jnp.where(sel, lp, lse)
        o_ref[rows, :] = out
        lse_ref[rows, :] = lse


def _attn_b_call(qb, kb, vb, group, batch, seq):
    d = DILATIONS[group]
    t = qb.shape[0]
    rows = t // d
    ngroups = len(DILATIONS)
    qv = qb.reshape(rows, d * Q_B_W)
    kv = kb.reshape(rows, d * KV_B_W)
    vv = vb.reshape(rows, d * KV_B_W)
    cpb = seq // d // ATTN_CHUNK
    bpc = ATTN_CHUNK // BLOCK
    qmap = lambda b, r, c: (b * cpb + c, r * ngroups + group)
    cur = lambda b, r, c: (b * cpb + c, r)
    prev = lambda b, r, c: (b * cpb * bpc + jnp.maximum(c * bpc - 1, 0), r)
    o, lse = pl.pallas_call(
        _attn_b_kernel,
        grid=(batch, d, cpb),
        in_specs=[pl.BlockSpec((ATTN_CHUNK, KV_B_W), qmap),
                  pl.BlockSpec((ATTN_CHUNK, KV_B_W), cur), pl.BlockSpec((BLOCK, KV_B_W), prev),
                  pl.BlockSpec((ATTN_CHUNK, KV_B_W), cur), pl.BlockSpec((BLOCK, KV_B_W), prev)],
        out_specs=[pl.BlockSpec((ATTN_CHUNK, KV_B_W), cur), pl.BlockSpec((ATTN_CHUNK, KV_B_W), cur)],
        out_shape=[jax.ShapeDtypeStruct((rows, d * KV_B_W), F32)] * 2,
        compiler_params=pltpu.CompilerParams(dimension_semantics=("parallel", "parallel", "parallel"),
                                             vmem_limit_bytes=VMEM_LIMIT),
        name=f"attn_b_prompt_d{d}",
    )(qv, kv, kv, vv, vv)
    return o.reshape(t, KV_B_W), lse.reshape(t, KV_B_W)


SAMPLE_T = 8
CACHE_A = 128
CACHE_B = 2048
KEYS_A = CACHE_A + BLOCK
KEYS_B = CACHE_B + BLOCK
SAMPLE_BATCH_TILE = 2


def _with_new_rows(new):
    return jnp.concatenate([new, jnp.zeros((BLOCK - SAMPLE_T, new.shape[1]), F32)], axis=0).astype(BF16)


def _attn_sample_kernel(sink_ref, q_ref, kan_ref, van_ref, kbn_ref, vbn_ref, cka_ref, cva_ref, ckb_ref, cvb_ref,
                        oa_ref, ob_ref, ka_s, va_s, kb_s, vb_s):
    half_f = _lane_block_masks(LANES, HEAD_DIM, F32)
    low_half = lax.broadcasted_iota(jnp.int32, (1, LANES), 1) < HEAD_DIM
    lane_b = lax.broadcasted_iota(jnp.int32, (1, KV_B_W), 1)

    ra = lax.broadcasted_iota(jnp.int32, (N_HEADS_A * SAMPLE_T, KEYS_A), 0)
    ia = lax.broadcasted_iota(jnp.int32, (N_HEADS_A * SAMPLE_T, KEYS_A), 1)
    dist_a = CACHE_A + (ra & (SAMPLE_T - 1)) - ia
    mask_a = (dist_a >= 0) & (dist_a <= BAND_STEPS)
    nrow_b = len(DILATIONS) * N_KV_B * SAMPLE_T
    rb = lax.broadcasted_iota(jnp.int32, (nrow_b, KEYS_B), 0)
    ib = lax.broadcasted_iota(jnp.int32, (nrow_b, KEYS_B), 1)
    rows_per_group = N_KV_B * SAMPLE_T
    dil = jnp.where(rb < rows_per_group, DILATIONS[0], jnp.where(rb < 2 * rows_per_group, DILATIONS[1], DILATIONS[2]))
    dist_b = CACHE_B + (rb & (SAMPLE_T - 1)) - ib
    mask_b = (dist_b >= 0) & (dist_b <= BAND_STEPS * dil) & ((dist_b & (dil - 1)) == 0)

    sink_col = jnp.concatenate(
        [jnp.full((SAMPLE_T, 1), sink_ref[HEAD_PERM_A[p]], F32) for p in range(N_HEADS_A)], axis=0)

    for bi in range(SAMPLE_BATCH_TILE):
        rows = slice(bi * SAMPLE_T, (bi + 1) * SAMPLE_T)
        ka_s[0:CACHE_A, :] = cka_ref[bi].astype(BF16)
        ka_s[CACHE_A:KEYS_A, :] = _with_new_rows(kan_ref[rows, :])
        va_s[0:CACHE_A, :] = cva_ref[bi].astype(BF16)
        va_s[CACHE_A:KEYS_A, :] = _with_new_rows(van_ref[rows, :])
        qbd = jnp.concatenate(
            [q_ref[rows, (p // 2) * LANES:(p // 2 + 1) * LANES] * half_f[p % 2] for p in range(N_HEADS_A)],
            axis=0).astype(BF16)
        s = jnp.where(mask_a, _dot_nt(qbd, ka_s[...]), NEG_INF)
        m = jnp.max(s, axis=-1, keepdims=True)
        e = jnp.exp(s - m)
        den = jnp.sum(e, axis=-1, keepdims=True) + jnp.exp(sink_col - m)
        o = _dot(e.astype(BF16), va_s[...]) * (1.0 / den)
        for c in range(N_HEADS_A // 2):
            lo = o[(2 * c) * SAMPLE_T:(2 * c + 1) * SAMPLE_T]
            hi = o[(2 * c + 1) * SAMPLE_T:(2 * c + 2) * SAMPLE_T]
            oa_ref[rows, c * LANES:(c + 1) * LANES] = jnp.where(low_half, lo, hi)
        kb_s[0:CACHE_B, :] = ckb_ref[bi].astype(BF16)
        kb_s[CACHE_B:KEYS_B, :] = _with_new_rows(kbn_ref[rows, :])
        vb_s[0:CACHE_B, :] = cvb_ref[bi].astype(BF16)
        vb_s[CACHE_B:KEYS_B, :] = _with_new_rows(vbn_ref[rows, :])
        qrows = []
        for g in range(len(DILATIONS)):
            for kvh in range(N_KV_B):
                c = (Q_A_W // LANES) + 2 * g + kvh // 2
                part = q_ref[rows, c * LANES:(c + 1) * LANES] * half_f[kvh % 2]
                zero = jnp.zeros_like(part)
                qrows.append(jnp.concatenate([part, zero] if kvh // 2 == 0 else [zero, part], axis=1))
        qbd = jnp.concatenate(qrows, axis=0).astype(BF16)
        s = jnp.where(mask_b, _dot_nt(qbd, kb_s[...]), NEG_INF)
        m = jnp.max(s, axis=-1, keepdims=True)
        e = jnp.exp(s - m)
        l = jnp.sum(e, axis=-1, keepdims=True)
        o = _dot(e.astype(BF16), vb_s[...])
        out = None
        for kvh in reversed(range(N_KV_B)):
            sl = [slice((g * N_KV_B + kvh) * SAMPLE_T, (g * N_KV_B + kvh + 1) * SAMPLE_T)
                  for g in range(len(DILATIONS))]
            mj = jnp.maximum(jnp.maximum(m[sl[0]], m[sl[1]]), m[sl[2]])
            ws = [jnp.exp(m[x] - mj) for x in sl]
            den = ws[0] * l[sl[0]] + ws[1] * l[sl[1]] + ws[2] * l[sl[2]]
            num = ws[0] * o[sl[0]] + ws[1] * o[sl[1]] + ws[2] * o[sl[2]]
            okv = num * (1.0 / den)
            out = okv if out is None else jnp.where(lane_b < (kvh + 1) * HEAD_DIM, okv, out)
        ob_ref[rows, :] = out


def _attn_sample_call(sink, q32, ka32, va32, kb32, vb32, cka, cva, ckb, cvb):
    nb = cka.shape[0]
    bt = SAMPLE_BATCH_TILE
    t = q32.shape[0]
    tok = lambda w: pl.BlockSpec((bt * SAMPLE_T, w), lambda i: (i, 0))
    cache = lambda n, w: pl.BlockSpec((bt, n, w), lambda i: (i, 0, 0))
    return pl.pallas_call(
        _attn_sample_kernel,
        grid=(nb // bt,),
        in_specs=[pl.BlockSpec(memory_space=pltpu.SMEM), tok(Q_A_W + Q_B_W),
                  tok(KV_A_W), tok(KV_A_W), tok(KV_B_W), tok(KV_B_W),
                  cache(CACHE_A, KV_A_W), cache(CACHE_A, KV_A_W), cache(CACHE_B, KV_B_W), cache(CACHE_B, KV_B_W)],
        out_specs=[tok(Q_A_W), tok(KV_B_W)],
        out_shape=[jax.ShapeDtypeStruct((t, Q_A_W), F32), jax.ShapeDtypeStruct((t, KV_B_W), F32)],
        scratch_shapes=[pltpu.VMEM((KEYS_A, KV_A_W), BF16), pltpu.VMEM((KEYS_A, KV_A_W), BF16),
                        pltpu.VMEM((KEYS_B, KV_B_W), BF16), pltpu.VMEM((KEYS_B, KV_B_W), BF16)],
        compiler_params=pltpu.CompilerParams(dimension_semantics=("parallel",), vmem_limit_bytes=VMEM_LIMIT),
        name="attn_sample",
    )(sink, q32, ka32, va32, kb32, vb32, cka, cva, ckb, cvb)


def _mix_out(h, oa, ob, gm_ref, wgate_ref, wba_ref, wbb_ref, wout_ref):
    n = _rms(h, gm_ref[...]).astype(BF16)
    gates = _dot(n, wgate_ref[...])
    ga = jax.nn.sigmoid(gates[:, :D_MODEL])
    gb = jax.nn.sigmoid(gates[:, D_MODEL:])
    mixed = ga * _dot(oa.astype(BF16), wba_ref[...]) + gb * _dot(ob.astype(BF16), wbb_ref[...])
    return h + _dot(mixed.astype(BF16), wout_ref[...])


def _combine_groups(o_refs, lse_refs):
    ls = [r[...] for r in lse_refs]
    m = jnp.maximum(jnp.maximum(ls[0], ls[1]), ls[2])
    es = [jnp.exp(x - m) for x in ls]
    den = es[0] + es[1] + es[2]
    num = es[0] * o_refs[0][...] + es[1] * o_refs[1][...] + es[2] * o_refs[2][...]
    return num * (1.0 / den)


def _softmax_rows(s):
    m = jnp.max(s, axis=-1, keepdims=True)
    e = jnp.exp(s - m)
    return e * (1.0 / jnp.sum(e, axis=-1, keepdims=True))


def _back_prompt_kernel(h_ref, oa_ref, o0_ref, o1_ref, o2_ref, l0_ref, l1_ref, l2_ref, mk_ref, mv_ref,
                        gm_ref, wgate_ref, wba_ref, wbb_ref, wout_ref, gq_ref, wq_ref, wo_ref,
                        g2_ref, wg_ref, wu_ref, wd_ref, gf_ref, y_ref):
    ob = _combine_groups((o0_ref, o1_ref, o2_ref), (l0_ref, l1_ref, l2_ref))
    h2 = _mix_out(h_ref[...], oa_ref[...], ob, gm_ref, wgate_ref, wba_ref, wbb_ref, wout_ref)
    q = _dot(_rms(h2, gq_ref[...]).astype(BF16), wq_ref[...])
    heads = []
    for hd in range(MEM_HEADS):
        cols = slice(hd * MEM_HEAD_DIM, (hd + 1) * MEM_HEAD_DIM)
        p = _softmax_rows(_dot_nt(q[:, cols].astype(BF16), mk_ref[0, :, cols]) * MEM_SCALE)
        heads.append(_dot(p.astype(BF16), mv_ref[0, :, cols]))
    oc = jnp.concatenate(heads, axis=1).astype(BF16)
    h3 = h2 + _dot(oc, wo_ref[...])
    y_ref[...] = _rms(_ffn_half(h3, g2_ref, wg_ref, wu_ref, wd_ref), gf_ref[...])


def _back_prompt_call(h, oa, obs, lses, mk, mv, w, seq):
    t = h.shape[0]
    tm = TOKEN_TILE
    tiles_per_batch = seq // tm
    row = lambda i: (i, 0)
    mem = lambda i: (i // tiles_per_batch, 0, 0)
    tok = lambda width: pl.BlockSpec((tm, width), row)
    in_specs = ([tok(D_MODEL), tok(Q_A_W)] + [tok(KV_B_W)] * 6
                + [pl.BlockSpec((1, N_MEM, MEM_W), mem), pl.BlockSpec((1, N_MEM, MEM_W), mem)]
                + [_const_spec((1, D_MODEL)), _const_spec((D_MODEL, 2 * D_MODEL)), _const_spec((Q_A_W, D_MODEL)),
                   _const_spec((KV_B_W, D_MODEL)), _const_spec((D_MODEL, D_MODEL)),
                   _const_spec((1, D_MODEL)), _const_spec((D_MODEL, MEM_W)), _const_spec((MEM_W, D_MODEL)),
                   _const_spec((1, D_MODEL)), _const_spec((D_MODEL, D_FF)), _const_spec((D_MODEL, D_FF)),
                   _const_spec((D_FF, D_MODEL)), _const_spec((1, D_MODEL))])
    return pl.pallas_call(
        _back_prompt_kernel,
        grid=(t // tm,),
        in_specs=in_specs,
        out_specs=tok(D_MODEL),
        out_shape=jax.ShapeDtypeStruct((t, D_MODEL), F32),
        compiler_params=pltpu.CompilerParams(dimension_semantics=("parallel",), vmem_limit_bytes=VMEM_LIMIT),
        name="back_prompt",
    )(h, oa, *obs, *lses, mk, mv, w["mix_norm"], w["w_gate"], w["w_branch_a"], w["w_branch_b"], w["w_out"],
      w["mem_q_norm"], w["w_mem_q"], w["w_mem_o"], w["ffn2_norm"], w["ffn2_wg"], w["ffn2_wu"], w["ffn2_wd"],
      w["final_norm"])


def _back_sample_a_kernel(h_ref, oa_ref, ob_ref, gm_ref, wgate_ref, wba_ref, wbb_ref, wout_ref, gq_ref, wq_ref,
                          h2_ref, q_ref):
    h2 = _mix_out(h_ref[...], oa_ref[...], ob_ref[...], gm_ref, wgate_ref, wba_ref, wbb_ref, wout_ref)
    h2_ref[...] = h2
    q_ref[...] = _dot(_rms(h2, gq_ref[...]).astype(BF16), wq_ref[...])


def _back_sample_a_call(h, oa, ob, w):
    t = h.shape[0]
    tm = TOKEN_TILE
    tok = lambda width: pl.BlockSpec((tm, width), lambda i: (i, 0))
    return pl.pallas_call(
        _back_sample_a_kernel,
        grid=(t // tm,),
        in_specs=[tok(D_MODEL), tok(Q_A_W), tok(KV_B_W),
                  _const_spec((1, D_MODEL)), _const_spec((D_MODEL, 2 * D_MODEL)), _const_spec((Q_A_W, D_MODEL)),
                  _const_spec((KV_B_W, D_MODEL)), _const_spec((D_MODEL, D_MODEL)),
                  _const_spec((1, D_MODEL)), _const_spec((D_MODEL, MEM_W))],
        out_specs=[tok(D_MODEL), tok(MEM_W)],
        out_shape=[jax.ShapeDtypeStruct((t, D_MODEL), F32), jax.ShapeDtypeStruct((t, MEM_W), F32)],
        compiler_params=pltpu.CompilerParams(dimension_semantics=("parallel",), vmem_limit_bytes=VMEM_LIMIT),
        name="back_sample_mix",
    )(h, oa, ob, w["mix_norm"], w["w_gate"], w["w_branch_a"], w["w_branch_b"], w["w_out"],
      w["mem_q_norm"], w["w_mem_q"])


CROSS_BATCH_TILE = 8


def _cross_sample_kernel(q_ref, mk_ref, mv_ref, o_ref):
    head_f = _lane_block_masks(MEM_W, MEM_HEAD_DIM, F32)
    for bi in range(CROSS_BATCH_TILE):
        rows = slice(bi * SAMPLE_T, (bi + 1) * SAMPLE_T)
        q = q_ref[rows, :]
        qbd = jnp.concatenate([q * head_f[hd] for hd in range(MEM_HEADS)], axis=0).astype(BF16)
        p = _softmax_rows(_dot_nt(qbd, mk_ref[bi].astype(BF16)) * MEM_SCALE)
        o = _dot(p.astype(BF16), mv_ref[bi].astype(BF16))
        for hd in range(MEM_HEADS):
            cols = slice(hd * MEM_HEAD_DIM, (hd + 1) * MEM_HEAD_DIM)
            o_ref[rows, cols] = o[hd * SAMPLE_T:(hd + 1) * SAMPLE_T, cols]


def _cross_sample_call(q, mk, mv):
    t = q.shape[0]
    bt = CROSS_BATCH_TILE
    tok = pl.BlockSpec((bt * SAMPLE_T, MEM_W), lambda i: (i, 0))
    mem = pl.BlockSpec((bt, N_MEM, MEM_W), lambda i: (i, 0, 0))
    return pl.pallas_call(
        _cross_sample_kernel,
        grid=(mk.shape[0] // bt,),
        in_specs=[tok, mem, mem],
        out_specs=tok,
        out_shape=jax.ShapeDtypeStruct((t, MEM_W), F32),
        compiler_params=pltpu.CompilerParams(dimension_semantics=("parallel",), vmem_limit_bytes=VMEM_LIMIT),
        name="cross_sample",
    )(q, mk, mv)


def _back_sample_b_kernel(h2_ref, oc_ref, wo_ref, g2_ref, wg_ref, wu_ref, wd_ref, gf_ref, y_ref):
    h3 = h2_ref[...] + _dot(oc_ref[...].astype(BF16), wo_ref[...])
    y_ref[...] = _rms(_ffn_half(h3, g2_ref, wg_ref, wu_ref, wd_ref), gf_ref[...])


def _back_sample_b_call(h2, oc, w):
    t = h2.shape[0]
    tm = TOKEN_TILE
    tok = lambda width: pl.BlockSpec((tm, width), lambda i: (i, 0))
    return pl.pallas_call(
        _back_sample_b_kernel,
        grid=(t // tm,),
        in_specs=[tok(D_MODEL), tok(MEM_W), _const_spec((MEM_W, D_MODEL)),
                  _const_spec((1, D_MODEL)), _const_spec((D_MODEL, D_FF)), _const_spec((D_MODEL, D_FF)),
                  _const_spec((D_FF, D_MODEL)), _const_spec((1, D_MODEL))],
        out_specs=tok(D_MODEL),
        out_shape=jax.ShapeDtypeStruct((t, D_MODEL), F32),
        compiler_params=pltpu.CompilerParams(dimension_semantics=("parallel",), vmem_limit_bytes=VMEM_LIMIT),
        name="back_sample_ffn",
    )(h2, oc, w["w_mem_o"], w["ffn2_norm"], w["ffn2_wg"], w["ffn2_wu"], w["ffn2_wd"], w["final_norm"])


def _mem_kv_kernel(mem_ref, g_ref, wkv_ref, k32_ref, v32_ref, k16_ref, v16_ref):
    u = _rms(mem_ref[...], g_ref[...]).astype(BF16)
    kv = _dot(u, wkv_ref[...])
    k, v = kv[:, :MEM_W], kv[:, MEM_W:]
    k32_ref[...] = k
    v32_ref[...] = v
    k16_ref[...] = k.astype(BF16)
    v16_ref[...] = v.astype(BF16)


def _mem_kv_call(mem, w):
    rows = mem.shape[0]
    full = lambda width: pl.BlockSpec((rows, width), lambda i: (0, 0))
    return pl.pallas_call(
        _mem_kv_kernel,
        grid=(1,),
        in_specs=[full(D_MODEL), _const_spec((1, D_MODEL)), _const_spec((D_MODEL, 2 * MEM_W))],
        out_specs=[full(MEM_W)] * 4,
        out_shape=[jax.ShapeDtypeStruct((rows, MEM_W), F32)] * 2 + [jax.ShapeDtypeStruct((rows, MEM_W), BF16)] * 2,
        compiler_params=pltpu.CompilerParams(dimension_semantics=("arbitrary",), vmem_limit_bytes=VMEM_LIMIT),
        name="mem_kv",
    )(mem, w["mem_kv_norm"], w["w_mem_kv"])


def _rope_tables(pos):
    inv_freq = jnp.power(jnp.float32(ROPE_THETA), -jnp.arange(ROT_HALF, dtype=jnp.float32) / ROT_HALF)
    ang = pos.astype(jnp.float32)[:, None] * inv_freq[None, :]
    cos, sin = jnp.cos(ang), jnp.sin(ang)
    n = pos.shape[0]
    pad = HEAD_DIM - 2 * ROT_HALF
    one_head = lambda a, b, fill: jnp.concatenate([a, b, jnp.full((n, pad), fill, jnp.float32)], axis=1)
    zeros = jnp.zeros((n, ROT_HALF), jnp.float32)
    two_heads = lambda x: jnp.concatenate([x, x], axis=1)
    return (two_heads(one_head(cos, cos, 1.0)), two_heads(one_head(-sin, zeros, 0.0)),
            two_heads(one_head(zeros, sin, 0.0)))


def _prep_weights(p):
    bf = lambda x: x.astype(BF16)
    vec = lambda x: x.reshape(1, -1)
    w_in = p["w_in"]
    perm_cols = np.concatenate([np.arange(h * HEAD_DIM, (h + 1) * HEAD_DIM) for h in HEAD_PERM_A])
    w_qkv = jnp.concatenate([w_in[:, perm_cols], w_in[:, Q_A_W:QKV_W]], axis=1)
    return {
        "ffn1_norm": vec(p["ffn1_norm"]), "ffn1_wg": bf(p["ffn1_w_gate"]), "ffn1_wu": bf(p["ffn1_w_up"]),
        "ffn1_wd": bf(p["ffn1_w_down"]),
        "mix_norm": vec(p["mix_norm"]), "w_qkv": bf(w_qkv), "w_gate": bf(w_in[:, QKV_W:]),
        "w_branch_a": bf(p["w_branch_a"][perm_cols, :]), "w_branch_b": bf(p["w_branch_b"]), "w_out": bf(p["w_out"]),
        "mem_q_norm": vec(p["mem_q_norm"]), "mem_kv_norm": vec(p["mem_kv_norm"]),
        "w_mem_q": bf(p["w_mem_q"]), "w_mem_kv": bf(jnp.concatenate([p["w_mem_k"], p["w_mem_v"]], axis=1)),
        "w_mem_o": bf(p["w_mem_o"]),
        "ffn2_norm": vec(p["ffn2_norm"]), "ffn2_wg": bf(p["ffn2_w_gate"]), "ffn2_wu": bf(p["ffn2_w_up"]),
        "ffn2_wd": bf(p["ffn2_w_down"]), "final_norm": vec(p["final_norm"]),
    }


def kernel(x_prompt, x_sample, cache_swa_k, cache_swa_v, cache_dil_k, cache_dil_v, cache_mem_k, cache_mem_v, mem_prompt, ffn1_norm, ffn1_w_gate, ffn1_w_up, ffn1_w_down, mix_norm, w_in, attn_sink, w_branch_a, w_branch_b, w_out, mem_q_norm, mem_kv_norm, w_mem_q, w_mem_k, w_mem_v, w_mem_o, ffn2_norm, ffn2_w_gate, ffn2_w_up, ffn2_w_down, final_norm):
    depth = ffn1_norm.shape[0]
    assert depth == 1
    batch, seq, d = x_prompt.shape
    nb, t_new, _ = x_sample.shape
    assert d == D_MODEL and t_new == SAMPLE_T and seq % (ATTN_CHUNK * DILATIONS[-1]) == 0
    assert cache_swa_k.shape[2] == CACHE_A and cache_dil_k.shape[2] == CACHE_B
    layer = lambda x: x[0]
    w = _prep_weights(dict(
        ffn1_norm=layer(ffn1_norm), ffn1_w_gate=layer(ffn1_w_gate), ffn1_w_up=layer(ffn1_w_up),
        ffn1_w_down=layer(ffn1_w_down), mix_norm=layer(mix_norm), w_in=layer(w_in),
        w_branch_a=layer(w_branch_a), w_branch_b=layer(w_branch_b), w_out=layer(w_out),
        mem_q_norm=layer(mem_q_norm), mem_kv_norm=layer(mem_kv_norm), w_mem_q=layer(w_mem_q),
        w_mem_k=layer(w_mem_k), w_mem_v=layer(w_mem_v), w_mem_o=layer(w_mem_o), ffn2_norm=layer(ffn2_norm),
        ffn2_w_gate=layer(ffn2_w_gate), ffn2_w_up=layer(ffn2_w_up), ffn2_w_down=layer(ffn2_w_down),
        final_norm=final_norm))
    sink = layer(attn_sink)

    tp = batch * seq
    mk32, mv32, mk16, mv16 = _mem_kv_call(mem_prompt.reshape(batch * N_MEM, d), w)
    tables_p = _rope_tables(jnp.arange(seq, dtype=jnp.int32))
    (h_p, qa, ka, va, qb, kb, vb, ka32, va32, kb32, vb32) = _front_call(
        x_prompt.reshape(tp, d), tables_p, w, sample=False)
    oa = _attn_a_call(sink, qa, ka, va, batch, seq)
    obs, lses = zip(*[_attn_b_call(qb, kb, vb, g, batch, seq) for g in range(len(DILATIONS))])
    y_p = _back_prompt_call(h_p, oa, obs, lses, mk16.reshape(batch, N_MEM, MEM_W),
                            mv16.reshape(batch, N_MEM, MEM_W), w, seq)

    ts = nb * t_new
    pos_s = PAST_LEN + (jnp.arange(TOKEN_TILE, dtype=jnp.int32) % t_new)
    tables_s = _rope_tables(pos_s)
    (h_s, q32, ka32_s, va32_s, kb32_s, vb32_s) = _front_call(x_sample.reshape(ts, d), tables_s, w, sample=True)
    oa_s, ob_s = _attn_sample_call(
        sink, q32, ka32_s, va32_s, kb32_s, vb32_s,
        layer(cache_swa_k).reshape(nb, CACHE_A, KV_A_W), layer(cache_swa_v).reshape(nb, CACHE_A, KV_A_W),
        layer(cache_dil_k).reshape(nb, CACHE_B, KV_B_W), layer(cache_dil_v).reshape(nb, CACHE_B, KV_B_W))
    h2_s, qc_s = _back_sample_a_call(h_s, oa_s, ob_s, w)
    oc_s = _cross_sample_call(qc_s, layer(cache_mem_k).reshape(nb, N_MEM, MEM_W),
                              layer(cache_mem_v).reshape(nb, N_MEM, MEM_W))
    y_s = _back_sample_b_call(h2_s, oc_s, w)

    keep_a, keep_b = min(CACHE_A, seq), min(CACHE_B, seq)
    tail = lambda x, keep, heads: x.reshape(batch, seq, heads, HEAD_DIM)[:, seq - keep:][None]
    new = lambda x, heads: x.reshape(1, nb, t_new, heads, HEAD_DIM)
    memo = lambda x: x.reshape(1, batch, N_MEM, MEM_HEADS, MEM_HEAD_DIM)
    return (y_p.reshape(batch, seq, d), y_s.reshape(nb, t_new, d),
            tail(ka32, keep_a, N_KV_A), tail(va32, keep_a, N_KV_A),
            tail(kb32, keep_b, N_KV_B), tail(vb32, keep_b, N_KV_B),
            memo(mk32), memo(mv32),
            new(ka32_s, N_KV_A), new(va32_s, N_KV_A), new(kb32_s, N_KV_B), new(vb32_s, N_KV_B))
```

```python
import functools

import jax
import jax.numpy as jnp
import numpy as np
from jax import lax
from jax.experimental import pallas as pl
from jax.experimental.pallas import tpu as pltpu

F32 = jnp.float32
BF16 = jnp.bfloat16

D_MODEL = 1024
D_FF = 2816
HEAD_DIM = 64
ROT_HALF = 8
ROPE_THETA = 500000.0
ATTN_SCALE = HEAD_DIM ** -0.5
RMS_EPS = 1e-6
PAST_LEN = 16384

N_HEADS_A = 8
N_KV_A = 2
N_KV_B = 4
DILATIONS = (1, 4, 16)
BAND_STEPS = 128
Q_A_W = N_HEADS_A * HEAD_DIM
KV_A_W = N_KV_A * HEAD_DIM
Q_B_W = len(DILATIONS) * N_KV_B * HEAD_DIM
KV_B_W = N_KV_B * HEAD_DIM
QKV_W = Q_A_W + 2 * KV_A_W + Q_B_W + 2 * KV_B_W
MEM_HEADS = 4
MEM_HEAD_DIM = 128
MEM_W = MEM_HEADS * MEM_HEAD_DIM
MEM_SCALE = MEM_HEAD_DIM ** -0.5
N_MEM = 256

HEAD_PERM_A = (0, 4, 1, 5, 2, 6, 3, 7)

LANES = 128
BLOCK = 128
TOKEN_TILE = 512
ATTN_CHUNK = 512
FF_CHUNKS = ((0, 1536), (1536, 2816))
VMEM_LIMIT = 56 * 1024 * 1024
NEG_INF = float("-inf")


def _rms(x, g):
    ms = jnp.mean(x * x, axis=-1, keepdims=True)
    return x * lax.rsqrt(ms + RMS_EPS) * g


def _dot(a, b):
    return jnp.dot(a, b, preferred_element_type=F32)


def _dot_nt(a, b):
    return lax.dot_general(a, b, (((1,), (1,)), ((), ())), preferred_element_type=F32)


def _ffn_half(x, g_ref, wg_ref, wu_ref, wd_ref):
    u = _rms(x, g_ref[...]).astype(BF16)
    acc = None
    for lo, hi in FF_CHUNKS:
        gate = _dot(u, wg_ref[:, lo:hi])
        up = _dot(u, wu_ref[:, lo:hi])
        act = (gate * jax.nn.sigmoid(gate) * up).astype(BF16)
        part = _dot(act, wd_ref[lo:hi, :])
        acc = part if acc is None else acc + part
    return x + 0.5 * acc


def _front_kernel(x_ref, cos_ref, sa_ref, sb_ref, g1_ref, wg_ref, wu_ref, wd_ref, gm_ref, wqkv_ref,
                  h_ref, *rest, sample):
    x = x_ref[...]
    h = _ffn_half(x, g1_ref, wg_ref, wu_ref, wd_ref)
    h_ref[...] = h
    n = _rms(h, gm_ref[...]).astype(BF16)
    z = _dot(n, wqkv_ref[...])
    cos, sa, sb = cos_ref[...], sa_ref[...], sb_ref[...]

    def chunk(c):
        return z[:, c * LANES:(c + 1) * LANES]

    def rope(c):
        zc = chunk(c)
        return zc * cos + pltpu.roll(zc, LANES - ROT_HALF, 1) * sa + pltpu.roll(zc, ROT_HALF, 1) * sb

    nqa, nka = Q_A_W // LANES, KV_A_W // LANES
    nqb, nkb = Q_B_W // LANES, KV_B_W // LANES
    c0 = 0
    qa = [rope(c0 + c) * ATTN_SCALE for c in range(nqa)]
    c0 += nqa
    ka = [rope(c0 + c) for c in range(nka)]
    c0 += nka
    va = [chunk(c0 + c) for c in range(nka)]
    c0 += nka
    qb = [rope(c0 + c) * ATTN_SCALE for c in range(nqb)]
    c0 += nqb
    kb = [rope(c0 + c) for c in range(nkb)]
    c0 += nkb
    vb = [chunk(c0 + c) for c in range(nkb)]

    def store(ref, parts, dtype):
        for c, p in enumerate(parts):
            ref[:, c * LANES:(c + 1) * LANES] = p.astype(dtype)

    def store_chunks(scr, parts):
        for c, p in enumerate(parts):
            scr[c] = p

    def store_split(ref, scr, d):
        n = scr.shape[1] // d
        for r in range(d):
            for c in range(scr.shape[0]):
                ref[0, r, :, c * LANES:(c + 1) * LANES] = scr[c, pl.ds(r, n, stride=d), :].astype(BF16)

    if sample:
        q32_ref, ka32_ref, va32_ref, kb32_ref, vb32_ref = rest
        store(q32_ref, qa + qb, F32)
    else:
        (qa_ref, ka_ref, va_ref, qb0_ref, kb_ref, vb_ref, q4_ref, k4_ref, v4_ref, q16_ref, k16_ref, v16_ref,
         ka32_ref, va32_ref, kb32_ref, vb32_ref, q1_s, q2_s, kb_s, vb_s) = rest
        gw = KV_B_W // LANES
        store(qa_ref, qa, BF16)
        store(ka_ref, ka, BF16)
        store(va_ref, va, BF16)
        store(qb0_ref, qb[0:gw], BF16)
        store(kb_ref, kb, BF16)
        store(vb_ref, vb, BF16)
        store_chunks(q1_s, qb[gw:2 * gw])
        store_chunks(q2_s, qb[2 * gw:3 * gw])
        store_chunks(kb_s, kb)
        store_chunks(vb_s, vb)
        store_split(q4_ref, q1_s, DILATIONS[1])
        store_split(k4_ref, kb_s, DILATIONS[1])
        store_split(v4_ref, vb_s, DILATIONS[1])
        store_split(q16_ref, q2_s, DILATIONS[2])
        store_split(k16_ref, kb_s, DILATIONS[2])
        store_split(v16_ref, vb_s, DILATIONS[2])
    store(ka32_ref, ka, F32)
    store(va32_ref, va, F32)
    store(kb32_ref, kb, F32)
    store(vb32_ref, vb, F32)


def _const_spec(shape):
    nd = len(shape)
    return pl.BlockSpec(shape, lambda *_: (0,) * nd, pipeline_mode=pl.Buffered(1))


def _front_call(x, tables, w, *, sample, batch=1):
    t = x.shape[0]
    tm = TOKEN_TILE
    nsteps = t // tm
    tpb = nsteps // batch
    seq = t // batch
    cos_t, sa_t, sb_t = tables
    tab_blocks = cos_t.shape[0] // tm
    row = lambda i: (i, 0)
    tab = lambda i: (i % tab_blocks, 0)

    def tok(width, dtype):
        return jax.ShapeDtypeStruct((t, width), dtype), pl.BlockSpec((tm, width), row)

    def split(d):
        return (jax.ShapeDtypeStruct((batch, d, seq // d, KV_B_W), BF16),
                pl.BlockSpec((1, d, tm // d, KV_B_W), lambda i: (i // tpb, 0, i % tpb, 0)))

    outs = [tok(D_MODEL, F32)]
    scratch = []
    if sample:
        outs += [tok(Q_A_W + Q_B_W, F32)]
    else:
        outs += [tok(Q_A_W, BF16), tok(KV_A_W, BF16), tok(KV_A_W, BF16),
                 tok(KV_B_W, BF16), tok(KV_B_W, BF16), tok(KV_B_W, BF16)]
        outs += [split(DILATIONS[1])] * 3 + [split(DILATIONS[2])] * 3
        scratch = [pltpu.VMEM((KV_B_W // LANES, tm, LANES), F32)] * 4
    outs += [tok(KV_A_W, F32), tok(KV_A_W, F32), tok(KV_B_W, F32), tok(KV_B_W, F32)]
    out_shape, out_specs = zip(*outs)
    in_specs = [pl.BlockSpec((tm, D_MODEL), row),
                pl.BlockSpec((tm, LANES), tab), pl.BlockSpec((tm, LANES), tab), pl.BlockSpec((tm, LANES), tab),
                _const_spec((1, D_MODEL)), _const_spec((D_MODEL, D_FF)), _const_spec((D_MODEL, D_FF)),
                _const_spec((D_FF, D_MODEL)), _const_spec((1, D_MODEL)), _const_spec((D_MODEL, QKV_W))]
    return pl.pallas_call(
        functools.partial(_front_kernel, sample=sample),
        grid=(nsteps,),
        in_specs=in_specs,
        out_specs=list(out_specs),
        out_shape=list(out_shape),
        scratch_shapes=scratch,
        compiler_params=pltpu.CompilerParams(dimension_semantics=("parallel",), vmem_limit_bytes=VMEM_LIMIT),
        name="front_sample" if sample else "front_prompt",
    )(x, cos_t, sa_t, sb_t, w["ffn1_norm"], w["ffn1_wg"], w["ffn1_wu"], w["ffn1_wd"], w["mix_norm"], w["w_qkv"])


def _lane_block_masks(width, block, dtype):
    lane = lax.broadcasted_iota(jnp.int32, (1, width), 1)
    return [((lane >= i * block) & (lane < (i + 1) * block)).astype(dtype) for i in range(width // block)]


def _band_mask():
    qi = lax.broadcasted_iota(jnp.int32, (BLOCK, 2 * BLOCK), 0)
    kj = lax.broadcasted_iota(jnp.int32, (BLOCK, 2 * BLOCK), 1)
    dist = qi + BLOCK - kj
    return (dist >= 0) & (dist <= BAND_STEPS), kj >= BLOCK


def _band_softmax_pv(qbd, kcat, vcat, mask, nheads):
    s = _dot_nt(qbd, kcat)
    ps, ms, ls = [], [], []
    for p in range(nheads):
        sp = jnp.where(mask, s[p * BLOCK:(p + 1) * BLOCK], NEG_INF)
        m = jnp.max(sp, axis=-1, keepdims=True)
        e = jnp.exp(sp - m)
        ls.append(jnp.sum(e, axis=-1, keepdims=True))
        ms.append(m)
        ps.append(e.astype(BF16))
    o = _dot(jnp.concatenate(ps, axis=0), vcat)
    return o, ms, ls


def _kv_window(cur_ref, prev_ref, qb):
    if qb == 0:
        return jnp.concatenate([prev_ref[...], cur_ref[0:BLOCK]], axis=0)
    return cur_ref[(qb - 1) * BLOCK:(qb + 1) * BLOCK]


def _attn_a_kernel(sink_ref, q_ref, kc_ref, kp_ref, vc_ref, vp_ref, o_ref):
    first_chunk = pl.program_id(1) == 0
    band, in_cur = _band_mask()
    half_bf = _lane_block_masks(LANES, HEAD_DIM, BF16)
    low_half = lax.broadcasted_iota(jnp.int32, (1, LANES), 1) < HEAD_DIM
    for qb in range(ATTN_CHUNK // BLOCK):
        mask = band & (in_cur | jnp.logical_not(first_chunk)) if qb == 0 else band
        kcat = _kv_window(kc_ref, kp_ref, qb)
        vcat = _kv_window(vc_ref, vp_ref, qb)
        rows = slice(qb * BLOCK, (qb + 1) * BLOCK)
        qbd = jnp.concatenate(
            [q_ref[rows, (p // 2) * LANES:(p // 2 + 1) * LANES] * half_bf[p % 2] for p in range(N_HEADS_A)], axis=0)
        o, ms, ls = _band_softmax_pv(qbd, kcat, vcat, mask, N_HEADS_A)
        normed = []
        for p in range(N_HEADS_A):
            den = ls[p] + jnp.exp(sink_ref[HEAD_PERM_A[p]] - ms[p])
            normed.append(o[p * BLOCK:(p + 1) * BLOCK] * (1.0 / den))
        for c in range(N_HEADS_A // 2):
            o_ref[rows, c * LANES:(c + 1) * LANES] = jnp.where(low_half, normed[2 * c], normed[2 * c + 1]).astype(BF16)


def _attn_a_call(sink, qa, ka, va, batch, seq):
    t = qa.shape[0]
    cpb = seq // ATTN_CHUNK
    bpc = ATTN_CHUNK // BLOCK
    cur = lambda b, c: (b * cpb + c, 0)
    prev = lambda b, c: (b * cpb * bpc + jnp.maximum(c * bpc - 1, 0), 0)
    return pl.pallas_call(
        _attn_a_kernel,
        grid=(batch, cpb),
        in_specs=[pl.BlockSpec(memory_space=pltpu.SMEM),
                  pl.BlockSpec((ATTN_CHUNK, Q_A_W), cur),
                  pl.BlockSpec((ATTN_CHUNK, KV_A_W), cur), pl.BlockSpec((BLOCK, KV_A_W), prev),
                  pl.BlockSpec((ATTN_CHUNK, KV_A_W), cur), pl.BlockSpec((BLOCK, KV_A_W), prev)],
        out_specs=pl.BlockSpec((ATTN_CHUNK, Q_A_W), cur),
        out_shape=jax.ShapeDtypeStruct((t, Q_A_W), BF16),
        compiler_params=pltpu.CompilerParams(dimension_semantics=("parallel", "parallel"),
                                             vmem_limit_bytes=VMEM_LIMIT),
        name="attn_a_prompt",
    )(sink, qa, ka, ka, va, va)


def _attn_b_kernel(q_ref, kc_ref, kp_ref, vc_ref, vp_ref, o_ref, lse_ref):
    first_chunk = pl.program_id(2) == 0
    band, in_cur = _band_mask()
    head_bf = _lane_block_masks(KV_B_W, HEAD_DIM, BF16)
    lane = lax.broadcasted_iota(jnp.int32, (1, KV_B_W), 1)
    for qb in range(ATTN_CHUNK // BLOCK):
        mask = band & (in_cur | jnp.logical_not(first_chunk)) if qb == 0 else band
        kcat = _kv_window(kc_ref, kp_ref, qb)
        vcat = _kv_window(vc_ref, vp_ref, qb)
        rows = slice(qb * BLOCK, (qb + 1) * BLOCK)
        q = q_ref[rows, :]
        qbd = jnp.concatenate([q * head_bf[p] for p in range(N_KV_B)], axis=0)
        o, ms, ls = _band_softmax_pv(qbd, kcat, vcat, mask, N_KV_B)
        out = lse = None
        for p in reversed(range(N_KV_B)):
            op = o[p * BLOCK:(p + 1) * BLOCK] * (1.0 / ls[p])
            lp = jnp.broadcast_to(ms[p] + jnp.log(ls[p]), (BLOCK, KV_B_W))
            if out is None:
                out, lse = op, lp
            else:
                sel = lane < (p + 1) * HEAD_DIM
                out, lse = jnp.where(sel, op, out), jnp.where(sel, lp, lse)
        o_ref[rows, :] = out
        lse_ref[rows, :] = lse


def _attn_b_call(q, k, v, d):
    batch, _, rows, _ = q.shape
    cpb = rows // ATTN_CHUNK
    bpc = ATTN_CHUNK // BLOCK
    cur = pl.BlockSpec((None, None, ATTN_CHUNK, KV_B_W), lambda b, r, c: (b, r, c, 0))
    prev = pl.BlockSpec((None, None, BLOCK, KV_B_W), lambda b, r, c: (b, r, jnp.maximum(c * bpc - 1, 0), 0))
    return pl.pallas_call(
        _attn_b_kernel,
        grid=(batch, d, cpb),
        in_specs=[cur, cur, prev, cur, prev],
        out_specs=[cur, cur],
        out_shape=[jax.ShapeDtypeStruct((batch, d, rows, KV_B_W), F32)] * 2,
        compiler_params=pltpu.CompilerParams(dimension_semantics=("parallel", "parallel", "parallel"),
                                             vmem_limit_bytes=VMEM_LIMIT),
        name=f"attn_b_prompt_d{d}",
    )(q, k, k, v, v)


SAMPLE_T = 8
CACHE_A = 128
CACHE_B = 2048
SAMPLE_BATCH_TILE = 2


def _new_rows_block(new):
    return jnp.concatenate([new, jnp.zeros((BLOCK - SAMPLE_T, new.shape[1]), F32)], axis=0).astype(BF16)


def _cached_softmax_pv(qbd, kt, vt, knew, vnew, mask_c, mask_n):
    s_c = jnp.where(mask_c, _dot(qbd, kt), NEG_INF)
    s_n = jnp.where(mask_n, _dot_nt(qbd, knew), NEG_INF)
    m = jnp.maximum(jnp.max(s_c, axis=-1, keepdims=True), jnp.max(s_n, axis=-1, keepdims=True))
    e_c = jnp.exp(s_c - m)
    e_n = jnp.exp(s_n - m)
    l = jnp.sum(e_c, axis=-1, keepdims=True) + jnp.sum(e_n, axis=-1, keepdims=True)
    o = _dot_nt(e_c.astype(BF16), vt) + _dot(e_n.astype(BF16), vnew)
    return o, m, l


def _attn_sample_kernel(sink_ref, q_ref, kan_ref, van_ref, kbn_ref, vbn_ref, cka_ref, cva_ref, ckb_ref, cvb_ref,
                        oa_ref, ob_ref):
    half_f = _lane_block_masks(LANES, HEAD_DIM, F32)
    low_half = lax.broadcasted_iota(jnp.int32, (1, LANES), 1) < HEAD_DIM
    lane_b = lax.broadcasted_iota(jnp.int32, (1, KV_B_W), 1)

    def dist(nrows, ncols, offset):
        r = lax.broadcasted_iota(jnp.int32, (nrows, ncols), 0)
        i = lax.broadcasted_iota(jnp.int32, (nrows, ncols), 1)
        return r, offset + (r & (SAMPLE_T - 1)) - i

    nrow_a = N_HEADS_A * SAMPLE_T
    _, dac = dist(nrow_a, CACHE_A, CACHE_A)
    _, dan = dist(nrow_a, BLOCK, 0)
    mask_ac = (dac >= 0) & (dac <= BAND_STEPS)
    mask_an = (dan >= 0) & (dan <= BAND_STEPS)
    nrow_b = len(DILATIONS) * N_KV_B * SAMPLE_T
    rows_per_group = N_KV_B * SAMPLE_T

    def mask_b(ncols, offset):
        r, db = dist(nrow_b, ncols, offset)
        dil = jnp.where(r < rows_per_group, DILATIONS[0], jnp.where(r < 2 * rows_per_group, DILATIONS[1], DILATIONS[2]))
        return (db >= 0) & (db <= BAND_STEPS * dil) & ((db & (dil - 1)) == 0)

    mask_bc = mask_b(CACHE_B, CACHE_B)
    mask_bn = mask_b(BLOCK, 0)

    sink_col = jnp.concatenate(
        [jnp.full((SAMPLE_T, 1), sink_ref[HEAD_PERM_A[p]], F32) for p in range(N_HEADS_A)], axis=0)

    for bi in range(SAMPLE_BATCH_TILE):
        rows = slice(bi * SAMPLE_T, (bi + 1) * SAMPLE_T)
        qbd = jnp.concatenate(
            [q_ref[rows, (p // 2) * LANES:(p // 2 + 1) * LANES] * half_f[p % 2] for p in range(N_HEADS_A)],
            axis=0).astype(BF16)
        o, m, l = _cached_softmax_pv(qbd, cka_ref[bi].astype(BF16), cva_ref[bi].astype(BF16),
                                     _new_rows_block(kan_ref[rows, :]), _new_rows_block(van_ref[rows, :]),
                                     mask_ac, mask_an)
        o = o * (1.0 / (l + jnp.exp(sink_col - m)))
        for c in range(N_HEADS_A // 2):
            lo = o[(2 * c) * SAMPLE_T:(2 * c + 1) * SAMPLE_T]
            hi = o[(2 * c + 1) * SAMPLE_T:(2 * c + 2) * SAMPLE_T]
            oa_ref[rows, c * LANES:(c + 1) * LANES] = jnp.where(low_half, lo, hi)
        qrows = []
        for g in range(len(DILATIONS)):
            for kvh in range(N_KV_B):
                c = (Q_A_W // LANES) + 2 * g + kvh // 2
                part = q_ref[rows, c * LANES:(c + 1) * LANES] * half_f[kvh % 2]
                zero = jnp.zeros_like(part)
                qrows.append(jnp.concatenate([part, zero] if kvh // 2 == 0 else [zero, part], axis=1))
        qbd = jnp.concatenate(qrows, axis=0).astype(BF16)
        o, m, l = _cached_softmax_pv(qbd, ckb_ref[bi].astype(BF16), cvb_ref[bi].astype(BF16),
                                     _new_rows_block(kbn_ref[rows, :]), _new_rows_block(vbn_ref[rows, :]),
                                     mask_bc, mask_bn)
        out = None
        for kvh in reversed(range(N_KV_B)):
            sl = [slice((g * N_KV_B + kvh) * SAMPLE_T, (g * N_KV_B + kvh + 1) * SAMPLE_T)
                  for g in range(len(DILATIONS))]
            mj = jnp.maximum(jnp.maximum(m[sl[0]], m[sl[1]]), m[sl[2]])
            ws = [jnp.exp(m[x] - mj) for x in sl]
            den = ws[0] * l[sl[0]] + ws[1] * l[sl[1]] + ws[2] * l[sl[2]]
            num = ws[0] * o[sl[0]] + ws[1] * o[sl[1]] + ws[2] * o[sl[2]]
            okv = num * (1.0 / den)
            out = okv if out is None else jnp.where(lane_b < (kvh + 1) * HEAD_DIM, okv, out)
        ob_ref[rows, :] = out


def _attn_sample_call(sink, q32, ka32, va32, kb32, vb32, cka, cva, ckb, cvb):
    nb = cka.shape[0]
    bt = SAMPLE_BATCH_TILE
    t = q32.shape[0]
    tok = lambda w: pl.BlockSpec((bt * SAMPLE_T, w), lambda i: (i, 0))
    cache = lambda w, n: pl.BlockSpec((bt, w, n), lambda i: (i, 0, 0))
    return pl.pallas_call(
        _attn_sample_kernel,
        grid=(nb // bt,),
        in_specs=[pl.BlockSpec(memory_space=pltpu.SMEM), tok(Q_A_W + Q_B_W),
                  tok(KV_A_W), tok(KV_A_W), tok(KV_B_W), tok(KV_B_W),
                  cache(KV_A_W, CACHE_A), cache(KV_A_W, CACHE_A), cache(KV_B_W, CACHE_B), cache(KV_B_W, CACHE_B)],
        out_specs=[tok(Q_A_W), tok(KV_B_W)],
        out_shape=[jax.ShapeDtypeStruct((t, Q_A_W), F32), jax.ShapeDtypeStruct((t, KV_B_W), F32)],
        compiler_params=pltpu.CompilerParams(dimension_semantics=("parallel",), vmem_limit_bytes=VMEM_LIMIT),
        name="attn_sample",
    )(sink, q32, ka32, va32, kb32, vb32, cka, cva, ckb, cvb)


def _mix_out(h, oa, ob, gm_ref, wgate_ref, wba_ref, wbb_ref, wout_ref):
    n = _rms(h, gm_ref[...]).astype(BF16)
    gates = _dot(n, wgate_ref[...])
    ga = jax.nn.sigmoid(gates[:, :D_MODEL])
    gb = jax.nn.sigmoid(gates[:, D_MODEL:])
    mixed = ga * _dot(oa.astype(BF16), wba_ref[...]) + gb * _dot(ob.astype(BF16), wbb_ref[...])
    return h + _dot(mixed.astype(BF16), wout_ref[...])


def _combine_groups(os, ls):
    m = jnp.maximum(jnp.maximum(ls[0], ls[1]), ls[2])
    es = [jnp.exp(x - m) for x in ls]
    den = es[0] + es[1] + es[2]
    num = es[0] * os[0] + es[1] * os[1] + es[2] * os[2]
    return num * (1.0 / den)


def _merge_split(split_ref, scr):
    d, n = split_ref.shape[1], split_ref.shape[2]
    for r in range(d):
        for c in range(scr.shape[0]):
            scr[c, pl.ds(r, n, stride=d), :] = split_ref[0, r, :, c * LANES:(c + 1) * LANES]
    return jnp.concatenate([scr[c] for c in range(scr.shape[0])], axis=1)


def _softmax_rows(s):
    m = jnp.max(s, axis=-1, keepdims=True)
    e = jnp.exp(s - m)
    return e * (1.0 / jnp.sum(e, axis=-1, keepdims=True))


def _back_prompt_kernel(h_ref, oa_ref, o0_ref, o1_ref, o2_ref, l0_ref, l1_ref, l2_ref, mk_ref, mv_ref,
                        gm_ref, wgate_ref, wba_ref, wbb_ref, wout_ref, gq_ref, wq_ref, wo_ref,
                        g2_ref, wg_ref, wu_ref, wd_ref, gf_ref, y_ref, o1_s, o2_s, l1_s, l2_s):
    ob = _combine_groups(
        (o0_ref[0, 0], _merge_split(o1_ref, o1_s), _merge_split(o2_ref, o2_s)),
        (l0_ref[0, 0], _merge_split(l1_ref, l1_s), _merge_split(l2_ref, l2_s)))
    h2 = _mix_out(h_ref[...], oa_ref[...], ob, gm_ref, wgate_ref, wba_ref, wbb_ref, wout_ref)
    q = _dot(_rms(h2, gq_ref[...]).astype(BF16), wq_ref[...])
    heads = []
    for hd in range(MEM_HEADS):
        cols = slice(hd * MEM_HEAD_DIM, (hd + 1) * MEM_HEAD_DIM)
        p = _softmax_rows(_dot_nt(q[:, cols].astype(BF16), mk_ref[0, :, cols]) * MEM_SCALE)
        heads.append(_dot(p.astype(BF16), mv_ref[0, :, cols]))
    oc = jnp.concatenate(heads, axis=1).astype(BF16)
    h3 = h2 + _dot(oc, wo_ref[...])
    y_ref[...] = _rms(_ffn_half(h3, g2_ref, wg_ref, wu_ref, wd_ref), gf_ref[...])


def _back_prompt_call(h, oa, obs, lses, mk, mv, w, seq):
    t = h.shape[0]
    tm = TOKEN_TILE
    tiles_per_batch = seq // tm
    row = lambda i: (i, 0)
    mem = lambda i: (i // tiles_per_batch, 0, 0)
    tok = lambda width: pl.BlockSpec((tm, width), row)
    split = lambda d: pl.BlockSpec((1, d, tm // d, KV_B_W),
                                   lambda i: (i // tiles_per_batch, 0, i % tiles_per_batch, 0))
    groups = [split(d) for d in DILATIONS]
    in_specs = ([tok(D_MODEL), tok(Q_A_W)] + groups + groups
                + [pl.BlockSpec((1, N_MEM, MEM_W), mem), pl.BlockSpec((1, N_MEM, MEM_W), mem)]
                + [_const_spec((1, D_MODEL)), _const_spec((D_MODEL, 2 * D_MODEL)), _const_spec((Q_A_W, D_MODEL)),
                   _const_spec((KV_B_W, D_MODEL)), _const_spec((D_MODEL, D_MODEL)),
                   _const_spec((1, D_MODEL)), _const_spec((D_MODEL, MEM_W)), _const_spec((MEM_W, D_MODEL)),
                   _const_spec((1, D_MODEL)), _const_spec((D_MODEL, D_FF)), _const_spec((D_MODEL, D_FF)),
                   _const_spec((D_FF, D_MODEL)), _const_spec((1, D_MODEL))])
    return pl.pallas_call(
        _back_prompt_kernel,
        grid=(t // tm,),
        in_specs=in_specs,
        out_specs=tok(D_MODEL),
        out_shape=jax.ShapeDtypeStruct((t, D_MODEL), F32),
        scratch_shapes=[pltpu.VMEM((KV_B_W // LANES, tm, LANES), F32)] * 4,
        compiler_params=pltpu.CompilerParams(dimension_semantics=("parallel",), vmem_limit_bytes=VMEM_LIMIT),
        name="back_prompt",
    )(h, oa, *obs, *lses, mk, mv, w["mix_norm"], w["w_gate"], w["w_branch_a"], w["w_branch_b"], w["w_out"],
      w["mem_q_norm"], w["w_mem_q"], w["w_mem_o"], w["ffn2_norm"], w["ffn2_wg"], w["ffn2_wu"], w["ffn2_wd"],
      w["final_norm"])


def _back_sample_a_kernel(h_ref, oa_ref, ob_ref, gm_ref, wgate_ref, wba_ref, wbb_ref, wout_ref, gq_ref, wq_ref,
                          h2_ref, q_ref):
    h2 = _mix_out(h_ref[...], oa_ref[...], ob_ref[...], gm_ref, wgate_ref, wba_ref, wbb_ref, wout_ref)
    h2_ref[...] = h2
    q_ref[...] = _dot(_rms(h2, gq_ref[...]).astype(BF16), wq_ref[...])


def _back_sample_a_call(h, oa, ob, w):
    t = h.shape[0]
    tm = TOKEN_TILE
    tok = lambda width: pl.BlockSpec((tm, width), lambda i: (i, 0))
    return pl.pallas_call(
        _back_sample_a_kernel,
        grid=(t // tm,),
        in_specs=[tok(D_MODEL), tok(Q_A_W), tok(KV_B_W),
                  _const_spec((1, D_MODEL)), _const_spec((D_MODEL, 2 * D_MODEL)), _const_spec((Q_A_W, D_MODEL)),
                  _const_spec((KV_B_W, D_MODEL)), _const_spec((D_MODEL, D_MODEL)),
                  _const_spec((1, D_MODEL)), _const_spec((D_MODEL, MEM_W))],
        out_specs=[tok(D_MODEL), tok(MEM_W)],
        out_shape=[jax.ShapeDtypeStruct((t, D_MODEL), F32), jax.ShapeDtypeStruct((t, MEM_W), F32)],
        compiler_params=pltpu.CompilerParams(dimension_semantics=("parallel",), vmem_limit_bytes=VMEM_LIMIT),
        name="back_sample_mix",
    )(h, oa, ob, w["mix_norm"], w["w_gate"], w["w_branch_a"], w["w_branch_b"], w["w_out"],
      w["mem_q_norm"], w["w_mem_q"])


CROSS_BATCH_TILE = 8


def _cross_sample_kernel(q_ref, mk_ref, mv_ref, o_ref):
    nrow = MEM_HEADS * SAMPLE_T
    qhead = lax.broadcasted_iota(jnp.int32, (nrow, N_MEM * MEM_HEADS), 0) >> (SAMPLE_T.bit_length() - 1)
    khead = lax.broadcasted_iota(jnp.int32, (nrow, N_MEM * MEM_HEADS), 1) & (MEM_HEADS - 1)
    own_head = qhead == khead
    for bi in range(CROSS_BATCH_TILE):
        rows = slice(bi * SAMPLE_T, (bi + 1) * SAMPLE_T)
        qs = jnp.concatenate(
            [q_ref[rows, hd * MEM_HEAD_DIM:(hd + 1) * MEM_HEAD_DIM] for hd in range(MEM_HEADS)], axis=0)
        s = _dot_nt(qs.astype(BF16), mk_ref[bi].astype(BF16)) * MEM_SCALE
        p = _softmax_rows(jnp.where(own_head, s, NEG_INF))
        o = _dot(p.astype(BF16), mv_ref[bi].astype(BF16))
        for hd in range(MEM_HEADS):
            o_ref[rows, hd * MEM_HEAD_DIM:(hd + 1) * MEM_HEAD_DIM] = o[hd * SAMPLE_T:(hd + 1) * SAMPLE_T]


def _cross_sample_call(q, mk, mv):
    t = q.shape[0]
    bt = CROSS_BATCH_TILE
    tok = pl.BlockSpec((bt * SAMPLE_T, MEM_W), lambda i: (i, 0))
    mem = pl.BlockSpec((bt, N_MEM * MEM_HEADS, MEM_HEAD_DIM), lambda i: (i, 0, 0))
    return pl.pallas_call(
        _cross_sample_kernel,
        grid=(mk.shape[0] // bt,),
        in_specs=[tok, mem, mem],
        out_specs=tok,
        out_shape=jax.ShapeDtypeStruct((t, MEM_W), F32),
        compiler_params=pltpu.CompilerParams(dimension_semantics=("parallel",), vmem_limit_bytes=VMEM_LIMIT),
        name="cross_sample",
    )(q, mk, mv)


def _back_sample_b_kernel(h2_ref, oc_ref, wo_ref, g2_ref, wg_ref, wu_ref, wd_ref, gf_ref, y_ref):
    h3 = h2_ref[...] + _dot(oc_ref[...].astype(BF16), wo_ref[...])
    y_ref[...] = _rms(_ffn_half(h3, g2_ref, wg_ref, wu_ref, wd_ref), gf_ref[...])


def _back_sample_b_call(h2, oc, w):
    t = h2.shape[0]
    tm = TOKEN_TILE
    tok = lambda width: pl.BlockSpec((tm, width), lambda i: (i, 0))
    return pl.pallas_call(
        _back_sample_b_kernel,
        grid=(t // tm,),
        in_specs=[tok(D_MODEL), tok(MEM_W), _const_spec((MEM_W, D_MODEL)),
                  _const_spec((1, D_MODEL)), _const_spec((D_MODEL, D_FF)), _const_spec((D_MODEL, D_FF)),
                  _const_spec((D_FF, D_MODEL)), _const_spec((1, D_MODEL))],
        out_specs=tok(D_MODEL),
        out_shape=jax.ShapeDtypeStruct((t, D_MODEL), F32),
        compiler_params=pltpu.CompilerParams(dimension_semantics=("parallel",), vmem_limit_bytes=VMEM_LIMIT),
        name="back_sample_ffn",
    )(h2, oc, w["w_mem_o"], w["ffn2_norm"], w["ffn2_wg"], w["ffn2_wu"], w["ffn2_wd"], w["final_norm"])


def _mem_kv_kernel(mem_ref, g_ref, wkv_ref, k32_ref, v32_ref, k16_ref, v16_ref):
    u = _rms(mem_ref[...], g_ref[...]).astype(BF16)
    kv = _dot(u, wkv_ref[...])
    k, v = kv[:, :MEM_W], kv[:, MEM_W:]
    k32_ref[...] = k
    v32_ref[...] = v
    k16_ref[...] = k.astype(BF16)
    v16_ref[...] = v.astype(BF16)


def _mem_kv_call(mem, w):
    rows = mem.shape[0]
    full = lambda width: pl.BlockSpec((rows, width), lambda i: (0, 0))
    return pl.pallas_call(
        _mem_kv_kernel,
        grid=(1,),
        in_specs=[full(D_MODEL), _const_spec((1, D_MODEL)), _const_spec((D_MODEL, 2 * MEM_W))],
        out_specs=[full(MEM_W)] * 4,
        out_shape=[jax.ShapeDtypeStruct((rows, MEM_W), F32)] * 2 + [jax.ShapeDtypeStruct((rows, MEM_W), BF16)] * 2,
        compiler_params=pltpu.CompilerParams(dimension_semantics=("arbitrary",), vmem_limit_bytes=VMEM_LIMIT),
        name="mem_kv",
    )(mem, w["mem_kv_norm"], w["w_mem_kv"])


def _rope_tables(pos):
    inv_freq = jnp.power(jnp.float32(ROPE_THETA), -jnp.arange(ROT_HALF, dtype=jnp.float32) / ROT_HALF)
    ang = pos.astype(jnp.float32)[:, None] * inv_freq[None, :]
    cos, sin = jnp.cos(ang), jnp.sin(ang)
    n = pos.shape[0]
    pad = HEAD_DIM - 2 * ROT_HALF
    one_head = lambda a, b, fill: jnp.concatenate([a, b, jnp.full((n, pad), fill, jnp.float32)], axis=1)
    zeros = jnp.zeros((n, ROT_HALF), jnp.float32)
    two_heads = lambda x: jnp.concatenate([x, x], axis=1)
    return (two_heads(one_head(cos, cos, 1.0)), two_heads(one_head(-sin, zeros, 0.0)),
            two_heads(one_head(zeros, sin, 0.0)))


def _prep_weights(p):
    bf = lambda x: x.astype(BF16)
    vec = lambda x: x.reshape(1, -1)
    w_in = p["w_in"]
    perm_cols = np.concatenate([np.arange(h * HEAD_DIM, (h + 1) * HEAD_DIM) for h in HEAD_PERM_A])
    w_qkv = jnp.concatenate([w_in[:, perm_cols], w_in[:, Q_A_W:QKV_W]], axis=1)
    return {
        "ffn1_norm": vec(p["ffn1_norm"]), "ffn1_wg": bf(p["ffn1_w_gate"]), "ffn1_wu": bf(p["ffn1_w_up"]),
        "ffn1_wd": bf(p["ffn1_w_down"]),
        "mix_norm": vec(p["mix_norm"]), "w_qkv": bf(w_qkv), "w_gate": bf(w_in[:, QKV_W:]),
        "w_branch_a": bf(p["w_branch_a"][perm_cols, :]), "w_branch_b": bf(p["w_branch_b"]), "w_out": bf(p["w_out"]),
        "mem_q_norm": vec(p["mem_q_norm"]), "mem_kv_norm": vec(p["mem_kv_norm"]),
        "w_mem_q": bf(p["w_mem_q"]), "w_mem_kv": bf(jnp.concatenate([p["w_mem_k"], p["w_mem_v"]], axis=1)),
        "w_mem_o": bf(p["w_mem_o"]),
        "ffn2_norm": vec(p["ffn2_norm"]), "ffn2_wg": bf(p["ffn2_w_gate"]), "ffn2_wu": bf(p["ffn2_w_up"]),
        "ffn2_wd": bf(p["ffn2_w_down"]), "final_norm": vec(p["final_norm"]),
    }


def kernel(x_prompt, x_sample, cache_swa_k, cache_swa_v, cache_dil_k, cache_dil_v, cache_mem_k, cache_mem_v, mem_prompt, ffn1_norm, ffn1_w_gate, ffn1_w_up, ffn1_w_down, mix_norm, w_in, attn_sink, w_branch_a, w_branch_b, w_out, mem_q_norm, mem_kv_norm, w_mem_q, w_mem_k, w_mem_v, w_mem_o, ffn2_norm, ffn2_w_gate, ffn2_w_up, ffn2_w_down, final_norm):
    depth = ffn1_norm.shape[0]
    assert depth == 1
    batch, seq, d = x_prompt.shape
    nb, t_new, _ = x_sample.shape
    assert d == D_MODEL and t_new == SAMPLE_T and seq % (ATTN_CHUNK * DILATIONS[-1]) == 0
    assert cache_swa_k.shape[2] == CACHE_A and cache_dil_k.shape[2] == CACHE_B
    layer = lambda x: x[0]
    w = _prep_weights(dict(
        ffn1_norm=layer(ffn1_norm), ffn1_w_gate=layer(ffn1_w_gate), ffn1_w_up=layer(ffn1_w_up),
        ffn1_w_down=layer(ffn1_w_down), mix_norm=layer(mix_norm), w_in=layer(w_in),
        w_branch_a=layer(w_branch_a), w_branch_b=layer(w_branch_b), w_out=layer(w_out),
        mem_q_norm=layer(mem_q_norm), mem_kv_norm=layer(mem_kv_norm), w_mem_q=layer(w_mem_q),
        w_mem_k=layer(w_mem_k), w_mem_v=layer(w_mem_v), w_mem_o=layer(w_mem_o), ffn2_norm=layer(ffn2_norm),
        ffn2_w_gate=layer(ffn2_w_gate), ffn2_w_up=layer(ffn2_w_up), ffn2_w_down=layer(ffn2_w_down),
        final_norm=final_norm))
    sink = layer(attn_sink)

    tp = batch * seq
    mk32, mv32, mk16, mv16 = _mem_kv_call(mem_prompt.reshape(batch * N_MEM, d), w)
    tables_p = _rope_tables(jnp.arange(seq, dtype=jnp.int32))
    (h_p, qa, ka, va, qb0, kb, vb, q4, k4, v4, q16, k16, v16, ka32, va32, kb32, vb32) = _front_call(
        x_prompt.reshape(tp, d), tables_p, w, sample=False, batch=batch)
    oa = _attn_a_call(sink, qa, ka, va, batch, seq)
    unsplit = lambda x: x.reshape(batch, 1, seq, KV_B_W)
    groups = ((unsplit(qb0), unsplit(kb), unsplit(vb)), (q4, k4, v4), (q16, k16, v16))
    obs, lses = zip(*[_attn_b_call(*qkv, d) for qkv, d in zip(groups, DILATIONS)])
    y_p = _back_prompt_call(h_p, oa, obs, lses, mk16.reshape(batch, N_MEM, MEM_W),
                            mv16.reshape(batch, N_MEM, MEM_W), w, seq)

    ts = nb * t_new
    pos_s = PAST_LEN + (jnp.arange(TOKEN_TILE, dtype=jnp.int32) % t_new)
    tables_s = _rope_tables(pos_s)
    (h_s, q32, ka32_s, va32_s, kb32_s, vb32_s) = _front_call(x_sample.reshape(ts, d), tables_s, w, sample=True)
    by_pos = lambda c: jnp.transpose(layer(c), (0, 2, 3, 1)).reshape(nb, c.shape[3] * c.shape[4], c.shape[2])
    by_slot_head = lambda c: layer(c).reshape(nb, N_MEM * MEM_HEADS, MEM_HEAD_DIM)
    oa_s, ob_s = _attn_sample_call(
        sink, q32, ka32_s, va32_s, kb32_s, vb32_s,
        by_pos(cache_swa_k), by_pos(cache_swa_v), by_pos(cache_dil_k), by_pos(cache_dil_v))
    h2_s, qc_s = _back_sample_a_call(h_s, oa_s, ob_s, w)
    oc_s = _cross_sample_call(qc_s, by_slot_head(cache_mem_k), by_slot_head(cache_mem_v))
    y_s = _back_sample_b_call(h2_s, oc_s, w)

    keep_a, keep_b = min(CACHE_A, seq), min(CACHE_B, seq)
    tail = lambda x, keep, heads: x.reshape(batch, seq, heads * HEAD_DIM)[:, seq - keep:].reshape(
        1, batch, keep, heads, HEAD_DIM)
    new = lambda x, heads: x.reshape(1, nb, t_new, heads, HEAD_DIM)
    memo = lambda x: x.reshape(1, batch, N_MEM, MEM_HEADS, MEM_HEAD_DIM)
    return (y_p.reshape(batch, seq, d), y_s.reshape(nb, t_new, d),
            tail(ka32, keep_a, N_KV_A), tail(va32, keep_a, N_KV_A),
            tail(kb32, keep_b, N_KV_B), tail(vb32, keep_b, N_KV_B),
            memo(mk32), memo(mv32),
            new(ka32_s, N_KV_A), new(va32_s, N_KV_A), new(kb32_s, N_KV_B), new(vb32_s, N_KV_B))
```

```python
import functools

import jax
import jax.numpy as jnp
import numpy as np
from jax import lax
from jax.experimental import pallas as pl
from jax.experimental.pallas import tpu as pltpu

F32 = jnp.float32
BF16 = jnp.bfloat16

D_MODEL = 1024
D_FF = 2816
HEAD_DIM = 64
ROT_HALF = 8
ROPE_THETA = 500000.0
ATTN_SCALE = HEAD_DIM ** -0.5
RMS_EPS = 1e-6
PAST_LEN = 16384

N_HEADS_A = 8
N_KV_A = 2
N_KV_B = 4
DILATIONS = (1, 4, 16)
BAND_STEPS = 128
Q_A_W = N_HEADS_A * HEAD_DIM
KV_A_W = N_KV_A * HEAD_DIM
Q_B_W = len(DILATIONS) * N_KV_B * HEAD_DIM
KV_B_W = N_KV_B * HEAD_DIM
QKV_W = Q_A_W + 2 * KV_A_W + Q_B_W + 2 * KV_B_W
MEM_HEADS = 4
MEM_HEAD_DIM = 128
MEM_W = MEM_HEADS * MEM_HEAD_DIM
MEM_SCALE = MEM_HEAD_DIM ** -0.5
N_MEM = 256

HEAD_PERM_A = (0, 4, 1, 5, 2, 6, 3, 7)

LANES = 128
BLOCK = 128
TOKEN_TILE = 512
ATTN_CHUNK = 512
FF_CHUNKS = ((0, 1536), (1536, 2816))
VMEM_LIMIT = 56 * 1024 * 1024
NEG_INF = float("-inf")


def _rms(x, g):
    ms = jnp.mean(x * x, axis=-1, keepdims=True)
    return x * lax.rsqrt(ms + RMS_EPS) * g


def _dot(a, b):
    return jnp.dot(a, b, preferred_element_type=F32)


def _dot_nt(a, b):
    return lax.dot_general(a, b, (((1,), (1,)), ((), ())), preferred_element_type=F32)


def _ffn_half(x, g_ref, wg_ref, wu_ref, wd_ref):
    u = _rms(x, g_ref[...]).astype(BF16)
    acc = None
    for lo, hi in FF_CHUNKS:
        gate = _dot(u, wg_ref[:, lo:hi])
        up = _dot(u, wu_ref[:, lo:hi])
        act = (gate * jax.nn.sigmoid(gate) * up).astype(BF16)
        part = _dot(act, wd_ref[lo:hi, :])
        acc = part if acc is None else acc + part
    return x + 0.5 * acc


def _swap_halves(x):
    return pltpu.roll(x, HEAD_DIM, 1)


def _low_half():
    return lax.broadcasted_iota(jnp.int32, (1, LANES), 1) < HEAD_DIM


def _pair_kv_groups(chunks):
    low = _low_half()
    c0, c1, c2, c3 = chunks
    return [jnp.where(low, c0, _swap_halves(c2)), jnp.where(low, _swap_halves(c0), c2),
            jnp.where(low, c1, _swap_halves(c3)), jnp.where(low, _swap_halves(c1), c3)]


def _unpair_kv_groups(per_head):
    low = _low_half()
    out = []
    for c in range(N_HEADS_A // 2):
        if c < N_HEADS_A // 4:
            out.append(jnp.where(low, per_head[4 * c], _swap_halves(per_head[4 * c + 2])))
        else:
            out.append(jnp.where(low, _swap_halves(per_head[4 * c - 7]), per_head[4 * c - 5]))
    return out


def _front_kernel(x_ref, cos_ref, sa_ref, sb_ref, g1_ref, wg_ref, wu_ref, wd_ref, gm_ref, wqkv_ref,
                  h_ref, *rest, sample):
    x = x_ref[...]
    h = _ffn_half(x, g1_ref, wg_ref, wu_ref, wd_ref)
    h_ref[...] = h
    n = _rms(h, gm_ref[...]).astype(BF16)
    z = _dot(n, wqkv_ref[...])
    cos, sa, sb = cos_ref[...], sa_ref[...], sb_ref[...]

    def chunk(c):
        return z[:, c * LANES:(c + 1) * LANES]

    def rope(c):
        zc = chunk(c)
        return zc * cos + pltpu.roll(zc, LANES - ROT_HALF, 1) * sa + pltpu.roll(zc, ROT_HALF, 1) * sb

    nqa, nka = Q_A_W // LANES, KV_A_W // LANES
    nqb, nkb = Q_B_W // LANES, KV_B_W // LANES
    c0 = 0
    qa = _pair_kv_groups([rope(c0 + c) * ATTN_SCALE for c in range(nqa)])
    c0 += nqa
    ka = [rope(c0 + c) for c in range(nka)]
    c0 += nka
    va = [chunk(c0 + c) for c in range(nka)]
    c0 += nka
    qb = [rope(c0 + c) * ATTN_SCALE for c in range(nqb)]
    c0 += nqb
    kb = [rope(c0 + c) for c in range(nkb)]
    c0 += nkb
    vb = [chunk(c0 + c) for c in range(nkb)]

    def store(ref, parts, dtype):
        for c, p in enumerate(parts):
            ref[:, c * LANES:(c + 1) * LANES] = p.astype(dtype)

    def store_chunks(scr, parts):
        for c, p in enumerate(parts):
            scr[c] = p

    def store_split(ref, scr, d):
        n = scr.shape[1] // d
        for r in range(d):
            for c in range(scr.shape[0]):
                ref[0, r, :, c * LANES:(c + 1) * LANES] = scr[c, pl.ds(r, n, stride=d), :].astype(BF16)

    if sample:
        q32_ref, ka32_ref, va32_ref, kb32_ref, vb32_ref = rest
        store(q32_ref, qa + qb, F32)
    else:
        (qa_ref, ka_ref, va_ref, qb0_ref, kb_ref, vb_ref, q4_ref, k4_ref, v4_ref, q16_ref, k16_ref, v16_ref,
         ka32_ref, va32_ref, kb32_ref, vb32_ref, q1_s, q2_s, kb_s, vb_s) = rest
        gw = KV_B_W // LANES
        store(qa_ref, qa, BF16)
        store(ka_ref, ka, BF16)
        store(va_ref, va, BF16)
        store(qb0_ref, qb[0:gw], BF16)
        store(kb_ref, kb, BF16)
        store(vb_ref, vb, BF16)
        store_chunks(q1_s, qb[gw:2 * gw])
        store_chunks(q2_s, qb[2 * gw:3 * gw])
        store_chunks(kb_s, kb)
        store_chunks(vb_s, vb)
        store_split(q4_ref, q1_s, DILATIONS[1])
        store_split(k4_ref, kb_s, DILATIONS[1])
        store_split(v4_ref, vb_s, DILATIONS[1])
        store_split(q16_ref, q2_s, DILATIONS[2])
        store_split(k16_ref, kb_s, DILATIONS[2])
        store_split(v16_ref, vb_s, DILATIONS[2])
    store(ka32_ref, ka, F32)
    store(va32_ref, va, F32)
    store(kb32_ref, kb, F32)
    store(vb32_ref, vb, F32)


def _const_spec(shape, index=None):
    index = (0,) * len(shape) if index is None else index
    return pl.BlockSpec(shape, lambda *_: index, pipeline_mode=pl.Buffered(1))


def _front_call(x, tables, w, *, sample, batch=1):
    t = x.shape[0]
    tm = TOKEN_TILE
    nsteps = t // tm
    tpb = nsteps // batch
    seq = t // batch
    cos_t, sa_t, sb_t = tables
    tab_blocks = cos_t.shape[0] // tm
    row = lambda i: (i, 0)
    tab = lambda i: (i % tab_blocks, 0)

    def tok(width, dtype):
        return jax.ShapeDtypeStruct((t, width), dtype), pl.BlockSpec((tm, width), row)

    def split(d):
        return (jax.ShapeDtypeStruct((batch, d, seq // d, KV_B_W), BF16),
                pl.BlockSpec((1, d, tm // d, KV_B_W), lambda i: (i // tpb, 0, i % tpb, 0)))

    outs = [tok(D_MODEL, F32)]
    scratch = []
    if sample:
        outs += [tok(Q_A_W + Q_B_W, F32)]
    else:
        outs += [tok(Q_A_W, BF16), tok(KV_A_W, BF16), tok(KV_A_W, BF16),
                 tok(KV_B_W, BF16), tok(KV_B_W, BF16), tok(KV_B_W, BF16)]
        outs += [split(DILATIONS[1])] * 3 + [split(DILATIONS[2])] * 3
        scratch = [pltpu.VMEM((KV_B_W // LANES, tm, LANES), F32)] * 4
    outs += [tok(KV_A_W, F32), tok(KV_A_W, F32), tok(KV_B_W, F32), tok(KV_B_W, F32)]
    out_shape, out_specs = zip(*outs)
    in_specs = [pl.BlockSpec((tm, D_MODEL), row),
                pl.BlockSpec((tm, LANES), tab), pl.BlockSpec((tm, LANES), tab), pl.BlockSpec((tm, LANES), tab),
                _const_spec((1, D_MODEL)), _const_spec((D_MODEL, D_FF)), _const_spec((D_MODEL, D_FF)),
                _const_spec((D_FF, D_MODEL)), _const_spec((1, D_MODEL)), _const_spec((D_MODEL, QKV_W), (0, 0))]
    return pl.pallas_call(
        functools.partial(_front_kernel, sample=sample),
        grid=(nsteps,),
        in_specs=in_specs,
        out_specs=list(out_specs),
        out_shape=list(out_shape),
        scratch_shapes=scratch,
        compiler_params=pltpu.CompilerParams(dimension_semantics=("parallel",), vmem_limit_bytes=VMEM_LIMIT),
        name="front_sample" if sample else "front_prompt",
    )(x, cos_t, sa_t, sb_t, w["ffn1_norm"], w["ffn1_wg"], w["ffn1_wu"], w["ffn1_wd"], w["mix_norm"], w["w_in"])


def _lane_block_masks(width, block, dtype):
    lane = lax.broadcasted_iota(jnp.int32, (1, width), 1)
    return [((lane >= i * block) & (lane < (i + 1) * block)).astype(dtype) for i in range(width // block)]


def _band_mask():
    qi = lax.broadcasted_iota(jnp.int32, (BLOCK, 2 * BLOCK), 0)
    kj = lax.broadcasted_iota(jnp.int32, (BLOCK, 2 * BLOCK), 1)
    dist = qi + BLOCK - kj
    return (dist >= 0) & (dist <= BAND_STEPS), kj >= BLOCK


def _band_softmax_pv(qbd, kcat, vcat, mask, nheads):
    s = _dot_nt(qbd, kcat)
    ps, ms, ls = [], [], []
    for p in range(nheads):
        sp = jnp.where(mask, s[p * BLOCK:(p + 1) * BLOCK], NEG_INF)
        m = jnp.max(sp, axis=-1, keepdims=True)
        e = jnp.exp(sp - m)
        ls.append(jnp.sum(e, axis=-1, keepdims=True))
        ms.append(m)
        ps.append(e.astype(BF16))
    o = _dot(jnp.concatenate(ps, axis=0), vcat)
    return o, ms, ls


def _kv_window(cur_ref, prev_ref, qb):
    if qb == 0:
        return jnp.concatenate([prev_ref[...], cur_ref[0:BLOCK]], axis=0)
    return cur_ref[(qb - 1) * BLOCK:(qb + 1) * BLOCK]


def _attn_a_kernel(sink_ref, q_ref, kc_ref, kp_ref, vc_ref, vp_ref, o_ref):
    first_chunk = pl.program_id(1) == 0
    band, in_cur = _band_mask()
    half_bf = _lane_block_masks(LANES, HEAD_DIM, BF16)
    for qb in range(ATTN_CHUNK // BLOCK):
        mask = band & (in_cur | jnp.logical_not(first_chunk)) if qb == 0 else band
        kcat = _kv_window(kc_ref, kp_ref, qb)
        vcat = _kv_window(vc_ref, vp_ref, qb)
        rows = slice(qb * BLOCK, (qb + 1) * BLOCK)
        qbd = jnp.concatenate(
            [q_ref[rows, (p // 2) * LANES:(p // 2 + 1) * LANES] * half_bf[p % 2] for p in range(N_HEADS_A)], axis=0)
        o, ms, ls = _band_softmax_pv(qbd, kcat, vcat, mask, N_HEADS_A)
        normed = []
        for p in range(N_HEADS_A):
            den = ls[p] + jnp.exp(sink_ref[HEAD_PERM_A[p]] - ms[p])
            normed.append(o[p * BLOCK:(p + 1) * BLOCK] * (1.0 / den))
        for c, pair in enumerate(_unpair_kv_groups(normed)):
            o_ref[rows, c * LANES:(c + 1) * LANES] = pair.astype(BF16)


def _attn_a_call(sink, qa, ka, va, batch, seq):
    t = qa.shape[0]
    cpb = seq // ATTN_CHUNK
    bpc = ATTN_CHUNK // BLOCK
    cur = lambda b, c: (b * cpb + c, 0)
    prev = lambda b, c: (b * cpb * bpc + jnp.maximum(c * bpc - 1, 0), 0)
    return pl.pallas_call(
        _attn_a_kernel,
        grid=(batch, cpb),
        in_specs=[pl.BlockSpec(memory_space=pltpu.SMEM),
                  pl.BlockSpec((ATTN_CHUNK, Q_A_W), cur),
                  pl.BlockSpec((ATTN_CHUNK, KV_A_W), cur), pl.BlockSpec((BLOCK, KV_A_W), prev),
                  pl.BlockSpec((ATTN_CHUNK, KV_A_W), cur), pl.BlockSpec((BLOCK, KV_A_W), prev)],
        out_specs=pl.BlockSpec((ATTN_CHUNK, Q_A_W), cur),
        out_shape=jax.ShapeDtypeStruct((t, Q_A_W), BF16),
        compiler_params=pltpu.CompilerParams(dimension_semantics=("parallel", "parallel"),
                                             vmem_limit_bytes=VMEM_LIMIT),
        name="attn_a_prompt",
    )(sink, qa, ka, ka, va, va)


def _attn_b_kernel(q_ref, kc_ref, kp_ref, vc_ref, vp_ref, o_ref, lse_ref):
    first_chunk = pl.program_id(2) == 0
    band, in_cur = _band_mask()
    head_bf = _lane_block_masks(KV_B_W, HEAD_DIM, BF16)
    lane = lax.broadcasted_iota(jnp.int32, (1, KV_B_W), 1)
    for qb in range(ATTN_CHUNK // BLOCK):
        mask = band & (in_cur | jnp.logical_not(first_chunk)) if qb == 0 else band
        kcat = _kv_window(kc_ref, kp_ref, qb)
        vcat = _kv_window(vc_ref, vp_ref, qb)
        rows = slice(qb * BLOCK, (qb + 1) * BLOCK)
        q = q_ref[rows, :]
        qbd = jnp.concatenate([q * head_bf[p] for p in range(N_KV_B)], axis=0)
        o, ms, ls = _band_softmax_pv(qbd, kcat, vcat, mask, N_KV_B)
        out = lse = None
        for p in reversed(range(N_KV_B)):
            op = o[p * BLOCK:(p + 1) * BLOCK] * (1.0 / ls[p])
            lp = jnp.broadcast_to(ms[p] + jnp.log(ls[p]), (BLOCK, KV_B_W))
            if out is None:
                out, lse = op, lp
            else:
                sel = lane < (p + 1) * HEAD_DIM
                out, lse = jnp.where(sel, op, out), jnp.where(sel, lp, lse)
        o_ref[rows, :] = out
        lse_ref[rows, :] = lse


def _attn_b_call(q, k, v, d):
    batch, _, rows, _ = q.shape
    cpb = rows // ATTN_CHUNK
    bpc = ATTN_CHUNK // BLOCK
    cur = pl.BlockSpec((None, None, ATTN_CHUNK, KV_B_W), lambda b, r, c: (b, r, c, 0))
    prev = pl.BlockSpec((None, None, BLOCK, KV_B_W), lambda b, r, c: (b, r, jnp.maximum(c * bpc - 1, 0), 0))
    return pl.pallas_call(
        _attn_b_kernel,
        grid=(batch, d, cpb),
        in_specs=[cur, cur, prev, cur, prev],
        out_specs=[cur, cur],
        out_shape=[jax.ShapeDtypeStruct((batch, d, rows, KV_B_W), F32)] * 2,
        compiler_params=pltpu.CompilerParams(dimension_semantics=("parallel", "parallel", "parallel"),
                                             vmem_limit_bytes=VMEM_LIMIT),
        name=f"attn_b_prompt_d{d}",
    )(q, k, k, v, v)


SAMPLE_T = 8
CACHE_A = 128
CACHE_B = 2048
SAMPLE_BATCH_TILE = 2


def _new_rows_block(new):
    return jnp.concatenate([new, jnp.zeros((BLOCK - SAMPLE_T, new.shape[1]), F32)], axis=0).astype(BF16)


def _cached_softmax_pv(qbd, kt, vt, knew, vnew, mask_c, mask_n):
    s_c = jnp.where(mask_c, _dot(qbd, kt), NEG_INF)
    s_n = jnp.where(mask_n, _dot_nt(qbd, knew), NEG_INF)
    m = jnp.maximum(jnp.max(s_c, axis=-1, keepdims=True), jnp.max(s_n, axis=-1, keepdims=True))
    e_c = jnp.exp(s_c - m)
    e_n = jnp.exp(s_n - m)
    l = jnp.sum(e_c, axis=-1, keepdims=True) + jnp.sum(e_n, axis=-1, keepdims=True)
    o = _dot_nt(e_c.astype(BF16), vt) + _dot(e_n.astype(BF16), vnew)
    return o, m, l


def _attn_sample_kernel(sink_ref, q_ref, kan_ref, van_ref, kbn_ref, vbn_ref, cka_ref, cva_ref, ckb_ref, cvb_ref,
                        oa_ref, ob_ref):
    half_f = _lane_block_masks(LANES, HEAD_DIM, F32)
    lane_b = lax.broadcasted_iota(jnp.int32, (1, KV_B_W), 1)

    def dist(nrows, ncols, offset):
        r = lax.broadcasted_iota(jnp.int32, (nrows, ncols), 0)
        i = lax.broadcasted_iota(jnp.int32, (nrows, ncols), 1)
        return r, offset + (r & (SAMPLE_T - 1)) - i

    nrow_a = N_HEADS_A * SAMPLE_T
    _, dac = dist(nrow_a, CACHE_A, CACHE_A)
    _, dan = dist(nrow_a, BLOCK, 0)
    mask_ac = (dac >= 0) & (dac <= BAND_STEPS)
    mask_an = (dan >= 0) & (dan <= BAND_STEPS)
    nrow_b = len(DILATIONS) * N_KV_B * SAMPLE_T
    rows_per_group = N_KV_B * SAMPLE_T

    def mask_b(ncols, offset):
        r, db = dist(nrow_b, ncols, offset)
        dil = jnp.where(r < rows_per_group, DILATIONS[0], jnp.where(r < 2 * rows_per_group, DILATIONS[1], DILATIONS[2]))
        return (db >= 0) & (db <= BAND_STEPS * dil) & ((db & (dil - 1)) == 0)

    mask_bc = mask_b(CACHE_B, CACHE_B)
    mask_bn = mask_b(BLOCK, 0)

    sink_col = jnp.concatenate(
        [jnp.full((SAMPLE_T, 1), sink_ref[HEAD_PERM_A[p]], F32) for p in range(N_HEADS_A)], axis=0)

    for bi in range(SAMPLE_BATCH_TILE):
        rows = slice(bi * SAMPLE_T, (bi + 1) * SAMPLE_T)
        qbd = jnp.concatenate(
            [q_ref[rows, (p // 2) * LANES:(p // 2 + 1) * LANES] * half_f[p % 2] for p in range(N_HEADS_A)],
            axis=0).astype(BF16)
        o, m, l = _cached_softmax_pv(qbd, cka_ref[bi].astype(BF16), cva_ref[bi].astype(BF16),
                                     _new_rows_block(kan_ref[rows, :]), _new_rows_block(van_ref[rows, :]),
                                     mask_ac, mask_an)
        o = o * (1.0 / (l + jnp.exp(sink_col - m)))
        per_head = [o[p * SAMPLE_T:(p + 1) * SAMPLE_T] for p in range(N_HEADS_A)]
        for c, pair in enumerate(_unpair_kv_groups(per_head)):
            oa_ref[rows, c * LANES:(c + 1) * LANES] = pair
        qrows = []
        for g in range(len(DILATIONS)):
            for kvh in range(N_KV_B):
                c = (Q_A_W // LANES) + 2 * g + kvh // 2
                part = q_ref[rows, c * LANES:(c + 1) * LANES] * half_f[kvh % 2]
                zero = jnp.zeros_like(part)
                qrows.append(jnp.concatenate([part, zero] if kvh // 2 == 0 else [zero, part], axis=1))
        qbd = jnp.concatenate(qrows, axis=0).astype(BF16)
        o, m, l = _cached_softmax_pv(qbd, ckb_ref[bi].astype(BF16), cvb_ref[bi].astype(BF16),
                                     _new_rows_block(kbn_ref[rows, :]), _new_rows_block(vbn_ref[rows, :]),
                                     mask_bc, mask_bn)
        out = None
        for kvh in reversed(range(N_KV_B)):
            sl = [slice((g * N_KV_B + kvh) * SAMPLE_T, (g * N_KV_B + kvh + 1) * SAMPLE_T)
                  for g in range(len(DILATIONS))]
            mj = jnp.maximum(jnp.maximum(m[sl[0]], m[sl[1]]), m[sl[2]])
            ws = [jnp.exp(m[x] - mj) for x in sl]
            den = ws[0] * l[sl[0]] + ws[1] * l[sl[1]] + ws[2] * l[sl[2]]
            num = ws[0] * o[sl[0]] + ws[1] * o[sl[1]] + ws[2] * o[sl[2]]
            okv = num * (1.0 / den)
            out = okv if out is None else jnp.where(lane_b < (kvh + 1) * HEAD_DIM, okv, out)
        ob_ref[rows, :] = out


def _attn_sample_call(sink, q32, ka32, va32, kb32, vb32, cka, cva, ckb, cvb):
    nb = cka.shape[0]
    bt = SAMPLE_BATCH_TILE
    t = q32.shape[0]
    tok = lambda w: pl.BlockSpec((bt * SAMPLE_T, w), lambda i: (i, 0))
    cache = lambda w, n: pl.BlockSpec((bt, w, n), lambda i: (i, 0, 0))
    return pl.pallas_call(
        _attn_sample_kernel,
        grid=(nb // bt,),
        in_specs=[pl.BlockSpec(memory_space=pltpu.SMEM), tok(Q_A_W + Q_B_W),
                  tok(KV_A_W), tok(KV_A_W), tok(KV_B_W), tok(KV_B_W),
                  cache(KV_A_W, CACHE_A), cache(KV_A_W, CACHE_A), cache(KV_B_W, CACHE_B), cache(KV_B_W, CACHE_B)],
        out_specs=[tok(Q_A_W), tok(KV_B_W)],
        out_shape=[jax.ShapeDtypeStruct((t, Q_A_W), F32), jax.ShapeDtypeStruct((t, KV_B_W), F32)],
        compiler_params=pltpu.CompilerParams(dimension_semantics=("parallel",), vmem_limit_bytes=VMEM_LIMIT),
        name="attn_sample",
    )(sink, q32, ka32, va32, kb32, vb32, cka, cva, ckb, cvb)


def _mix_out(h, oa, ob, gm_ref, wgate_ref, wba_ref, wbb_ref, wout_ref):
    n = _rms(h, gm_ref[...]).astype(BF16)
    gates = _dot(n, wgate_ref[...])
    ga = jax.nn.sigmoid(gates[:, :D_MODEL])
    gb = jax.nn.sigmoid(gates[:, D_MODEL:])
    mixed = ga * _dot(oa.astype(BF16), wba_ref[...]) + gb * _dot(ob.astype(BF16), wbb_ref[...])
    return h + _dot(mixed.astype(BF16), wout_ref[...])


def _combine_groups(os, ls):
    m = jnp.maximum(jnp.maximum(ls[0], ls[1]), ls[2])
    es = [jnp.exp(x - m) for x in ls]
    den = es[0] + es[1] + es[2]
    num = es[0] * os[0] + es[1] * os[1] + es[2] * os[2]
    return num * (1.0 / den)


def _merge_split(split_ref, scr):
    d, n = split_ref.shape[1], split_ref.shape[2]
    for r in range(d):
        for c in range(scr.shape[0]):
            scr[c, pl.ds(r, n, stride=d), :] = split_ref[0, r, :, c * LANES:(c + 1) * LANES]
    return jnp.concatenate([scr[c] for c in range(scr.shape[0])], axis=1)


def _softmax_rows(s):
    m = jnp.max(s, axis=-1, keepdims=True)
    e = jnp.exp(s - m)
    return e * (1.0 / jnp.sum(e, axis=-1, keepdims=True))


def _back_prompt_kernel(h_ref, oa_ref, o0_ref, o1_ref, o2_ref, l0_ref, l1_ref, l2_ref, mk_ref, mv_ref,
                        gm_ref, wgate_ref, wba_ref, wbb_ref, wout_ref, gq_ref, wq_ref, wo_ref,
                        g2_ref, wg_ref, wu_ref, wd_ref, gf_ref, y_ref, o1_s, o2_s, l1_s, l2_s):
    ob = _combine_groups(
        (o0_ref[0, 0], _merge_split(o1_ref, o1_s), _merge_split(o2_ref, o2_s)),
        (l0_ref[0, 0], _merge_split(l1_ref, l1_s), _merge_split(l2_ref, l2_s)))
    h2 = _mix_out(h_ref[...], oa_ref[...], ob, gm_ref, wgate_ref, wba_ref, wbb_ref, wout_ref)
    q = _dot(_rms(h2, gq_ref[...]).astype(BF16), wq_ref[...])
    heads = []
    for hd in range(MEM_HEADS):
        cols = slice(hd * MEM_HEAD_DIM, (hd + 1) * MEM_HEAD_DIM)
        p = _softmax_rows(_dot_nt(q[:, cols].astype(BF16), mk_ref[0, :, cols]) * MEM_SCALE)
        heads.append(_dot(p.astype(BF16), mv_ref[0, :, cols]))
    oc = jnp.concatenate(heads, axis=1).astype(BF16)
    h3 = h2 + _dot(oc, wo_ref[...])
    y_ref[...] = _rms(_ffn_half(h3, g2_ref, wg_ref, wu_ref, wd_ref), gf_ref[...])


def _back_prompt_call(h, oa, obs, lses, mk, mv, w, seq):
    t = h.shape[0]
    tm = TOKEN_TILE
    tiles_per_batch = seq // tm
    row = lambda i: (i, 0)
    mem = lambda i: (i // tiles_per_batch, 0, 0)
    tok = lambda width: pl.BlockSpec((tm, width), row)
    split = lambda d: pl.BlockSpec((1, d, tm // d, KV_B_W),
                                   lambda i: (i // tiles_per_batch, 0, i % tiles_per_batch, 0))
    groups = [split(d) for d in DILATIONS]
    in_specs = ([tok(D_MODEL), tok(Q_A_W)] + groups + groups
                + [pl.BlockSpec((1, N_MEM, MEM_W), mem), pl.BlockSpec((1, N_MEM, MEM_W), mem)]
                + [_const_spec((1, D_MODEL)), _const_spec((D_MODEL, 2 * D_MODEL), (0, 1)), _const_spec((Q_A_W, D_MODEL)),
                   _const_spec((KV_B_W, D_MODEL)), _const_spec((D_MODEL, D_MODEL)),
                   _const_spec((1, D_MODEL)), _const_spec((D_MODEL, MEM_W)), _const_spec((MEM_W, D_MODEL)),
                   _const_spec((1, D_MODEL)), _const_spec((D_MODEL, D_FF)), _const_spec((D_MODEL, D_FF)),
                   _const_spec((D_FF, D_MODEL)), _const_spec((1, D_MODEL))])
    return pl.pallas_call(
        _back_prompt_kernel,
        grid=(t // tm,),
        in_specs=in_specs,
        out_specs=tok(D_MODEL),
        out_shape=jax.ShapeDtypeStruct((t, D_MODEL), F32),
        scratch_shapes=[pltpu.VMEM((KV_B_W // LANES, tm, LANES), F32)] * 4,
        compiler_params=pltpu.CompilerParams(dimension_semantics=("parallel",), vmem_limit_bytes=VMEM_LIMIT),
        name="back_prompt",
    )(h, oa, *obs, *lses, mk, mv, w["mix_norm"], w["w_in"], w["w_branch_a"], w["w_branch_b"], w["w_out"],
      w["mem_q_norm"], w["w_mem_q"], w["w_mem_o"], w["ffn2_norm"], w["ffn2_wg"], w["ffn2_wu"], w["ffn2_wd"],
      w["final_norm"])


def _back_sample_a_kernel(h_ref, oa_ref, ob_ref, gm_ref, wgate_ref, wba_ref, wbb_ref, wout_ref, gq_ref, wq_ref,
                          h2_ref, q_ref):
    h2 = _mix_out(h_ref[...], oa_ref[...], ob_ref[...], gm_ref, wgate_ref, wba_ref, wbb_ref, wout_ref)
    h2_ref[...] = h2
    q_ref[...] = _dot(_rms(h2, gq_ref[...]).astype(BF16), wq_ref[...])


def _back_sample_a_call(h, oa, ob, w):
    t = h.shape[0]
    tm = TOKEN_TILE
    tok = lambda width: pl.BlockSpec((tm, width), lambda i: (i, 0))
    return pl.pallas_call(
        _back_sample_a_kernel,
        grid=(t // tm,),
        in_specs=[tok(D_MODEL), tok(Q_A_W), tok(KV_B_W),
                  _const_spec((1, D_MODEL)), _const_spec((D_MODEL, 2 * D_MODEL), (0, 1)), _const_spec((Q_A_W, D_MODEL)),
                  _const_spec((KV_B_W, D_MODEL)), _const_spec((D_MODEL, D_MODEL)),
                  _const_spec((1, D_MODEL)), _const_spec((D_MODEL, MEM_W))],
        out_specs=[tok(D_MODEL), tok(MEM_W)],
        out_shape=[jax.ShapeDtypeStruct((t, D_MODEL), F32), jax.ShapeDtypeStruct((t, MEM_W), F32)],
        compiler_params=pltpu.CompilerParams(dimension_semantics=("parallel",), vmem_limit_bytes=VMEM_LIMIT),
        name="back_sample_mix",
    )(h, oa, ob, w["mix_norm"], w["w_in"], w["w_branch_a"], w["w_branch_b"], w["w_out"],
      w["mem_q_norm"], w["w_mem_q"])


CROSS_BATCH_TILE = 8


def _cross_sample_kernel(q_ref, mk_ref, mv_ref, o_ref):
    nrow = MEM_HEADS * SAMPLE_T
    qhead = lax.broadcasted_iota(jnp.int32, (nrow, N_MEM * MEM_HEADS), 0) >> (SAMPLE_T.bit_length() - 1)
    khead = lax.broadcasted_iota(jnp.int32, (nrow, N_MEM * MEM_HEADS), 1) & (MEM_HEADS - 1)
    own_head = qhead == khead
    for bi in range(CROSS_BATCH_TILE):
        rows = slice(bi * SAMPLE_T, (bi + 1) * SAMPLE_T)
        qs = jnp.concatenate(
            [q_ref[rows, hd * MEM_HEAD_DIM:(hd + 1) * MEM_HEAD_DIM] for hd in range(MEM_HEADS)], axis=0)
        s = _dot_nt(qs.astype(BF16), mk_ref[bi].astype(BF16)) * MEM_SCALE
        p = _softmax_rows(jnp.where(own_head, s, NEG_INF))
        o = _dot(p.astype(BF16), mv_ref[bi].astype(BF16))
        for hd in range(MEM_HEADS):
            o_ref[rows, hd * MEM_HEAD_DIM:(hd + 1) * MEM_HEAD_DIM] = o[hd * SAMPLE_T:(hd + 1) * SAMPLE_T]


def _cross_sample_call(q, mk, mv):
    t = q.shape[0]
    bt = CROSS_BATCH_TILE
    tok = pl.BlockSpec((bt * SAMPLE_T, MEM_W), lambda i: (i, 0))
    mem = pl.BlockSpec((bt, N_MEM * MEM_HEADS, MEM_HEAD_DIM), lambda i: (i, 0, 0))
    return pl.pallas_call(
        _cross_sample_kernel,
        grid=(mk.shape[0] // bt,),
        in_specs=[tok, mem, mem],
        out_specs=tok,
        out_shape=jax.ShapeDtypeStruct((t, MEM_W), F32),
        compiler_params=pltpu.CompilerParams(dimension_semantics=("parallel",), vmem_limit_bytes=VMEM_LIMIT),
        name="cross_sample",
    )(q, mk, mv)


def _back_sample_b_kernel(h2_ref, oc_ref, wo_ref, g2_ref, wg_ref, wu_ref, wd_ref, gf_ref, y_ref):
    h3 = h2_ref[...] + _dot(oc_ref[...].astype(BF16), wo_ref[...])
    y_ref[...] = _rms(_ffn_half(h3, g2_ref, wg_ref, wu_ref, wd_ref), gf_ref[...])


def _back_sample_b_call(h2, oc, w):
    t = h2.shape[0]
    tm = TOKEN_TILE
    tok = lambda width: pl.BlockSpec((tm, width), lambda i: (i, 0))
    return pl.pallas_call(
        _back_sample_b_kernel,
        grid=(t // tm,),
        in_specs=[tok(D_MODEL), tok(MEM_W), _const_spec((MEM_W, D_MODEL)),
                  _const_spec((1, D_MODEL)), _const_spec((D_MODEL, D_FF)), _const_spec((D_MODEL, D_FF)),
                  _const_spec((D_FF, D_MODEL)), _const_spec((1, D_MODEL))],
        out_specs=tok(D_MODEL),
        out_shape=jax.ShapeDtypeStruct((t, D_MODEL), F32),
        compiler_params=pltpu.CompilerParams(dimension_semantics=("parallel",), vmem_limit_bytes=VMEM_LIMIT),
        name="back_sample_ffn",
    )(h2, oc, w["w_mem_o"], w["ffn2_norm"], w["ffn2_wg"], w["ffn2_wu"], w["ffn2_wd"], w["final_norm"])


def _mem_kv_kernel(mem_ref, g_ref, wk_ref, wv_ref, k32_ref, v32_ref, k16_ref, v16_ref):
    u = _rms(mem_ref[...], g_ref[...]).astype(BF16)
    k = _dot(u, wk_ref[...])
    v = _dot(u, wv_ref[...])
    k32_ref[...] = k
    v32_ref[...] = v
    k16_ref[...] = k.astype(BF16)
    v16_ref[...] = v.astype(BF16)


def _mem_kv_call(mem, w):
    rows = mem.shape[0]
    full = lambda width: pl.BlockSpec((rows, width), lambda i: (0, 0))
    return pl.pallas_call(
        _mem_kv_kernel,
        grid=(1,),
        in_specs=[full(D_MODEL), _const_spec((1, D_MODEL)), _const_spec((D_MODEL, MEM_W)),
                  _const_spec((D_MODEL, MEM_W))],
        out_specs=[full(MEM_W)] * 4,
        out_shape=[jax.ShapeDtypeStruct((rows, MEM_W), F32)] * 2 + [jax.ShapeDtypeStruct((rows, MEM_W), BF16)] * 2,
        compiler_params=pltpu.CompilerParams(dimension_semantics=("arbitrary",), vmem_limit_bytes=VMEM_LIMIT),
        name="mem_kv",
    )(mem, w["mem_kv_norm"], w["w_mem_k"], w["w_mem_v"])


def _rope_tables(pos):
    within = np.arange(LANES) % HEAD_DIM
    freq_of_lane = jnp.asarray(within % ROT_HALF, jnp.int32)
    inv_freq = jnp.power(jnp.float32(ROPE_THETA), -jnp.arange(ROT_HALF, dtype=jnp.float32) / ROT_HALF)
    ang = pos.astype(jnp.float32)[:, None] * inv_freq[freq_of_lane][None, :]
    first = jnp.asarray(within < ROT_HALF)[None, :]
    second = jnp.asarray((within >= ROT_HALF) & (within < 2 * ROT_HALF))[None, :]
    cos, sin = jnp.cos(ang), jnp.sin(ang)
    return (jnp.where(first | second, cos, 1.0), jnp.where(first, -sin, 0.0), jnp.where(second, sin, 0.0))


def _prep_weights(p):
    bf = lambda x: x.astype(BF16)
    vec = lambda x: x.reshape(1, -1)
    return {
        "ffn1_norm": vec(p["ffn1_norm"]), "ffn1_wg": bf(p["ffn1_w_gate"]), "ffn1_wu": bf(p["ffn1_w_up"]),
        "ffn1_wd": bf(p["ffn1_w_down"]),
        "mix_norm": vec(p["mix_norm"]), "w_in": bf(p["w_in"]),
        "w_branch_a": bf(p["w_branch_a"]), "w_branch_b": bf(p["w_branch_b"]), "w_out": bf(p["w_out"]),
        "mem_q_norm": vec(p["mem_q_norm"]), "mem_kv_norm": vec(p["mem_kv_norm"]),
        "w_mem_q": bf(p["w_mem_q"]), "w_mem_k": bf(p["w_mem_k"]), "w_mem_v": bf(p["w_mem_v"]),
        "w_mem_o": bf(p["w_mem_o"]),
        "ffn2_norm": vec(p["ffn2_norm"]), "ffn2_wg": bf(p["ffn2_w_gate"]), "ffn2_wu": bf(p["ffn2_w_up"]),
        "ffn2_wd": bf(p["ffn2_w_down"]), "final_norm": vec(p["final_norm"]),
    }


def kernel(x_prompt, x_sample, cache_swa_k, cache_swa_v, cache_dil_k, cache_dil_v, cache_mem_k, cache_mem_v, mem_prompt, ffn1_norm, ffn1_w_gate, ffn1_w_up, ffn1_w_down, mix_norm, w_in, attn_sink, w_branch_a, w_branch_b, w_out, mem_q_norm, mem_kv_norm, w_mem_q, w_mem_k, w_mem_v, w_mem_o, ffn2_norm, ffn2_w_gate, ffn2_w_up, ffn2_w_down, final_norm):
    depth = ffn1_norm.shape[0]
    assert depth == 1
    batch, seq, d = x_prompt.shape
    nb, t_new, _ = x_sample.shape
    assert d == D_MODEL and t_new == SAMPLE_T and seq % (ATTN_CHUNK * DILATIONS[-1]) == 0
    assert cache_swa_k.shape[2] == CACHE_A and cache_dil_k.shape[2] == CACHE_B
    layer = lambda x: x[0]
    w = _prep_weights(dict(
        ffn1_norm=layer(ffn1_norm), ffn1_w_gate=layer(ffn1_w_gate), ffn1_w_up=layer(ffn1_w_up),
        ffn1_w_down=layer(ffn1_w_down), mix_norm=layer(mix_norm), w_in=layer(w_in),
        w_branch_a=layer(w_branch_a), w_branch_b=layer(w_branch_b), w_out=layer(w_out),
        mem_q_norm=layer(mem_q_norm), mem_kv_norm=layer(mem_kv_norm), w_mem_q=layer(w_mem_q),
        w_mem_k=layer(w_mem_k), w_mem_v=layer(w_mem_v), w_mem_o=layer(w_mem_o), ffn2_norm=layer(ffn2_norm),
        ffn2_w_gate=layer(ffn2_w_gate), ffn2_w_up=layer(ffn2_w_up), ffn2_w_down=layer(ffn2_w_down),
        final_norm=final_norm))
    sink = layer(attn_sink)

    tp = batch * seq
    mk32, mv32, mk16, mv16 = _mem_kv_call(mem_prompt.reshape(batch * N_MEM, d), w)
    tables_p = _rope_tables(jnp.arange(seq, dtype=jnp.int32))
    (h_p, qa, ka, va, qb0, kb, vb, q4, k4, v4, q16, k16, v16, ka32, va32, kb32, vb32) = _front_call(
        x_prompt.reshape(tp, d), tables_p, w, sample=False, batch=batch)
    oa = _attn_a_call(sink, qa, ka, va, batch, seq)
    unsplit = lambda x: x.reshape(batch, 1, seq, KV_B_W)
    groups = ((unsplit(qb0), unsplit(kb), unsplit(vb)), (q4, k4, v4), (q16, k16, v16))
    obs, lses = zip(*[_attn_b_call(*qkv, d) for qkv, d in zip(groups, DILATIONS)])
    y_p = _back_prompt_call(h_p, oa, obs, lses, mk16.reshape(batch, N_MEM, MEM_W),
                            mv16.reshape(batch, N_MEM, MEM_W), w, seq)

    ts = nb * t_new
    pos_s = PAST_LEN + (jnp.arange(TOKEN_TILE, dtype=jnp.int32) % t_new)
    tables_s = _rope_tables(pos_s)
    (h_s, q32, ka32_s, va32_s, kb32_s, vb32_s) = _front_call(x_sample.reshape(ts, d), tables_s, w, sample=True)
    by_pos = lambda c: jnp.transpose(layer(c), (0, 2, 3, 1)).reshape(nb, c.shape[3] * c.shape[4], c.shape[2])
    by_slot_head = lambda c: layer(c).reshape(nb, N_MEM * MEM_HEADS, MEM_HEAD_DIM)
    oa_s, ob_s = _attn_sample_call(
        sink, q32, ka32_s, va32_s, kb32_s, vb32_s,
        by_pos(cache_swa_k), by_pos(cache_swa_v), by_pos(cache_dil_k), by_pos(cache_dil_v))
    h2_s, qc_s = _back_sample_a_call(h_s, oa_s, ob_s, w)
    oc_s = _cross_sample_call(qc_s, by_slot_head(cache_mem_k), by_slot_head(cache_mem_v))
    y_s = _back_sample_b_call(h2_s, oc_s, w)

    keep_a, keep_b = min(CACHE_A, seq), min(CACHE_B, seq)
    tail = lambda x, keep, heads: x.reshape(batch, seq, heads * HEAD_DIM)[:, seq - keep:].reshape(
        1, batch, keep, heads, HEAD_DIM)
    new = lambda x, heads: x.reshape(1, nb, t_new, heads, HEAD_DIM)
    memo = lambda x: x.reshape(1, batch, N_MEM, MEM_HEADS, MEM_HEAD_DIM)
    return (y_p.reshape(batch, seq, d), y_s.reshape(nb, t_new, d),
            tail(ka32, keep_a, N_KV_A), tail(va32, keep_a, N_KV_A),
            tail(kb32, keep_b, N_KV_B), tail(vb32, keep_b, N_KV_B),
            memo(mk32), memo(mv32),
            new(ka32_s, N_KV_A), new(va32_s, N_KV_A), new(kb32_s, N_KV_B), new(vb32_s, N_KV_B))
```

```python
import functools

import jax
import jax.numpy as jnp
import numpy as np
from jax import lax
from jax.experimental import pallas as pl
from jax.experimental.pallas import tpu as pltpu

F32 = jnp.float32
BF16 = jnp.bfloat16

D_MODEL = 1024
D_FF = 2816
HEAD_DIM = 64
ROT_HALF = 8
ROPE_THETA = 500000.0
ATTN_SCALE = HEAD_DIM ** -0.5
RMS_EPS = 1e-6
PAST_LEN = 16384

N_HEADS_A = 8
N_KV_A = 2
N_KV_B = 4
DILATIONS = (1, 4, 16)
BAND_STEPS = 128
Q_A_W = N_HEADS_A * HEAD_DIM
KV_A_W = N_KV_A * HEAD_DIM
Q_B_W = len(DILATIONS) * N_KV_B * HEAD_DIM
KV_B_W = N_KV_B * HEAD_DIM
QKV_W = Q_A_W + 2 * KV_A_W + Q_B_W + 2 * KV_B_W
MEM_HEADS = 4
MEM_HEAD_DIM = 128
MEM_W = MEM_HEADS * MEM_HEAD_DIM
MEM_SCALE = MEM_HEAD_DIM ** -0.5
N_MEM = 256

HEAD_PERM_A = (0, 4, 1, 5, 2, 6, 3, 7)

LANES = 128
BLOCK = 128
TOKEN_TILE = 512
ROW_PARTS = 2
ATTN_CHUNK = 512
FF_CHUNKS = ((0, 1024), (1024, 2048), (2048, 2816))
VMEM_LIMIT = 60 * 1024 * 1024
NEG_INF = float("-inf")


def _rms(x, g):
    ms = jnp.mean(x * x, axis=-1, keepdims=True)
    return x * lax.rsqrt(ms + RMS_EPS) * g


def _dot(a, b):
    return jnp.dot(a, b, preferred_element_type=F32)


def _dot_nt(a, b):
    return lax.dot_general(a, b, (((1,), (1,)), ((), ())), preferred_element_type=F32)


def _ffn_half(xs, g_ref, wg_ref, wu_ref, wd_ref, side_stage=None):
    side_stage = side_stage or (lambda k, s: None)
    us = [_rms(x, g_ref[...]).astype(BF16) for x in xs]
    accs = [None] * len(xs)
    for k, (lo, hi) in enumerate(FF_CHUNKS):
        side_stage(k, 0)
        gates = [_dot(u, wg_ref[:, lo:hi]) for u in us]
        ups = [_dot(u, wu_ref[:, lo:hi]) for u in us]
        side_stage(k, 1)
        acts = [(gate * jax.nn.sigmoid(gate) * up).astype(BF16) for gate, up in zip(gates, ups)]
        side_stage(k, 2)
        for i, act in enumerate(acts):
            part = _dot(act, wd_ref[lo:hi, :])
            accs[i] = part if accs[i] is None else accs[i] + part
    return [x + 0.5 * acc for x, acc in zip(xs, accs)]


def _row_blocks(ref):
    n = ref.shape[0] // ROW_PARTS
    return [slice(i * n, (i + 1) * n) for i in range(ROW_PARTS)]


def _swap_halves(x):
    return pltpu.roll(x, HEAD_DIM, 1)


def _low_half():
    return lax.broadcasted_iota(jnp.int32, (1, LANES), 1) < HEAD_DIM


def _pair_kv_groups(chunks):
    low = _low_half()
    c0, c1, c2, c3 = chunks
    return [jnp.where(low, c0, _swap_halves(c2)), jnp.where(low, _swap_halves(c0), c2),
            jnp.where(low, c1, _swap_halves(c3)), jnp.where(low, _swap_halves(c1), c3)]


def _merge_pairs(per_head):
    low = _low_half()
    return [jnp.where(low, per_head[2 * c], per_head[2 * c + 1]) for c in range(N_HEADS_A // 2)]


N_FRONT_IN = 10
N_SIDE_IN = 10
N_SIDE_OUT = 2
N_SIDE_SCRATCH = 5


def _front_kernel(*refs, sample, fused=False):
    x_ref, cos_ref, sa_ref, sb_ref, g1_ref, wg_ref, wu_ref, wd_ref, gm_ref, wqkv_ref = refs[:N_FRONT_IN]
    refs = refs[N_FRONT_IN:]
    side = lambda j, s: None
    if fused:
        side_in, refs = refs[:N_SIDE_IN], refs[N_SIDE_IN:]
        h_ref, rest = refs[0], refs[1:]
        n_front_out = len(rest) - N_SIDE_OUT - N_SIDE_SCRATCH - 4
        side_out = rest[n_front_out:n_front_out + N_SIDE_OUT]
        side_scratch = rest[len(rest) - N_SIDE_SCRATCH:]
        rest = rest[:n_front_out] + rest[n_front_out + N_SIDE_OUT:len(rest) - N_SIDE_SCRATCH]
        side = _make_side_attention(side_in, side_out, side_scratch)
    else:
        h_ref, rest = refs[0], refs[1:]
    blocks = _row_blocks(x_ref)
    hs = _ffn_half([x_ref[rb, :] for rb in blocks], g1_ref, wg_ref, wu_ref, wd_ref, side_stage=side)
    for rb, h in zip(blocks, hs):
        h_ref[rb, :] = h
    ns = [_rms(h, gm_ref[...]).astype(BF16) for h in hs]
    last = len(FF_CHUNKS)
    assert not fused or SIDE_ELEMS == last + 1
    side(last, 0)
    z_lo = [_dot(n, wqkv_ref[:, :QKV_W // 2]) for n in ns]
    side(last, 1)
    side(last, 2)
    zs = [jnp.concatenate([lo, _dot(n, wqkv_ref[:, QKV_W // 2:])], axis=1) for lo, n in zip(z_lo, ns)]
    for rb, z in zip(blocks, zs):
        _front_emit(rb, z, cos_ref, sa_ref, sb_ref, rest, sample)


def _front_emit(rb, z, cos_ref, sa_ref, sb_ref, rest, sample):
    cos, sa, sb = cos_ref[rb, :], sa_ref[rb, :], sb_ref[rb, :]

    def chunk(c):
        return z[:, c * LANES:(c + 1) * LANES]

    def rope(c):
        zc = chunk(c)
        return zc * cos + pltpu.roll(zc, LANES - ROT_HALF, 1) * sa + pltpu.roll(zc, ROT_HALF, 1) * sb

    nqa, nka = Q_A_W // LANES, KV_A_W // LANES
    nqb, nkb = Q_B_W // LANES, KV_B_W // LANES
    c0 = 0
    qa = _pair_kv_groups([rope(c0 + c) * ATTN_SCALE for c in range(nqa)])
    c0 += nqa
    ka = [rope(c0 + c) for c in range(nka)]
    c0 += nka
    va = [chunk(c0 + c) for c in range(nka)]
    c0 += nka
    qb = [rope(c0 + c) * ATTN_SCALE for c in range(nqb)]
    c0 += nqb
    kb = [rope(c0 + c) for c in range(nkb)]
    c0 += nkb
    vb = [chunk(c0 + c) for c in range(nkb)]

    def store(ref, parts, dtype):
        for c, p in enumerate(parts):
            ref[rb, c * LANES:(c + 1) * LANES] = p.astype(dtype)

    def fill(scr, parts):
        for c, p in enumerate(parts):
            scr[c, rb, :] = p

    def store_split(ref, scr, d):
        n = (rb.stop - rb.start) // d
        dst = slice(rb.start // d, rb.start // d + n)
        for r in range(d):
            for c in range(scr.shape[0]):
                ref[0, r, dst, c * LANES:(c + 1) * LANES] = scr[c, pl.ds(rb.start + r, n, stride=d), :].astype(BF16)

    if sample:
        q32_ref, ka32_ref, va32_ref, kb32_ref, vb32_ref = rest
        store(q32_ref, qa + qb, F32)
    else:
        (qa_ref, ka_ref, va_ref, qb0_ref, kb_ref, vb_ref, q4_ref, k4_ref, v4_ref, q16_ref, k16_ref, v16_ref,
         ka32_ref, va32_ref, kb32_ref, vb32_ref, q4_s, q16_s, kb_s, vb_s) = rest
        gw = KV_B_W // LANES
        store(qa_ref, qa, BF16)
        store(ka_ref, ka, BF16)
        store(va_ref, va, BF16)
        store(qb0_ref, qb[0:gw], BF16)
        store(kb_ref, kb, BF16)
        store(vb_ref, vb, BF16)
        fill(q4_s, qb[gw:2 * gw])
        fill(q16_s, qb[2 * gw:3 * gw])
        fill(kb_s, kb)
        fill(vb_s, vb)
        store_split(q4_ref, q4_s, DILATIONS[1])
        store_split(q16_ref, q16_s, DILATIONS[2])
        store_split(k4_ref, kb_s, DILATIONS[1])
        store_split(k16_ref, kb_s, DILATIONS[2])
        store_split(v4_ref, vb_s, DILATIONS[1])
        store_split(v16_ref, vb_s, DILATIONS[2])
    store(ka32_ref, ka, F32)
    store(va32_ref, va, F32)
    store(kb32_ref, kb, F32)
    store(vb32_ref, vb, F32)


def _const_spec(shape, index=None):
    index = (0,) * len(shape) if index is None else index
    return pl.BlockSpec(shape, lambda *_: index, pipeline_mode=pl.Buffered(1))


def _front_call(x, tables, w, *, sample, batch=1, side=None):
    t = x.shape[0]
    tm = TOKEN_TILE
    nsteps = t // tm
    tpb = nsteps // batch
    seq = t // batch
    cos_t, sa_t, sb_t = tables
    tab_blocks = cos_t.shape[0] // tm
    row = lambda i: (i, 0)
    tab = lambda i: (i % tab_blocks, 0)

    def tok(width, dtype):
        return jax.ShapeDtypeStruct((t, width), dtype), pl.BlockSpec((tm, width), row)

    def split(d):
        return (jax.ShapeDtypeStruct((batch, d, seq // d, KV_B_W), BF16),
                pl.BlockSpec((1, d, tm // d, KV_B_W), lambda i: (i // tpb, 0, i % tpb, 0)))

    outs = [tok(D_MODEL, F32)]
    scratch = []
    if sample:
        outs += [tok(Q_A_W + Q_B_W, F32)]
    else:
        outs += [tok(Q_A_W, BF16), tok(KV_A_W, BF16), tok(KV_A_W, BF16),
                 tok(KV_B_W, BF16), tok(KV_B_W, BF16), tok(KV_B_W, BF16)]
        outs += [split(DILATIONS[1])] * 3 + [split(DILATIONS[2])] * 3
        scratch = [pltpu.VMEM((KV_B_W // LANES, tm, LANES), F32)] * 4
    outs += [tok(KV_A_W, F32), tok(KV_A_W, F32), tok(KV_B_W, F32), tok(KV_B_W, F32)]
    out_shape, out_specs = zip(*outs)
    in_specs = [pl.BlockSpec((tm, D_MODEL), row),
                pl.BlockSpec((tm, LANES), tab), pl.BlockSpec((tm, LANES), tab), pl.BlockSpec((tm, LANES), tab),
                _const_spec((1, D_MODEL)), _const_spec((D_MODEL, D_FF)), _const_spec((D_MODEL, D_FF)),
                _const_spec((D_FF, D_MODEL)), _const_spec((1, D_MODEL)), _const_spec((D_MODEL, QKV_W), (0, 0))]
    operands = [x, cos_t, sa_t, sb_t, w["ffn1_norm"], w["ffn1_wg"], w["ffn1_wu"], w["ffn1_wd"], w["mix_norm"],
                w["w_in"]]
    out_shape, out_specs = list(out_shape), list(out_specs)
    semantics = "parallel"
    if side is not None:
        q32 = side[1]
        ts = q32.shape[0]
        assert ts == nsteps * SIDE_ELEMS * SAMPLE_T and SIDE_ELEMS % 2 == 0
        stok = lambda width: pl.BlockSpec((SIDE_ELEMS * SAMPLE_T, width), row)
        in_specs += ([pl.BlockSpec(memory_space=pltpu.SMEM), stok(Q_A_W + Q_B_W),
                      stok(KV_A_W), stok(KV_A_W), stok(KV_B_W), stok(KV_B_W)]
                     + [pl.BlockSpec(memory_space=pl.ANY)] * 4)
        operands += list(side)
        out_shape += [jax.ShapeDtypeStruct((ts, Q_A_W), F32), jax.ShapeDtypeStruct((ts, KV_B_W), F32)]
        out_specs += [stok(Q_A_W), stok(KV_B_W)]
        scratch = scratch + [pltpu.VMEM((2, KV_A_W, CACHE_A), F32), pltpu.VMEM((2, KV_A_W, CACHE_A), F32),
                             pltpu.VMEM((2, KV_B_W, CACHE_B), F32), pltpu.VMEM((2, KV_B_W, CACHE_B), F32),
                             pltpu.SemaphoreType.DMA((4, 2))]
        semantics = "arbitrary"
    return pl.pallas_call(
        functools.partial(_front_kernel, sample=sample, fused=side is not None),
        grid=(nsteps,),
        in_specs=in_specs,
        out_specs=out_specs,
        out_shape=out_shape,
        scratch_shapes=scratch,
        compiler_params=pltpu.CompilerParams(dimension_semantics=(semantics,), vmem_limit_bytes=VMEM_LIMIT),
        name="front_sample" if sample else "front_prompt",
    )(*operands)


def _lane_block_masks(width, block, dtype):
    lane = lax.broadcasted_iota(jnp.int32, (1, width), 1)
    return [((lane >= i * block) & (lane < (i + 1) * block)).astype(dtype) for i in range(width // block)]


def _band_mask():
    qi = lax.broadcasted_iota(jnp.int32, (BLOCK, 2 * BLOCK), 0)
    kj = lax.broadcasted_iota(jnp.int32, (BLOCK, 2 * BLOCK), 1)
    dist = qi + BLOCK - kj
    return (dist >= 0) & (dist <= BAND_STEPS), kj >= BLOCK


def _band_softmax_pv(qbd, kcat, vcat, mask, nheads):
    s = _dot_nt(qbd, kcat)
    ps, ms, ls = [], [], []
    for p in range(nheads):
        sp = jnp.where(mask, s[p * BLOCK:(p + 1) * BLOCK], NEG_INF)
        m = jnp.max(sp, axis=-1, keepdims=True)
        e = jnp.exp(sp - m)
        ls.append(jnp.sum(e, axis=-1, keepdims=True))
        ms.append(m)
        ps.append(e.astype(BF16))
    o = _dot(jnp.concatenate(ps, axis=0), vcat)
    return o, ms, ls


def _kv_window(cur_ref, prev_ref, qb):
    if qb == 0:
        return jnp.concatenate([prev_ref[...], cur_ref[0:BLOCK]], axis=0)
    return cur_ref[(qb - 1) * BLOCK:(qb + 1) * BLOCK]


def _attn_a_kernel(sink_ref, q_ref, kc_ref, kp_ref, vc_ref, vp_ref, o_ref):
    first_chunk = pl.program_id(1) == 0
    band, in_cur = _band_mask()
    half_bf = _lane_block_masks(LANES, HEAD_DIM, BF16)
    for qb in range(ATTN_CHUNK // BLOCK):
        mask = band & (in_cur | jnp.logical_not(first_chunk)) if qb == 0 else band
        kcat = _kv_window(kc_ref, kp_ref, qb)
        vcat = _kv_window(vc_ref, vp_ref, qb)
        rows = slice(qb * BLOCK, (qb + 1) * BLOCK)
        qbd = jnp.concatenate(
            [q_ref[rows, (p // 2) * LANES:(p // 2 + 1) * LANES] * half_bf[p % 2] for p in range(N_HEADS_A)], axis=0)
        o, ms, ls = _band_softmax_pv(qbd, kcat, vcat, mask, N_HEADS_A)
        normed = []
        for p in range(N_HEADS_A):
            den = ls[p] + jnp.exp(sink_ref[HEAD_PERM_A[p]] - ms[p])
            normed.append(o[p * BLOCK:(p + 1) * BLOCK] * (1.0 / den))
        for c, pair in enumerate(_merge_pairs(normed)):
            o_ref[rows, c * LANES:(c + 1) * LANES] = pair.astype(BF16)


def _attn_a_call(sink, qa, ka, va, batch, seq):
    t = qa.shape[0]
    cpb = seq // ATTN_CHUNK
    bpc = ATTN_CHUNK // BLOCK
    cur = lambda b, c: (b * cpb + c, 0)
    prev = lambda b, c: (b * cpb * bpc + jnp.maximum(c * bpc - 1, 0), 0)
    return pl.pallas_call(
        _attn_a_kernel,
        grid=(batch, cpb),
        in_specs=[pl.BlockSpec(memory_space=pltpu.SMEM),
                  pl.BlockSpec((ATTN_CHUNK, Q_A_W), cur),
                  pl.BlockSpec((ATTN_CHUNK, KV_A_W), cur), pl.BlockSpec((BLOCK, KV_A_W), prev),
                  pl.BlockSpec((ATTN_CHUNK, KV_A_W), cur), pl.BlockSpec((BLOCK, KV_A_W), prev)],
        out_specs=pl.BlockSpec((ATTN_CHUNK, Q_A_W), cur),
        out_shape=jax.ShapeDtypeStruct((t, Q_A_W), BF16),
        compiler_params=pltpu.CompilerParams(dimension_semantics=("parallel", "parallel"),
                                             vmem_limit_bytes=VMEM_LIMIT),
        name="attn_a_prompt",
    )(sink, qa, ka, ka, va, va)


def _attn_b_kernel(q_ref, kc_ref, kp_ref, vc_ref, vp_ref, o_ref, lse_ref):
    first_chunk = pl.program_id(2) == 0
    band, in_cur = _band_mask()
    head_bf = _lane_block_masks(KV_B_W, HEAD_DIM, BF16)
    lane = lax.broadcasted_iota(jnp.int32, (1, KV_B_W), 1)
    for qb in range(ATTN_CHUNK // BLOCK):
        mask = band & (in_cur | jnp.logical_not(first_chunk)) if qb == 0 else band
        kcat = _kv_window(kc_ref, kp_ref, qb)
        vcat = _kv_window(vc_ref, vp_ref, qb)
        rows = slice(qb * BLOCK, (qb + 1) * BLOCK)
        q = q_ref[rows, :]
        qbd = jnp.concatenate([q * head_bf[p] for p in range(N_KV_B)], axis=0)
        o, ms, ls = _band_softmax_pv(qbd, kcat, vcat, mask, N_KV_B)
        out = lse = None
        for p in reversed(range(N_KV_B)):
            op = o[p * BLOCK:(p + 1) * BLOCK] * (1.0 / ls[p])
            lp = jnp.broadcast_to(ms[p] + jnp.log(ls[p]), (BLOCK, KV_B_W))
            if out is None:
                out, lse = op, lp
            else:
                sel = lane < (p + 1) * HEAD_DIM
                out, lse = jnp.where(sel, op, out), jnp.where(sel, lp, lse)
        o_ref[rows, :] = out
        lse_ref[rows, :] = lse


def _attn_b_call(q, k, v, d):
    batch, _, rows, _ = q.shape
    cpb = rows // ATTN_CHUNK
    bpc = ATTN_CHUNK // BLOCK
    cur = pl.BlockSpec((None, None, ATTN_CHUNK, KV_B_W), lambda b, r, c: (b, r, c, 0))
    prev = pl.BlockSpec((None, None, BLOCK, KV_B_W), lambda b, r, c: (b, r, jnp.maximum(c * bpc - 1, 0), 0))
    return pl.pallas_call(
        _attn_b_kernel,
        grid=(batch, d, cpb),
        in_specs=[cur, cur, prev, cur, prev],
        out_specs=[cur, cur],
        out_shape=[jax.ShapeDtypeStruct((batch, d, rows, KV_B_W), F32)] * 2,
        compiler_params=pltpu.CompilerParams(dimension_semantics=("parallel", "parallel", "parallel"),
                                             vmem_limit_bytes=VMEM_LIMIT),
        name=f"attn_b_prompt_d{d}",
    )(q, k, k, v, v)


SAMPLE_T = 8
CACHE_A = 128
CACHE_B = 2048
SIDE_ELEMS = 4


def _new_rows_block(new):
    return jnp.concatenate([new, jnp.zeros((BLOCK - SAMPLE_T, new.shape[1]), F32)], axis=0)


def _cached_scores(qbd, kt, knew):
    return _dot(qbd, kt), _dot_nt(qbd, knew)


def _cached_softmax(s_c, s_n, mask_c, mask_n):
    s_c = jnp.where(mask_c, s_c, NEG_INF)
    s_n = jnp.where(mask_n, s_n, NEG_INF)
    m = jnp.maximum(jnp.max(s_c, axis=-1, keepdims=True), jnp.max(s_n, axis=-1, keepdims=True))
    e_c = jnp.exp(s_c - m)
    e_n = jnp.exp(s_n - m)
    l = jnp.sum(e_c, axis=-1, keepdims=True) + jnp.sum(e_n, axis=-1, keepdims=True)
    return e_c, e_n, m, l


def _cached_pv(e_c, e_n, vt, vnew):
    return _dot_nt(e_c, vt) + _dot(e_n, vnew)


def _make_side_attention(side_in, side_out, side_scratch):
    sink_ref, q_ref, kan_ref, van_ref, kbn_ref, vbn_ref, cka_hbm, cva_hbm, ckb_hbm, cvb_hbm = side_in
    oa_ref, ob_ref = side_out
    bufs, sem = side_scratch[:4], side_scratch[4]
    hbm = (cka_hbm, cva_hbm, ckb_hbm, cvb_hbm)
    step = pl.program_id(0)
    last_step = pl.num_programs(0) - 1

    def copies(elem, slot):
        return [pltpu.make_async_copy(src.at[elem], dst.at[slot], sem.at[k, slot])
                for k, (src, dst) in enumerate(zip(hbm, bufs))]

    def start(elem, slot):
        for c in copies(elem, slot):
            c.start()

    def wait(elem, slot):
        for c in copies(elem, slot):
            c.wait()

    half_f = _lane_block_masks(LANES, HEAD_DIM, F32)
    lane_b = lax.broadcasted_iota(jnp.int32, (1, KV_B_W), 1)

    def dist(nrows, ncols, offset):
        r = lax.broadcasted_iota(jnp.int32, (nrows, ncols), 0)
        i = lax.broadcasted_iota(jnp.int32, (nrows, ncols), 1)
        return r, offset + (r & (SAMPLE_T - 1)) - i

    nrow_a = N_HEADS_A * SAMPLE_T
    _, dac = dist(nrow_a, CACHE_A, CACHE_A)
    _, dan = dist(nrow_a, BLOCK, 0)
    mask_ac = (dac >= 0) & (dac <= BAND_STEPS)
    mask_an = (dan >= 0) & (dan <= BAND_STEPS)
    nrow_b = len(DILATIONS) * N_KV_B * SAMPLE_T
    rows_per_group = N_KV_B * SAMPLE_T

    def mask_b(ncols, offset):
        r, db = dist(nrow_b, ncols, offset)
        dil = jnp.where(r < rows_per_group, DILATIONS[0], jnp.where(r < 2 * rows_per_group, DILATIONS[1], DILATIONS[2]))
        return (db >= 0) & (db <= BAND_STEPS * dil) & ((db & (dil - 1)) == 0)

    mask_bc = mask_b(CACHE_B, CACHE_B)
    mask_bn = mask_b(BLOCK, 0)

    sink_col = jnp.concatenate(
        [jnp.full((SAMPLE_T, 1), sink_ref[HEAD_PERM_A[p]], F32) for p in range(N_HEADS_A)], axis=0)

    def side(j):
        elem, slot = step * SIDE_ELEMS + j, j % 2
        if j == 0:
            @pl.when(step == 0)
            def _():
                start(elem, slot)
        wait(elem, slot)
        if j + 1 < SIDE_ELEMS:
            start(elem + 1, 1 - slot)
        else:
            @pl.when(step < last_step)
            def _():
                start(elem + 1, 1 - slot)
        cka_ref, cva_ref, ckb_ref, cvb_ref = [b.at[slot] for b in bufs]
        rows = slice(j * SAMPLE_T, (j + 1) * SAMPLE_T)
        qbd_a = jnp.concatenate(
            [q_ref[rows, (p // 2) * LANES:(p // 2 + 1) * LANES] * half_f[p % 2] for p in range(N_HEADS_A)],
            axis=0)
        qrows = []
        for g in range(len(DILATIONS)):
            for kvh in range(N_KV_B):
                c = (Q_A_W // LANES) + 2 * g + kvh // 2
                part = q_ref[rows, c * LANES:(c + 1) * LANES] * half_f[kvh % 2]
                zero = jnp.zeros_like(part)
                qrows.append(jnp.concatenate([part, zero] if kvh // 2 == 0 else [zero, part], axis=1))
        qbd_b = jnp.concatenate(qrows, axis=0)
        scores_a = _cached_scores(qbd_a, cka_ref[...], _new_rows_block(kan_ref[rows, :]))
        scores_b = _cached_scores(qbd_b, ckb_ref[...], _new_rows_block(kbn_ref[rows, :]))
        yield
        ea_c, ea_n, ma, la = _cached_softmax(*scores_a, mask_ac, mask_an)
        eb_c, eb_n, m, l = _cached_softmax(*scores_b, mask_bc, mask_bn)
        yield
        o = _cached_pv(ea_c, ea_n, cva_ref[...], _new_rows_block(van_ref[rows, :]))
        o = o * (1.0 / (la + jnp.exp(sink_col - ma)))
        per_head = [o[p * SAMPLE_T:(p + 1) * SAMPLE_T] for p in range(N_HEADS_A)]
        for c, pair in enumerate(_merge_pairs(per_head)):
            oa_ref[rows, c * LANES:(c + 1) * LANES] = pair
        o = _cached_pv(eb_c, eb_n, cvb_ref[...], _new_rows_block(vbn_ref[rows, :]))
        out = None
        for kvh in reversed(range(N_KV_B)):
            sl = [slice((g * N_KV_B + kvh) * SAMPLE_T, (g * N_KV_B + kvh + 1) * SAMPLE_T)
                  for g in range(len(DILATIONS))]
            mj = jnp.maximum(jnp.maximum(m[sl[0]], m[sl[1]]), m[sl[2]])
            ws = [jnp.exp(m[x] - mj) for x in sl]
            den = ws[0] * l[sl[0]] + ws[1] * l[sl[1]] + ws[2] * l[sl[2]]
            num = ws[0] * o[sl[0]] + ws[1] * o[sl[1]] + ws[2] * o[sl[2]]
            okv = num * (1.0 / den)
            out = okv if out is None else jnp.where(lane_b < (kvh + 1) * HEAD_DIM, okv, out)
        ob_ref[rows, :] = out
        yield

    running = {}

    def stage(j, s):
        if s == 0:
            running[j] = side(j)
        next(running[j])

    return stage


def _mix_out(h, oa, ob, gm_ref, wgate_ref, wba_ref, wbb_ref, wout_ref):
    n = _rms(h, gm_ref[...]).astype(BF16)
    gates = _dot(n, wgate_ref[...])
    ga = jax.nn.sigmoid(gates[:, :D_MODEL])
    gb = jax.nn.sigmoid(gates[:, D_MODEL:])
    mixed = ga * _dot(oa.astype(BF16), wba_ref[...]) + gb * _dot(ob.astype(BF16), wbb_ref[...])
    return h + _dot(mixed.astype(BF16), wout_ref[...])


def _combine_groups(os, ls):
    m = jnp.maximum(jnp.maximum(ls[0], ls[1]), ls[2])
    es = [jnp.exp(x - m) for x in ls]
    den = es[0] + es[1] + es[2]
    num = es[0] * os[0] + es[1] * os[1] + es[2] * os[2]
    return num * (1.0 / den)


def _merge_split(split_ref, scr, rb):
    d = split_ref.shape[1]
    n = (rb.stop - rb.start) // d
    src = slice(rb.start // d, rb.start // d + n)
    for r in range(d):
        for c in range(scr.shape[0]):
            scr[c, pl.ds(rb.start + r, n, stride=d), :] = split_ref[0, r, src, c * LANES:(c + 1) * LANES]
    return jnp.concatenate([scr[c, rb, :] for c in range(scr.shape[0])], axis=1)


def _softmax_rows(s):
    m = jnp.max(s, axis=-1, keepdims=True)
    e = jnp.exp(s - m)
    return e * (1.0 / jnp.sum(e, axis=-1, keepdims=True))


def _back_prompt_kernel(h_ref, oa_ref, o0_ref, o1_ref, o2_ref, l0_ref, l1_ref, l2_ref, mk_ref, mv_ref,
                        gm_ref, wgate_ref, wba_ref, wbb_ref, wout_ref, gq_ref, wq_ref, wo_ref,
                        g2_ref, wg_ref, wu_ref, wd_ref, gf_ref, y_ref, o1_s, o2_s, l1_s, l2_s):
    blocks = _row_blocks(h_ref)
    h3s = []
    for rb in blocks:
        ob = _combine_groups(
            (o0_ref[0, 0, rb, :], _merge_split(o1_ref, o1_s, rb), _merge_split(o2_ref, o2_s, rb)),
            (l0_ref[0, 0, rb, :], _merge_split(l1_ref, l1_s, rb), _merge_split(l2_ref, l2_s, rb)))
        h2 = _mix_out(h_ref[rb, :], oa_ref[rb, :], ob, gm_ref, wgate_ref, wba_ref, wbb_ref, wout_ref)
        q = _dot(_rms(h2, gq_ref[...]).astype(BF16), wq_ref[...])
        heads = []
        for hd in range(MEM_HEADS):
            cols = slice(hd * MEM_HEAD_DIM, (hd + 1) * MEM_HEAD_DIM)
            p = _softmax_rows(_dot_nt(q[:, cols].astype(BF16), mk_ref[0, :, cols]) * MEM_SCALE)
            heads.append(_dot(p.astype(BF16), mv_ref[0, :, cols]))
        oc = jnp.concatenate(heads, axis=1).astype(BF16)
        h3s.append(h2 + _dot(oc, wo_ref[...]))
    for rb, y in zip(blocks, _ffn_half(h3s, g2_ref, wg_ref, wu_ref, wd_ref)):
        y_ref[rb, :] = _rms(y, gf_ref[...])


def _back_prompt_call(h, oa, obs, lses, mk, mv, w, seq):
    t = h.shape[0]
    tm = TOKEN_TILE
    tiles_per_batch = seq // tm
    row = lambda i: (i, 0)
    mem = lambda i: (i // tiles_per_batch, 0, 0)
    tok = lambda width: pl.BlockSpec((tm, width), row)
    split = lambda d: pl.BlockSpec((1, d, tm // d, KV_B_W),
                                   lambda i: (i // tiles_per_batch, 0, i % tiles_per_batch, 0))
    groups = [split(d) for d in DILATIONS]
    in_specs = ([tok(D_MODEL), tok(Q_A_W)] + groups + groups
                + [pl.BlockSpec((1, N_MEM, MEM_W), mem), pl.BlockSpec((1, N_MEM, MEM_W), mem)]
                + [_const_spec((1, D_MODEL)), _const_spec((D_MODEL, 2 * D_MODEL), (0, 1)), _const_spec((Q_A_W, D_MODEL)),
                   _const_spec((KV_B_W, D_MODEL)), _const_spec((D_MODEL, D_MODEL)),
                   _const_spec((1, D_MODEL)), _const_spec((D_MODEL, MEM_W)), _const_spec((MEM_W, D_MODEL)),
                   _const_spec((1, D_MODEL)), _const_spec((D_MODEL, D_FF)), _const_spec((D_MODEL, D_FF)),
                   _const_spec((D_FF, D_MODEL)), _const_spec((1, D_MODEL))])
    return pl.pallas_call(
        _back_prompt_kernel,
        grid=(t // tm,),
        in_specs=in_specs,
        out_specs=tok(D_MODEL),
        out_shape=jax.ShapeDtypeStruct((t, D_MODEL), F32),
        scratch_shapes=[pltpu.VMEM((KV_B_W // LANES, tm, LANES), F32)] * 4,
        compiler_params=pltpu.CompilerParams(dimension_semantics=("parallel",), vmem_limit_bytes=VMEM_LIMIT),
        name="back_prompt",
    )(h, oa, *obs, *lses, mk, mv, w["mix_norm"], w["w_in"], w["w_branch_a"], w["w_branch_b"], w["w_out"],
      w["mem_q_norm"], w["w_mem_q"], w["w_mem_o"], w["ffn2_norm"], w["ffn2_wg"], w["ffn2_wu"], w["ffn2_wd"],
      w["final_norm"])


def _back_sample_a_kernel(h_ref, oa_ref, ob_ref, gm_ref, wgate_ref, wba_ref, wbb_ref, wout_ref, gq_ref, wq_ref,
                          h2_ref, q_ref):
    h2 = _mix_out(h_ref[...], oa_ref[...], ob_ref[...], gm_ref, wgate_ref, wba_ref, wbb_ref, wout_ref)
    h2_ref[...] = h2
    q_ref[...] = _dot(_rms(h2, gq_ref[...]).astype(BF16), wq_ref[...])


def _back_sample_a_call(h, oa, ob, w):
    t = h.shape[0]
    tm = TOKEN_TILE
    tok = lambda width: pl.BlockSpec((tm, width), lambda i: (i, 0))
    return pl.pallas_call(
        _back_sample_a_kernel,
        grid=(t // tm,),
        in_specs=[tok(D_MODEL), tok(Q_A_W), tok(KV_B_W),
                  _const_spec((1, D_MODEL)), _const_spec((D_MODEL, 2 * D_MODEL), (0, 1)), _const_spec((Q_A_W, D_MODEL)),
                  _const_spec((KV_B_W, D_MODEL)), _const_spec((D_MODEL, D_MODEL)),
                  _const_spec((1, D_MODEL)), _const_spec((D_MODEL, MEM_W))],
        out_specs=[tok(D_MODEL), tok(MEM_W)],
        out_shape=[jax.ShapeDtypeStruct((t, D_MODEL), F32), jax.ShapeDtypeStruct((t, MEM_W), F32)],
        compiler_params=pltpu.CompilerParams(dimension_semantics=("parallel",), vmem_limit_bytes=VMEM_LIMIT),
        name="back_sample_mix",
    )(h, oa, ob, w["mix_norm"], w["w_in"], w["w_branch_a"], w["w_branch_b"], w["w_out"],
      w["mem_q_norm"], w["w_mem_q"])


CROSS_BATCH_TILE = 8


def _cross_sample_kernel(q_ref, mk_ref, mv_ref, o_ref):
    nrow = MEM_HEADS * SAMPLE_T
    qhead = lax.broadcasted_iota(jnp.int32, (nrow, N_MEM * MEM_HEADS), 0) >> (SAMPLE_T.bit_length() - 1)
    khead = lax.broadcasted_iota(jnp.int32, (nrow, N_MEM * MEM_HEADS), 1) & (MEM_HEADS - 1)
    own_head = qhead == khead
    for bi in range(CROSS_BATCH_TILE):
        rows = slice(bi * SAMPLE_T, (bi + 1) * SAMPLE_T)
        qs = jnp.concatenate(
            [q_ref[rows, hd * MEM_HEAD_DIM:(hd + 1) * MEM_HEAD_DIM] for hd in range(MEM_HEADS)], axis=0)
        s = _dot_nt(qs.astype(BF16), mk_ref[bi].astype(BF16)) * MEM_SCALE
        p = _softmax_rows(jnp.where(own_head, s, NEG_INF))
        o = _dot(p.astype(BF16), mv_ref[bi].astype(BF16))
        for hd in range(MEM_HEADS):
            o_ref[rows, hd * MEM_HEAD_DIM:(hd + 1) * MEM_HEAD_DIM] = o[hd * SAMPLE_T:(hd + 1) * SAMPLE_T]


def _cross_sample_call(q, mk, mv):
    t = q.shape[0]
    bt = CROSS_BATCH_TILE
    tok = pl.BlockSpec((bt * SAMPLE_T, MEM_W), lambda i: (i, 0))
    mem = pl.BlockSpec((bt, N_MEM * MEM_HEADS, MEM_HEAD_DIM), lambda i: (i, 0, 0))
    return pl.pallas_call(
        _cross_sample_kernel,
        grid=(mk.shape[0] // bt,),
        in_specs=[tok, mem, mem],
        out_specs=tok,
        out_shape=jax.ShapeDtypeStruct((t, MEM_W), F32),
        compiler_params=pltpu.CompilerParams(dimension_semantics=("parallel",), vmem_limit_bytes=VMEM_LIMIT),
        name="cross_sample",
    )(q, mk, mv)


def _back_sample_b_kernel(h2_ref, oc_ref, wo_ref, g2_ref, wg_ref, wu_ref, wd_ref, gf_ref, y_ref):
    blocks = _row_blocks(h2_ref)
    h3s = [h2_ref[rb, :] + _dot(oc_ref[rb, :].astype(BF16), wo_ref[...]) for rb in blocks]
    for rb, y in zip(blocks, _ffn_half(h3s, g2_ref, wg_ref, wu_ref, wd_ref)):
        y_ref[rb, :] = _rms(y, gf_ref[...])


def _back_sample_b_call(h2, oc, w):
    t = h2.shape[0]
    tm = TOKEN_TILE
    tok = lambda width: pl.BlockSpec((tm, width), lambda i: (i, 0))
    return pl.pallas_call(
        _back_sample_b_kernel,
        grid=(t // tm,),
        in_specs=[tok(D_MODEL), tok(MEM_W), _const_spec((MEM_W, D_MODEL)),
                  _const_spec((1, D_MODEL)), _const_spec((D_MODEL, D_FF)), _const_spec((D_MODEL, D_FF)),
                  _const_spec((D_FF, D_MODEL)), _const_spec((1, D_MODEL))],
        out_specs=tok(D_MODEL),
        out_shape=jax.ShapeDtypeStruct((t, D_MODEL), F32),
        compiler_params=pltpu.CompilerParams(dimension_semantics=("parallel",), vmem_limit_bytes=VMEM_LIMIT),
        name="back_sample_ffn",
    )(h2, oc, w["w_mem_o"], w["ffn2_norm"], w["ffn2_wg"], w["ffn2_wu"], w["ffn2_wd"], w["final_norm"])


def _mem_kv_kernel(mem_ref, g_ref, wk_ref, wv_ref, k32_ref, v32_ref, k16_ref, v16_ref):
    u = _rms(mem_ref[...], g_ref[...]).astype(BF16)
    k = _dot(u, wk_ref[...])
    v = _dot(u, wv_ref[...])
    k32_ref[...] = k
    v32_ref[...] = v
    k16_ref[...] = k.astype(BF16)
    v16_ref[...] = v.astype(BF16)


def _mem_kv_call(mem, w):
    rows = mem.shape[0]
    full = lambda width: pl.BlockSpec((rows, width), lambda i: (0, 0))
    return pl.pallas_call(
        _mem_kv_kernel,
        grid=(1,),
        in_specs=[full(D_MODEL), _const_spec((1, D_MODEL)), _const_spec((D_MODEL, MEM_W)),
                  _const_spec((D_MODEL, MEM_W))],
        out_specs=[full(MEM_W)] * 4,
        out_shape=[jax.ShapeDtypeStruct((rows, MEM_W), F32)] * 2 + [jax.ShapeDtypeStruct((rows, MEM_W), BF16)] * 2,
        compiler_params=pltpu.CompilerParams(dimension_semantics=("arbitrary",), vmem_limit_bytes=VMEM_LIMIT),
        name="mem_kv",
    )(mem, w["mem_kv_norm"], w["w_mem_k"], w["w_mem_v"])


def _rope_tables(pos):
    within = np.arange(LANES) % HEAD_DIM
    freq_of_lane = jnp.asarray(within % ROT_HALF, jnp.int32)
    inv_freq = jnp.power(jnp.float32(ROPE_THETA), -jnp.arange(ROT_HALF, dtype=jnp.float32) / ROT_HALF)
    ang = pos.astype(jnp.float32)[:, None] * inv_freq[freq_of_lane][None, :]
    first = jnp.asarray(within < ROT_HALF)[None, :]
    second = jnp.asarray((within >= ROT_HALF) & (within < 2 * ROT_HALF))[None, :]
    cos, sin = jnp.cos(ang), jnp.sin(ang)
    return (jnp.where(first | second, cos, 1.0), jnp.where(first, -sin, 0.0), jnp.where(second, sin, 0.0))


def _prep_weights(p):
    bf = lambda x: x.astype(BF16)
    vec = lambda x: x.reshape(1, -1)
    return {
        "ffn1_norm": vec(p["ffn1_norm"]), "ffn1_wg": bf(p["ffn1_w_gate"]), "ffn1_wu": bf(p["ffn1_w_up"]),
        "ffn1_wd": bf(p["ffn1_w_down"]),
        "mix_norm": vec(p["mix_norm"]), "w_in": bf(p["w_in"]),
        "w_branch_a": bf(p["w_branch_a"].reshape(N_HEADS_A, HEAD_DIM, D_MODEL)[np.array(HEAD_PERM_A)].reshape(
            Q_A_W, D_MODEL)),
        "w_branch_b": bf(p["w_branch_b"]), "w_out": bf(p["w_out"]),
        "mem_q_norm": vec(p["mem_q_norm"]), "mem_kv_norm": vec(p["mem_kv_norm"]),
        "w_mem_q": bf(p["w_mem_q"]), "w_mem_k": bf(p["w_mem_k"]), "w_mem_v": bf(p["w_mem_v"]),
        "w_mem_o": bf(p["w_mem_o"]),
        "ffn2_norm": vec(p["ffn2_norm"]), "ffn2_wg": bf(p["ffn2_w_gate"]), "ffn2_wu": bf(p["ffn2_w_up"]),
        "ffn2_wd": bf(p["ffn2_w_down"]), "final_norm": vec(p["final_norm"]),
    }


def kernel(x_prompt, x_sample, cache_swa_k, cache_swa_v, cache_dil_k, cache_dil_v, cache_mem_k, cache_mem_v, mem_prompt, ffn1_norm, ffn1_w_gate, ffn1_w_up, ffn1_w_down, mix_norm, w_in, attn_sink, w_branch_a, w_branch_b, w_out, mem_q_norm, mem_kv_norm, w_mem_q, w_mem_k, w_mem_v, w_mem_o, ffn2_norm, ffn2_w_gate, ffn2_w_up, ffn2_w_down, final_norm):
    depth = ffn1_norm.shape[0]
    assert depth == 1
    batch, seq, d = x_prompt.shape
    nb, t_new, _ = x_sample.shape
    assert d == D_MODEL and t_new == SAMPLE_T and seq % (ATTN_CHUNK * DILATIONS[-1]) == 0
    assert cache_swa_k.shape[2] == CACHE_A and cache_dil_k.shape[2] == CACHE_B
    layer = lambda x: x[0]
    w = _prep_weights(dict(
        ffn1_norm=layer(ffn1_norm), ffn1_w_gate=layer(ffn1_w_gate), ffn1_w_up=layer(ffn1_w_up),
        ffn1_w_down=layer(ffn1_w_down), mix_norm=layer(mix_norm), w_in=layer(w_in),
        w_branch_a=layer(w_branch_a), w_branch_b=layer(w_branch_b), w_out=layer(w_out),
        mem_q_norm=layer(mem_q_norm), mem_kv_norm=layer(mem_kv_norm), w_mem_q=layer(w_mem_q),
        w_mem_k=layer(w_mem_k), w_mem_v=layer(w_mem_v), w_mem_o=layer(w_mem_o), ffn2_norm=layer(ffn2_norm),
        ffn2_w_gate=layer(ffn2_w_gate), ffn2_w_up=layer(ffn2_w_up), ffn2_w_down=layer(ffn2_w_down),
        final_norm=final_norm))
    sink = layer(attn_sink)

    ts = nb * t_new
    pos_s = PAST_LEN + (jnp.arange(TOKEN_TILE, dtype=jnp.int32) % t_new)
    tables_s = _rope_tables(pos_s)
    (h_s, q32, ka32_s, va32_s, kb32_s, vb32_s) = _front_call(x_sample.reshape(ts, d), tables_s, w, sample=True)
    by_pos = lambda c: jnp.transpose(layer(c), (0, 2, 3, 1)).reshape(nb, c.shape[3] * c.shape[4], c.shape[2])
    by_slot_head = lambda c: layer(c).reshape(nb, N_MEM * MEM_HEADS, MEM_HEAD_DIM)

    tp = batch * seq
    mk32, mv32, mk16, mv16 = _mem_kv_call(mem_prompt.reshape(batch * N_MEM, d), w)
    tables_p = _rope_tables(jnp.arange(seq, dtype=jnp.int32))
    side = (sink, q32, ka32_s, va32_s, kb32_s, vb32_s,
            by_pos(cache_swa_k), by_pos(cache_swa_v), by_pos(cache_dil_k), by_pos(cache_dil_v))
    (h_p, qa, ka, va, qb0, kb, vb, q4, k4, v4, q16, k16, v16, ka32, va32, kb32, vb32, oa_s, ob_s) = _front_call(
        x_prompt.reshape(tp, d), tables_p, w, sample=False, batch=batch, side=side)
    oa = _attn_a_call(sink, qa, ka, va, batch, seq)
    unsplit = lambda x: x.reshape(batch, 1, seq, KV_B_W)
    groups = ((unsplit(qb0), unsplit(kb), unsplit(vb)), (q4, k4, v4), (q16, k16, v16))
    obs, lses = zip(*[_attn_b_call(*qkv, d) for qkv, d in zip(groups, DILATIONS)])
    y_p = _back_prompt_call(h_p, oa, obs, lses, mk16.reshape(batch, N_MEM, MEM_W),
                            mv16.reshape(batch, N_MEM, MEM_W), w, seq)

    h2_s, qc_s = _back_sample_a_call(h_s, oa_s, ob_s, w)
    oc_s = _cross_sample_call(qc_s, by_slot_head(cache_mem_k), by_slot_head(cache_mem_v))
    y_s = _back_sample_b_call(h2_s, oc_s, w)

    keep_a, keep_b = min(CACHE_A, seq), min(CACHE_B, seq)
    tail = lambda x, keep, heads: x.reshape(batch, seq, heads * HEAD_DIM)[:, seq - keep:].reshape(
        1, batch, keep, heads, HEAD_DIM)
    new = lambda x, heads: x.reshape(1, nb, t_new, heads, HEAD_DIM)
    memo = lambda x: x.reshape(1, batch, N_MEM, MEM_HEADS, MEM_HEAD_DIM)
    return (y_p.reshape(batch, seq, d), y_s.reshape(nb, t_new, d),
            tail(ka32, keep_a, N_KV_A), tail(va32, keep_a, N_KV_A),
            tail(kb32, keep_b, N_KV_B), tail(vb32, keep_b, N_KV_B),
            memo(mk32), memo(mv32),
            new(ka32_s, N_KV_A), new(va32_s, N_KV_A), new(kb32_s, N_KV_B), new(vb32_s, N_KV_B))
```

```python
import functools

import jax
import jax.numpy as jnp
import numpy as np
from jax import lax
from jax.experimental import pallas as pl
from jax.experimental.pallas import tpu as pltpu

F32 = jnp.float32
BF16 = jnp.bfloat16

D_MODEL = 1024
D_FF = 2816
HEAD_DIM = 64
ROT_HALF = 8
ROPE_THETA = 500000.0
ATTN_SCALE = HEAD_DIM ** -0.5
RMS_EPS = 1e-6
PAST_LEN = 16384

N_HEADS_A = 8
N_KV_A = 2
N_KV_B = 4
DILATIONS = (1, 4, 16)
BAND_STEPS = 128
Q_A_W = N_HEADS_A * HEAD_DIM
KV_A_W = N_KV_A * HEAD_DIM
Q_B_W = len(DILATIONS) * N_KV_B * HEAD_DIM
KV_B_W = N_KV_B * HEAD_DIM
QKV_W = Q_A_W + 2 * KV_A_W + Q_B_W + 2 * KV_B_W
MEM_HEADS = 4
MEM_HEAD_DIM = 128
MEM_W = MEM_HEADS * MEM_HEAD_DIM
MEM_SCALE = MEM_HEAD_DIM ** -0.5
N_MEM = 256

HEAD_PERM_A = (0, 4, 1, 5, 2, 6, 3, 7)

LANES = 128
BLOCK = 128
TOKEN_TILE = 512
ROW_PARTS = 2
ATTN_CHUNK = 512
FF_CHUNKS = ((0, 1536), (1536, 2816))
VMEM_LIMIT = 60 * 1024 * 1024
NEG_INF = float("-inf")


def _rms(x, g):
    ms = jnp.mean(x * x, axis=-1, keepdims=True)
    return x * lax.rsqrt(ms + RMS_EPS) * g


def _dot(a, b):
    return jnp.dot(a, b, preferred_element_type=F32)


def _dot_nt(a, b):
    return lax.dot_general(a, b, (((1,), (1,)), ((), ())), preferred_element_type=F32)


def _ffn_half(xs, g_ref, wg_ref, wu_ref, wd_ref):
    us = [_rms(x, g_ref[...]).astype(BF16) for x in xs]
    accs = [None] * len(xs)
    for lo, hi in FF_CHUNKS:
        gates = [_dot(u, wg_ref[:, lo:hi]) for u in us]
        ups = [_dot(u, wu_ref[:, lo:hi]) for u in us]
        acts = [(gate * jax.nn.sigmoid(gate) * up).astype(BF16) for gate, up in zip(gates, ups)]
        for i, act in enumerate(acts):
            part = _dot(act, wd_ref[lo:hi, :])
            accs[i] = part if accs[i] is None else accs[i] + part
    return [x + 0.5 * acc for x, acc in zip(xs, accs)]


def _row_blocks(ref):
    n = ref.shape[0] // ROW_PARTS
    return [slice(i * n, (i + 1) * n) for i in range(ROW_PARTS)]


def _swap_halves(x):
    return pltpu.roll(x, HEAD_DIM, 1)


def _low_half():
    return lax.broadcasted_iota(jnp.int32, (1, LANES), 1) < HEAD_DIM


def _pair_kv_groups(chunks):
    low = _low_half()
    c0, c1, c2, c3 = chunks
    return [jnp.where(low, c0, _swap_halves(c2)), jnp.where(low, _swap_halves(c0), c2),
            jnp.where(low, c1, _swap_halves(c3)), jnp.where(low, _swap_halves(c1), c3)]


def _merge_pairs(per_head):
    low = _low_half()
    return [jnp.where(low, per_head[2 * c], per_head[2 * c + 1]) for c in range(N_HEADS_A // 2)]


def _front_kernel(x_ref, cos_ref, sa_ref, sb_ref, g1_ref, wg_ref, wu_ref, wd_ref, gm_ref, wqkv_ref,
                  h_ref, *rest, sample):
    blocks = _row_blocks(x_ref)
    hs = _ffn_half([x_ref[rb, :] for rb in blocks], g1_ref, wg_ref, wu_ref, wd_ref)
    for rb, h in zip(blocks, hs):
        h_ref[rb, :] = h
    ns = [_rms(h, gm_ref[...]).astype(BF16) for h in hs]
    zs = [_dot(n, wqkv_ref[...]) for n in ns]
    for rb, z in zip(blocks, zs):
        _front_emit(rb, z, cos_ref, sa_ref, sb_ref, rest, sample)


def _front_emit(rb, z, cos_ref, sa_ref, sb_ref, rest, sample):
    cos, sa, sb = cos_ref[rb, :], sa_ref[rb, :], sb_ref[rb, :]

    def chunk(c):
        return z[:, c * LANES:(c + 1) * LANES]

    def rope(c):
        zc = chunk(c)
        return zc * cos + pltpu.roll(zc, LANES - ROT_HALF, 1) * sa + pltpu.roll(zc, ROT_HALF, 1) * sb

    nqa, nka = Q_A_W // LANES, KV_A_W // LANES
    nqb, nkb = Q_B_W // LANES, KV_B_W // LANES
    c0 = 0
    qa = _pair_kv_groups([rope(c0 + c) * ATTN_SCALE for c in range(nqa)])
    c0 += nqa
    ka = [rope(c0 + c) for c in range(nka)]
    c0 += nka
    va = [chunk(c0 + c) for c in range(nka)]
    c0 += nka
    qb = [rope(c0 + c) * ATTN_SCALE for c in range(nqb)]
    c0 += nqb
    kb = [rope(c0 + c) for c in range(nkb)]
    c0 += nkb
    vb = [chunk(c0 + c) for c in range(nkb)]

    def store(ref, parts, dtype):
        for c, p in enumerate(parts):
            ref[rb, c * LANES:(c + 1) * LANES] = p.astype(dtype)

    def fill(scr, parts):
        for c, p in enumerate(parts):
            scr[c, rb, :] = p

    def store_split(ref, scr, d):
        n = (rb.stop - rb.start) // d
        dst = slice(rb.start // d, rb.start // d + n)
        for r in range(d):
            for c in range(scr.shape[0]):
                ref[0, r, dst, c * LANES:(c + 1) * LANES] = scr[c, pl.ds(rb.start + r, n, stride=d), :].astype(BF16)

    if sample:
        q32_ref, ka32_ref, va32_ref, kb32_ref, vb32_ref = rest
        store(q32_ref, qa + qb, F32)
    else:
        (qa_ref, ka_ref, va_ref, qb0_ref, kb_ref, vb_ref, q4_ref, k4_ref, v4_ref, q16_ref, k16_ref, v16_ref,
         ka32_ref, va32_ref, kb32_ref, vb32_ref, q4_s, q16_s, kb_s, vb_s) = rest
        gw = KV_B_W // LANES
        store(qa_ref, qa, BF16)
        store(ka_ref, ka, BF16)
        store(va_ref, va, BF16)
        store(qb0_ref, qb[0:gw], BF16)
        store(kb_ref, kb, BF16)
        store(vb_ref, vb, BF16)
        fill(q4_s, qb[gw:2 * gw])
        fill(q16_s, qb[2 * gw:3 * gw])
        fill(kb_s, kb)
        fill(vb_s, vb)
        store_split(q4_ref, q4_s, DILATIONS[1])
        store_split(q16_ref, q16_s, DILATIONS[2])
        store_split(k4_ref, kb_s, DILATIONS[1])
        store_split(k16_ref, kb_s, DILATIONS[2])
        store_split(v4_ref, vb_s, DILATIONS[1])
        store_split(v16_ref, vb_s, DILATIONS[2])
    store(ka32_ref, ka, F32)
    store(va32_ref, va, F32)
    store(kb32_ref, kb, F32)
    store(vb32_ref, vb, F32)


def _const_spec(shape, index=None):
    index = (0,) * len(shape) if index is None else index
    return pl.BlockSpec(shape, lambda *_: index, pipeline_mode=pl.Buffered(1))


def _front_call(x, tables, w, *, sample, batch=1):
    t = x.shape[0]
    tm = TOKEN_TILE
    nsteps = t // tm
    tpb = nsteps // batch
    seq = t // batch
    cos_t, sa_t, sb_t = tables
    tab_blocks = cos_t.shape[0] // tm
    row = lambda i: (i, 0)
    tab = lambda i: (i % tab_blocks, 0)

    def tok(width, dtype):
        return jax.ShapeDtypeStruct((t, width), dtype), pl.BlockSpec((tm, width), row)

    def split(d):
        return (jax.ShapeDtypeStruct((batch, d, seq // d, KV_B_W), BF16),
                pl.BlockSpec((1, d, tm // d, KV_B_W), lambda i: (i // tpb, 0, i % tpb, 0)))

    outs = [tok(D_MODEL, F32)]
    scratch = []
    if sample:
        outs += [tok(Q_A_W + Q_B_W, F32)]
    else:
        outs += [tok(Q_A_W, BF16), tok(KV_A_W, BF16), tok(KV_A_W, BF16),
                 tok(KV_B_W, BF16), tok(KV_B_W, BF16), tok(KV_B_W, BF16)]
        outs += [split(DILATIONS[1])] * 3 + [split(DILATIONS[2])] * 3
        scratch = [pltpu.VMEM((KV_B_W // LANES, tm, LANES), F32)] * 4
    outs += [tok(KV_A_W, F32), tok(KV_A_W, F32), tok(KV_B_W, F32), tok(KV_B_W, F32)]
    out_shape, out_specs = zip(*outs)
    in_specs = [pl.BlockSpec((tm, D_MODEL), row),
                pl.BlockSpec((tm, LANES), tab), pl.BlockSpec((tm, LANES), tab), pl.BlockSpec((tm, LANES), tab),
                _const_spec((1, D_MODEL)), _const_spec((D_MODEL, D_FF)), _const_spec((D_MODEL, D_FF)),
                _const_spec((D_FF, D_MODEL)), _const_spec((1, D_MODEL)), _const_spec((D_MODEL, QKV_W), (0, 0))]
    return pl.pallas_call(
        functools.partial(_front_kernel, sample=sample),
        grid=(nsteps,),
        in_specs=in_specs,
        out_specs=list(out_specs),
        out_shape=list(out_shape),
        scratch_shapes=scratch,
        compiler_params=pltpu.CompilerParams(dimension_semantics=("parallel",), vmem_limit_bytes=VMEM_LIMIT),
        name="front_sample" if sample else "front_prompt",
    )(x, cos_t, sa_t, sb_t, w["ffn1_norm"], w["ffn1_wg"], w["ffn1_wu"], w["ffn1_wd"], w["mix_norm"], w["w_in"])


def _lane_block_masks(width, block, dtype):
    lane = lax.broadcasted_iota(jnp.int32, (1, width), 1)
    return [((lane >= i * block) & (lane < (i + 1) * block)).astype(dtype) for i in range(width // block)]


def _band_mask():
    qi = lax.broadcasted_iota(jnp.int32, (BLOCK, 2 * BLOCK), 0)
    kj = lax.broadcasted_iota(jnp.int32, (BLOCK, 2 * BLOCK), 1)
    dist = qi + BLOCK - kj
    return (dist >= 0) & (dist <= BAND_STEPS), kj >= BLOCK


def _band_softmax_pv(qbd, kcat, vcat, mask, nheads):
    s = _dot_nt(qbd, kcat)
    ps, ms, ls = [], [], []
    for p in range(nheads):
        sp = jnp.where(mask, s[p * BLOCK:(p + 1) * BLOCK], NEG_INF)
        m = jnp.max(sp, axis=-1, keepdims=True)
        e = jnp.exp(sp - m)
        ls.append(jnp.sum(e, axis=-1, keepdims=True))
        ms.append(m)
        ps.append(e.astype(BF16))
    o = _dot(jnp.concatenate(ps, axis=0), vcat)
    return o, ms, ls


def _kv_window(cur_ref, prev_ref, qb):
    if qb == 0:
        return jnp.concatenate([prev_ref[...], cur_ref[0:BLOCK]], axis=0)
    return cur_ref[(qb - 1) * BLOCK:(qb + 1) * BLOCK]


def _attn_a_kernel(sink_ref, q_ref, kc_ref, kp_ref, vc_ref, vp_ref, o_ref):
    first_chunk = pl.program_id(1) == 0
    band, in_cur = _band_mask()
    half_bf = _lane_block_masks(LANES, HEAD_DIM, BF16)
    for qb in range(ATTN_CHUNK // BLOCK):
        mask = band & (in_cur | jnp.logical_not(first_chunk)) if qb == 0 else band
        kcat = _kv_window(kc_ref, kp_ref, qb)
        vcat = _kv_window(vc_ref, vp_ref, qb)
        rows = slice(qb * BLOCK, (qb + 1) * BLOCK)
        qbd = jnp.concatenate(
            [q_ref[rows, (p // 2) * LANES:(p // 2 + 1) * LANES] * half_bf[p % 2] for p in range(N_HEADS_A)], axis=0)
        o, ms, ls = _band_softmax_pv(qbd, kcat, vcat, mask, N_HEADS_A)
        normed = []
        for p in range(N_HEADS_A):
            den = ls[p] + jnp.exp(sink_ref[HEAD_PERM_A[p]] - ms[p])
            normed.append(o[p * BLOCK:(p + 1) * BLOCK] * (1.0 / den))
        for c, pair in enumerate(_merge_pairs(normed)):
            o_ref[rows, c * LANES:(c + 1) * LANES] = pair.astype(BF16)


def _attn_a_call(sink, qa, ka, va, batch, seq):
    t = qa.shape[0]
    cpb = seq // ATTN_CHUNK
    bpc = ATTN_CHUNK // BLOCK
    cur = lambda b, c: (b * cpb + c, 0)
    prev = lambda b, c: (b * cpb * bpc + jnp.maximum(c * bpc - 1, 0), 0)
    return pl.pallas_call(
        _attn_a_kernel,
        grid=(batch, cpb),
        in_specs=[pl.BlockSpec(memory_space=pltpu.SMEM),
                  pl.BlockSpec((ATTN_CHUNK, Q_A_W), cur),
                  pl.BlockSpec((ATTN_CHUNK, KV_A_W), cur), pl.BlockSpec((BLOCK, KV_A_W), prev),
                  pl.BlockSpec((ATTN_CHUNK, KV_A_W), cur), pl.BlockSpec((BLOCK, KV_A_W), prev)],
        out_specs=pl.BlockSpec((ATTN_CHUNK, Q_A_W), cur),
        out_shape=jax.ShapeDtypeStruct((t, Q_A_W), BF16),
        compiler_params=pltpu.CompilerParams(dimension_semantics=("parallel", "parallel"),
                                             vmem_limit_bytes=VMEM_LIMIT),
        name="attn_a_prompt",
    )(sink, qa, ka, ka, va, va)


def _attn_b_kernel(q_ref, kc_ref, kp_ref, vc_ref, vp_ref, o_ref, lse_ref):
    first_chunk = pl.program_id(2) == 0
    band, in_cur = _band_mask()
    head_bf = _lane_block_masks(KV_B_W, HEAD_DIM, BF16)
    lane = lax.broadcasted_iota(jnp.int32, (1, KV_B_W), 1)
    for qb in range(ATTN_CHUNK // BLOCK):
        mask = band & (in_cur | jnp.logical_not(first_chunk)) if qb == 0 else band
        kcat = _kv_window(kc_ref, kp_ref, qb)
        vcat = _kv_window(vc_ref, vp_ref, qb)
        rows = slice(qb * BLOCK, (qb + 1) * BLOCK)
        q = q_ref[rows, :]
        qbd = jnp.concatenate([q * head_bf[p] for p in range(N_KV_B)], axis=0)
        o, ms, ls = _band_softmax_pv(qbd, kcat, vcat, mask, N_KV_B)
        out = lse = None
        for p in reversed(range(N_KV_B)):
            op = o[p * BLOCK:(p + 1) * BLOCK] * (1.0 / ls[p])
            lp = jnp.broadcast_to(ms[p] + jnp.log(ls[p]), (BLOCK, KV_B_W))
            if out is None:
                out, lse = op, lp
            else:
                sel = lane < (p + 1) * HEAD_DIM
                out, lse = jnp.where(sel, op, out), jnp.where(sel, lp, lse)
        o_ref[rows, :] = out
        lse_ref[rows, :] = lse


def _attn_b_call(q, k, v, d):
    batch, _, rows, _ = q.shape
    cpb = rows // ATTN_CHUNK
    bpc = ATTN_CHUNK // BLOCK
    cur = pl.BlockSpec((None, None, ATTN_CHUNK, KV_B_W), lambda b, r, c: (b, r, c, 0))
    prev = pl.BlockSpec((None, None, BLOCK, KV_B_W), lambda b, r, c: (b, r, jnp.maximum(c * bpc - 1, 0), 0))
    return pl.pallas_call(
        _attn_b_kernel,
        grid=(batch, d, cpb),
        in_specs=[cur, cur, prev, cur, prev],
        out_specs=[cur, cur],
        out_shape=[jax.ShapeDtypeStruct((batch, d, rows, KV_B_W), F32)] * 2,
        compiler_params=pltpu.CompilerParams(dimension_semantics=("parallel", "parallel", "parallel"),
                                             vmem_limit_bytes=VMEM_LIMIT),
        name=f"attn_b_prompt_d{d}",
    )(q, k, k, v, v)


SAMPLE_T = 8
CACHE_A = 128
CACHE_B = 2048
SAMPLE_ELEMS = 4


def _new_rows_block(new):
    return jnp.concatenate([new, jnp.zeros((BLOCK - SAMPLE_T, new.shape[1]), F32)], axis=0)


def _cached_scores(qbd, kt, knew):
    return _dot(qbd, kt), _dot_nt(qbd, knew)


def _cached_softmax(s_c, s_n, mask_c, mask_n):
    s_c = jnp.where(mask_c, s_c, NEG_INF)
    s_n = jnp.where(mask_n, s_n, NEG_INF)
    m = jnp.maximum(jnp.max(s_c, axis=-1, keepdims=True), jnp.max(s_n, axis=-1, keepdims=True))
    e_c = jnp.exp(s_c - m)
    e_n = jnp.exp(s_n - m)
    l = jnp.sum(e_c, axis=-1, keepdims=True) + jnp.sum(e_n, axis=-1, keepdims=True)
    return e_c, e_n, m, l


def _cached_pv(e_c, e_n, vt, vnew):
    return _dot_nt(e_c, vt) + _dot(e_n, vnew)


def _attn_sample_kernel(sink_ref, q_ref, kan_ref, van_ref, kbn_ref, vbn_ref, cka_ref, cva_ref, ckb_ref, cvb_ref,
                        oa_ref, ob_ref):
    half_f = _lane_block_masks(LANES, HEAD_DIM, F32)
    lane_b = lax.broadcasted_iota(jnp.int32, (1, KV_B_W), 1)

    def dist(nrows, ncols, offset):
        r = lax.broadcasted_iota(jnp.int32, (nrows, ncols), 0)
        i = lax.broadcasted_iota(jnp.int32, (nrows, ncols), 1)
        return r, offset + (r & (SAMPLE_T - 1)) - i

    nrow_a = N_HEADS_A * SAMPLE_T
    _, dac = dist(nrow_a, CACHE_A, CACHE_A)
    _, dan = dist(nrow_a, BLOCK, 0)
    mask_ac = (dac >= 0) & (dac <= BAND_STEPS)
    mask_an = (dan >= 0) & (dan <= BAND_STEPS)
    nrow_b = len(DILATIONS) * N_KV_B * SAMPLE_T
    rows_per_group = N_KV_B * SAMPLE_T

    def mask_b(ncols, offset):
        r, db = dist(nrow_b, ncols, offset)
        dil = jnp.where(r < rows_per_group, DILATIONS[0], jnp.where(r < 2 * rows_per_group, DILATIONS[1], DILATIONS[2]))
        return (db >= 0) & (db <= BAND_STEPS * dil) & ((db & (dil - 1)) == 0)

    mask_bc = mask_b(CACHE_B, CACHE_B)
    mask_bn = mask_b(BLOCK, 0)

    sink_col = jnp.concatenate(
        [jnp.full((SAMPLE_T, 1), sink_ref[HEAD_PERM_A[p]], F32) for p in range(N_HEADS_A)], axis=0)

    def element(j):
        rows = slice(j * SAMPLE_T, (j + 1) * SAMPLE_T)
        qbd_a = jnp.concatenate(
            [q_ref[rows, (p // 2) * LANES:(p // 2 + 1) * LANES] * half_f[p % 2] for p in range(N_HEADS_A)],
            axis=0)
        qrows = []
        for g in range(len(DILATIONS)):
            for kvh in range(N_KV_B):
                c = (Q_A_W // LANES) + 2 * g + kvh // 2
                part = q_ref[rows, c * LANES:(c + 1) * LANES] * half_f[kvh % 2]
                zero = jnp.zeros_like(part)
                qrows.append(jnp.concatenate([part, zero] if kvh // 2 == 0 else [zero, part], axis=1))
        qbd_b = jnp.concatenate(qrows, axis=0)
        scores_a = _cached_scores(qbd_a, cka_ref[j], _new_rows_block(kan_ref[rows, :]))
        scores_b = _cached_scores(qbd_b, ckb_ref[j], _new_rows_block(kbn_ref[rows, :]))
        yield
        ea_c, ea_n, ma, la = _cached_softmax(*scores_a, mask_ac, mask_an)
        eb_c, eb_n, m, l = _cached_softmax(*scores_b, mask_bc, mask_bn)
        yield
        o = _cached_pv(ea_c, ea_n, cva_ref[j], _new_rows_block(van_ref[rows, :]))
        o = o * (1.0 / (la + jnp.exp(sink_col - ma)))
        per_head = [o[p * SAMPLE_T:(p + 1) * SAMPLE_T] for p in range(N_HEADS_A)]
        for c, pair in enumerate(_merge_pairs(per_head)):
            oa_ref[rows, c * LANES:(c + 1) * LANES] = pair
        o = _cached_pv(eb_c, eb_n, cvb_ref[j], _new_rows_block(vbn_ref[rows, :]))
        out = None
        for kvh in reversed(range(N_KV_B)):
            sl = [slice((g * N_KV_B + kvh) * SAMPLE_T, (g * N_KV_B + kvh + 1) * SAMPLE_T)
                  for g in range(len(DILATIONS))]
            mj = jnp.maximum(jnp.maximum(m[sl[0]], m[sl[1]]), m[sl[2]])
            ws = [jnp.exp(m[x] - mj) for x in sl]
            den = ws[0] * l[sl[0]] + ws[1] * l[sl[1]] + ws[2] * l[sl[2]]
            num = ws[0] * o[sl[0]] + ws[1] * o[sl[1]] + ws[2] * o[sl[2]]
            okv = num * (1.0 / den)
            out = okv if out is None else jnp.where(lane_b < (kvh + 1) * HEAD_DIM, okv, out)
        ob_ref[rows, :] = out
        yield

    elements = [element(j) for j in range(SAMPLE_ELEMS)]
    for _ in range(3):
        for e in elements:
            next(e)


def _attn_sample_call(sink, q32, ka32, va32, kb32, vb32, cka, cva, ckb, cvb):
    nb = cka.shape[0]
    bt = SAMPLE_ELEMS
    t = q32.shape[0]
    tok = lambda w: pl.BlockSpec((bt * SAMPLE_T, w), lambda i: (i, 0))
    cache = lambda w, n: pl.BlockSpec((bt, w, n), lambda i: (i, 0, 0))
    return pl.pallas_call(
        _attn_sample_kernel,
        grid=(nb // bt,),
        in_specs=[pl.BlockSpec(memory_space=pltpu.SMEM), tok(Q_A_W + Q_B_W),
                  tok(KV_A_W), tok(KV_A_W), tok(KV_B_W), tok(KV_B_W),
                  cache(KV_A_W, CACHE_A), cache(KV_A_W, CACHE_A), cache(KV_B_W, CACHE_B), cache(KV_B_W, CACHE_B)],
        out_specs=[tok(Q_A_W), tok(KV_B_W)],
        out_shape=[jax.ShapeDtypeStruct((t, Q_A_W), F32), jax.ShapeDtypeStruct((t, KV_B_W), F32)],
        compiler_params=pltpu.CompilerParams(dimension_semantics=("parallel",), vmem_limit_bytes=VMEM_LIMIT),
        name="attn_sample",
    )(sink, q32, ka32, va32, kb32, vb32, cka, cva, ckb, cvb)


def _mix_out(h, oa, ob, gm_ref, wgate_ref, wba_ref, wbb_ref, wout_ref):
    n = _rms(h, gm_ref[...]).astype(BF16)
    gates = _dot(n, wgate_ref[...])
    ga = jax.nn.sigmoid(gates[:, :D_MODEL])
    gb = jax.nn.sigmoid(gates[:, D_MODEL:])
    mixed = ga * _dot(oa.astype(BF16), wba_ref[...]) + gb * _dot(ob.astype(BF16), wbb_ref[...])
    return h + _dot(mixed.astype(BF16), wout_ref[...])


def _combine_groups(os, ls):
    m = jnp.maximum(jnp.maximum(ls[0], ls[1]), ls[2])
    es = [jnp.exp(x - m) for x in ls]
    den = es[0] + es[1] + es[2]
    num = es[0] * os[0] + es[1] * os[1] + es[2] * os[2]
    return num * (1.0 / den)


def _merge_split(split_ref, scr, rb):
    d = split_ref.shape[1]
    n = (rb.stop - rb.start) // d
    src = slice(rb.start // d, rb.start // d + n)
    for r in range(d):
        for c in range(scr.shape[0]):
            scr[c, pl.ds(rb.start + r, n, stride=d), :] = split_ref[0, r, src, c * LANES:(c + 1) * LANES]
    return jnp.concatenate([scr[c, rb, :] for c in range(scr.shape[0])], axis=1)


def _softmax_rows(s):
    m = jnp.max(s, axis=-1, keepdims=True)
    e = jnp.exp(s - m)
    return e * (1.0 / jnp.sum(e, axis=-1, keepdims=True))


def _back_prompt_kernel(h_ref, oa_ref, o0_ref, o1_ref, o2_ref, l0_ref, l1_ref, l2_ref, mk_ref, mv_ref,
                        gm_ref, wgate_ref, wba_ref, wbb_ref, wout_ref, gq_ref, wq_ref, wo_ref,
                        g2_ref, wg_ref, wu_ref, wd_ref, gf_ref, y_ref, o1_s, o2_s, l1_s, l2_s):
    blocks = _row_blocks(h_ref)
    h3s = []
    for rb in blocks:
        ob = _combine_groups(
            (o0_ref[0, 0, rb, :], _merge_split(o1_ref, o1_s, rb), _merge_split(o2_ref, o2_s, rb)),
            (l0_ref[0, 0, rb, :], _merge_split(l1_ref, l1_s, rb), _merge_split(l2_ref, l2_s, rb)))
        h2 = _mix_out(h_ref[rb, :], oa_ref[rb, :], ob, gm_ref, wgate_ref, wba_ref, wbb_ref, wout_ref)
        q = _dot(_rms(h2, gq_ref[...]).astype(BF16), wq_ref[...])
        heads = []
        for hd in range(MEM_HEADS):
            cols = slice(hd * MEM_HEAD_DIM, (hd + 1) * MEM_HEAD_DIM)
            p = _softmax_rows(_dot_nt(q[:, cols].astype(BF16), mk_ref[0, :, cols]) * MEM_SCALE)
            heads.append(_dot(p.astype(BF16), mv_ref[0, :, cols]))
        oc = jnp.concatenate(heads, axis=1).astype(BF16)
        h3s.append(h2 + _dot(oc, wo_ref[...]))
    for rb, y in zip(blocks, _ffn_half(h3s, g2_ref, wg_ref, wu_ref, wd_ref)):
        y_ref[rb, :] = _rms(y, gf_ref[...])


def _back_prompt_call(h, oa, obs, lses, mk, mv, w, seq):
    t = h.shape[0]
    tm = TOKEN_TILE
    tiles_per_batch = seq // tm
    row = lambda i: (i, 0)
    mem = lambda i: (i // tiles_per_batch, 0, 0)
    tok = lambda width: pl.BlockSpec((tm, width), row)
    split = lambda d: pl.BlockSpec((1, d, tm // d, KV_B_W),
                                   lambda i: (i // tiles_per_batch, 0, i % tiles_per_batch, 0))
    groups = [split(d) for d in DILATIONS]
    in_specs = ([tok(D_MODEL), tok(Q_A_W)] + groups + groups
                + [pl.BlockSpec((1, N_MEM, MEM_W), mem), pl.BlockSpec((1, N_MEM, MEM_W), mem)]
                + [_const_spec((1, D_MODEL)), _const_spec((D_MODEL, 2 * D_MODEL), (0, 1)), _const_spec((Q_A_W, D_MODEL)),
                   _const_spec((KV_B_W, D_MODEL)), _const_spec((D_MODEL, D_MODEL)),
                   _const_spec((1, D_MODEL)), _const_spec((D_MODEL, MEM_W)), _const_spec((MEM_W, D_MODEL)),
                   _const_spec((1, D_MODEL)), _const_spec((D_MODEL, D_FF)), _const_spec((D_MODEL, D_FF)),
                   _const_spec((D_FF, D_MODEL)), _const_spec((1, D_MODEL))])
    return pl.pallas_call(
        _back_prompt_kernel,
        grid=(t // tm,),
        in_specs=in_specs,
        out_specs=tok(D_MODEL),
        out_shape=jax.ShapeDtypeStruct((t, D_MODEL), F32),
        scratch_shapes=[pltpu.VMEM((KV_B_W // LANES, tm, LANES), F32)] * 4,
        compiler_params=pltpu.CompilerParams(dimension_semantics=("parallel",), vmem_limit_bytes=VMEM_LIMIT),
        name="back_prompt",
    )(h, oa, *obs, *lses, mk, mv, w["mix_norm"], w["w_in"], w["w_branch_a"], w["w_branch_b"], w["w_out"],
      w["mem_q_norm"], w["w_mem_q"], w["w_mem_o"], w["ffn2_norm"], w["ffn2_wg"], w["ffn2_wu"], w["ffn2_wd"],
      w["final_norm"])


def _back_sample_a_kernel(h_ref, oa_ref, ob_ref, gm_ref, wgate_ref, wba_ref, wbb_ref, wout_ref, gq_ref, wq_ref,
                          h2_ref, q_ref):
    h2 = _mix_out(h_ref[...], oa_ref[...], ob_ref[...], gm_ref, wgate_ref, wba_ref, wbb_ref, wout_ref)
    h2_ref[...] = h2
    q_ref[...] = _dot(_rms(h2, gq_ref[...]).astype(BF16), wq_ref[...])


def _back_sample_a_call(h, oa, ob, w):
    t = h.shape[0]
    tm = TOKEN_TILE
    tok = lambda width: pl.BlockSpec((tm, width), lambda i: (i, 0))
    return pl.pallas_call(
        _back_sample_a_kernel,
        grid=(t // tm,),
        in_specs=[tok(D_MODEL), tok(Q_A_W), tok(KV_B_W),
                  _const_spec((1, D_MODEL)), _const_spec((D_MODEL, 2 * D_MODEL), (0, 1)), _const_spec((Q_A_W, D_MODEL)),
                  _const_spec((KV_B_W, D_MODEL)), _const_spec((D_MODEL, D_MODEL)),
                  _const_spec((1, D_MODEL)), _const_spec((D_MODEL, MEM_W))],
        out_specs=[tok(D_MODEL), tok(MEM_W)],
        out_shape=[jax.ShapeDtypeStruct((t, D_MODEL), F32), jax.ShapeDtypeStruct((t, MEM_W), F32)],
        compiler_params=pltpu.CompilerParams(dimension_semantics=("parallel",), vmem_limit_bytes=VMEM_LIMIT),
        name="back_sample_mix",
    )(h, oa, ob, w["mix_norm"], w["w_in"], w["w_branch_a"], w["w_branch_b"], w["w_out"],
      w["mem_q_norm"], w["w_mem_q"])


CROSS_BATCH_TILE = 8


def _cross_sample_kernel(q_ref, mk_ref, mv_ref, o_ref):
    nrow = MEM_HEADS * SAMPLE_T
    qhead = lax.broadcasted_iota(jnp.int32, (nrow, N_MEM * MEM_HEADS), 0) >> (SAMPLE_T.bit_length() - 1)
    khead = lax.broadcasted_iota(jnp.int32, (nrow, N_MEM * MEM_HEADS), 1) & (MEM_HEADS - 1)
    own_head = qhead == khead
    for bi in range(CROSS_BATCH_TILE):
        rows = slice(bi * SAMPLE_T, (bi + 1) * SAMPLE_T)
        qs = jnp.concatenate(
            [q_ref[rows, hd * MEM_HEAD_DIM:(hd + 1) * MEM_HEAD_DIM] for hd in range(MEM_HEADS)], axis=0)
        s = _dot_nt(qs.astype(BF16), mk_ref[bi].astype(BF16)) * MEM_SCALE
        p = _softmax_rows(jnp.where(own_head, s, NEG_INF))
        o = _dot(p.astype(BF16), mv_ref[bi].astype(BF16))
        for hd in range(MEM_HEADS):
            o_ref[rows, hd * MEM_HEAD_DIM:(hd + 1) * MEM_HEAD_DIM] = o[hd * SAMPLE_T:(hd + 1) * SAMPLE_T]


def _cross_sample_call(q, mk, mv):
    t = q.shape[0]
    bt = CROSS_BATCH_TILE
    tok = pl.BlockSpec((bt * SAMPLE_T, MEM_W), lambda i: (i, 0))
    mem = pl.BlockSpec((bt, N_MEM * MEM_HEADS, MEM_HEAD_DIM), lambda i: (i, 0, 0))
    return pl.pallas_call(
        _cross_sample_kernel,
        grid=(mk.shape[0] // bt,),
        in_specs=[tok, mem, mem],
        out_specs=tok,
        out_shape=jax.ShapeDtypeStruct((t, MEM_W), F32),
        compiler_params=pltpu.CompilerParams(dimension_semantics=("parallel",), vmem_limit_bytes=VMEM_LIMIT),
        name="cross_sample",
    )(q, mk, mv)


def _back_sample_b_kernel(h2_ref, oc_ref, wo_ref, g2_ref, wg_ref, wu_ref, wd_ref, gf_ref, y_ref):
    blocks = _row_blocks(h2_ref)
    h3s = [h2_ref[rb, :] + _dot(oc_ref[rb, :].astype(BF16), wo_ref[...]) for rb in blocks]
    for rb, y in zip(blocks, _ffn_half(h3s, g2_ref, wg_ref, wu_ref, wd_ref)):
        y_ref[rb, :] = _rms(y, gf_ref[...])


def _back_sample_b_call(h2, oc, w):
    t = h2.shape[0]
    tm = TOKEN_TILE
    tok = lambda width: pl.BlockSpec((tm, width), lambda i: (i, 0))
    return pl.pallas_call(
        _back_sample_b_kernel,
        grid=(t // tm,),
        in_specs=[tok(D_MODEL), tok(MEM_W), _const_spec((MEM_W, D_MODEL)),
                  _const_spec((1, D_MODEL)), _const_spec((D_MODEL, D_FF)), _const_spec((D_MODEL, D_FF)),
                  _const_spec((D_FF, D_MODEL)), _const_spec((1, D_MODEL))],
        out_specs=tok(D_MODEL),
        out_shape=jax.ShapeDtypeStruct((t, D_MODEL), F32),
        compiler_params=pltpu.CompilerParams(dimension_semantics=("parallel",), vmem_limit_bytes=VMEM_LIMIT),
        name="back_sample_ffn",
    )(h2, oc, w["w_mem_o"], w["ffn2_norm"], w["ffn2_wg"], w["ffn2_wu"], w["ffn2_wd"], w["final_norm"])


def _mem_kv_kernel(mem_ref, g_ref, wk_ref, wv_ref, k32_ref, v32_ref, k16_ref, v16_ref):
    u = _rms(mem_ref[...], g_ref[...]).astype(BF16)
    k = _dot(u, wk_ref[...])
    v = _dot(u, wv_ref[...])
    k32_ref[...] = k
    v32_ref[...] = v
    k16_ref[...] = k.astype(BF16)
    v16_ref[...] = v.astype(BF16)


def _mem_kv_call(mem, w):
    rows = mem.shape[0]
    full = lambda width: pl.BlockSpec((rows, width), lambda i: (0, 0))
    return pl.pallas_call(
        _mem_kv_kernel,
        grid=(1,),
        in_specs=[full(D_MODEL), _const_spec((1, D_MODEL)), _const_spec((D_MODEL, MEM_W)),
                  _const_spec((D_MODEL, MEM_W))],
        out_specs=[full(MEM_W)] * 4,
        out_shape=[jax.ShapeDtypeStruct((rows, MEM_W), F32)] * 2 + [jax.ShapeDtypeStruct((rows, MEM_W), BF16)] * 2,
        compiler_params=pltpu.CompilerParams(dimension_semantics=("arbitrary",), vmem_limit_bytes=VMEM_LIMIT),
        name="mem_kv",
    )(mem, w["mem_kv_norm"], w["w_mem_k"], w["w_mem_v"])


def _rope_tables(pos):
    within = np.arange(LANES) % HEAD_DIM
    freq_of_lane = jnp.asarray(within % ROT_HALF, jnp.int32)
    inv_freq = jnp.power(jnp.float32(ROPE_THETA), -jnp.arange(ROT_HALF, dtype=jnp.float32) / ROT_HALF)
    ang = pos.astype(jnp.float32)[:, None] * inv_freq[freq_of_lane][None, :]
    first = jnp.asarray(within < ROT_HALF)[None, :]
    second = jnp.asarray((within >= ROT_HALF) & (within < 2 * ROT_HALF))[None, :]
    cos, sin = jnp.cos(ang), jnp.sin(ang)
    return (jnp.where(first | second, cos, 1.0), jnp.where(first, -sin, 0.0), jnp.where(second, sin, 0.0))


def _prep_weights(p):
    bf = lambda x: x.astype(BF16)
    vec = lambda x: x.reshape(1, -1)
    return {
        "ffn1_norm": vec(p["ffn1_norm"]), "ffn1_wg": bf(p["ffn1_w_gate"]), "ffn1_wu": bf(p["ffn1_w_up"]),
        "ffn1_wd": bf(p["ffn1_w_down"]),
        "mix_norm": vec(p["mix_norm"]), "w_in": bf(p["w_in"]),
        "w_branch_a": bf(p["w_branch_a"].reshape(N_HEADS_A, HEAD_DIM, D_MODEL)[np.array(HEAD_PERM_A)].reshape(
            Q_A_W, D_MODEL)),
        "w_branch_b": bf(p["w_branch_b"]), "w_out": bf(p["w_out"]),
        "mem_q_norm": vec(p["mem_q_norm"]), "mem_kv_norm": vec(p["mem_kv_norm"]),
        "w_mem_q": bf(p["w_mem_q"]), "w_mem_k": bf(p["w_mem_k"]), "w_mem_v": bf(p["w_mem_v"]),
        "w_mem_o": bf(p["w_mem_o"]),
        "ffn2_norm": vec(p["ffn2_norm"]), "ffn2_wg": bf(p["ffn2_w_gate"]), "ffn2_wu": bf(p["ffn2_w_up"]),
        "ffn2_wd": bf(p["ffn2_w_down"]), "final_norm": vec(p["final_norm"]),
    }


def kernel(x_prompt, x_sample, cache_swa_k, cache_swa_v, cache_dil_k, cache_dil_v, cache_mem_k, cache_mem_v, mem_prompt, ffn1_norm, ffn1_w_gate, ffn1_w_up, ffn1_w_down, mix_norm, w_in, attn_sink, w_branch_a, w_branch_b, w_out, mem_q_norm, mem_kv_norm, w_mem_q, w_mem_k, w_mem_v, w_mem_o, ffn2_norm, ffn2_w_gate, ffn2_w_up, ffn2_w_down, final_norm):
    depth = ffn1_norm.shape[0]
    assert depth == 1
    batch, seq, d = x_prompt.shape
    nb, t_new, _ = x_sample.shape
    assert d == D_MODEL and t_new == SAMPLE_T and seq % (ATTN_CHUNK * DILATIONS[-1]) == 0
    assert cache_swa_k.shape[2] == CACHE_A and cache_dil_k.shape[2] == CACHE_B
    layer = lambda x: x[0]
    w = _prep_weights(dict(
        ffn1_norm=layer(ffn1_norm), ffn1_w_gate=layer(ffn1_w_gate), ffn1_w_up=layer(ffn1_w_up),
        ffn1_w_down=layer(ffn1_w_down), mix_norm=layer(mix_norm), w_in=layer(w_in),
        w_branch_a=layer(w_branch_a), w_branch_b=layer(w_branch_b), w_out=layer(w_out),
        mem_q_norm=layer(mem_q_norm), mem_kv_norm=layer(mem_kv_norm), w_mem_q=layer(w_mem_q),
        w_mem_k=layer(w_mem_k), w_mem_v=layer(w_mem_v), w_mem_o=layer(w_mem_o), ffn2_norm=layer(ffn2_norm),
        ffn2_w_gate=layer(ffn2_w_gate), ffn2_w_up=layer(ffn2_w_up), ffn2_w_down=layer(ffn2_w_down),
        final_norm=final_norm))
    sink = layer(attn_sink)

    ts = nb * t_new
    pos_s = PAST_LEN + (jnp.arange(TOKEN_TILE, dtype=jnp.int32) % t_new)
    tables_s = _rope_tables(pos_s)
    (h_s, q32, ka32_s, va32_s, kb32_s, vb32_s) = _front_call(x_sample.reshape(ts, d), tables_s, w, sample=True)
    by_pos = lambda c: jnp.transpose(layer(c), (0, 2, 3, 1)).reshape(nb, c.shape[3] * c.shape[4], c.shape[2])
    by_slot_head = lambda c: layer(c).reshape(nb, N_MEM * MEM_HEADS, MEM_HEAD_DIM)

    oa_s, ob_s = _attn_sample_call(
        sink, q32, ka32_s, va32_s, kb32_s, vb32_s,
        by_pos(cache_swa_k), by_pos(cache_swa_v), by_pos(cache_dil_k), by_pos(cache_dil_v))

    tp = batch * seq
    mk32, mv32, mk16, mv16 = _mem_kv_call(mem_prompt.reshape(batch * N_MEM, d), w)
    tables_p = _rope_tables(jnp.arange(seq, dtype=jnp.int32))
    (h_p, qa, ka, va, qb0, kb, vb, q4, k4, v4, q16, k16, v16, ka32, va32, kb32, vb32) = _front_call(
        x_prompt.reshape(tp, d), tables_p, w, sample=False, batch=batch)
    oa = _attn_a_call(sink, qa, ka, va, batch, seq)
    unsplit = lambda x: x.reshape(batch, 1, seq, KV_B_W)
    groups = ((unsplit(qb0), unsplit(kb), unsplit(vb)), (q4, k4, v4), (q16, k16, v16))
    obs, lses = zip(*[_attn_b_call(*qkv, d) for qkv, d in zip(groups, DILATIONS)])
    y_p = _back_prompt_call(h_p, oa, obs, lses, mk16.reshape(batch, N_MEM, MEM_W),
                            mv16.reshape(batch, N_MEM, MEM_W), w, seq)

    h2_s, qc_s = _back_sample_a_call(h_s, oa_s, ob_s, w)
    oc_s = _cross_sample_call(qc_s, by_slot_head(cache_mem_k), by_slot_head(cache_mem_v))
    y_s = _back_sample_b_call(h2_s, oc_s, w)

    keep_a, keep_b = min(CACHE_A, seq), min(CACHE_B, seq)
    tail = lambda x, keep, heads: x.reshape(batch, seq, heads * HEAD_DIM)[:, seq - keep:].reshape(
        1, batch, keep, heads, HEAD_DIM)
    new = lambda x, heads: x.reshape(1, nb, t_new, heads, HEAD_DIM)
    memo = lambda x: x.reshape(1, batch, N_MEM, MEM_HEADS, MEM_HEAD_DIM)
    return (y_p.reshape(batch, seq, d), y_s.reshape(nb, t_new, d),
            tail(ka32, keep_a, N_KV_A), tail(va32, keep_a, N_KV_A),
            tail(kb32, keep_b, N_KV_B), tail(vb32, keep_b, N_KV_B),
            memo(mk32), memo(mv32),
            new(ka32_s, N_KV_A), new(va32_s, N_KV_A), new(kb32_s, N_KV_B), new(vb32_s, N_KV_B))
```

```python
import functools

import jax
import jax.numpy as jnp
import numpy as np
from jax import lax
from jax.experimental import pallas as pl
from jax.experimental.pallas import tpu as pltpu

F32 = jnp.float32
BF16 = jnp.bfloat16

D_MODEL = 1024
D_FF = 2816
HEAD_DIM = 64
ROT_HALF = 8
ROPE_THETA = 500000.0
ATTN_SCALE = HEAD_DIM ** -0.5
RMS_EPS = 1e-6
PAST_LEN = 16384

N_HEADS_A = 8
N_KV_A = 2
N_KV_B = 4
DILATIONS = (1, 4, 16)
BAND_STEPS = 128
Q_A_W = N_HEADS_A * HEAD_DIM
KV_A_W = N_KV_A * HEAD_DIM
Q_B_W = len(DILATIONS) * N_KV_B * HEAD_DIM
KV_B_W = N_KV_B * HEAD_DIM
QKV_W = Q_A_W + 2 * KV_A_W + Q_B_W + 2 * KV_B_W
MEM_HEADS = 4
MEM_HEAD_DIM = 128
MEM_W = MEM_HEADS * MEM_HEAD_DIM
MEM_SCALE = MEM_HEAD_DIM ** -0.5
N_MEM = 256

HEAD_PERM_A = (0, 4, 1, 5, 2, 6, 3, 7)

LANES = 128
BLOCK = 128
TOKEN_TILE = 512
FRONT_ROW_PARTS = 2
BACK_ROW_PARTS = 1
ATTN_CHUNK = 512
FF_CHUNKS = ((0, 1536), (1536, 2816))
VMEM_LIMIT = 60 * 1024 * 1024
NEG_INF = float("-inf")


def _rms(x, g):
    ms = jnp.mean(x * x, axis=-1, keepdims=True)
    return x * lax.rsqrt(ms + RMS_EPS) * g


def _dot(a, b):
    return jnp.dot(a, b, preferred_element_type=F32)


def _dot_nt(a, b):
    return lax.dot_general(a, b, (((1,), (1,)), ((), ())), preferred_element_type=F32)


def _ffn_half(xs, g_ref, wg_ref, wu_ref, wd_ref):
    us = [_rms(x, g_ref[...]).astype(BF16) for x in xs]
    accs = [None] * len(xs)
    for lo, hi in FF_CHUNKS:
        gates = [_dot(u, wg_ref[:, lo:hi]) for u in us]
        ups = [_dot(u, wu_ref[:, lo:hi]) for u in us]
        acts = [(gate * jax.nn.sigmoid(gate) * up).astype(BF16) for gate, up in zip(gates, ups)]
        for i, act in enumerate(acts):
            part = _dot(act, wd_ref[lo:hi, :])
            accs[i] = part if accs[i] is None else accs[i] + part
    return [x + 0.5 * acc for x, acc in zip(xs, accs)]


def _row_blocks(ref, parts):
    n = ref.shape[0] // parts
    return [slice(i * n, (i + 1) * n) for i in range(parts)]


def _swap_halves(x):
    return pltpu.roll(x, HEAD_DIM, 1)


def _low_half():
    return lax.broadcasted_iota(jnp.int32, (1, LANES), 1) < HEAD_DIM


def _pair_kv_groups(chunks):
    low = _low_half()
    c0, c1, c2, c3 = chunks
    return [jnp.where(low, c0, _swap_halves(c2)), jnp.where(low, _swap_halves(c0), c2),
            jnp.where(low, c1, _swap_halves(c3)), jnp.where(low, _swap_halves(c1), c3)]


def _merge_pairs(per_head):
    low = _low_half()
    return [jnp.where(low, per_head[2 * c], per_head[2 * c + 1]) for c in range(N_HEADS_A // 2)]


def _front_kernel(x_ref, cos_ref, sa_ref, sb_ref, g1_ref, wg_ref, wu_ref, wd_ref, gm_ref, wqkv_ref,
                  h_ref, *rest, sample):
    blocks = _row_blocks(x_ref, FRONT_ROW_PARTS)
    hs = _ffn_half([x_ref[rb, :] for rb in blocks], g1_ref, wg_ref, wu_ref, wd_ref)
    for rb, h in zip(blocks, hs):
        h_ref[rb, :] = h
    ns = [_rms(h, gm_ref[...]).astype(BF16) for h in hs]
    zs = [_dot(n, wqkv_ref[...]) for n in ns]
    for rb, z in zip(blocks, zs):
        _front_emit(rb, z, cos_ref, sa_ref, sb_ref, rest, sample)


def _front_emit(rb, z, cos_ref, sa_ref, sb_ref, rest, sample):
    cos, sa, sb = cos_ref[rb, :], sa_ref[rb, :], sb_ref[rb, :]

    def chunk(c):
        return z[:, c * LANES:(c + 1) * LANES]

    def rope(c):
        zc = chunk(c)
        return zc * cos + pltpu.roll(zc, LANES - ROT_HALF, 1) * sa + pltpu.roll(zc, ROT_HALF, 1) * sb

    nqa, nka = Q_A_W // LANES, KV_A_W // LANES
    nqb, nkb = Q_B_W // LANES, KV_B_W // LANES
    c0 = 0
    qa = _pair_kv_groups([rope(c0 + c) * ATTN_SCALE for c in range(nqa)])
    c0 += nqa
    ka = [rope(c0 + c) for c in range(nka)]
    c0 += nka
    va = [chunk(c0 + c) for c in range(nka)]
    c0 += nka
    qb = [rope(c0 + c) * ATTN_SCALE for c in range(nqb)]
    c0 += nqb
    kb = [rope(c0 + c) for c in range(nkb)]
    c0 += nkb
    vb = [chunk(c0 + c) for c in range(nkb)]

    def store(ref, parts, dtype):
        for c, p in enumerate(parts):
            ref[rb, c * LANES:(c + 1) * LANES] = p.astype(dtype)

    def fill(scr, parts):
        for c, p in enumerate(parts):
            scr[c, rb, :] = p

    def store_split(ref, scr, d):
        n = (rb.stop - rb.start) // d
        dst = slice(rb.start // d, rb.start // d + n)
        for r in range(d):
            for c in range(scr.shape[0]):
                ref[0, r, dst, c * LANES:(c + 1) * LANES] = scr[c, pl.ds(rb.start + r, n, stride=d), :].astype(BF16)

    if sample:
        q32_ref, ka32_ref, va32_ref, kb32_ref, vb32_ref = rest
        store(q32_ref, qa + qb, F32)
    else:
        (qa_ref, ka_ref, va_ref, qb0_ref, kb_ref, vb_ref, q4_ref, k4_ref, v4_ref, q16_ref, k16_ref, v16_ref,
         ka32_ref, va32_ref, kb32_ref, vb32_ref, q4_s, q16_s, kb_s, vb_s) = rest
        gw = KV_B_W // LANES
        store(qa_ref, qa, BF16)
        store(ka_ref, ka, BF16)
        store(va_ref, va, BF16)
        store(qb0_ref, qb[0:gw], BF16)
        store(kb_ref, kb, BF16)
        store(vb_ref, vb, BF16)
        fill(q4_s, qb[gw:2 * gw])
        fill(q16_s, qb[2 * gw:3 * gw])
        fill(kb_s, kb)
        fill(vb_s, vb)
        store_split(q4_ref, q4_s, DILATIONS[1])
        store_split(q16_ref, q16_s, DILATIONS[2])
        store_split(k4_ref, kb_s, DILATIONS[1])
        store_split(k16_ref, kb_s, DILATIONS[2])
        store_split(v4_ref, vb_s, DILATIONS[1])
        store_split(v16_ref, vb_s, DILATIONS[2])
    store(ka32_ref, ka, F32)
    store(va32_ref, va, F32)
    store(kb32_ref, kb, F32)
    store(vb32_ref, vb, F32)


def _const_spec(shape, index=None):
    index = (0,) * len(shape) if index is None else index
    return pl.BlockSpec(shape, lambda *_: index, pipeline_mode=pl.Buffered(1))


def _front_call(x, tables, w, *, sample, batch=1):
    t = x.shape[0]
    tm = TOKEN_TILE
    nsteps = t // tm
    tpb = nsteps // batch
    seq = t // batch
    cos_t, sa_t, sb_t = tables
    tab_blocks = cos_t.shape[0] // tm
    row = lambda i: (i, 0)
    tab = lambda i: (i % tab_blocks, 0)

    def tok(width, dtype):
        return jax.ShapeDtypeStruct((t, width), dtype), pl.BlockSpec((tm, width), row)

    def split(d):
        return (jax.ShapeDtypeStruct((batch, d, seq // d, KV_B_W), BF16),
                pl.BlockSpec((1, d, tm // d, KV_B_W), lambda i: (i // tpb, 0, i % tpb, 0)))

    outs = [tok(D_MODEL, F32)]
    scratch = []
    if sample:
        outs += [tok(Q_A_W + Q_B_W, F32)]
    else:
        outs += [tok(Q_A_W, BF16), tok(KV_A_W, BF16), tok(KV_A_W, BF16),
                 tok(KV_B_W, BF16), tok(KV_B_W, BF16), tok(KV_B_W, BF16)]
        outs += [split(DILATIONS[1])] * 3 + [split(DILATIONS[2])] * 3
        scratch = [pltpu.VMEM((KV_B_W // LANES, tm, LANES), F32)] * 4
    outs += [tok(KV_A_W, F32), tok(KV_A_W, F32), tok(KV_B_W, F32), tok(KV_B_W, F32)]
    out_shape, out_specs = zip(*outs)
    in_specs = [pl.BlockSpec((tm, D_MODEL), row),
                pl.BlockSpec((tm, LANES), tab), pl.BlockSpec((tm, LANES), tab), pl.BlockSpec((tm, LANES), tab),
                _const_spec((1, D_MODEL)), _const_spec((D_MODEL, D_FF)), _const_spec((D_MODEL, D_FF)),
                _const_spec((D_FF, D_MODEL)), _const_spec((1, D_MODEL)), _const_spec((D_MODEL, QKV_W), (0, 0))]
    return pl.pallas_call(
        functools.partial(_front_kernel, sample=sample),
        grid=(nsteps,),
        in_specs=in_specs,
        out_specs=list(out_specs),
        out_shape=list(out_shape),
        scratch_shapes=scratch,
        compiler_params=pltpu.CompilerParams(dimension_semantics=("parallel",), vmem_limit_bytes=VMEM_LIMIT),
        name="front_sample" if sample else "front_prompt",
    )(x, cos_t, sa_t, sb_t, w["ffn1_norm"], w["ffn1_wg"], w["ffn1_wu"], w["ffn1_wd"], w["mix_norm"], w["w_in"])


def _lane_block_masks(width, block, dtype):
    lane = lax.broadcasted_iota(jnp.int32, (1, width), 1)
    return [((lane >= i * block) & (lane < (i + 1) * block)).astype(dtype) for i in range(width // block)]


def _band_mask():
    qi = lax.broadcasted_iota(jnp.int32, (BLOCK, 2 * BLOCK), 0)
    kj = lax.broadcasted_iota(jnp.int32, (BLOCK, 2 * BLOCK), 1)
    dist = qi + BLOCK - kj
    return (dist >= 0) & (dist <= BAND_STEPS), kj >= BLOCK


def _band_block(load_qbd, load_k, load_v, mask, nheads, finish):
    s = _dot_nt(load_qbd(), load_k())
    yield
    ps, ms, ls = [], [], []
    for p in range(nheads):
        sp = jnp.where(mask, s[p * BLOCK:(p + 1) * BLOCK], NEG_INF)
        m = jnp.max(sp, axis=-1, keepdims=True)
        e = jnp.exp(sp - m)
        ls.append(jnp.sum(e, axis=-1, keepdims=True))
        ms.append(m)
        ps.append(e.astype(BF16))
    pc = jnp.concatenate(ps, axis=0)
    yield
    finish(_dot(pc, load_v()), ms, ls)
    yield


def _issue_skewed(items, nstages=3):
    for tick in range(len(items) + nstages - 1):
        for s in reversed(range(nstages)):
            i = tick - s
            if 0 <= i < len(items):
                next(items[i])


def _kv_window(cur_ref, prev_ref, qb):
    if qb == 0:
        return jnp.concatenate([prev_ref[...], cur_ref[0:BLOCK]], axis=0)
    return cur_ref[(qb - 1) * BLOCK:(qb + 1) * BLOCK]


def _attn_a_kernel(sink_ref, q_ref, kc_ref, kp_ref, vc_ref, vp_ref, o_ref):
    first_chunk = pl.program_id(1) == 0
    band, in_cur = _band_mask()
    half_bf = _lane_block_masks(LANES, HEAD_DIM, BF16)

    def block(qb):
        mask = band & (in_cur | jnp.logical_not(first_chunk)) if qb == 0 else band
        rows = slice(qb * BLOCK, (qb + 1) * BLOCK)

        def load_qbd():
            return jnp.concatenate(
                [q_ref[rows, (p // 2) * LANES:(p // 2 + 1) * LANES] * half_bf[p % 2] for p in range(N_HEADS_A)],
                axis=0)

        def finish(o, ms, ls):
            normed = []
            for p in range(N_HEADS_A):
                den = ls[p] + jnp.exp(sink_ref[HEAD_PERM_A[p]] - ms[p])
                normed.append(o[p * BLOCK:(p + 1) * BLOCK] * (1.0 / den))
            for c, pair in enumerate(_merge_pairs(normed)):
                o_ref[rows, c * LANES:(c + 1) * LANES] = pair.astype(BF16)

        return _band_block(load_qbd, lambda: _kv_window(kc_ref, kp_ref, qb), lambda: _kv_window(vc_ref, vp_ref, qb),
                           mask, N_HEADS_A, finish)

    for qb in range(ATTN_CHUNK // BLOCK):
        for _ in block(qb):
            pass


def _attn_a_call(sink, qa, ka, va, batch, seq):
    t = qa.shape[0]
    cpb = seq // ATTN_CHUNK
    bpc = ATTN_CHUNK // BLOCK
    cur = lambda b, c: (b * cpb + c, 0)
    prev = lambda b, c: (b * cpb * bpc + jnp.maximum(c * bpc - 1, 0), 0)
    return pl.pallas_call(
        _attn_a_kernel,
        grid=(batch, cpb),
        in_specs=[pl.BlockSpec(memory_space=pltpu.SMEM),
                  pl.BlockSpec((ATTN_CHUNK, Q_A_W), cur),
                  pl.BlockSpec((ATTN_CHUNK, KV_A_W), cur), pl.BlockSpec((BLOCK, KV_A_W), prev),
                  pl.BlockSpec((ATTN_CHUNK, KV_A_W), cur), pl.BlockSpec((BLOCK, KV_A_W), prev)],
        out_specs=pl.BlockSpec((ATTN_CHUNK, Q_A_W), cur),
        out_shape=jax.ShapeDtypeStruct((t, Q_A_W), BF16),
        compiler_params=pltpu.CompilerParams(dimension_semantics=("parallel", "parallel"),
                                             vmem_limit_bytes=VMEM_LIMIT),
        name="attn_a_prompt",
    )(sink, qa, ka, ka, va, va)


def _attn_b_kernel(q_ref, kc_ref, kp_ref, vc_ref, vp_ref, o_ref, lse_ref):
    first_chunk = pl.program_id(2) == 0
    band, in_cur = _band_mask()
    head_bf = _lane_block_masks(KV_B_W, HEAD_DIM, BF16)
    lane = lax.broadcasted_iota(jnp.int32, (1, KV_B_W), 1)

    def block(qb):
        mask = band & (in_cur | jnp.logical_not(first_chunk)) if qb == 0 else band
        rows = slice(qb * BLOCK, (qb + 1) * BLOCK)

        def load_qbd():
            q = q_ref[rows, :]
            return jnp.concatenate([q * head_bf[p] for p in range(N_KV_B)], axis=0)

        def finish(o, ms, ls):
            out = lse = None
            for p in reversed(range(N_KV_B)):
                op = o[p * BLOCK:(p + 1) * BLOCK] * (1.0 / ls[p])
                lp = jnp.broadcast_to(ms[p] + jnp.log(ls[p]), (BLOCK, KV_B_W))
                if out is None:
                    out, lse = op, lp
                else:
                    sel = lane < (p + 1) * HEAD_DIM
                    out, lse = jnp.where(sel, op, out), jnp.where(sel, lp, lse)
            o_ref[rows, :] = out
            lse_ref[rows, :] = lse

        return _band_block(load_qbd, lambda: _kv_window(kc_ref, kp_ref, qb), lambda: _kv_window(vc_ref, vp_ref, qb),
                           mask, N_KV_B, finish)

    _issue_skewed([block(qb) for qb in range(ATTN_CHUNK // BLOCK)])


def _attn_b_call(q, k, v, d):
    batch, _, rows, _ = q.shape
    cpb = rows // ATTN_CHUNK
    bpc = ATTN_CHUNK // BLOCK
    cur = pl.BlockSpec((None, None, ATTN_CHUNK, KV_B_W), lambda b, r, c: (b, r, c, 0))
    prev = pl.BlockSpec((None, None, BLOCK, KV_B_W), lambda b, r, c: (b, r, jnp.maximum(c * bpc - 1, 0), 0))
    return pl.pallas_call(
        _attn_b_kernel,
        grid=(batch, d, cpb),
        in_specs=[cur, cur, prev, cur, prev],
        out_specs=[cur, cur],
        out_shape=[jax.ShapeDtypeStruct((batch, d, rows, KV_B_W), F32)] * 2,
        compiler_params=pltpu.CompilerParams(dimension_semantics=("parallel", "parallel", "parallel"),
                                             vmem_limit_bytes=VMEM_LIMIT),
        name=f"attn_b_prompt_d{d}",
    )(q, k, k, v, v)


SAMPLE_T = 8
CACHE_A = 128
CACHE_B = 2048
SAMPLE_ELEMS = 4


def _new_rows_block(new):
    return jnp.concatenate([new, jnp.zeros((BLOCK - SAMPLE_T, new.shape[1]), F32)], axis=0)


def _cached_scores(qbd, kt, knew):
    return _dot(qbd, kt), _dot_nt(qbd, knew)


def _cached_softmax(s_c, s_n, mask_c, mask_n):
    s_c = jnp.where(mask_c, s_c, NEG_INF)
    s_n = jnp.where(mask_n, s_n, NEG_INF)
    m = jnp.maximum(jnp.max(s_c, axis=-1, keepdims=True), jnp.max(s_n, axis=-1, keepdims=True))
    e_c = jnp.exp(s_c - m)
    e_n = jnp.exp(s_n - m)
    l = jnp.sum(e_c, axis=-1, keepdims=True) + jnp.sum(e_n, axis=-1, keepdims=True)
    return e_c, e_n, m, l


def _cached_pv(e_c, e_n, vt, vnew):
    return _dot_nt(e_c, vt) + _dot(e_n, vnew)


def _attn_sample_kernel(sink_ref, q_ref, kan_ref, van_ref, kbn_ref, vbn_ref, cka_ref, cva_ref, ckb_ref, cvb_ref,
                        oa_ref, ob_ref):
    half_f = _lane_block_masks(LANES, HEAD_DIM, F32)
    lane_b = lax.broadcasted_iota(jnp.int32, (1, KV_B_W), 1)

    def dist(nrows, ncols, offset):
        r = lax.broadcasted_iota(jnp.int32, (nrows, ncols), 0)
        i = lax.broadcasted_iota(jnp.int32, (nrows, ncols), 1)
        return r, offset + (r & (SAMPLE_T - 1)) - i

    nrow_a = N_HEADS_A * SAMPLE_T
    _, dac = dist(nrow_a, CACHE_A, CACHE_A)
    _, dan = dist(nrow_a, BLOCK, 0)
    mask_ac = (dac >= 0) & (dac <= BAND_STEPS)
    mask_an = (dan >= 0) & (dan <= BAND_STEPS)
    nrow_b = len(DILATIONS) * N_KV_B * SAMPLE_T
    rows_per_group = N_KV_B * SAMPLE_T

    def mask_b(ncols, offset):
        r, db = dist(nrow_b, ncols, offset)
        dil = jnp.where(r < rows_per_group, DILATIONS[0], jnp.where(r < 2 * rows_per_group, DILATIONS[1], DILATIONS[2]))
        return (db >= 0) & (db <= BAND_STEPS * dil) & ((db & (dil - 1)) == 0)

    mask_bc = mask_b(CACHE_B, CACHE_B)
    mask_bn = mask_b(BLOCK, 0)

    sink_col = jnp.concatenate(
        [jnp.full((SAMPLE_T, 1), sink_ref[HEAD_PERM_A[p]], F32) for p in range(N_HEADS_A)], axis=0)

    def element(j):
        rows = slice(j * SAMPLE_T, (j + 1) * SAMPLE_T)
        qbd_a = jnp.concatenate(
            [q_ref[rows, (p // 2) * LANES:(p // 2 + 1) * LANES] * half_f[p % 2] for p in range(N_HEADS_A)],
            axis=0)
        qrows = []
        for g in range(len(DILATIONS)):
            for kvh in range(N_KV_B):
                c = (Q_A_W // LANES) + 2 * g + kvh // 2
                part = q_ref[rows, c * LANES:(c + 1) * LANES] * half_f[kvh % 2]
                zero = jnp.zeros_like(part)
                qrows.append(jnp.concatenate([part, zero] if kvh // 2 == 0 else [zero, part], axis=1))
        qbd_b = jnp.concatenate(qrows, axis=0)
        scores_a = _cached_scores(qbd_a, cka_ref[j], _new_rows_block(kan_ref[rows, :]))
        scores_b = _cached_scores(qbd_b, ckb_ref[j], _new_rows_block(kbn_ref[rows, :]))
        yield
        ea_c, ea_n, ma, la = _cached_softmax(*scores_a, mask_ac, mask_an)
        eb_c, eb_n, m, l = _cached_softmax(*scores_b, mask_bc, mask_bn)
        yield
        o = _cached_pv(ea_c, ea_n, cva_ref[j], _new_rows_block(van_ref[rows, :]))
        o = o * (1.0 / (la + jnp.exp(sink_col - ma)))
        per_head = [o[p * SAMPLE_T:(p + 1) * SAMPLE_T] for p in range(N_HEADS_A)]
        for c, pair in enumerate(_merge_pairs(per_head)):
            oa_ref[rows, c * LANES:(c + 1) * LANES] = pair
        o = _cached_pv(eb_c, eb_n, cvb_ref[j], _new_rows_block(vbn_ref[rows, :]))
        out = None
        for kvh in reversed(range(N_KV_B)):
            sl = [slice((g * N_KV_B + kvh) * SAMPLE_T, (g * N_KV_B + kvh + 1) * SAMPLE_T)
                  for g in range(len(DILATIONS))]
            mj = jnp.maximum(jnp.maximum(m[sl[0]], m[sl[1]]), m[sl[2]])
            ws = [jnp.exp(m[x] - mj) for x in sl]
            den = ws[0] * l[sl[0]] + ws[1] * l[sl[1]] + ws[2] * l[sl[2]]
            num = ws[0] * o[sl[0]] + ws[1] * o[sl[1]] + ws[2] * o[sl[2]]
            okv = num * (1.0 / den)
            out = okv if out is None else jnp.where(lane_b < (kvh + 1) * HEAD_DIM, okv, out)
        ob_ref[rows, :] = out
        yield

    elements = [element(j) for j in range(SAMPLE_ELEMS)]
    for _ in range(3):
        for e in elements:
            next(e)


def _attn_sample_call(sink, q32, ka32, va32, kb32, vb32, cka, cva, ckb, cvb):
    nb = cka.shape[0]
    bt = SAMPLE_ELEMS
    t = q32.shape[0]
    tok = lambda w: pl.BlockSpec((bt * SAMPLE_T, w), lambda i: (i, 0))
    cache = lambda w, n: pl.BlockSpec((bt, w, n), lambda i: (i, 0, 0))
    return pl.pallas_call(
        _attn_sample_kernel,
        grid=(nb // bt,),
        in_specs=[pl.BlockSpec(memory_space=pltpu.SMEM), tok(Q_A_W + Q_B_W),
                  tok(KV_A_W), tok(KV_A_W), tok(KV_B_W), tok(KV_B_W),
                  cache(KV_A_W, CACHE_A), cache(KV_A_W, CACHE_A), cache(KV_B_W, CACHE_B), cache(KV_B_W, CACHE_B)],
        out_specs=[tok(Q_A_W), tok(KV_B_W)],
        out_shape=[jax.ShapeDtypeStruct((t, Q_A_W), F32), jax.ShapeDtypeStruct((t, KV_B_W), F32)],
        compiler_params=pltpu.CompilerParams(dimension_semantics=("parallel",), vmem_limit_bytes=VMEM_LIMIT),
        name="attn_sample",
    )(sink, q32, ka32, va32, kb32, vb32, cka, cva, ckb, cvb)


def _mix_out(h, oa, ob, gm_ref, wgate_ref, wba_ref, wbb_ref, wout_ref):
    n = _rms(h, gm_ref[...]).astype(BF16)
    gates = _dot(n, wgate_ref[...])
    ga = jax.nn.sigmoid(gates[:, :D_MODEL])
    gb = jax.nn.sigmoid(gates[:, D_MODEL:])
    mixed = ga * _dot(oa.astype(BF16), wba_ref[...]) + gb * _dot(ob.astype(BF16), wbb_ref[...])
    return h + _dot(mixed.astype(BF16), wout_ref[...])


def _combine_groups(os, ls):
    m = jnp.maximum(jnp.maximum(ls[0], ls[1]), ls[2])
    es = [jnp.exp(x - m) for x in ls]
    den = es[0] + es[1] + es[2]
    num = es[0] * os[0] + es[1] * os[1] + es[2] * os[2]
    return num * (1.0 / den)


def _merge_split(split_ref, scr, rb):
    d = split_ref.shape[1]
    n = (rb.stop - rb.start) // d
    src = slice(rb.start // d, rb.start // d + n)
    for r in range(d):
        for c in range(scr.shape[0]):
            scr[c, pl.ds(rb.start + r, n, stride=d), :] = split_ref[0, r, src, c * LANES:(c + 1) * LANES]
    return jnp.concatenate([scr[c, rb, :] for c in range(scr.shape[0])], axis=1)


def _softmax_rows(s):
    m = jnp.max(s, axis=-1, keepdims=True)
    e = jnp.exp(s - m)
    return e * (1.0 / jnp.sum(e, axis=-1, keepdims=True))


def _back_prompt_kernel(h_ref, oa_ref, o0_ref, o1_ref, o2_ref, l0_ref, l1_ref, l2_ref, mk_ref, mv_ref,
                        gm_ref, wgate_ref, wba_ref, wbb_ref, wout_ref, gq_ref, wq_ref, wo_ref,
                        g2_ref, wg_ref, wu_ref, wd_ref, gf_ref, y_ref, o1_s, o2_s, l1_s, l2_s):
    blocks = _row_blocks(h_ref, BACK_ROW_PARTS)
    h3s = []
    for rb in blocks:
        ob = _combine_groups(
            (o0_ref[0, 0, rb, :], _merge_split(o1_ref, o1_s, rb), _merge_split(o2_ref, o2_s, rb)),
            (l0_ref[0, 0, rb, :], _merge_split(l1_ref, l1_s, rb), _merge_split(l2_ref, l2_s, rb)))
        h2 = _mix_out(h_ref[rb, :], oa_ref[rb, :], ob, gm_ref, wgate_ref, wba_ref, wbb_ref, wout_ref)
        q = _dot(_rms(h2, gq_ref[...]).astype(BF16), wq_ref[...])
        heads = []
        for hd in range(MEM_HEADS):
            cols = slice(hd * MEM_HEAD_DIM, (hd + 1) * MEM_HEAD_DIM)
            p = _softmax_rows(_dot_nt(q[:, cols].astype(BF16), mk_ref[0, :, cols]) * MEM_SCALE)
            heads.append(_dot(p.astype(BF16), mv_ref[0, :, cols]))
        oc = jnp.concatenate(heads, axis=1).astype(BF16)
        h3s.append(h2 + _dot(oc, wo_ref[...]))
    for rb, y in zip(blocks, _ffn_half(h3s, g2_ref, wg_ref, wu_ref, wd_ref)):
        y_ref[rb, :] = _rms(y, gf_ref[...])


def _back_prompt_call(h, oa, obs, lses, mk, mv, w, seq):
    t = h.shape[0]
    tm = TOKEN_TILE
    tiles_per_batch = seq // tm
    row = lambda i: (i, 0)
    mem = lambda i: (i // tiles_per_batch, 0, 0)
    tok = lambda width: pl.BlockSpec((tm, width), row)
    split = lambda d: pl.BlockSpec((1, d, tm // d, KV_B_W),
                                   lambda i: (i // tiles_per_batch, 0, i % tiles_per_batch, 0))
    groups = [split(d) for d in DILATIONS]
    in_specs = ([tok(D_MODEL), tok(Q_A_W)] + groups + groups
                + [pl.BlockSpec((1, N_MEM, MEM_W), mem), pl.BlockSpec((1, N_MEM, MEM_W), mem)]
                + [_const_spec((1, D_MODEL)), _const_spec((D_MODEL, 2 * D_MODEL), (0, 1)), _const_spec((Q_A_W, D_MODEL)),
                   _const_spec((KV_B_W, D_MODEL)), _const_spec((D_MODEL, D_MODEL)),
                   _const_spec((1, D_MODEL)), _const_spec((D_MODEL, MEM_W)), _const_spec((MEM_W, D_MODEL)),
                   _const_spec((1, D_MODEL)), _const_spec((D_MODEL, D_FF)), _const_spec((D_MODEL, D_FF)),
                   _const_spec((D_FF, D_MODEL)), _const_spec((1, D_MODEL))])
    return pl.pallas_call(
        _back_prompt_kernel,
        grid=(t // tm,),
        in_specs=in_specs,
        out_specs=tok(D_MODEL),
        out_shape=jax.ShapeDtypeStruct((t, D_MODEL), F32),
        scratch_shapes=[pltpu.VMEM((KV_B_W // LANES, tm, LANES), F32)] * 4,
        compiler_params=pltpu.CompilerParams(dimension_semantics=("parallel",), vmem_limit_bytes=VMEM_LIMIT),
        name="back_prompt",
    )(h, oa, *obs, *lses, mk, mv, w["mix_norm"], w["w_in"], w["w_branch_a"], w["w_branch_b"], w["w_out"],
      w["mem_q_norm"], w["w_mem_q"], w["w_mem_o"], w["ffn2_norm"], w["ffn2_wg"], w["ffn2_wu"], w["ffn2_wd"],
      w["final_norm"])


def _back_sample_a_kernel(h_ref, oa_ref, ob_ref, gm_ref, wgate_ref, wba_ref, wbb_ref, wout_ref, gq_ref, wq_ref,
                          h2_ref, q_ref):
    h2 = _mix_out(h_ref[...], oa_ref[...], ob_ref[...], gm_ref, wgate_ref, wba_ref, wbb_ref, wout_ref)
    h2_ref[...] = h2
    q_ref[...] = _dot(_rms(h2, gq_ref[...]).astype(BF16), wq_ref[...])


def _back_sample_a_call(h, oa, ob, w):
    t = h.shape[0]
    tm = TOKEN_TILE
    tok = lambda width: pl.BlockSpec((tm, width), lambda i: (i, 0))
    return pl.pallas_call(
        _back_sample_a_kernel,
        grid=(t // tm,),
        in_specs=[tok(D_MODEL), tok(Q_A_W), tok(KV_B_W),
                  _const_spec((1, D_MODEL)), _const_spec((D_MODEL, 2 * D_MODEL), (0, 1)), _const_spec((Q_A_W, D_MODEL)),
                  _const_spec((KV_B_W, D_MODEL)), _const_spec((D_MODEL, D_MODEL)),
                  _const_spec((1, D_MODEL)), _const_spec((D_MODEL, MEM_W))],
        out_specs=[tok(D_MODEL), tok(MEM_W)],
        out_shape=[jax.ShapeDtypeStruct((t, D_MODEL), F32), jax.ShapeDtypeStruct((t, MEM_W), F32)],
        compiler_params=pltpu.CompilerParams(dimension_semantics=("parallel",), vmem_limit_bytes=VMEM_LIMIT),
        name="back_sample_mix",
    )(h, oa, ob, w["mix_norm"], w["w_in"], w["w_branch_a"], w["w_branch_b"], w["w_out"],
      w["mem_q_norm"], w["w_mem_q"])


CROSS_BATCH_TILE = 8


def _cross_sample_kernel(q_ref, mk_ref, mv_ref, o_ref):
    nrow = MEM_HEADS * SAMPLE_T
    qhead = lax.broadcasted_iota(jnp.int32, (nrow, N_MEM * MEM_HEADS), 0) >> (SAMPLE_T.bit_length() - 1)
    khead = lax.broadcasted_iota(jnp.int32, (nrow, N_MEM * MEM_HEADS), 1) & (MEM_HEADS - 1)
    own_head = qhead == khead
    def element(bi):
        rows = slice(bi * SAMPLE_T, (bi + 1) * SAMPLE_T)
        qs = jnp.concatenate(
            [q_ref[rows, hd * MEM_HEAD_DIM:(hd + 1) * MEM_HEAD_DIM] for hd in range(MEM_HEADS)], axis=0)
        s = _dot_nt(qs, mk_ref[bi]) * MEM_SCALE
        yield
        p = _softmax_rows(jnp.where(own_head, s, NEG_INF))
        yield
        o = _dot(p, mv_ref[bi])
        for hd in range(MEM_HEADS):
            o_ref[rows, hd * MEM_HEAD_DIM:(hd + 1) * MEM_HEAD_DIM] = o[hd * SAMPLE_T:(hd + 1) * SAMPLE_T]
        yield

    _issue_skewed([element(bi) for bi in range(CROSS_BATCH_TILE)])


def _cross_sample_call(q, mk, mv):
    t = q.shape[0]
    bt = CROSS_BATCH_TILE
    tok = pl.BlockSpec((bt * SAMPLE_T, MEM_W), lambda i: (i, 0))
    mem = pl.BlockSpec((bt, N_MEM * MEM_HEADS, MEM_HEAD_DIM), lambda i: (i, 0, 0))
    return pl.pallas_call(
        _cross_sample_kernel,
        grid=(mk.shape[0] // bt,),
        in_specs=[tok, mem, mem],
        out_specs=tok,
        out_shape=jax.ShapeDtypeStruct((t, MEM_W), F32),
        compiler_params=pltpu.CompilerParams(dimension_semantics=("parallel",), vmem_limit_bytes=VMEM_LIMIT),
        name="cross_sample",
    )(q, mk, mv)


def _back_sample_b_kernel(h2_ref, oc_ref, wo_ref, g2_ref, wg_ref, wu_ref, wd_ref, gf_ref, y_ref):
    blocks = _row_blocks(h2_ref, BACK_ROW_PARTS)
    h3s = [h2_ref[rb, :] + _dot(oc_ref[rb, :].astype(BF16), wo_ref[...]) for rb in blocks]
    for rb, y in zip(blocks, _ffn_half(h3s, g2_ref, wg_ref, wu_ref, wd_ref)):
        y_ref[rb, :] = _rms(y, gf_ref[...])


def _back_sample_b_call(h2, oc, w):
    t = h2.shape[0]
    tm = TOKEN_TILE
    tok = lambda width: pl.BlockSpec((tm, width), lambda i: (i, 0))
    return pl.pallas_call(
        _back_sample_b_kernel,
        grid=(t // tm,),
        in_specs=[tok(D_MODEL), tok(MEM_W), _const_spec((MEM_W, D_MODEL)),
                  _const_spec((1, D_MODEL)), _const_spec((D_MODEL, D_FF)), _const_spec((D_MODEL, D_FF)),
                  _const_spec((D_FF, D_MODEL)), _const_spec((1, D_MODEL))],
        out_specs=tok(D_MODEL),
        out_shape=jax.ShapeDtypeStruct((t, D_MODEL), F32),
        compiler_params=pltpu.CompilerParams(dimension_semantics=("parallel",), vmem_limit_bytes=VMEM_LIMIT),
        name="back_sample_ffn",
    )(h2, oc, w["w_mem_o"], w["ffn2_norm"], w["ffn2_wg"], w["ffn2_wu"], w["ffn2_wd"], w["final_norm"])


def _mem_kv_kernel(mem_ref, g_ref, wk_ref, wv_ref, k32_ref, v32_ref, k16_ref, v16_ref):
    u = _rms(mem_ref[...], g_ref[...]).astype(BF16)
    k = _dot(u, wk_ref[...])
    v = _dot(u, wv_ref[...])
    k32_ref[...] = k
    v32_ref[...] = v
    k16_ref[...] = k.astype(BF16)
    v16_ref[...] = v.astype(BF16)


def _mem_kv_call(mem, w):
    rows = mem.shape[0]
    full = lambda width: pl.BlockSpec((rows, width), lambda i: (0, 0))
    return pl.pallas_call(
        _mem_kv_kernel,
        grid=(1,),
        in_specs=[full(D_MODEL), _const_spec((1, D_MODEL)), _const_spec((D_MODEL, MEM_W)),
                  _const_spec((D_MODEL, MEM_W))],
        out_specs=[full(MEM_W)] * 4,
        out_shape=[jax.ShapeDtypeStruct((rows, MEM_W), F32)] * 2 + [jax.ShapeDtypeStruct((rows, MEM_W), BF16)] * 2,
        compiler_params=pltpu.CompilerParams(dimension_semantics=("arbitrary",), vmem_limit_bytes=VMEM_LIMIT),
        name="mem_kv",
    )(mem, w["mem_kv_norm"], w["w_mem_k"], w["w_mem_v"])


def _rope_tables(pos):
    within = np.arange(LANES) % HEAD_DIM
    inv_freq = jnp.power(jnp.float32(ROPE_THETA), -jnp.arange(ROT_HALF, dtype=jnp.float32) / ROT_HALF)
    ang = pos.astype(jnp.float32)[:, None] * inv_freq[None, :]
    cos = jnp.tile(jnp.cos(ang), (1, LANES // ROT_HALF))
    sin = jnp.tile(jnp.sin(ang), (1, LANES // ROT_HALF))
    first = jnp.asarray(within < ROT_HALF)[None, :]
    second = jnp.asarray((within >= ROT_HALF) & (within < 2 * ROT_HALF))[None, :]
    return (jnp.where(first | second, cos, 1.0), jnp.where(first, -sin, 0.0), jnp.where(second, sin, 0.0))


def _prep_weights(p):
    bf = lambda x: x.astype(BF16)
    vec = lambda x: x.reshape(1, -1)
    return {
        "ffn1_norm": vec(p["ffn1_norm"]), "ffn1_wg": bf(p["ffn1_w_gate"]), "ffn1_wu": bf(p["ffn1_w_up"]),
        "ffn1_wd": bf(p["ffn1_w_down"]),
        "mix_norm": vec(p["mix_norm"]), "w_in": bf(p["w_in"]),
        "w_branch_a": bf(p["w_branch_a"].reshape(N_HEADS_A, HEAD_DIM, D_MODEL)[np.array(HEAD_PERM_A)].reshape(
            Q_A_W, D_MODEL)),
        "w_branch_b": bf(p["w_branch_b"]), "w_out": bf(p["w_out"]),
        "mem_q_norm": vec(p["mem_q_norm"]), "mem_kv_norm": vec(p["mem_kv_norm"]),
        "w_mem_q": bf(p["w_mem_q"]), "w_mem_k": bf(p["w_mem_k"]), "w_mem_v": bf(p["w_mem_v"]),
        "w_mem_o": bf(p["w_mem_o"]),
        "ffn2_norm": vec(p["ffn2_norm"]), "ffn2_wg": bf(p["ffn2_w_gate"]), "ffn2_wu": bf(p["ffn2_w_up"]),
        "ffn2_wd": bf(p["ffn2_w_down"]), "final_norm": vec(p["final_norm"]),
    }


def kernel(x_prompt, x_sample, cache_swa_k, cache_swa_v, cache_dil_k, cache_dil_v, cache_mem_k, cache_mem_v, mem_prompt, ffn1_norm, ffn1_w_gate, ffn1_w_up, ffn1_w_down, mix_norm, w_in, attn_sink, w_branch_a, w_branch_b, w_out, mem_q_norm, mem_kv_norm, w_mem_q, w_mem_k, w_mem_v, w_mem_o, ffn2_norm, ffn2_w_gate, ffn2_w_up, ffn2_w_down, final_norm):
    depth = ffn1_norm.shape[0]
    assert depth == 1
    batch, seq, d = x_prompt.shape
    nb, t_new, _ = x_sample.shape
    assert d == D_MODEL and t_new == SAMPLE_T and seq % (ATTN_CHUNK * DILATIONS[-1]) == 0
    assert cache_swa_k.shape[2] == CACHE_A and cache_dil_k.shape[2] == CACHE_B
    layer = lambda x: x[0]
    w = _prep_weights(dict(
        ffn1_norm=layer(ffn1_norm), ffn1_w_gate=layer(ffn1_w_gate), ffn1_w_up=layer(ffn1_w_up),
        ffn1_w_down=layer(ffn1_w_down), mix_norm=layer(mix_norm), w_in=layer(w_in),
        w_branch_a=layer(w_branch_a), w_branch_b=layer(w_branch_b), w_out=layer(w_out),
        mem_q_norm=layer(mem_q_norm), mem_kv_norm=layer(mem_kv_norm), w_mem_q=layer(w_mem_q),
        w_mem_k=layer(w_mem_k), w_mem_v=layer(w_mem_v), w_mem_o=layer(w_mem_o), ffn2_norm=layer(ffn2_norm),
        ffn2_w_gate=layer(ffn2_w_gate), ffn2_w_up=layer(ffn2_w_up), ffn2_w_down=layer(ffn2_w_down),
        final_norm=final_norm))
    sink = layer(attn_sink)

    ts = nb * t_new
    pos_s = PAST_LEN + (jnp.arange(TOKEN_TILE, dtype=jnp.int32) % t_new)
    tables_s = _rope_tables(pos_s)
    (h_s, q32, ka32_s, va32_s, kb32_s, vb32_s) = _front_call(x_sample.reshape(ts, d), tables_s, w, sample=True)
    by_pos = lambda c: jnp.transpose(layer(c), (0, 2, 3, 1)).reshape(nb, c.shape[3] * c.shape[4], c.shape[2])
    by_slot_head = lambda c: layer(c).reshape(nb, N_MEM * MEM_HEADS, MEM_HEAD_DIM)

    oa_s, ob_s = _attn_sample_call(
        sink, q32, ka32_s, va32_s, kb32_s, vb32_s,
        by_pos(cache_swa_k), by_pos(cache_swa_v), by_pos(cache_dil_k), by_pos(cache_dil_v))

    tp = batch * seq
    mk32, mv32, mk16, mv16 = _mem_kv_call(mem_prompt.reshape(batch * N_MEM, d), w)
    tables_p = _rope_tables(jnp.arange(seq, dtype=jnp.int32))
    (h_p, qa, ka, va, qb0, kb, vb, q4, k4, v4, q16, k16, v16, ka32, va32, kb32, vb32) = _front_call(
        x_prompt.reshape(tp, d), tables_p, w, sample=False, batch=batch)
    oa = _attn_a_call(sink, qa, ka, va, batch, seq)
    unsplit = lambda x: x.reshape(batch, 1, seq, KV_B_W)
    groups = ((unsplit(qb0), unsplit(kb), unsplit(vb)), (q4, k4, v4), (q16, k16, v16))
    obs, lses = zip(*[_attn_b_call(*qkv, d) for qkv, d in zip(groups, DILATIONS)])
    y_p = _back_prompt_call(h_p, oa, obs, lses, mk16.reshape(batch, N_MEM, MEM_W),
                            mv16.reshape(batch, N_MEM, MEM_W), w, seq)

    h2_s, qc_s = _back_sample_a_call(h_s, oa_s, ob_s, w)
    oc_s = _cross_sample_call(qc_s, by_slot_head(cache_mem_k), by_slot_head(cache_mem_v))
    y_s = _back_sample_b_call(h2_s, oc_s, w)

    keep_a, keep_b = min(CACHE_A, seq), min(CACHE_B, seq)
    tail = lambda x, keep, heads: x.reshape(batch, seq, heads * HEAD_DIM)[:, seq - keep:].reshape(
        1, batch, keep, heads, HEAD_DIM)
    new = lambda x, heads: x.reshape(1, nb, t_new, heads, HEAD_DIM)
    memo = lambda x: x.reshape(1, batch, N_MEM, MEM_HEADS, MEM_HEAD_DIM)
    return (y_p.reshape(batch, seq, d), y_s.reshape(nb, t_new, d),
            tail(ka32, keep_a, N_KV_A), tail(va32, keep_a, N_KV_A),
            tail(kb32, keep_b, N_KV_B), tail(vb32, keep_b, N_KV_B),
            memo(mk32), memo(mv32),
            new(ka32_s, N_KV_A), new(va32_s, N_KV_A), new(kb32_s, N_KV_B), new(vb32_s, N_KV_B))
```

```python
import functools

import jax
import jax.numpy as jnp
import numpy as np
from jax import lax
from jax.experimental import pallas as pl
from jax.experimental.pallas import tpu as pltpu

F32 = jnp.float32
BF16 = jnp.bfloat16

D_MODEL = 1024
D_FF = 2816
HEAD_DIM = 64
ROT_HALF = 8
ROPE_THETA = 500000.0
ATTN_SCALE = HEAD_DIM ** -0.5
RMS_EPS = 1e-6
PAST_LEN = 16384

N_HEADS_A = 8
N_KV_A = 2
N_KV_B = 4
DILATIONS = (1, 4, 16)
BAND_STEPS = 128
Q_A_W = N_HEADS_A * HEAD_DIM
KV_A_W = N_KV_A * HEAD_DIM
Q_B_W = len(DILATIONS) * N_KV_B * HEAD_DIM
KV_B_W = N_KV_B * HEAD_DIM
QKV_W = Q_A_W + 2 * KV_A_W + Q_B_W + 2 * KV_B_W
MEM_HEADS = 4
MEM_HEAD_DIM = 128
MEM_W = MEM_HEADS * MEM_HEAD_DIM
MEM_SCALE = MEM_HEAD_DIM ** -0.5
N_MEM = 256

HEAD_PERM_A = (0, 4, 1, 5, 2, 6, 3, 7)

LANES = 128
BLOCK = 128
TOKEN_TILE = 512
FRONT_ROW_PARTS = 2
BACK_ROW_PARTS = 1
ATTN_CHUNK = 1024
FF_CHUNKS = ((0, 1536), (1536, 2816))
VMEM_LIMIT = 60 * 1024 * 1024
NEG_INF = float("-inf")


def _rms(x, g):
    ms = jnp.mean(x * x, axis=-1, keepdims=True)
    return x * lax.rsqrt(ms + RMS_EPS) * g


def _dot(a, b):
    return jnp.dot(a, b, preferred_element_type=F32)


def _dot_nt(a, b):
    return lax.dot_general(a, b, (((1,), (1,)), ((), ())), preferred_element_type=F32)


def _ffn_half(xs, g_ref, wg_ref, wu_ref, wd_ref):
    us = [_rms(x, g_ref[...]).astype(BF16) for x in xs]
    accs = [None] * len(xs)
    for lo, hi in FF_CHUNKS:
        gates = [_dot(u, wg_ref[:, lo:hi]) for u in us]
        ups = [_dot(u, wu_ref[:, lo:hi]) for u in us]
        acts = [(gate * jax.nn.sigmoid(gate) * up).astype(BF16) for gate, up in zip(gates, ups)]
        for i, act in enumerate(acts):
            part = _dot(act, wd_ref[lo:hi, :])
            accs[i] = part if accs[i] is None else accs[i] + part
    return [x + 0.5 * acc for x, acc in zip(xs, accs)]


def _row_blocks(ref, parts):
    n = ref.shape[0] // parts
    return [slice(i * n, (i + 1) * n) for i in range(parts)]


def _swap_halves(x):
    return pltpu.roll(x, HEAD_DIM, 1)


def _low_half():
    return lax.broadcasted_iota(jnp.int32, (1, LANES), 1) < HEAD_DIM


def _pair_kv_groups(chunks):
    low = _low_half()
    c0, c1, c2, c3 = chunks
    return [jnp.where(low, c0, _swap_halves(c2)), jnp.where(low, _swap_halves(c0), c2),
            jnp.where(low, c1, _swap_halves(c3)), jnp.where(low, _swap_halves(c1), c3)]


def _merge_pairs(per_head):
    low = _low_half()
    return [jnp.where(low, per_head[2 * c], per_head[2 * c + 1]) for c in range(N_HEADS_A // 2)]


def _front_kernel(x_ref, cos_ref, sa_ref, sb_ref, g1_ref, wg_ref, wu_ref, wd_ref, gm_ref, wqkv_ref,
                  h_ref, *rest, sample):
    blocks = _row_blocks(x_ref, FRONT_ROW_PARTS)
    hs = _ffn_half([x_ref[rb, :] for rb in blocks], g1_ref, wg_ref, wu_ref, wd_ref)
    for rb, h in zip(blocks, hs):
        h_ref[rb, :] = h
    ns = [_rms(h, gm_ref[...]).astype(BF16) for h in hs]
    zs = [_dot(n, wqkv_ref[...]) for n in ns]
    for rb, z in zip(blocks, zs):
        _front_emit(rb, z, cos_ref, sa_ref, sb_ref, rest, sample)


def _front_emit(rb, z, cos_ref, sa_ref, sb_ref, rest, sample):
    cos, sa, sb = cos_ref[rb, :], sa_ref[rb, :], sb_ref[rb, :]

    def chunk(c):
        return z[:, c * LANES:(c + 1) * LANES]

    def rope(c):
        zc = chunk(c)
        return zc * cos + pltpu.roll(zc, LANES - ROT_HALF, 1) * sa + pltpu.roll(zc, ROT_HALF, 1) * sb

    nqa, nka = Q_A_W // LANES, KV_A_W // LANES
    nqb, nkb = Q_B_W // LANES, KV_B_W // LANES
    c0 = 0
    qa = _pair_kv_groups([rope(c0 + c) * ATTN_SCALE for c in range(nqa)])
    c0 += nqa
    ka = [rope(c0 + c) for c in range(nka)]
    c0 += nka
    va = [chunk(c0 + c) for c in range(nka)]
    c0 += nka
    qb = [rope(c0 + c) * ATTN_SCALE for c in range(nqb)]
    c0 += nqb
    kb = [rope(c0 + c) for c in range(nkb)]
    c0 += nkb
    vb = [chunk(c0 + c) for c in range(nkb)]

    def store(ref, parts, dtype):
        for c, p in enumerate(parts):
            ref[rb, c * LANES:(c + 1) * LANES] = p.astype(dtype)

    def fill(scr, parts):
        for c, p in enumerate(parts):
            scr[c, rb, :] = p

    def store_split(ref, scr, d):
        n = (rb.stop - rb.start) // d
        dst = slice(rb.start // d, rb.start // d + n)
        for r in range(d):
            for c in range(scr.shape[0]):
                ref[0, r, dst, c * LANES:(c + 1) * LANES] = scr[c, pl.ds(rb.start + r, n, stride=d), :].astype(BF16)

    if sample:
        q32_ref, ka32_ref, va32_ref, kb32_ref, vb32_ref = rest
        store(q32_ref, qa + qb, F32)
    else:
        (qa_ref, ka_ref, va_ref, qb0_ref, kb_ref, vb_ref, q4_ref, k4_ref, v4_ref, q16_ref, k16_ref, v16_ref,
         ka32_ref, va32_ref, kb32_ref, vb32_ref, q4_s, q16_s, kb_s, vb_s) = rest
        gw = KV_B_W // LANES
        store(qa_ref, qa, BF16)
        store(ka_ref, ka, BF16)
        store(va_ref, va, BF16)
        store(qb0_ref, qb[0:gw], BF16)
        store(kb_ref, kb, BF16)
        store(vb_ref, vb, BF16)
        fill(q4_s, qb[gw:2 * gw])
        fill(q16_s, qb[2 * gw:3 * gw])
        fill(kb_s, kb)
        fill(vb_s, vb)
        store_split(q4_ref, q4_s, DILATIONS[1])
        store_split(q16_ref, q16_s, DILATIONS[2])
        store_split(k4_ref, kb_s, DILATIONS[1])
        store_split(k16_ref, kb_s, DILATIONS[2])
        store_split(v4_ref, vb_s, DILATIONS[1])
        store_split(v16_ref, vb_s, DILATIONS[2])
    store(ka32_ref, ka, F32)
    store(va32_ref, va, F32)
    store(kb32_ref, kb, F32)
    store(vb32_ref, vb, F32)


def _const_spec(shape, index=None):
    index = (0,) * len(shape) if index is None else index
    return pl.BlockSpec(shape, lambda *_: index, pipeline_mode=pl.Buffered(1))


def _front_call(x, tables, w, *, sample, batch=1):
    t = x.shape[0]
    tm = TOKEN_TILE
    nsteps = t // tm
    tpb = nsteps // batch
    seq = t // batch
    cos_t, sa_t, sb_t = tables
    tab_blocks = cos_t.shape[0] // tm
    row = lambda i: (i, 0)
    tab = lambda i: (i % tab_blocks, 0)

    def tok(width, dtype):
        return jax.ShapeDtypeStruct((t, width), dtype), pl.BlockSpec((tm, width), row)

    def split(d):
        return (jax.ShapeDtypeStruct((batch, d, seq // d, KV_B_W), BF16),
                pl.BlockSpec((1, d, tm // d, KV_B_W), lambda i: (i // tpb, 0, i % tpb, 0)))

    outs = [tok(D_MODEL, F32)]
    scratch = []
    if sample:
        outs += [tok(Q_A_W + Q_B_W, F32)]
    else:
        outs += [tok(Q_A_W, BF16), tok(KV_A_W, BF16), tok(KV_A_W, BF16),
                 tok(KV_B_W, BF16), tok(KV_B_W, BF16), tok(KV_B_W, BF16)]
        outs += [split(DILATIONS[1])] * 3 + [split(DILATIONS[2])] * 3
        scratch = [pltpu.VMEM((KV_B_W // LANES, tm, LANES), F32)] * 4
    outs += [tok(KV_A_W, F32), tok(KV_A_W, F32), tok(KV_B_W, F32), tok(KV_B_W, F32)]
    out_shape, out_specs = zip(*outs)
    in_specs = [pl.BlockSpec((tm, D_MODEL), row),
                pl.BlockSpec((tm, LANES), tab), pl.BlockSpec((tm, LANES), tab), pl.BlockSpec((tm, LANES), tab),
                _const_spec((1, D_MODEL)), _const_spec((D_MODEL, D_FF)), _const_spec((D_MODEL, D_FF)),
                _const_spec((D_FF, D_MODEL)), _const_spec((1, D_MODEL)), _const_spec((D_MODEL, QKV_W), (0, 0))]
    return pl.pallas_call(
        functools.partial(_front_kernel, sample=sample),
        grid=(nsteps,),
        in_specs=in_specs,
        out_specs=list(out_specs),
        out_shape=list(out_shape),
        scratch_shapes=scratch,
        compiler_params=pltpu.CompilerParams(dimension_semantics=("parallel",), vmem_limit_bytes=VMEM_LIMIT),
        name="front_sample" if sample else "front_prompt",
    )(x, cos_t, sa_t, sb_t, w["ffn1_norm"], w["ffn1_wg"], w["ffn1_wu"], w["ffn1_wd"], w["mix_norm"], w["w_in"])


def _lane_block_masks(width, block, dtype):
    lane = lax.broadcasted_iota(jnp.int32, (1, width), 1)
    return [((lane >= i * block) & (lane < (i + 1) * block)).astype(dtype) for i in range(width // block)]


def _band_mask():
    qi = lax.broadcasted_iota(jnp.int32, (BLOCK, 2 * BLOCK), 0)
    kj = lax.broadcasted_iota(jnp.int32, (BLOCK, 2 * BLOCK), 1)
    dist = qi + BLOCK - kj
    return (dist >= 0) & (dist <= BAND_STEPS), kj >= BLOCK


def _band_block(load_qbd, load_k, load_v, mask, nheads, finish):
    s = _dot_nt(load_qbd(), load_k())
    yield
    ps, ms, ls = [], [], []
    for p in range(nheads):
        sp = jnp.where(mask, s[p * BLOCK:(p + 1) * BLOCK], NEG_INF)
        m = jnp.max(sp, axis=-1, keepdims=True)
        e = jnp.exp(sp - m)
        ls.append(jnp.sum(e, axis=-1, keepdims=True))
        ms.append(m)
        ps.append(e.astype(BF16))
    pc = jnp.concatenate(ps, axis=0)
    yield
    finish(_dot(pc, load_v()), ms, ls)
    yield


def _issue_skewed(items, nstages=3):
    for tick in range(len(items) + nstages - 1):
        for s in reversed(range(nstages)):
            i = tick - s
            if 0 <= i < len(items):
                next(items[i])


def _kv_window(cur_ref, prev_ref, qb):
    if qb == 0:
        return jnp.concatenate([prev_ref[...], cur_ref[0:BLOCK]], axis=0)
    return cur_ref[(qb - 1) * BLOCK:(qb + 1) * BLOCK]


def _attn_a_kernel(sink_ref, q_ref, kc_ref, kp_ref, vc_ref, vp_ref, o_ref):
    first_chunk = pl.program_id(1) == 0
    band, in_cur = _band_mask()
    half_bf = _lane_block_masks(LANES, HEAD_DIM, BF16)

    def block(qb):
        mask = band & (in_cur | jnp.logical_not(first_chunk)) if qb == 0 else band
        rows = slice(qb * BLOCK, (qb + 1) * BLOCK)

        def load_qbd():
            return jnp.concatenate(
                [q_ref[rows, (p // 2) * LANES:(p // 2 + 1) * LANES] * half_bf[p % 2] for p in range(N_HEADS_A)],
                axis=0)

        def finish(o, ms, ls):
            normed = []
            for p in range(N_HEADS_A):
                den = ls[p] + jnp.exp(sink_ref[HEAD_PERM_A[p]] - ms[p])
                normed.append(o[p * BLOCK:(p + 1) * BLOCK] * (1.0 / den))
            for c, pair in enumerate(_merge_pairs(normed)):
                o_ref[rows, c * LANES:(c + 1) * LANES] = pair.astype(BF16)

        return _band_block(load_qbd, lambda: _kv_window(kc_ref, kp_ref, qb), lambda: _kv_window(vc_ref, vp_ref, qb),
                           mask, N_HEADS_A, finish)

    for qb in range(q_ref.shape[0] // BLOCK):
        for _ in block(qb):
            pass


def _attn_a_call(sink, qa, ka, va, batch, seq):
    t = qa.shape[0]
    cpb = seq // ATTN_CHUNK
    bpc = ATTN_CHUNK // BLOCK
    cur = lambda b, c: (b * cpb + c, 0)
    prev = lambda b, c: (b * cpb * bpc + jnp.maximum(c * bpc - 1, 0), 0)
    return pl.pallas_call(
        _attn_a_kernel,
        grid=(batch, cpb),
        in_specs=[pl.BlockSpec(memory_space=pltpu.SMEM),
                  pl.BlockSpec((ATTN_CHUNK, Q_A_W), cur),
                  pl.BlockSpec((ATTN_CHUNK, KV_A_W), cur), pl.BlockSpec((BLOCK, KV_A_W), prev),
                  pl.BlockSpec((ATTN_CHUNK, KV_A_W), cur), pl.BlockSpec((BLOCK, KV_A_W), prev)],
        out_specs=pl.BlockSpec((ATTN_CHUNK, Q_A_W), cur),
        out_shape=jax.ShapeDtypeStruct((t, Q_A_W), BF16),
        compiler_params=pltpu.CompilerParams(dimension_semantics=("parallel", "parallel"),
                                             vmem_limit_bytes=VMEM_LIMIT),
        name="attn_a_prompt",
    )(sink, qa, ka, ka, va, va)


def _attn_b_kernel(q_ref, kc_ref, kp_ref, vc_ref, vp_ref, o_ref, lse_ref):
    first_chunk = pl.program_id(2) == 0
    band, in_cur = _band_mask()
    head_bf = _lane_block_masks(KV_B_W, HEAD_DIM, BF16)
    lane = lax.broadcasted_iota(jnp.int32, (1, KV_B_W), 1)

    def block(qb):
        mask = band & (in_cur | jnp.logical_not(first_chunk)) if qb == 0 else band
        rows = slice(qb * BLOCK, (qb + 1) * BLOCK)

        def load_qbd():
            q = q_ref[rows, :]
            return jnp.concatenate([q * head_bf[p] for p in range(N_KV_B)], axis=0)

        def finish(o, ms, ls):
            out = lse = None
            for p in reversed(range(N_KV_B)):
                op = o[p * BLOCK:(p + 1) * BLOCK] * (1.0 / ls[p])
                lp = jnp.broadcast_to(ms[p] + jnp.log(ls[p]), (BLOCK, KV_B_W))
                if out is None:
                    out, lse = op, lp
                else:
                    sel = lane < (p + 1) * HEAD_DIM
                    out, lse = jnp.where(sel, op, out), jnp.where(sel, lp, lse)
            o_ref[rows, :] = out
            lse_ref[rows, :] = lse

        return _band_block(load_qbd, lambda: _kv_window(kc_ref, kp_ref, qb), lambda: _kv_window(vc_ref, vp_ref, qb),
                           mask, N_KV_B, finish)

    _issue_skewed([block(qb) for qb in range(q_ref.shape[0] // BLOCK)])


def _attn_b_call(q, k, v, d):
    batch, _, rows, _ = q.shape
    chunk = min(ATTN_CHUNK, rows)
    cpb = rows // chunk
    bpc = chunk // BLOCK
    cur = pl.BlockSpec((None, None, chunk, KV_B_W), lambda b, r, c: (b, r, c, 0))
    prev = pl.BlockSpec((None, None, BLOCK, KV_B_W), lambda b, r, c: (b, r, jnp.maximum(c * bpc - 1, 0), 0))
    return pl.pallas_call(
        _attn_b_kernel,
        grid=(batch, d, cpb),
        in_specs=[cur, cur, prev, cur, prev],
        out_specs=[cur, cur],
        out_shape=[jax.ShapeDtypeStruct((batch, d, rows, KV_B_W), F32)] * 2,
        compiler_params=pltpu.CompilerParams(dimension_semantics=("parallel", "parallel", "parallel"),
                                             vmem_limit_bytes=VMEM_LIMIT),
        name=f"attn_b_prompt_d{d}",
    )(q, k, k, v, v)


SAMPLE_T = 8
CACHE_A = 128
CACHE_B = 2048
SAMPLE_ELEMS = 4


def _new_rows_block(new):
    return jnp.concatenate([new, jnp.zeros((BLOCK - SAMPLE_T, new.shape[1]), F32)], axis=0)


def _cached_scores(qbd, kt, knew):
    return _dot(qbd, kt), _dot_nt(qbd, knew)


def _cached_softmax(s_c, s_n, mask_c, mask_n):
    s_c = jnp.where(mask_c, s_c, NEG_INF)
    s_n = jnp.where(mask_n, s_n, NEG_INF)
    m = jnp.maximum(jnp.max(s_c, axis=-1, keepdims=True), jnp.max(s_n, axis=-1, keepdims=True))
    e_c = jnp.exp(s_c - m)
    e_n = jnp.exp(s_n - m)
    l = jnp.sum(e_c, axis=-1, keepdims=True) + jnp.sum(e_n, axis=-1, keepdims=True)
    return e_c, e_n, m, l


def _cached_pv(e_c, e_n, vt, vnew):
    return _dot_nt(e_c, vt) + _dot(e_n, vnew)


def _attn_sample_kernel(sink_ref, q_ref, kan_ref, van_ref, kbn_ref, vbn_ref, cka_ref, cva_ref, ckb_ref, cvb_ref,
                        oa_ref, ob_ref):
    half_f = _lane_block_masks(LANES, HEAD_DIM, F32)
    lane_b = lax.broadcasted_iota(jnp.int32, (1, KV_B_W), 1)

    def dist(nrows, ncols, offset):
        r = lax.broadcasted_iota(jnp.int32, (nrows, ncols), 0)
        i = lax.broadcasted_iota(jnp.int32, (nrows, ncols), 1)
        return r, offset + (r & (SAMPLE_T - 1)) - i

    nrow_a = N_HEADS_A * SAMPLE_T
    _, dac = dist(nrow_a, CACHE_A, CACHE_A)
    _, dan = dist(nrow_a, BLOCK, 0)
    mask_ac = (dac >= 0) & (dac <= BAND_STEPS)
    mask_an = (dan >= 0) & (dan <= BAND_STEPS)
    nrow_b = len(DILATIONS) * N_KV_B * SAMPLE_T
    rows_per_group = N_KV_B * SAMPLE_T

    def mask_b(ncols, offset):
        r, db = dist(nrow_b, ncols, offset)
        dil = jnp.where(r < rows_per_group, DILATIONS[0], jnp.where(r < 2 * rows_per_group, DILATIONS[1], DILATIONS[2]))
        return (db >= 0) & (db <= BAND_STEPS * dil) & ((db & (dil - 1)) == 0)

    mask_bc = mask_b(CACHE_B, CACHE_B)
    mask_bn = mask_b(BLOCK, 0)

    sink_col = jnp.concatenate(
        [jnp.full((SAMPLE_T, 1), sink_ref[HEAD_PERM_A[p]], F32) for p in range(N_HEADS_A)], axis=0)

    def element(j):
        rows = slice(j * SAMPLE_T, (j + 1) * SAMPLE_T)
        qbd_a = jnp.concatenate(
            [q_ref[rows, (p // 2) * LANES:(p // 2 + 1) * LANES] * half_f[p % 2] for p in range(N_HEADS_A)],
            axis=0)
        qrows = []
        for g in range(len(DILATIONS)):
            for kvh in range(N_KV_B):
                c = (Q_A_W // LANES) + 2 * g + kvh // 2
                part = q_ref[rows, c * LANES:(c + 1) * LANES] * half_f[kvh % 2]
                zero = jnp.zeros_like(part)
                qrows.append(jnp.concatenate([part, zero] if kvh // 2 == 0 else [zero, part], axis=1))
        qbd_b = jnp.concatenate(qrows, axis=0)
        scores_a = _cached_scores(qbd_a, cka_ref[j], _new_rows_block(kan_ref[rows, :]))
        scores_b = _cached_scores(qbd_b, ckb_ref[j], _new_rows_block(kbn_ref[rows, :]))
        yield
        ea_c, ea_n, ma, la = _cached_softmax(*scores_a, mask_ac, mask_an)
        eb_c, eb_n, m, l = _cached_softmax(*scores_b, mask_bc, mask_bn)
        yield
        o = _cached_pv(ea_c, ea_n, cva_ref[j], _new_rows_block(van_ref[rows, :]))
        o = o * (1.0 / (la + jnp.exp(sink_col - ma)))
        per_head = [o[p * SAMPLE_T:(p + 1) * SAMPLE_T] for p in range(N_HEADS_A)]
        for c, pair in enumerate(_merge_pairs(per_head)):
            oa_ref[rows, c * LANES:(c + 1) * LANES] = pair
        o = _cached_pv(eb_c, eb_n, cvb_ref[j], _new_rows_block(vbn_ref[rows, :]))
        out = None
        for kvh in reversed(range(N_KV_B)):
            sl = [slice((g * N_KV_B + kvh) * SAMPLE_T, (g * N_KV_B + kvh + 1) * SAMPLE_T)
                  for g in range(len(DILATIONS))]
            mj = jnp.maximum(jnp.maximum(m[sl[0]], m[sl[1]]), m[sl[2]])
            ws = [jnp.exp(m[x] - mj) for x in sl]
            den = ws[0] * l[sl[0]] + ws[1] * l[sl[1]] + ws[2] * l[sl[2]]
            num = ws[0] * o[sl[0]] + ws[1] * o[sl[1]] + ws[2] * o[sl[2]]
            okv = num * (1.0 / den)
            out = okv if out is None else jnp.where(lane_b < (kvh + 1) * HEAD_DIM, okv, out)
        ob_ref[rows, :] = out
        yield

    elements = [element(j) for j in range(SAMPLE_ELEMS)]
    for _ in range(3):
        for e in elements:
            next(e)


def _attn_sample_call(sink, q32, ka32, va32, kb32, vb32, cka, cva, ckb, cvb):
    nb = cka.shape[0]
    bt = SAMPLE_ELEMS
    t = q32.shape[0]
    tok = lambda w: pl.BlockSpec((bt * SAMPLE_T, w), lambda i: (i, 0))
    cache = lambda w, n: pl.BlockSpec((bt, w, n), lambda i: (i, 0, 0))
    return pl.pallas_call(
        _attn_sample_kernel,
        grid=(nb // bt,),
        in_specs=[pl.BlockSpec(memory_space=pltpu.SMEM), tok(Q_A_W + Q_B_W),
                  tok(KV_A_W), tok(KV_A_W), tok(KV_B_W), tok(KV_B_W),
                  cache(KV_A_W, CACHE_A), cache(KV_A_W, CACHE_A), cache(KV_B_W, CACHE_B), cache(KV_B_W, CACHE_B)],
        out_specs=[tok(Q_A_W), tok(KV_B_W)],
        out_shape=[jax.ShapeDtypeStruct((t, Q_A_W), F32), jax.ShapeDtypeStruct((t, KV_B_W), F32)],
        compiler_params=pltpu.CompilerParams(dimension_semantics=("parallel",), vmem_limit_bytes=VMEM_LIMIT),
        name="attn_sample",
    )(sink, q32, ka32, va32, kb32, vb32, cka, cva, ckb, cvb)


def _mix_out(h, oa, ob, gm_ref, wgate_ref, wba_ref, wbb_ref, wout_ref):
    n = _rms(h, gm_ref[...]).astype(BF16)
    gates = _dot(n, wgate_ref[...])
    ga = jax.nn.sigmoid(gates[:, :D_MODEL])
    gb = jax.nn.sigmoid(gates[:, D_MODEL:])
    mixed = ga * _dot(oa.astype(BF16), wba_ref[...]) + gb * _dot(ob.astype(BF16), wbb_ref[...])
    return h + _dot(mixed.astype(BF16), wout_ref[...])


def _combine_groups(os, ls):
    m = jnp.maximum(jnp.maximum(ls[0], ls[1]), ls[2])
    es = [jnp.exp(x - m) for x in ls]
    den = es[0] + es[1] + es[2]
    num = es[0] * os[0] + es[1] * os[1] + es[2] * os[2]
    return num * (1.0 / den)


def _merge_split(split_ref, scr, rb):
    d = split_ref.shape[1]
    n = (rb.stop - rb.start) // d
    src = slice(rb.start // d, rb.start // d + n)
    for r in range(d):
        for c in range(scr.shape[0]):
            scr[c, pl.ds(rb.start + r, n, stride=d), :] = split_ref[0, r, src, c * LANES:(c + 1) * LANES]
    return jnp.concatenate([scr[c, rb, :] for c in range(scr.shape[0])], axis=1)


def _softmax_rows(s):
    m = jnp.max(s, axis=-1, keepdims=True)
    e = jnp.exp(s - m)
    return e * (1.0 / jnp.sum(e, axis=-1, keepdims=True))


def _back_prompt_kernel(h_ref, oa_ref, o0_ref, o1_ref, o2_ref, l0_ref, l1_ref, l2_ref, mk_ref, mv_ref,
                        gm_ref, wgate_ref, wba_ref, wbb_ref, wout_ref, gq_ref, wq_ref, wo_ref,
                        g2_ref, wg_ref, wu_ref, wd_ref, gf_ref, y_ref, o1_s, o2_s, l1_s, l2_s):
    blocks = _row_blocks(h_ref, BACK_ROW_PARTS)
    h3s = []
    for rb in blocks:
        ob = _combine_groups(
            (o0_ref[0, 0, rb, :], _merge_split(o1_ref, o1_s, rb), _merge_split(o2_ref, o2_s, rb)),
            (l0_ref[0, 0, rb, :], _merge_split(l1_ref, l1_s, rb), _merge_split(l2_ref, l2_s, rb)))
        h2 = _mix_out(h_ref[rb, :], oa_ref[rb, :], ob, gm_ref, wgate_ref, wba_ref, wbb_ref, wout_ref)
        q = _dot(_rms(h2, gq_ref[...]).astype(BF16), wq_ref[...])
        heads = [None] * MEM_HEADS

        def head(hd):
            cols = slice(hd * MEM_HEAD_DIM, (hd + 1) * MEM_HEAD_DIM)
            s = _dot_nt(q[:, cols].astype(BF16), mk_ref[0, :, cols]) * MEM_SCALE
            yield
            e = jnp.exp(s - jnp.max(s, axis=-1, keepdims=True))
            inv = 1.0 / jnp.sum(e, axis=-1, keepdims=True)
            yield
            heads[hd] = _dot(e.astype(BF16), mv_ref[0, :, cols]) * inv
            yield

        _issue_skewed([head(hd) for hd in range(MEM_HEADS)])
        oc = jnp.concatenate(heads, axis=1).astype(BF16)
        h3s.append(h2 + _dot(oc, wo_ref[...]))
    for rb, y in zip(blocks, _ffn_half(h3s, g2_ref, wg_ref, wu_ref, wd_ref)):
        y_ref[rb, :] = _rms(y, gf_ref[...])


def _back_prompt_call(h, oa, obs, lses, mk, mv, w, seq):
    t = h.shape[0]
    tm = TOKEN_TILE
    tiles_per_batch = seq // tm
    row = lambda i: (i, 0)
    mem = lambda i: (i // tiles_per_batch, 0, 0)
    tok = lambda width: pl.BlockSpec((tm, width), row)
    split = lambda d: pl.BlockSpec((1, d, tm // d, KV_B_W),
                                   lambda i: (i // tiles_per_batch, 0, i % tiles_per_batch, 0))
    groups = [split(d) for d in DILATIONS]
    in_specs = ([tok(D_MODEL), tok(Q_A_W)] + groups + groups
                + [pl.BlockSpec((1, N_MEM, MEM_W), mem), pl.BlockSpec((1, N_MEM, MEM_W), mem)]
                + [_const_spec((1, D_MODEL)), _const_spec((D_MODEL, 2 * D_MODEL), (0, 1)), _const_spec((Q_A_W, D_MODEL)),
                   _const_spec((KV_B_W, D_MODEL)), _const_spec((D_MODEL, D_MODEL)),
                   _const_spec((1, D_MODEL)), _const_spec((D_MODEL, MEM_W)), _const_spec((MEM_W, D_MODEL)),
                   _const_spec((1, D_MODEL)), _const_spec((D_MODEL, D_FF)), _const_spec((D_MODEL, D_FF)),
                   _const_spec((D_FF, D_MODEL)), _const_spec((1, D_MODEL))])
    return pl.pallas_call(
        _back_prompt_kernel,
        grid=(t // tm,),
        in_specs=in_specs,
        out_specs=tok(D_MODEL),
        out_shape=jax.ShapeDtypeStruct((t, D_MODEL), F32),
        scratch_shapes=[pltpu.VMEM((KV_B_W // LANES, tm, LANES), F32)] * 4,
        compiler_params=pltpu.CompilerParams(dimension_semantics=("parallel",), vmem_limit_bytes=VMEM_LIMIT),
        name="back_prompt",
    )(h, oa, *obs, *lses, mk, mv, w["mix_norm"], w["w_in"], w["w_branch_a"], w["w_branch_b"], w["w_out"],
      w["mem_q_norm"], w["w_mem_q"], w["w_mem_o"], w["ffn2_norm"], w["ffn2_wg"], w["ffn2_wu"], w["ffn2_wd"],
      w["final_norm"])


def _back_sample_a_kernel(h_ref, oa_ref, ob_ref, gm_ref, wgate_ref, wba_ref, wbb_ref, wout_ref, gq_ref, wq_ref,
                          h2_ref, q_ref):
    h2 = _mix_out(h_ref[...], oa_ref[...], ob_ref[...], gm_ref, wgate_ref, wba_ref, wbb_ref, wout_ref)
    h2_ref[...] = h2
    q_ref[...] = _dot(_rms(h2, gq_ref[...]).astype(BF16), wq_ref[...])


def _back_sample_a_call(h, oa, ob, w):
    t = h.shape[0]
    tm = TOKEN_TILE
    tok = lambda width: pl.BlockSpec((tm, width), lambda i: (i, 0))
    return pl.pallas_call(
        _back_sample_a_kernel,
        grid=(t // tm,),
        in_specs=[tok(D_MODEL), tok(Q_A_W), tok(KV_B_W),
                  _const_spec((1, D_MODEL)), _const_spec((D_MODEL, 2 * D_MODEL), (0, 1)), _const_spec((Q_A_W, D_MODEL)),
                  _const_spec((KV_B_W, D_MODEL)), _const_spec((D_MODEL, D_MODEL)),
                  _const_spec((1, D_MODEL)), _const_spec((D_MODEL, MEM_W))],
        out_specs=[tok(D_MODEL), tok(MEM_W)],
        out_shape=[jax.ShapeDtypeStruct((t, D_MODEL), F32), jax.ShapeDtypeStruct((t, MEM_W), F32)],
        compiler_params=pltpu.CompilerParams(dimension_semantics=("parallel",), vmem_limit_bytes=VMEM_LIMIT),
        name="back_sample_mix",
    )(h, oa, ob, w["mix_norm"], w["w_in"], w["w_branch_a"], w["w_branch_b"], w["w_out"],
      w["mem_q_norm"], w["w_mem_q"])


CROSS_BATCH_TILE = 8


def _cross_sample_kernel(q_ref, mk_ref, mv_ref, o_ref):
    nrow = MEM_HEADS * SAMPLE_T
    qhead = lax.broadcasted_iota(jnp.int32, (nrow, N_MEM * MEM_HEADS), 0) >> (SAMPLE_T.bit_length() - 1)
    khead = lax.broadcasted_iota(jnp.int32, (nrow, N_MEM * MEM_HEADS), 1) & (MEM_HEADS - 1)
    own_head = qhead == khead
    def element(bi):
        rows = slice(bi * SAMPLE_T, (bi + 1) * SAMPLE_T)
        qs = jnp.concatenate(
            [q_ref[rows, hd * MEM_HEAD_DIM:(hd + 1) * MEM_HEAD_DIM] for hd in range(MEM_HEADS)], axis=0)
        s = _dot_nt(qs, mk_ref[bi]) * MEM_SCALE
        yield
        p = _softmax_rows(jnp.where(own_head, s, NEG_INF))
        yield
        o = _dot(p, mv_ref[bi])
        for hd in range(MEM_HEADS):
            o_ref[rows, hd * MEM_HEAD_DIM:(hd + 1) * MEM_HEAD_DIM] = o[hd * SAMPLE_T:(hd + 1) * SAMPLE_T]
        yield

    _issue_skewed([element(bi) for bi in range(CROSS_BATCH_TILE)])


def _cross_sample_call(q, mk, mv):
    t = q.shape[0]
    bt = CROSS_BATCH_TILE
    tok = pl.BlockSpec((bt * SAMPLE_T, MEM_W), lambda i: (i, 0))
    mem = pl.BlockSpec((bt, N_MEM * MEM_HEADS, MEM_HEAD_DIM), lambda i: (i, 0, 0))
    return pl.pallas_call(
        _cross_sample_kernel,
        grid=(mk.shape[0] // bt,),
        in_specs=[tok, mem, mem],
        out_specs=tok,
        out_shape=jax.ShapeDtypeStruct((t, MEM_W), F32),
        compiler_params=pltpu.CompilerParams(dimension_semantics=("parallel",), vmem_limit_bytes=VMEM_LIMIT),
        name="cross_sample",
    )(q, mk, mv)


def _back_sample_b_kernel(h2_ref, oc_ref, wo_ref, g2_ref, wg_ref, wu_ref, wd_ref, gf_ref, y_ref):
    blocks = _row_blocks(h2_ref, BACK_ROW_PARTS)
    h3s = [h2_ref[rb, :] + _dot(oc_ref[rb, :].astype(BF16), wo_ref[...]) for rb in blocks]
    for rb, y in zip(blocks, _ffn_half(h3s, g2_ref, wg_ref, wu_ref, wd_ref)):
        y_ref[rb, :] = _rms(y, gf_ref[...])


def _back_sample_b_call(h2, oc, w):
    t = h2.shape[0]
    tm = TOKEN_TILE
    tok = lambda width: pl.BlockSpec((tm, width), lambda i: (i, 0))
    return pl.pallas_call(
        _back_sample_b_kernel,
        grid=(t // tm,),
        in_specs=[tok(D_MODEL), tok(MEM_W), _const_spec((MEM_W, D_MODEL)),
                  _const_spec((1, D_MODEL)), _const_spec((D_MODEL, D_FF)), _const_spec((D_MODEL, D_FF)),
                  _const_spec((D_FF, D_MODEL)), _const_spec((1, D_MODEL))],
        out_specs=tok(D_MODEL),
        out_shape=jax.ShapeDtypeStruct((t, D_MODEL), F32),
        compiler_params=pltpu.CompilerParams(dimension_semantics=("parallel",), vmem_limit_bytes=VMEM_LIMIT),
        name="back_sample_ffn",
    )(h2, oc, w["w_mem_o"], w["ffn2_norm"], w["ffn2_wg"], w["ffn2_wu"], w["ffn2_wd"], w["final_norm"])


def _mem_kv_kernel(mem_ref, g_ref, wk_ref, wv_ref, k32_ref, v32_ref, k16_ref, v16_ref):
    u = _rms(mem_ref[...], g_ref[...]).astype(BF16)
    k = _dot(u, wk_ref[...])
    v = _dot(u, wv_ref[...])
    k32_ref[...] = k
    v32_ref[...] = v
    k16_ref[...] = k.astype(BF16)
    v16_ref[...] = v.astype(BF16)


def _mem_kv_call(mem, w):
    rows = mem.shape[0]
    full = lambda width: pl.BlockSpec((rows, width), lambda i: (0, 0))
    return pl.pallas_call(
        _mem_kv_kernel,
        grid=(1,),
        in_specs=[full(D_MODEL), _const_spec((1, D_MODEL)), _const_spec((D_MODEL, MEM_W)),
                  _const_spec((D_MODEL, MEM_W))],
        out_specs=[full(MEM_W)] * 4,
        out_shape=[jax.ShapeDtypeStruct((rows, MEM_W), F32)] * 2 + [jax.ShapeDtypeStruct((rows, MEM_W), BF16)] * 2,
        compiler_params=pltpu.CompilerParams(dimension_semantics=("arbitrary",), vmem_limit_bytes=VMEM_LIMIT),
        name="mem_kv",
    )(mem, w["mem_kv_norm"], w["w_mem_k"], w["w_mem_v"])


def _rope_tables(pos):
    within = np.arange(LANES) % HEAD_DIM
    inv_freq = jnp.power(jnp.float32(ROPE_THETA), -jnp.arange(ROT_HALF, dtype=jnp.float32) / ROT_HALF)
    ang = pos.astype(jnp.float32)[:, None] * inv_freq[None, :]
    cos = jnp.tile(jnp.cos(ang), (1, LANES // ROT_HALF))
    sin = jnp.tile(jnp.sin(ang), (1, LANES // ROT_HALF))
    first = jnp.asarray(within < ROT_HALF)[None, :]
    second = jnp.asarray((within >= ROT_HALF) & (within < 2 * ROT_HALF))[None, :]
    return (jnp.where(first | second, cos, 1.0), jnp.where(first, -sin, 0.0), jnp.where(second, sin, 0.0))


def _prep_weights(p):
    bf = lambda x: x.astype(BF16)
    vec = lambda x: x.reshape(1, -1)
    return {
        "ffn1_norm": vec(p["ffn1_norm"]), "ffn1_wg": bf(p["ffn1_w_gate"]), "ffn1_wu": bf(p["ffn1_w_up"]),
        "ffn1_wd": bf(p["ffn1_w_down"]),
        "mix_norm": vec(p["mix_norm"]), "w_in": bf(p["w_in"]),
        "w_branch_a": bf(p["w_branch_a"].reshape(N_HEADS_A, HEAD_DIM, D_MODEL)[np.array(HEAD_PERM_A)].reshape(
            Q_A_W, D_MODEL)),
        "w_branch_b": bf(p["w_branch_b"]), "w_out": bf(p["w_out"]),
        "mem_q_norm": vec(p["mem_q_norm"]), "mem_kv_norm": vec(p["mem_kv_norm"]),
        "w_mem_q": bf(p["w_mem_q"]), "w_mem_k": bf(p["w_mem_k"]), "w_mem_v": bf(p["w_mem_v"]),
        "w_mem_o": bf(p["w_mem_o"]),
        "ffn2_norm": vec(p["ffn2_norm"]), "ffn2_wg": bf(p["ffn2_w_gate"]), "ffn2_wu": bf(p["ffn2_w_up"]),
        "ffn2_wd": bf(p["ffn2_w_down"]), "final_norm": vec(p["final_norm"]),
    }


def kernel(x_prompt, x_sample, cache_swa_k, cache_swa_v, cache_dil_k, cache_dil_v, cache_mem_k, cache_mem_v, mem_prompt, ffn1_norm, ffn1_w_gate, ffn1_w_up, ffn1_w_down, mix_norm, w_in, attn_sink, w_branch_a, w_branch_b, w_out, mem_q_norm, mem_kv_norm, w_mem_q, w_mem_k, w_mem_v, w_mem_o, ffn2_norm, ffn2_w_gate, ffn2_w_up, ffn2_w_down, final_norm):
    depth = ffn1_norm.shape[0]
    assert depth == 1
    batch, seq, d = x_prompt.shape
    nb, t_new, _ = x_sample.shape
    assert d == D_MODEL and t_new == SAMPLE_T and seq % (TOKEN_TILE * DILATIONS[-1]) == 0 and seq % ATTN_CHUNK == 0
    assert cache_swa_k.shape[2] == CACHE_A and cache_dil_k.shape[2] == CACHE_B
    layer = lambda x: x[0]
    w = _prep_weights(dict(
        ffn1_norm=layer(ffn1_norm), ffn1_w_gate=layer(ffn1_w_gate), ffn1_w_up=layer(ffn1_w_up),
        ffn1_w_down=layer(ffn1_w_down), mix_norm=layer(mix_norm), w_in=layer(w_in),
        w_branch_a=layer(w_branch_a), w_branch_b=layer(w_branch_b), w_out=layer(w_out),
        mem_q_norm=layer(mem_q_norm), mem_kv_norm=layer(mem_kv_norm), w_mem_q=layer(w_mem_q),
        w_mem_k=layer(w_mem_k), w_mem_v=layer(w_mem_v), w_mem_o=layer(w_mem_o), ffn2_norm=layer(ffn2_norm),
        ffn2_w_gate=layer(ffn2_w_gate), ffn2_w_up=layer(ffn2_w_up), ffn2_w_down=layer(ffn2_w_down),
        final_norm=final_norm))
    sink = layer(attn_sink)

    ts = nb * t_new
    pos_s = PAST_LEN + (jnp.arange(TOKEN_TILE, dtype=jnp.int32) % t_new)
    tables_s = _rope_tables(pos_s)
    (h_s, q32, ka32_s, va32_s, kb32_s, vb32_s) = _front_call(x_sample.reshape(ts, d), tables_s, w, sample=True)
    by_pos = lambda c: jnp.transpose(layer(c), (0, 2, 3, 1)).reshape(nb, c.shape[3] * c.shape[4], c.shape[2])
    by_slot_head = lambda c: layer(c).reshape(nb, N_MEM * MEM_HEADS, MEM_HEAD_DIM)

    oa_s, ob_s = _attn_sample_call(
        sink, q32, ka32_s, va32_s, kb32_s, vb32_s,
        by_pos(cache_swa_k), by_pos(cache_swa_v), by_pos(cache_dil_k), by_pos(cache_dil_v))

    tp = batch * seq
    mk32, mv32, mk16, mv16 = _mem_kv_call(mem_prompt.reshape(batch * N_MEM, d), w)
    tables_p = _rope_tables(jnp.arange(seq, dtype=jnp.int32))
    (h_p, qa, ka, va, qb0, kb, vb, q4, k4, v4, q16, k16, v16, ka32, va32, kb32, vb32) = _front_call(
        x_prompt.reshape(tp, d), tables_p, w, sample=False, batch=batch)
    oa = _attn_a_call(sink, qa, ka, va, batch, seq)
    unsplit = lambda x: x.reshape(batch, 1, seq, KV_B_W)
    groups = ((unsplit(qb0), unsplit(kb), unsplit(vb)), (q4, k4, v4), (q16, k16, v16))
    obs, lses = zip(*[_attn_b_call(*qkv, d) for qkv, d in zip(groups, DILATIONS)])
    y_p = _back_prompt_call(h_p, oa, obs, lses, mk16.reshape(batch, N_MEM, MEM_W),
                            mv16.reshape(batch, N_MEM, MEM_W), w, seq)

    h2_s, qc_s = _back_sample_a_call(h_s, oa_s, ob_s, w)
    oc_s = _cross_sample_call(qc_s, by_slot_head(cache_mem_k), by_slot_head(cache_mem_v))
    y_s = _back_sample_b_call(h2_s, oc_s, w)

    keep_a, keep_b = min(CACHE_A, seq), min(CACHE_B, seq)
    tail = lambda x, keep, heads: x.reshape(batch, seq, heads * HEAD_DIM)[:, seq - keep:].reshape(
        1, batch, keep, heads, HEAD_DIM)
    new = lambda x, heads: x.reshape(1, nb, t_new, heads, HEAD_DIM)
    memo = lambda x: x.reshape(1, batch, N_MEM, MEM_HEADS, MEM_HEAD_DIM)
    return (y_p.reshape(batch, seq, d), y_s.reshape(nb, t_new, d),
            tail(ka32, keep_a, N_KV_A), tail(va32, keep_a, N_KV_A),
            tail(kb32, keep_b, N_KV_B), tail(vb32, keep_b, N_KV_B),
            memo(mk32), memo(mv32),
            new(ka32_s, N_KV_A), new(va32_s, N_KV_A), new(kb32_s, N_KV_B), new(vb32_s, N_KV_B))
```

```python
import functools

import jax
import jax.numpy as jnp
import numpy as np
from jax import lax
from jax.experimental import pallas as pl
from jax.experimental.pallas import tpu as pltpu

F32 = jnp.float32
BF16 = jnp.bfloat16

D_MODEL = 1024
D_FF = 2816
HEAD_DIM = 64
ROT_HALF = 8
ROPE_THETA = 500000.0
ATTN_SCALE = HEAD_DIM ** -0.5
RMS_EPS = 1e-6
PAST_LEN = 16384

N_HEADS_A = 8
N_KV_A = 2
N_KV_B = 4
DILATIONS = (1, 4, 16)
BAND_STEPS = 128
Q_A_W = N_HEADS_A * HEAD_DIM
KV_A_W = N_KV_A * HEAD_DIM
Q_B_W = len(DILATIONS) * N_KV_B * HEAD_DIM
KV_B_W = N_KV_B * HEAD_DIM
QKV_W = Q_A_W + 2 * KV_A_W + Q_B_W + 2 * KV_B_W
MEM_HEADS = 4
MEM_HEAD_DIM = 128
MEM_W = MEM_HEADS * MEM_HEAD_DIM
MEM_SCALE = MEM_HEAD_DIM ** -0.5
N_MEM = 256

HEAD_PERM_A = (0, 4, 1, 5, 2, 6, 3, 7)

LANES = 128
BLOCK = 128
TOKEN_TILE = 512
FRONT_ROW_PARTS = 2
BACK_ROW_PARTS = 1
ATTN_CHUNK = 1024
FF_CHUNKS = ((0, 1536), (1536, 2816))
VMEM_LIMIT = 60 * 1024 * 1024
NEG_INF = float("-inf")


def _rms(x, g):
    ms = jnp.mean(x * x, axis=-1, keepdims=True)
    return x * lax.rsqrt(ms + RMS_EPS) * g


def _sigmoid(x):
    return 0.5 * jnp.tanh(0.5 * x) + 0.5


def _dot(a, b):
    return jnp.dot(a, b, preferred_element_type=F32)


def _dot_nt(a, b):
    return lax.dot_general(a, b, (((1,), (1,)), ((), ())), preferred_element_type=F32)


def _ffn_half(xs, g_ref, wg_ref, wu_ref, wd_ref):
    us = [_rms(x, g_ref[...]).astype(BF16) for x in xs]
    accs = [None] * len(xs)
    for lo, hi in FF_CHUNKS:
        gates = [_dot(u, wg_ref[:, lo:hi]) for u in us]
        ups = [_dot(u, wu_ref[:, lo:hi]) for u in us]
        acts = [(gate * _sigmoid(gate) * up).astype(BF16) for gate, up in zip(gates, ups)]
        for i, act in enumerate(acts):
            part = _dot(act, wd_ref[lo:hi, :])
            accs[i] = part if accs[i] is None else accs[i] + part
    return [x + 0.5 * acc for x, acc in zip(xs, accs)]


def _row_blocks(ref, parts):
    n = ref.shape[0] // parts
    return [slice(i * n, (i + 1) * n) for i in range(parts)]


def _swap_halves(x):
    return pltpu.roll(x, HEAD_DIM, 1)


def _low_half():
    return lax.broadcasted_iota(jnp.int32, (1, LANES), 1) < HEAD_DIM


def _pair_kv_groups(chunks):
    low = _low_half()
    c0, c1, c2, c3 = chunks
    return [jnp.where(low, c0, _swap_halves(c2)), jnp.where(low, _swap_halves(c0), c2),
            jnp.where(low, c1, _swap_halves(c3)), jnp.where(low, _swap_halves(c1), c3)]


def _merge_pairs(per_head):
    low = _low_half()
    return [jnp.where(low, per_head[2 * c], per_head[2 * c + 1]) for c in range(N_HEADS_A // 2)]


def _front_kernel(x_ref, cos_ref, sin_ref, g1_ref, wg_ref, wu_ref, wd_ref, gm_ref, wqkv_ref,
                  h_ref, *rest, sample):
    blocks = _row_blocks(x_ref, FRONT_ROW_PARTS)
    hs = _ffn_half([x_ref[rb, :] for rb in blocks], g1_ref, wg_ref, wu_ref, wd_ref)
    for rb, h in zip(blocks, hs):
        h_ref[rb, :] = h
    ns = [_rms(h, gm_ref[...]).astype(BF16) for h in hs]
    zs = [_dot(n, wqkv_ref[...]) for n in ns]
    for rb, z in zip(blocks, zs):
        _front_emit(rb, z, cos_ref, sin_ref, rest, sample)


def _front_emit(rb, z, cos_ref, sin_ref, rest, sample):
    cos, sin = cos_ref[rb, :], sin_ref[rb, :]
    lane_in_head = lax.broadcasted_iota(jnp.int32, (1, LANES), 1) & (HEAD_DIM - 1)
    first = lane_in_head < ROT_HALF

    def chunk(c):
        return z[:, c * LANES:(c + 1) * LANES]

    def rope(c):
        zc = chunk(c)
        partner = jnp.where(first, pltpu.roll(zc, LANES - ROT_HALF, 1), pltpu.roll(zc, ROT_HALF, 1))
        return zc * cos + partner * sin

    nqa, nka = Q_A_W // LANES, KV_A_W // LANES
    nqb, nkb = Q_B_W // LANES, KV_B_W // LANES
    c0 = 0
    qa = _pair_kv_groups([rope(c0 + c) * ATTN_SCALE for c in range(nqa)])
    c0 += nqa
    ka = [rope(c0 + c) for c in range(nka)]
    c0 += nka
    va = [chunk(c0 + c) for c in range(nka)]
    c0 += nka
    qb = [rope(c0 + c) * ATTN_SCALE for c in range(nqb)]
    c0 += nqb
    kb = [rope(c0 + c) for c in range(nkb)]
    c0 += nkb
    vb = [chunk(c0 + c) for c in range(nkb)]

    def store(ref, parts, dtype):
        for c, p in enumerate(parts):
            ref[rb, c * LANES:(c + 1) * LANES] = p.astype(dtype)

    def fill(scr, parts):
        for c, p in enumerate(parts):
            scr[c, rb, :] = p

    def store_split(ref, scr, d):
        n = (rb.stop - rb.start) // d
        dst = slice(rb.start // d, rb.start // d + n)
        for r in range(d):
            for c in range(scr.shape[0]):
                ref[0, r, dst, c * LANES:(c + 1) * LANES] = scr[c, pl.ds(rb.start + r, n, stride=d), :].astype(BF16)

    if sample:
        q32_ref, ka32_ref, va32_ref, kb32_ref, vb32_ref = rest
        store(q32_ref, qa + qb, F32)
    else:
        (qa_ref, ka_ref, va_ref, qb0_ref, kb_ref, vb_ref, q4_ref, k4_ref, v4_ref, q16_ref, k16_ref, v16_ref,
         ka32_ref, va32_ref, kb32_ref, vb32_ref, q4_s, q16_s, kb_s, vb_s) = rest
        gw = KV_B_W // LANES
        store(qa_ref, qa, BF16)
        store(ka_ref, ka, BF16)
        store(va_ref, va, BF16)
        store(qb0_ref, qb[0:gw], BF16)
        store(kb_ref, kb, BF16)
        store(vb_ref, vb, BF16)
        fill(q4_s, qb[gw:2 * gw])
        fill(q16_s, qb[2 * gw:3 * gw])
        fill(kb_s, kb)
        fill(vb_s, vb)
        store_split(q4_ref, q4_s, DILATIONS[1])
        store_split(q16_ref, q16_s, DILATIONS[2])
        store_split(k4_ref, kb_s, DILATIONS[1])
        store_split(k16_ref, kb_s, DILATIONS[2])
        store_split(v4_ref, vb_s, DILATIONS[1])
        store_split(v16_ref, vb_s, DILATIONS[2])
    store(ka32_ref, ka, F32)
    store(va32_ref, va, F32)
    store(kb32_ref, kb, F32)
    store(vb32_ref, vb, F32)


def _const_spec(shape, index=None):
    index = (0,) * len(shape) if index is None else index
    return pl.BlockSpec(shape, lambda *_: index, pipeline_mode=pl.Buffered(1))


def _front_call(x, tables, w, *, sample, batch=1):
    t = x.shape[0]
    tm = TOKEN_TILE
    nsteps = t // tm
    tpb = nsteps // batch
    seq = t // batch
    cos_t, sin_t = tables
    tab_blocks = cos_t.shape[0] // tm
    row = lambda i: (i, 0)
    tab = lambda i: (i % tab_blocks, 0)

    def tok(width, dtype):
        return jax.ShapeDtypeStruct((t, width), dtype), pl.BlockSpec((tm, width), row)

    def split(d):
        return (jax.ShapeDtypeStruct((batch, d, seq // d, KV_B_W), BF16),
                pl.BlockSpec((1, d, tm // d, KV_B_W), lambda i: (i // tpb, 0, i % tpb, 0)))

    outs = [tok(D_MODEL, F32)]
    scratch = []
    if sample:
        outs += [tok(Q_A_W + Q_B_W, F32)]
    else:
        outs += [tok(Q_A_W, BF16), tok(KV_A_W, BF16), tok(KV_A_W, BF16),
                 tok(KV_B_W, BF16), tok(KV_B_W, BF16), tok(KV_B_W, BF16)]
        outs += [split(DILATIONS[1])] * 3 + [split(DILATIONS[2])] * 3
        scratch = [pltpu.VMEM((KV_B_W // LANES, tm, LANES), F32)] * 4
    outs += [tok(KV_A_W, F32), tok(KV_A_W, F32), tok(KV_B_W, F32), tok(KV_B_W, F32)]
    out_shape, out_specs = zip(*outs)
    in_specs = [pl.BlockSpec((tm, D_MODEL), row),
                pl.BlockSpec((tm, LANES), tab), pl.BlockSpec((tm, LANES), tab),
                _const_spec((1, D_MODEL)), _const_spec((D_MODEL, D_FF)), _const_spec((D_MODEL, D_FF)),
                _const_spec((D_FF, D_MODEL)), _const_spec((1, D_MODEL)), _const_spec((D_MODEL, QKV_W), (0, 0))]
    return pl.pallas_call(
        functools.partial(_front_kernel, sample=sample),
        grid=(nsteps,),
        in_specs=in_specs,
        out_specs=list(out_specs),
        out_shape=list(out_shape),
        scratch_shapes=scratch,
        compiler_params=pltpu.CompilerParams(dimension_semantics=("parallel",), vmem_limit_bytes=VMEM_LIMIT),
        name="front_sample" if sample else "front_prompt",
    )(x, cos_t, sin_t, w["ffn1_norm"], w["ffn1_wg"], w["ffn1_wu"], w["ffn1_wd"], w["mix_norm"], w["w_in"])


def _lane_block_masks(width, block, dtype):
    lane = lax.broadcasted_iota(jnp.int32, (1, width), 1)
    return [((lane >= i * block) & (lane < (i + 1) * block)).astype(dtype) for i in range(width // block)]


def _band_mask():
    qi = lax.broadcasted_iota(jnp.int32, (BLOCK, 2 * BLOCK), 0)
    kj = lax.broadcasted_iota(jnp.int32, (BLOCK, 2 * BLOCK), 1)
    dist = qi + BLOCK - kj
    return (dist >= 0) & (dist <= BAND_STEPS), kj >= BLOCK


def _band_block(load_qbd, load_k, load_v, mask, nheads, finish):
    s = _dot_nt(load_qbd(), load_k())
    yield
    ps, ms, ls = [], [], []
    for p in range(nheads):
        sp = jnp.where(mask, s[p * BLOCK:(p + 1) * BLOCK], NEG_INF)
        m = jnp.max(sp, axis=-1, keepdims=True)
        e = jnp.exp(sp - m)
        ls.append(jnp.sum(e, axis=-1, keepdims=True))
        ms.append(m)
        ps.append(e.astype(BF16))
    pc = jnp.concatenate(ps, axis=0)
    yield
    finish(_dot(pc, load_v()), ms, ls)
    yield


def _issue_skewed(items, nstages=3):
    for tick in range(len(items) + nstages - 1):
        for s in reversed(range(nstages)):
            i = tick - s
            if 0 <= i < len(items):
                next(items[i])


def _kv_window(cur_ref, prev_ref, qb):
    if qb == 0:
        return jnp.concatenate([prev_ref[...], cur_ref[0:BLOCK]], axis=0)
    return cur_ref[(qb - 1) * BLOCK:(qb + 1) * BLOCK]


def _attn_a_kernel(sink_ref, q_ref, kc_ref, kp_ref, vc_ref, vp_ref, o_ref):
    first_chunk = pl.program_id(1) == 0
    band, in_cur = _band_mask()
    half_bf = _lane_block_masks(LANES, HEAD_DIM, BF16)

    def block(qb):
        mask = band & (in_cur | jnp.logical_not(first_chunk)) if qb == 0 else band
        rows = slice(qb * BLOCK, (qb + 1) * BLOCK)

        def load_qbd():
            return jnp.concatenate(
                [q_ref[rows, (p // 2) * LANES:(p // 2 + 1) * LANES] * half_bf[p % 2] for p in range(N_HEADS_A)],
                axis=0)

        def finish(o, ms, ls):
            normed = []
            for p in range(N_HEADS_A):
                den = ls[p] + jnp.exp(sink_ref[HEAD_PERM_A[p]] - ms[p])
                normed.append(o[p * BLOCK:(p + 1) * BLOCK] * (1.0 / den))
            for c, pair in enumerate(_merge_pairs(normed)):
                o_ref[rows, c * LANES:(c + 1) * LANES] = pair.astype(BF16)

        return _band_block(load_qbd, lambda: _kv_window(kc_ref, kp_ref, qb), lambda: _kv_window(vc_ref, vp_ref, qb),
                           mask, N_HEADS_A, finish)

    for qb in range(q_ref.shape[0] // BLOCK):
        for _ in block(qb):
            pass


def _attn_a_call(sink, qa, ka, va, batch, seq):
    t = qa.shape[0]
    cpb = seq // ATTN_CHUNK
    bpc = ATTN_CHUNK // BLOCK
    cur = lambda b, c: (b * cpb + c, 0)
    prev = lambda b, c: (b * cpb * bpc + jnp.maximum(c * bpc - 1, 0), 0)
    return pl.pallas_call(
        _attn_a_kernel,
        grid=(batch, cpb),
        in_specs=[pl.BlockSpec(memory_space=pltpu.SMEM),
                  pl.BlockSpec((ATTN_CHUNK, Q_A_W), cur),
                  pl.BlockSpec((ATTN_CHUNK, KV_A_W), cur), pl.BlockSpec((BLOCK, KV_A_W), prev),
                  pl.BlockSpec((ATTN_CHUNK, KV_A_W), cur), pl.BlockSpec((BLOCK, KV_A_W), prev)],
        out_specs=pl.BlockSpec((ATTN_CHUNK, Q_A_W), cur),
        out_shape=jax.ShapeDtypeStruct((t, Q_A_W), BF16),
        compiler_params=pltpu.CompilerParams(dimension_semantics=("parallel", "parallel"),
                                             vmem_limit_bytes=VMEM_LIMIT),
        name="attn_a_prompt",
    )(sink, qa, ka, ka, va, va)


def _attn_b_kernel(q_ref, kc_ref, kp_ref, vc_ref, vp_ref, o_ref, lse_ref):
    first_chunk = pl.program_id(2) == 0
    band, in_cur = _band_mask()
    head_bf = _lane_block_masks(KV_B_W, HEAD_DIM, BF16)
    lane = lax.broadcasted_iota(jnp.int32, (1, KV_B_W), 1)

    def block(r, qb):
        mask = band & (in_cur | jnp.logical_not(first_chunk)) if qb == 0 else band
        rows = slice(qb * BLOCK, (qb + 1) * BLOCK)

        def load_qbd():
            q = q_ref[r, rows, :]
            return jnp.concatenate([q * head_bf[p] for p in range(N_KV_B)], axis=0)

        def finish(o, ms, ls):
            out = lse = None
            for p in reversed(range(N_KV_B)):
                op = o[p * BLOCK:(p + 1) * BLOCK] * (1.0 / ls[p])
                lp = jnp.broadcast_to(ms[p] + jnp.log(ls[p]), (BLOCK, KV_B_W))
                if out is None:
                    out, lse = op, lp
                else:
                    sel = lane < (p + 1) * HEAD_DIM
                    out, lse = jnp.where(sel, op, out), jnp.where(sel, lp, lse)
            o_ref[r, rows, :] = out
            lse_ref[r, rows, :] = lse

        return _band_block(load_qbd, lambda: _kv_window(kc_ref.at[r], kp_ref.at[r], qb),
                           lambda: _kv_window(vc_ref.at[r], vp_ref.at[r], qb), mask, N_KV_B, finish)

    _issue_skewed([block(r, qb) for r in range(q_ref.shape[0]) for qb in range(q_ref.shape[1] // BLOCK)])


def _attn_b_call(q, k, v, d):
    batch, _, rows, _ = q.shape
    chunk = min(ATTN_CHUNK, rows)
    rps = min(d, ATTN_CHUNK // chunk)
    cpb = rows // chunk
    bpc = chunk // BLOCK
    cur = pl.BlockSpec((None, rps, chunk, KV_B_W), lambda b, r, c: (b, r, c, 0))
    prev = pl.BlockSpec((None, rps, BLOCK, KV_B_W), lambda b, r, c: (b, r, jnp.maximum(c * bpc - 1, 0), 0))
    return pl.pallas_call(
        _attn_b_kernel,
        grid=(batch, d // rps, cpb),
        in_specs=[cur, cur, prev, cur, prev],
        out_specs=[cur, cur],
        out_shape=[jax.ShapeDtypeStruct((batch, d, rows, KV_B_W), F32)] * 2,
        compiler_params=pltpu.CompilerParams(dimension_semantics=("parallel", "parallel", "parallel"),
                                             vmem_limit_bytes=VMEM_LIMIT),
        name=f"attn_b_prompt_d{d}",
    )(q, k, k, v, v)


SAMPLE_T = 8
CACHE_A = 128
CACHE_B = 2048
SAMPLE_ELEMS = 4


def _new_rows_block(new):
    return jnp.concatenate([new, jnp.zeros((BLOCK - SAMPLE_T, new.shape[1]), F32)], axis=0)


def _cached_scores(qbd, kt, knew):
    return _dot(qbd, kt), _dot_nt(qbd, knew)


def _cached_softmax(s_c, s_n, mask_c, mask_n):
    s_c = jnp.where(mask_c, s_c, NEG_INF)
    s_n = jnp.where(mask_n, s_n, NEG_INF)
    m = jnp.maximum(jnp.max(s_c, axis=-1, keepdims=True), jnp.max(s_n, axis=-1, keepdims=True))
    e_c = jnp.exp(s_c - m)
    e_n = jnp.exp(s_n - m)
    l = jnp.sum(e_c, axis=-1, keepdims=True) + jnp.sum(e_n, axis=-1, keepdims=True)
    return e_c, e_n, m, l


def _cached_pv(e_c, e_n, vt, vnew):
    return _dot_nt(e_c, vt) + _dot(e_n, vnew)


def _attn_sample_kernel(sink_ref, q_ref, kan_ref, van_ref, kbn_ref, vbn_ref, cka_ref, cva_ref, ckb_ref, cvb_ref,
                        oa_ref, ob_ref):
    half_f = _lane_block_masks(LANES, HEAD_DIM, F32)
    lane_b = lax.broadcasted_iota(jnp.int32, (1, KV_B_W), 1)

    def dist(nrows, ncols, offset):
        r = lax.broadcasted_iota(jnp.int32, (nrows, ncols), 0)
        i = lax.broadcasted_iota(jnp.int32, (nrows, ncols), 1)
        return r, offset + (r & (SAMPLE_T - 1)) - i

    nrow_a = N_HEADS_A * SAMPLE_T
    _, dac = dist(nrow_a, CACHE_A, CACHE_A)
    _, dan = dist(nrow_a, BLOCK, 0)
    mask_ac = (dac >= 0) & (dac <= BAND_STEPS)
    mask_an = (dan >= 0) & (dan <= BAND_STEPS)
    nrow_b = len(DILATIONS) * N_KV_B * SAMPLE_T
    rows_per_group = N_KV_B * SAMPLE_T

    def mask_b(ncols, offset):
        r, db = dist(nrow_b, ncols, offset)
        dil = jnp.where(r < rows_per_group, DILATIONS[0], jnp.where(r < 2 * rows_per_group, DILATIONS[1], DILATIONS[2]))
        return (db >= 0) & (db <= BAND_STEPS * dil) & ((db & (dil - 1)) == 0)

    mask_bc = mask_b(CACHE_B, CACHE_B)
    mask_bn = mask_b(BLOCK, 0)

    sink_col = jnp.concatenate(
        [jnp.full((SAMPLE_T, 1), sink_ref[HEAD_PERM_A[p]], F32) for p in range(N_HEADS_A)], axis=0)

    def element(j):
        rows = slice(j * SAMPLE_T, (j + 1) * SAMPLE_T)
        qbd_a = jnp.concatenate(
            [q_ref[rows, (p // 2) * LANES:(p // 2 + 1) * LANES] * half_f[p % 2] for p in range(N_HEADS_A)],
            axis=0)
        qrows = []
        for g in range(len(DILATIONS)):
            for kvh in range(N_KV_B):
                c = (Q_A_W // LANES) + 2 * g + kvh // 2
                part = q_ref[rows, c * LANES:(c + 1) * LANES] * half_f[kvh % 2]
                zero = jnp.zeros_like(part)
                qrows.append(jnp.concatenate([part, zero] if kvh // 2 == 0 else [zero, part], axis=1))
        qbd_b = jnp.concatenate(qrows, axis=0)
        scores_a = _cached_scores(qbd_a, cka_ref[j], _new_rows_block(kan_ref[rows, :]))
        scores_b = _cached_scores(qbd_b, ckb_ref[j], _new_rows_block(kbn_ref[rows, :]))
        yield
        ea_c, ea_n, ma, la = _cached_softmax(*scores_a, mask_ac, mask_an)
        eb_c, eb_n, m, l = _cached_softmax(*scores_b, mask_bc, mask_bn)
        yield
        o = _cached_pv(ea_c, ea_n, cva_ref[j], _new_rows_block(van_ref[rows, :]))
        o = o * (1.0 / (la + jnp.exp(sink_col - ma)))
        per_head = [o[p * SAMPLE_T:(p + 1) * SAMPLE_T] for p in range(N_HEADS_A)]
        for c, pair in enumerate(_merge_pairs(per_head)):
            oa_ref[rows, c * LANES:(c + 1) * LANES] = pair
        o = _cached_pv(eb_c, eb_n, cvb_ref[j], _new_rows_block(vbn_ref[rows, :]))
        out = None
        for kvh in reversed(range(N_KV_B)):
            sl = [slice((g * N_KV_B + kvh) * SAMPLE_T, (g * N_KV_B + kvh + 1) * SAMPLE_T)
                  for g in range(len(DILATIONS))]
            mj = jnp.maximum(jnp.maximum(m[sl[0]], m[sl[1]]), m[sl[2]])
            ws = [jnp.exp(m[x] - mj) for x in sl]
            den = ws[0] * l[sl[0]] + ws[1] * l[sl[1]] + ws[2] * l[sl[2]]
            num = ws[0] * o[sl[0]] + ws[1] * o[sl[1]] + ws[2] * o[sl[2]]
            okv = num * (1.0 / den)
            out = okv if out is None else jnp.where(lane_b < (kvh + 1) * HEAD_DIM, okv, out)
        ob_ref[rows, :] = out
        yield

    elements = [element(j) for j in range(SAMPLE_ELEMS)]
    for _ in range(3):
        for e in elements:
            next(e)


def _attn_sample_call(sink, q32, ka32, va32, kb32, vb32, cka, cva, ckb, cvb):
    nb = cka.shape[0]
    bt = SAMPLE_ELEMS
    t = q32.shape[0]
    tok = lambda w: pl.BlockSpec((bt * SAMPLE_T, w), lambda i: (i, 0))
    cache = lambda w, n: pl.BlockSpec((bt, w, n), lambda i: (i, 0, 0))
    return pl.pallas_call(
        _attn_sample_kernel,
        grid=(nb // bt,),
        in_specs=[pl.BlockSpec(memory_space=pltpu.SMEM), tok(Q_A_W + Q_B_W),
                  tok(KV_A_W), tok(KV_A_W), tok(KV_B_W), tok(KV_B_W),
                  cache(KV_A_W, CACHE_A), cache(KV_A_W, CACHE_A), cache(KV_B_W, CACHE_B), cache(KV_B_W, CACHE_B)],
        out_specs=[tok(Q_A_W), tok(KV_B_W)],
        out_shape=[jax.ShapeDtypeStruct((t, Q_A_W), F32), jax.ShapeDtypeStruct((t, KV_B_W), F32)],
        compiler_params=pltpu.CompilerParams(dimension_semantics=("parallel",), vmem_limit_bytes=VMEM_LIMIT),
        name="attn_sample",
    )(sink, q32, ka32, va32, kb32, vb32, cka, cva, ckb, cvb)


def _mix_out(h, oa, ob, gm_ref, wgate_ref, wba_ref, wbb_ref, wout_ref):
    n = _rms(h, gm_ref[...]).astype(BF16)
    gates = _dot(n, wgate_ref[...])
    ga = _sigmoid(gates[:, :D_MODEL])
    gb = _sigmoid(gates[:, D_MODEL:])
    mixed = ga * _dot(oa.astype(BF16), wba_ref[...]) + gb * _dot(ob.astype(BF16), wbb_ref[...])
    return h + _dot(mixed.astype(BF16), wout_ref[...])


def _combine_groups(os, ls):
    m = jnp.maximum(jnp.maximum(ls[0], ls[1]), ls[2])
    es = [jnp.exp(x - m) for x in ls]
    den = es[0] + es[1] + es[2]
    num = es[0] * os[0] + es[1] * os[1] + es[2] * os[2]
    return num * (1.0 / den)


def _merge_split(split_ref, scr, rb):
    d = split_ref.shape[1]
    n = (rb.stop - rb.start) // d
    src = slice(rb.start // d, rb.start // d + n)
    for r in range(d):
        for c in range(scr.shape[0]):
            scr[c, pl.ds(rb.start + r, n, stride=d), :] = split_ref[0, r, src, c * LANES:(c + 1) * LANES]
    return jnp.concatenate([scr[c, rb, :] for c in range(scr.shape[0])], axis=1)


def _softmax_rows(s):
    m = jnp.max(s, axis=-1, keepdims=True)
    e = jnp.exp(s - m)
    return e * (1.0 / jnp.sum(e, axis=-1, keepdims=True))


def _back_prompt_kernel(h_ref, oa_ref, o0_ref, o1_ref, o2_ref, l0_ref, l1_ref, l2_ref, mk_ref, mv_ref,
                        gm_ref, wgate_ref, wba_ref, wbb_ref, wout_ref, gq_ref, wq_ref, wo_ref,
                        g2_ref, wg_ref, wu_ref, wd_ref, gf_ref, y_ref, o1_s, o2_s, l1_s, l2_s, ob_s):
    blocks = _row_blocks(h_ref, BACK_ROW_PARTS)
    h3s = []
    for rb in blocks:
        ob_s[rb, :] = _combine_groups(
            (o0_ref[0, 0, rb, :], _merge_split(o1_ref, o1_s, rb), _merge_split(o2_ref, o2_s, rb)),
            (l0_ref[0, 0, rb, :], _merge_split(l1_ref, l1_s, rb), _merge_split(l2_ref, l2_s, rb))).astype(BF16)
        h2 = _mix_out(h_ref[rb, :], oa_ref[rb, :], ob_s[rb, :], gm_ref, wgate_ref, wba_ref, wbb_ref, wout_ref)
        q = _dot(_rms(h2, gq_ref[...]).astype(BF16), wq_ref[...])
        heads = [None] * MEM_HEADS

        def head(hd):
            cols = slice(hd * MEM_HEAD_DIM, (hd + 1) * MEM_HEAD_DIM)
            s = _dot_nt(q[:, cols].astype(BF16), mk_ref[0, :, cols]) * MEM_SCALE
            yield
            e = jnp.exp(s - jnp.max(s, axis=-1, keepdims=True))
            inv = 1.0 / jnp.sum(e, axis=-1, keepdims=True)
            yield
            heads[hd] = _dot(e.astype(BF16), mv_ref[0, :, cols]) * inv
            yield

        _issue_skewed([head(hd) for hd in range(MEM_HEADS)])
        oc = jnp.concatenate(heads, axis=1).astype(BF16)
        h3s.append(h2 + _dot(oc, wo_ref[...]))
    for rb, y in zip(blocks, _ffn_half(h3s, g2_ref, wg_ref, wu_ref, wd_ref)):
        y_ref[rb, :] = _rms(y, gf_ref[...])


def _back_prompt_call(h, oa, obs, lses, mk, mv, w, seq):
    t = h.shape[0]
    tm = TOKEN_TILE
    tiles_per_batch = seq // tm
    row = lambda i: (i, 0)
    mem = lambda i: (i // tiles_per_batch, 0, 0)
    tok = lambda width: pl.BlockSpec((tm, width), row)
    split = lambda d: pl.BlockSpec((1, d, tm // d, KV_B_W),
                                   lambda i: (i // tiles_per_batch, 0, i % tiles_per_batch, 0))
    groups = [split(d) for d in DILATIONS]
    in_specs = ([tok(D_MODEL), tok(Q_A_W)] + groups + groups
                + [pl.BlockSpec((1, N_MEM, MEM_W), mem), pl.BlockSpec((1, N_MEM, MEM_W), mem)]
                + [_const_spec((1, D_MODEL)), _const_spec((D_MODEL, 2 * D_MODEL), (0, 1)), _const_spec((Q_A_W, D_MODEL)),
                   _const_spec((KV_B_W, D_MODEL)), _const_spec((D_MODEL, D_MODEL)),
                   _const_spec((1, D_MODEL)), _const_spec((D_MODEL, MEM_W)), _const_spec((MEM_W, D_MODEL)),
                   _const_spec((1, D_MODEL)), _const_spec((D_MODEL, D_FF)), _const_spec((D_MODEL, D_FF)),
                   _const_spec((D_FF, D_MODEL)), _const_spec((1, D_MODEL))])
    return pl.pallas_call(
        _back_prompt_kernel,
        grid=(t // tm,),
        in_specs=in_specs,
        out_specs=tok(D_MODEL),
        out_shape=jax.ShapeDtypeStruct((t, D_MODEL), F32),
        scratch_shapes=[pltpu.VMEM((KV_B_W // LANES, tm, LANES), F32)] * 4 + [pltpu.VMEM((tm, KV_B_W), BF16)],
        compiler_params=pltpu.CompilerParams(dimension_semantics=("parallel",), vmem_limit_bytes=VMEM_LIMIT),
        name="back_prompt",
    )(h, oa, *obs, *lses, mk, mv, w["mix_norm"], w["w_in"], w["w_branch_a"], w["w_branch_b"], w["w_out"],
      w["mem_q_norm"], w["w_mem_q"], w["w_mem_o"], w["ffn2_norm"], w["ffn2_wg"], w["ffn2_wu"], w["ffn2_wd"],
      w["final_norm"])


def _back_sample_a_kernel(h_ref, oa_ref, ob_ref, gm_ref, wgate_ref, wba_ref, wbb_ref, wout_ref, gq_ref, wq_ref,
                          h2_ref, q_ref):
    h2 = _mix_out(h_ref[...], oa_ref[...], ob_ref[...], gm_ref, wgate_ref, wba_ref, wbb_ref, wout_ref)
    h2_ref[...] = h2
    q_ref[...] = _dot(_rms(h2, gq_ref[...]).astype(BF16), wq_ref[...])


def _back_sample_a_call(h, oa, ob, w):
    t = h.shape[0]
    tm = TOKEN_TILE
    tok = lambda width: pl.BlockSpec((tm, width), lambda i: (i, 0))
    return pl.pallas_call(
        _back_sample_a_kernel,
        grid=(t // tm,),
        in_specs=[tok(D_MODEL), tok(Q_A_W), tok(KV_B_W),
                  _const_spec((1, D_MODEL)), _const_spec((D_MODEL, 2 * D_MODEL), (0, 1)), _const_spec((Q_A_W, D_MODEL)),
                  _const_spec((KV_B_W, D_MODEL)), _const_spec((D_MODEL, D_MODEL)),
                  _const_spec((1, D_MODEL)), _const_spec((D_MODEL, MEM_W))],
        out_specs=[tok(D_MODEL), tok(MEM_W)],
        out_shape=[jax.ShapeDtypeStruct((t, D_MODEL), F32), jax.ShapeDtypeStruct((t, MEM_W), F32)],
        compiler_params=pltpu.CompilerParams(dimension_semantics=("parallel",), vmem_limit_bytes=VMEM_LIMIT),
        name="back_sample_mix",
    )(h, oa, ob, w["mix_norm"], w["w_in"], w["w_branch_a"], w["w_branch_b"], w["w_out"],
      w["mem_q_norm"], w["w_mem_q"])


CROSS_BATCH_TILE = 8


def _cross_sample_kernel(q_ref, mk_ref, mv_ref, o_ref):
    nrow = MEM_HEADS * SAMPLE_T
    qhead = lax.broadcasted_iota(jnp.int32, (nrow, N_MEM * MEM_HEADS), 0) >> (SAMPLE_T.bit_length() - 1)
    khead = lax.broadcasted_iota(jnp.int32, (nrow, N_MEM * MEM_HEADS), 1) & (MEM_HEADS - 1)
    own_head = qhead == khead
    def element(bi):
        rows = slice(bi * SAMPLE_T, (bi + 1) * SAMPLE_T)
        qs = jnp.concatenate(
            [q_ref[rows, hd * MEM_HEAD_DIM:(hd + 1) * MEM_HEAD_DIM] for hd in range(MEM_HEADS)], axis=0)
        s = _dot_nt(qs, mk_ref[bi]) * MEM_SCALE
        yield
        p = _softmax_rows(jnp.where(own_head, s, NEG_INF))
        yield
        o = _dot(p, mv_ref[bi])
        for hd in range(MEM_HEADS):
            o_ref[rows, hd * MEM_HEAD_DIM:(hd + 1) * MEM_HEAD_DIM] = o[hd * SAMPLE_T:(hd + 1) * SAMPLE_T]
        yield

    _issue_skewed([element(bi) for bi in range(CROSS_BATCH_TILE)])


def _cross_sample_call(q, mk, mv):
    t = q.shape[0]
    bt = CROSS_BATCH_TILE
    tok = pl.BlockSpec((bt * SAMPLE_T, MEM_W), lambda i: (i, 0))
    mem = pl.BlockSpec((bt, N_MEM * MEM_HEADS, MEM_HEAD_DIM), lambda i: (i, 0, 0))
    return pl.pallas_call(
        _cross_sample_kernel,
        grid=(mk.shape[0] // bt,),
        in_specs=[tok, mem, mem],
        out_specs=tok,
        out_shape=jax.ShapeDtypeStruct((t, MEM_W), F32),
        compiler_params=pltpu.CompilerParams(dimension_semantics=("parallel",), vmem_limit_bytes=VMEM_LIMIT),
        name="cross_sample",
    )(q, mk, mv)


def _back_sample_b_kernel(h2_ref, oc_ref, wo_ref, g2_ref, wg_ref, wu_ref, wd_ref, gf_ref, y_ref):
    blocks = _row_blocks(h2_ref, BACK_ROW_PARTS)
    h3s = [h2_ref[rb, :] + _dot(oc_ref[rb, :].astype(BF16), wo_ref[...]) for rb in blocks]
    for rb, y in zip(blocks, _ffn_half(h3s, g2_ref, wg_ref, wu_ref, wd_ref)):
        y_ref[rb, :] = _rms(y, gf_ref[...])


def _back_sample_b_call(h2, oc, w):
    t = h2.shape[0]
    tm = TOKEN_TILE
    tok = lambda width: pl.BlockSpec((tm, width), lambda i: (i, 0))
    return pl.pallas_call(
        _back_sample_b_kernel,
        grid=(t // tm,),
        in_specs=[tok(D_MODEL), tok(MEM_W), _const_spec((MEM_W, D_MODEL)),
                  _const_spec((1, D_MODEL)), _const_spec((D_MODEL, D_FF)), _const_spec((D_MODEL, D_FF)),
                  _const_spec((D_FF, D_MODEL)), _const_spec((1, D_MODEL))],
        out_specs=tok(D_MODEL),
        out_shape=jax.ShapeDtypeStruct((t, D_MODEL), F32),
        compiler_params=pltpu.CompilerParams(dimension_semantics=("parallel",), vmem_limit_bytes=VMEM_LIMIT),
        name="back_sample_ffn",
    )(h2, oc, w["w_mem_o"], w["ffn2_norm"], w["ffn2_wg"], w["ffn2_wu"], w["ffn2_wd"], w["final_norm"])


def _mem_kv_kernel(mem_ref, g_ref, wk_ref, wv_ref, k32_ref, v32_ref, k16_ref, v16_ref):
    u = _rms(mem_ref[...], g_ref[...]).astype(BF16)
    k = _dot(u, wk_ref[...])
    v = _dot(u, wv_ref[...])
    k32_ref[...] = k
    v32_ref[...] = v
    k16_ref[...] = k.astype(BF16)
    v16_ref[...] = v.astype(BF16)


def _mem_kv_call(mem, w):
    rows = mem.shape[0]
    full = lambda width: pl.BlockSpec((rows, width), lambda i: (0, 0))
    return pl.pallas_call(
        _mem_kv_kernel,
        grid=(1,),
        in_specs=[full(D_MODEL), _const_spec((1, D_MODEL)), _const_spec((D_MODEL, MEM_W)),
                  _const_spec((D_MODEL, MEM_W))],
        out_specs=[full(MEM_W)] * 4,
        out_shape=[jax.ShapeDtypeStruct((rows, MEM_W), F32)] * 2 + [jax.ShapeDtypeStruct((rows, MEM_W), BF16)] * 2,
        compiler_params=pltpu.CompilerParams(dimension_semantics=("arbitrary",), vmem_limit_bytes=VMEM_LIMIT),
        name="mem_kv",
    )(mem, w["mem_kv_norm"], w["w_mem_k"], w["w_mem_v"])


def _rope_tables(pos):
    within = np.arange(LANES) % HEAD_DIM
    inv_freq = jnp.power(jnp.float32(ROPE_THETA), -jnp.arange(ROT_HALF, dtype=jnp.float32) / ROT_HALF)
    ang = pos.astype(jnp.float32)[:, None] * inv_freq[None, :]
    cos = jnp.tile(jnp.cos(ang), (1, LANES // ROT_HALF))
    sin = jnp.tile(jnp.sin(ang), (1, LANES // ROT_HALF))
    first = jnp.asarray(within < ROT_HALF)[None, :]
    second = jnp.asarray((within >= ROT_HALF) & (within < 2 * ROT_HALF))[None, :]
    return jnp.where(first | second, cos, 1.0), jnp.where(first, -sin, jnp.where(second, sin, 0.0))


def _prep_weights(p):
    bf = lambda x: x.astype(BF16)
    vec = lambda x: x.reshape(1, -1)
    return {
        "ffn1_norm": vec(p["ffn1_norm"]), "ffn1_wg": bf(p["ffn1_w_gate"]), "ffn1_wu": bf(p["ffn1_w_up"]),
        "ffn1_wd": bf(p["ffn1_w_down"]),
        "mix_norm": vec(p["mix_norm"]), "w_in": bf(p["w_in"]),
        "w_branch_a": bf(p["w_branch_a"].reshape(N_HEADS_A, HEAD_DIM, D_MODEL)[np.array(HEAD_PERM_A)].reshape(
            Q_A_W, D_MODEL)),
        "w_branch_b": bf(p["w_branch_b"]), "w_out": bf(p["w_out"]),
        "mem_q_norm": vec(p["mem_q_norm"]), "mem_kv_norm": vec(p["mem_kv_norm"]),
        "w_mem_q": bf(p["w_mem_q"]), "w_mem_k": bf(p["w_mem_k"]), "w_mem_v": bf(p["w_mem_v"]),
        "w_mem_o": bf(p["w_mem_o"]),
        "ffn2_norm": vec(p["ffn2_norm"]), "ffn2_wg": bf(p["ffn2_w_gate"]), "ffn2_wu": bf(p["ffn2_w_up"]),
        "ffn2_wd": bf(p["ffn2_w_down"]), "final_norm": vec(p["final_norm"]),
    }


def kernel(x_prompt, x_sample, cache_swa_k, cache_swa_v, cache_dil_k, cache_dil_v, cache_mem_k, cache_mem_v, mem_prompt, ffn1_norm, ffn1_w_gate, ffn1_w_up, ffn1_w_down, mix_norm, w_in, attn_sink, w_branch_a, w_branch_b, w_out, mem_q_norm, mem_kv_norm, w_mem_q, w_mem_k, w_mem_v, w_mem_o, ffn2_norm, ffn2_w_gate, ffn2_w_up, ffn2_w_down, final_norm):
    depth = ffn1_norm.shape[0]
    assert depth == 1
    batch, seq, d = x_prompt.shape
    nb, t_new, _ = x_sample.shape
    assert d == D_MODEL and t_new == SAMPLE_T and seq % (TOKEN_TILE * DILATIONS[-1]) == 0 and seq % ATTN_CHUNK == 0
    assert cache_swa_k.shape[2] == CACHE_A and cache_dil_k.shape[2] == CACHE_B
    layer = lambda x: x[0]
    w = _prep_weights(dict(
        ffn1_norm=layer(ffn1_norm), ffn1_w_gate=layer(ffn1_w_gate), ffn1_w_up=layer(ffn1_w_up),
        ffn1_w_down=layer(ffn1_w_down), mix_norm=layer(mix_norm), w_in=layer(w_in),
        w_branch_a=layer(w_branch_a), w_branch_b=layer(w_branch_b), w_out=layer(w_out),
        mem_q_norm=layer(mem_q_norm), mem_kv_norm=layer(mem_kv_norm), w_mem_q=layer(w_mem_q),
        w_mem_k=layer(w_mem_k), w_mem_v=layer(w_mem_v), w_mem_o=layer(w_mem_o), ffn2_norm=layer(ffn2_norm),
        ffn2_w_gate=layer(ffn2_w_gate), ffn2_w_up=layer(ffn2_w_up), ffn2_w_down=layer(ffn2_w_down),
        final_norm=final_norm))
    sink = layer(attn_sink)

    ts = nb * t_new
    pos_s = PAST_LEN + (jnp.arange(TOKEN_TILE, dtype=jnp.int32) % t_new)
    tables_s = _rope_tables(pos_s)
    (h_s, q32, ka32_s, va32_s, kb32_s, vb32_s) = _front_call(x_sample.reshape(ts, d), tables_s, w, sample=True)
    by_pos = lambda c: jnp.transpose(layer(c), (0, 2, 3, 1)).reshape(nb, c.shape[3] * c.shape[4], c.shape[2])
    by_slot_head = lambda c: layer(c).reshape(nb, N_MEM * MEM_HEADS, MEM_HEAD_DIM)

    oa_s, ob_s = _attn_sample_call(
        sink, q32, ka32_s, va32_s, kb32_s, vb32_s,
        by_pos(cache_swa_k), by_pos(cache_swa_v), by_pos(cache_dil_k), by_pos(cache_dil_v))

    tp = batch * seq
    mk32, mv32, mk16, mv16 = _mem_kv_call(mem_prompt.reshape(batch * N_MEM, d), w)
    tables_p = _rope_tables(jnp.arange(seq, dtype=jnp.int32))
    (h_p, qa, ka, va, qb0, kb, vb, q4, k4, v4, q16, k16, v16, ka32, va32, kb32, vb32) = _front_call(
        x_prompt.reshape(tp, d), tables_p, w, sample=False, batch=batch)
    oa = _attn_a_call(sink, qa, ka, va, batch, seq)
    unsplit = lambda x: x.reshape(batch, 1, seq, KV_B_W)
    groups = ((unsplit(qb0), unsplit(kb), unsplit(vb)), (q4, k4, v4), (q16, k16, v16))
    obs, lses = zip(*[_attn_b_call(*qkv, d) for qkv, d in zip(groups, DILATIONS)])
    y_p = _back_prompt_call(h_p, oa, obs, lses, mk16.reshape(batch, N_MEM, MEM_W),
                            mv16.reshape(batch, N_MEM, MEM_W), w, seq)

    h2_s, qc_s = _back_sample_a_call(h_s, oa_s, ob_s, w)
    oc_s = _cross_sample_call(qc_s, by_slot_head(cache_mem_k), by_slot_head(cache_mem_v))
    y_s = _back_sample_b_call(h2_s, oc_s, w)

    keep_a, keep_b = min(CACHE_A, seq), min(CACHE_B, seq)
    tail = lambda x, keep, heads: x.reshape(batch, seq, heads * HEAD_DIM)[:, seq - keep:].reshape(
        1, batch, keep, heads, HEAD_DIM)
    new = lambda x, heads: x.reshape(1, nb, t_new, heads, HEAD_DIM)
    memo = lambda x: x.reshape(1, batch, N_MEM, MEM_HEADS, MEM_HEAD_DIM)
    return (y_p.reshape(batch, seq, d), y_s.reshape(nb, t_new, d),
            tail(ka32, keep_a, N_KV_A), tail(va32, keep_a, N_KV_A),
            tail(kb32, keep_b, N_KV_B), tail(vb32, keep_b, N_KV_B),
            memo(mk32), memo(mv32),
            new(ka32_s, N_KV_A), new(va32_s, N_KV_A), new(kb32_s, N_KV_B), new(vb32_s, N_KV_B))
```

```python
import functools

import jax
import jax.numpy as jnp
import numpy as np
from jax import lax
from jax.experimental import pallas as pl
from jax.experimental.pallas import tpu as pltpu

F32 = jnp.float32
BF16 = jnp.bfloat16

D_MODEL = 1024
D_FF = 2816
HEAD_DIM = 64
ROT_HALF = 8
ROPE_THETA = 500000.0
ATTN_SCALE = HEAD_DIM ** -0.5
RMS_EPS = 1e-6
PAST_LEN = 16384

N_HEADS_A = 8
N_KV_A = 2
N_KV_B = 4
DILATIONS = (1, 4, 16)
BAND_STEPS = 128
Q_A_W = N_HEADS_A * HEAD_DIM
KV_A_W = N_KV_A * HEAD_DIM
Q_B_W = len(DILATIONS) * N_KV_B * HEAD_DIM
KV_B_W = N_KV_B * HEAD_DIM
QKV_W = Q_A_W + 2 * KV_A_W + Q_B_W + 2 * KV_B_W
MEM_HEADS = 4
MEM_HEAD_DIM = 128
MEM_W = MEM_HEADS * MEM_HEAD_DIM
MEM_SCALE = MEM_HEAD_DIM ** -0.5
N_MEM = 256

HEAD_PERM_A = (0, 4, 1, 5, 2, 6, 3, 7)

MAIN_W = Q_A_W + 2 * KV_A_W + 3 * KV_B_W
MAIN_KA_COL, MAIN_VA_COL = Q_A_W // KV_A_W, Q_A_W // KV_A_W + 1
MAIN_QB_COL = (Q_A_W + 2 * KV_A_W) // KV_B_W
SPLIT_W = 3 * KV_B_W
TAILS_W = 2 * KV_A_W + 2 * KV_B_W
TAIL_ROWS = 2048

LANES = 128
BLOCK = 128
TOKEN_TILE = 512
FRONT_ROW_PARTS = 2
BACK_ROW_PARTS = 1
ATTN_CHUNK = 1024
FF_CHUNKS = ((0, 1536), (1536, 2816))
VMEM_LIMIT = 60 * 1024 * 1024
NEG_INF = float("-inf")


def _rms(x, g):
    ms = jnp.mean(x * x, axis=-1, keepdims=True)
    return x * lax.rsqrt(ms + RMS_EPS) * g


def _sigmoid(x):
    return 0.5 * jnp.tanh(0.5 * x) + 0.5


def _dot(a, b):
    return jnp.dot(a, b, preferred_element_type=F32)


def _dot_nt(a, b):
    return lax.dot_general(a, b, (((1,), (1,)), ((), ())), preferred_element_type=F32)


def _ffn_half(xs, g_ref, wg_ref, wu_ref, wd_ref):
    us = [_rms(x, g_ref[...]).astype(BF16) for x in xs]
    accs = [None] * len(xs)
    for lo, hi in FF_CHUNKS:
        gates = [_dot(u, wg_ref[:, lo:hi]) for u in us]
        ups = [_dot(u, wu_ref[:, lo:hi]) for u in us]
        acts = [(gate * _sigmoid(gate) * up).astype(BF16) for gate, up in zip(gates, ups)]
        for i, act in enumerate(acts):
            part = _dot(act, wd_ref[lo:hi, :])
            accs[i] = part if accs[i] is None else accs[i] + part
    return [x + 0.5 * acc for x, acc in zip(xs, accs)]


def _row_blocks(ref, parts):
    n = ref.shape[0] // parts
    return [slice(i * n, (i + 1) * n) for i in range(parts)]


def _swap_halves(x):
    return pltpu.roll(x, HEAD_DIM, 1)


def _low_half():
    return lax.broadcasted_iota(jnp.int32, (1, LANES), 1) < HEAD_DIM


def _pair_kv_groups(chunks):
    low = _low_half()
    c0, c1, c2, c3 = chunks
    return [jnp.where(low, c0, _swap_halves(c2)), jnp.where(low, _swap_halves(c0), c2),
            jnp.where(low, c1, _swap_halves(c3)), jnp.where(low, _swap_halves(c1), c3)]


def _merge_pairs(per_head):
    low = _low_half()
    return [jnp.where(low, per_head[2 * c], per_head[2 * c + 1]) for c in range(N_HEADS_A // 2)]


def _front_kernel(x_ref, cos_ref, sin_ref, g1_ref, wg_ref, wu_ref, wd_ref, gm_ref, wqkv_ref,
                  h_ref, *rest, sample):
    blocks = _row_blocks(x_ref, FRONT_ROW_PARTS)
    hs = _ffn_half([x_ref[rb, :] for rb in blocks], g1_ref, wg_ref, wu_ref, wd_ref)
    for rb, h in zip(blocks, hs):
        h_ref[rb, :] = h
    ns = [_rms(h, gm_ref[...]).astype(BF16) for h in hs]
    zs = [_dot(n, wqkv_ref[...]) for n in ns]
    for rb, z in zip(blocks, zs):
        _front_emit(rb, z, cos_ref, sin_ref, rest, sample)


def _front_emit(rb, z, cos_ref, sin_ref, rest, sample):
    cos, sin = cos_ref[rb, :], sin_ref[rb, :]
    lane_in_head = lax.broadcasted_iota(jnp.int32, (1, LANES), 1) & (HEAD_DIM - 1)
    first = lane_in_head < ROT_HALF

    def chunk(c):
        return z[:, c * LANES:(c + 1) * LANES]

    def rope(c):
        zc = chunk(c)
        partner = jnp.where(first, pltpu.roll(zc, LANES - ROT_HALF, 1), pltpu.roll(zc, ROT_HALF, 1))
        return zc * cos + partner * sin

    nqa, nka = Q_A_W // LANES, KV_A_W // LANES
    nqb, nkb = Q_B_W // LANES, KV_B_W // LANES
    c0 = 0
    qa = _pair_kv_groups([rope(c0 + c) * ATTN_SCALE for c in range(nqa)])
    c0 += nqa
    ka = [rope(c0 + c) for c in range(nka)]
    c0 += nka
    va = [chunk(c0 + c) for c in range(nka)]
    c0 += nka
    qb = [rope(c0 + c) * ATTN_SCALE for c in range(nqb)]
    c0 += nqb
    kb = [rope(c0 + c) for c in range(nkb)]
    c0 += nkb
    vb = [chunk(c0 + c) for c in range(nkb)]

    def store(ref, parts, dtype):
        for c, p in enumerate(parts):
            ref[rb, c * LANES:(c + 1) * LANES] = p.astype(dtype)

    def fill(scr, parts):
        for c, p in enumerate(parts):
            scr[c, rb, :] = p

    def store_split(ref, slot, scr, d):
        n = (rb.stop - rb.start) // d
        dst = slice(rb.start // d, rb.start // d + n)
        for r in range(d):
            for c in range(scr.shape[0]):
                lanes = slice(slot * KV_B_W + c * LANES, slot * KV_B_W + (c + 1) * LANES)
                ref[0, r, dst, lanes] = scr[c, pl.ds(rb.start + r, n, stride=d), :].astype(BF16)

    if sample:
        q32_ref, ka32_ref, va32_ref, kb32_ref, vb32_ref = rest
        store(q32_ref, qa + qb, F32)
        store(ka32_ref, ka, F32)
        store(va32_ref, va, F32)
        store(kb32_ref, kb, F32)
        store(vb32_ref, vb, F32)
    else:
        main_ref, s4_ref, s16_ref, tails_ref, q4_s, q16_s, kb_s, vb_s = rest
        gw = KV_B_W // LANES
        store(main_ref, qa + ka + va + qb[0:gw] + kb + vb, BF16)
        fill(q4_s, qb[gw:2 * gw])
        fill(q16_s, qb[2 * gw:3 * gw])
        fill(kb_s, kb)
        fill(vb_s, vb)
        for slot, scr4, scr16 in ((0, q4_s, q16_s), (1, kb_s, kb_s), (2, vb_s, vb_s)):
            store_split(s4_ref, slot, scr4, DILATIONS[1])
            store_split(s16_ref, slot, scr16, DILATIONS[2])
        store(tails_ref, ka + va + kb + vb, F32)


def _const_spec(shape, index=None):
    index = (0,) * len(shape) if index is None else index
    return pl.BlockSpec(shape, lambda *_: index, pipeline_mode=pl.Buffered(1))


def _front_call(x, tables, w, *, sample, batch=1):
    t = x.shape[0]
    tm = TOKEN_TILE
    nsteps = t // tm
    tpb = nsteps // batch
    seq = t // batch
    cos_t, sin_t = tables
    tab_blocks = cos_t.shape[0] // tm
    row = lambda i: (i, 0)
    tab = lambda i: (i % tab_blocks, 0)

    def tok(width, dtype):
        return jax.ShapeDtypeStruct((t, width), dtype), pl.BlockSpec((tm, width), row)

    def split(d):
        return (jax.ShapeDtypeStruct((batch, d, seq // d, SPLIT_W), BF16),
                pl.BlockSpec((1, d, tm // d, SPLIT_W), lambda i: (i // tpb, 0, i % tpb, 0)))

    outs = [tok(D_MODEL, F32)]
    scratch = []
    if sample:
        outs += [tok(Q_A_W + Q_B_W, F32), tok(KV_A_W, F32), tok(KV_A_W, F32), tok(KV_B_W, F32), tok(KV_B_W, F32)]
    else:
        tail_tiles = TAIL_ROWS // tm
        tails = (jax.ShapeDtypeStruct((batch * TAIL_ROWS, TAILS_W), F32),
                 pl.BlockSpec((tm, TAILS_W),
                              lambda i: ((i // tpb) * tail_tiles + jnp.maximum(i % tpb - (tpb - tail_tiles), 0), 0)))
        outs += [tok(MAIN_W, BF16), split(DILATIONS[1]), split(DILATIONS[2]), tails]
        scratch = [pltpu.VMEM((KV_B_W // LANES, tm, LANES), F32)] * 4
    out_shape, out_specs = zip(*outs)
    in_specs = [pl.BlockSpec((tm, D_MODEL), row),
                pl.BlockSpec((tm, LANES), tab), pl.BlockSpec((tm, LANES), tab),
                _const_spec((1, D_MODEL)), _const_spec((D_MODEL, D_FF)), _const_spec((D_MODEL, D_FF)),
                _const_spec((D_FF, D_MODEL)), _const_spec((1, D_MODEL)), _const_spec((D_MODEL, QKV_W), (0, 0))]
    return pl.pallas_call(
        functools.partial(_front_kernel, sample=sample),
        grid=(nsteps,),
        in_specs=in_specs,
        out_specs=list(out_specs),
        out_shape=list(out_shape),
        scratch_shapes=scratch,
        compiler_params=pltpu.CompilerParams(dimension_semantics=("parallel" if sample else "arbitrary",),
                                             vmem_limit_bytes=VMEM_LIMIT),
        name="front_sample" if sample else "front_prompt",
    )(x, cos_t, sin_t, w["ffn1_norm"], w["ffn1_wg"], w["ffn1_wu"], w["ffn1_wd"], w["mix_norm"], w["w_in"])


def _lane_block_masks(width, block, dtype):
    lane = lax.broadcasted_iota(jnp.int32, (1, width), 1)
    return [((lane >= i * block) & (lane < (i + 1) * block)).astype(dtype) for i in range(width // block)]


def _band_mask():
    qi = lax.broadcasted_iota(jnp.int32, (BLOCK, 2 * BLOCK), 0)
    kj = lax.broadcasted_iota(jnp.int32, (BLOCK, 2 * BLOCK), 1)
    dist = qi + BLOCK - kj
    return (dist >= 0) & (dist <= BAND_STEPS), kj >= BLOCK


def _band_block(load_qbd, load_k, load_v, mask, nheads, finish):
    s = _dot_nt(load_qbd(), load_k())
    yield
    ps, ms, ls = [], [], []
    for p in range(nheads):
        sp = jnp.where(mask, s[p * BLOCK:(p + 1) * BLOCK], NEG_INF)
        m = jnp.max(sp, axis=-1, keepdims=True)
        e = jnp.exp(sp - m)
        ls.append(jnp.sum(e, axis=-1, keepdims=True))
        ms.append(m)
        ps.append(e.astype(BF16))
    pc = jnp.concatenate(ps, axis=0)
    yield
    finish(_dot(pc, load_v()), ms, ls)
    yield


def _issue_skewed(items, nstages=3):
    for tick in range(len(items) + nstages - 1):
        for s in reversed(range(nstages)):
            i = tick - s
            if 0 <= i < len(items):
                next(items[i])


def _kv_window(cur_ref, prev_ref, qb):
    if qb == 0:
        return jnp.concatenate([prev_ref[...], cur_ref[0:BLOCK]], axis=0)
    return cur_ref[(qb - 1) * BLOCK:(qb + 1) * BLOCK]


def _attn_a_kernel(sink_ref, q_ref, kc_ref, kp_ref, vc_ref, vp_ref, o_ref):
    first_chunk = pl.program_id(1) == 0
    band, in_cur = _band_mask()
    half_bf = _lane_block_masks(LANES, HEAD_DIM, BF16)

    def block(qb):
        mask = band & (in_cur | jnp.logical_not(first_chunk)) if qb == 0 else band
        rows = slice(qb * BLOCK, (qb + 1) * BLOCK)

        def load_qbd():
            return jnp.concatenate(
                [q_ref[rows, (p // 2) * LANES:(p // 2 + 1) * LANES] * half_bf[p % 2] for p in range(N_HEADS_A)],
                axis=0)

        def finish(o, ms, ls):
            normed = []
            for p in range(N_HEADS_A):
                den = ls[p] + jnp.exp(sink_ref[HEAD_PERM_A[p]] - ms[p])
                normed.append(o[p * BLOCK:(p + 1) * BLOCK] * (1.0 / den))
            for c, pair in enumerate(_merge_pairs(normed)):
                o_ref[rows, c * LANES:(c + 1) * LANES] = pair.astype(BF16)

        return _band_block(load_qbd, lambda: _kv_window(kc_ref, kp_ref, qb), lambda: _kv_window(vc_ref, vp_ref, qb),
                           mask, N_HEADS_A, finish)

    for qb in range(q_ref.shape[0] // BLOCK):
        for _ in block(qb):
            pass


def _attn_a_call(sink, main, batch, seq):
    t = main.shape[0]
    cpb = seq // ATTN_CHUNK
    bpc = ATTN_CHUNK // BLOCK
    cur = lambda col: (lambda b, c: (b * cpb + c, col))
    prev = lambda col: (lambda b, c: (b * cpb * bpc + jnp.maximum(c * bpc - 1, 0), col))
    kv = lambda col: [pl.BlockSpec((ATTN_CHUNK, KV_A_W), cur(col)), pl.BlockSpec((BLOCK, KV_A_W), prev(col))]
    return pl.pallas_call(
        _attn_a_kernel,
        grid=(batch, cpb),
        in_specs=[pl.BlockSpec(memory_space=pltpu.SMEM), pl.BlockSpec((ATTN_CHUNK, Q_A_W), cur(0))]
        + kv(MAIN_KA_COL) + kv(MAIN_VA_COL),
        out_specs=pl.BlockSpec((ATTN_CHUNK, Q_A_W), cur(0)),
        out_shape=jax.ShapeDtypeStruct((t, Q_A_W), BF16),
        compiler_params=pltpu.CompilerParams(dimension_semantics=("parallel", "parallel"),
                                             vmem_limit_bytes=VMEM_LIMIT),
        name="attn_a_prompt",
    )(sink, main, main, main, main, main)


def _attn_b_kernel(q_ref, kc_ref, kp_ref, vc_ref, vp_ref, ol_ref):
    first_chunk = pl.program_id(2) == 0
    band, in_cur = _band_mask()
    head_bf = _lane_block_masks(KV_B_W, HEAD_DIM, BF16)
    lane = lax.broadcasted_iota(jnp.int32, (1, KV_B_W), 1)

    def block(r, qb):
        mask = band & (in_cur | jnp.logical_not(first_chunk)) if qb == 0 else band
        rows = slice(qb * BLOCK, (qb + 1) * BLOCK)

        def load_qbd():
            q = q_ref[r, rows, :]
            return jnp.concatenate([q * head_bf[p] for p in range(N_KV_B)], axis=0)

        def finish(o, ms, ls):
            out = lse = None
            for p in reversed(range(N_KV_B)):
                op = o[p * BLOCK:(p + 1) * BLOCK] * (1.0 / ls[p])
                lp = jnp.broadcast_to(ms[p] + jnp.log(ls[p]), (BLOCK, KV_B_W))
                if out is None:
                    out, lse = op, lp
                else:
                    sel = lane < (p + 1) * HEAD_DIM
                    out, lse = jnp.where(sel, op, out), jnp.where(sel, lp, lse)
            ol_ref[r, rows, 0:KV_B_W] = out
            ol_ref[r, rows, KV_B_W:2 * KV_B_W] = lse

        return _band_block(load_qbd, lambda: _kv_window(kc_ref.at[r], kp_ref.at[r], qb),
                           lambda: _kv_window(vc_ref.at[r], vp_ref.at[r], qb), mask, N_KV_B, finish)

    _issue_skewed([block(r, qb) for r in range(q_ref.shape[0]) for qb in range(q_ref.shape[1] // BLOCK)])


def _attn_b_call(qkv, qcol, d):
    batch, _, rows, _ = qkv.shape
    chunk = min(ATTN_CHUNK, rows)
    rps = min(d, ATTN_CHUNK // chunk)
    cpb = rows // chunk
    bpc = chunk // BLOCK
    cur = lambda col: pl.BlockSpec((None, rps, chunk, KV_B_W), lambda b, r, c: (b, r, c, col))
    prev = lambda col: pl.BlockSpec((None, rps, BLOCK, KV_B_W),
                                    lambda b, r, c: (b, r, jnp.maximum(c * bpc - 1, 0), col))
    return pl.pallas_call(
        _attn_b_kernel,
        grid=(batch, d // rps, cpb),
        in_specs=[cur(qcol), cur(qcol + 1), prev(qcol + 1), cur(qcol + 2), prev(qcol + 2)],
        out_specs=pl.BlockSpec((None, rps, chunk, 2 * KV_B_W), lambda b, r, c: (b, r, c, 0)),
        out_shape=jax.ShapeDtypeStruct((batch, d, rows, 2 * KV_B_W), F32),
        compiler_params=pltpu.CompilerParams(dimension_semantics=("parallel", "parallel", "parallel"),
                                             vmem_limit_bytes=VMEM_LIMIT),
        name=f"attn_b_prompt_d{d}",
    )(qkv, qkv, qkv, qkv, qkv)


SAMPLE_T = 8
CACHE_A = 128
CACHE_B = 2048
SAMPLE_ELEMS = 4


def _new_rows_block(new):
    return jnp.concatenate([new, jnp.zeros((BLOCK - SAMPLE_T, new.shape[1]), F32)], axis=0)


def _cached_scores(qbd, kt, knew):
    return _dot(qbd, kt), _dot_nt(qbd, knew)


def _cached_softmax(s_c, s_n, mask_c, mask_n):
    s_c = jnp.where(mask_c, s_c, NEG_INF)
    s_n = jnp.where(mask_n, s_n, NEG_INF)
    m = jnp.maximum(jnp.max(s_c, axis=-1, keepdims=True), jnp.max(s_n, axis=-1, keepdims=True))
    e_c = jnp.exp(s_c - m)
    e_n = jnp.exp(s_n - m)
    l = jnp.sum(e_c, axis=-1, keepdims=True) + jnp.sum(e_n, axis=-1, keepdims=True)
    return e_c, e_n, m, l


def _cached_pv(e_c, e_n, vt, vnew):
    return _dot_nt(e_c, vt) + _dot(e_n, vnew)


def _attn_sample_kernel(sink_ref, q_ref, kan_ref, van_ref, kbn_ref, vbn_ref, cka_ref, cva_ref, ckb_ref, cvb_ref,
                        oa_ref, ob_ref):
    half_f = _lane_block_masks(LANES, HEAD_DIM, F32)
    lane_b = lax.broadcasted_iota(jnp.int32, (1, KV_B_W), 1)

    def dist(nrows, ncols, offset):
        r = lax.broadcasted_iota(jnp.int32, (nrows, ncols), 0)
        i = lax.broadcasted_iota(jnp.int32, (nrows, ncols), 1)
        return r, offset + (r & (SAMPLE_T - 1)) - i

    nrow_a = N_HEADS_A * SAMPLE_T
    _, dac = dist(nrow_a, CACHE_A, CACHE_A)
    _, dan = dist(nrow_a, BLOCK, 0)
    mask_ac = (dac >= 0) & (dac <= BAND_STEPS)
    mask_an = (dan >= 0) & (dan <= BAND_STEPS)
    nrow_b = len(DILATIONS) * N_KV_B * SAMPLE_T
    rows_per_group = N_KV_B * SAMPLE_T

    def mask_b(ncols, offset):
        r, db = dist(nrow_b, ncols, offset)
        dil = jnp.where(r < rows_per_group, DILATIONS[0], jnp.where(r < 2 * rows_per_group, DILATIONS[1], DILATIONS[2]))
        return (db >= 0) & (db <= BAND_STEPS * dil) & ((db & (dil - 1)) == 0)

    mask_bc = mask_b(CACHE_B, CACHE_B)
    mask_bn = mask_b(BLOCK, 0)

    sink_col = jnp.concatenate(
        [jnp.full((SAMPLE_T, 1), sink_ref[HEAD_PERM_A[p]], F32) for p in range(N_HEADS_A)], axis=0)

    def element(j):
        rows = slice(j * SAMPLE_T, (j + 1) * SAMPLE_T)
        qbd_a = jnp.concatenate(
            [q_ref[rows, (p // 2) * LANES:(p // 2 + 1) * LANES] * half_f[p % 2] for p in range(N_HEADS_A)],
            axis=0)
        qrows = []
        for g in range(len(DILATIONS)):
            for kvh in range(N_KV_B):
                c = (Q_A_W // LANES) + 2 * g + kvh // 2
                part = q_ref[rows, c * LANES:(c + 1) * LANES] * half_f[kvh % 2]
                zero = jnp.zeros_like(part)
                qrows.append(jnp.concatenate([part, zero] if kvh // 2 == 0 else [zero, part], axis=1))
        qbd_b = jnp.concatenate(qrows, axis=0)
        scores_a = _cached_scores(qbd_a, cka_ref[j], _new_rows_block(kan_ref[rows, :]))
        scores_b = _cached_scores(qbd_b, ckb_ref[j], _new_rows_block(kbn_ref[rows, :]))
        yield
        ea_c, ea_n, ma, la = _cached_softmax(*scores_a, mask_ac, mask_an)
        eb_c, eb_n, m, l = _cached_softmax(*scores_b, mask_bc, mask_bn)
        yield
        o = _cached_pv(ea_c, ea_n, cva_ref[j], _new_rows_block(van_ref[rows, :]))
        o = o * (1.0 / (la + jnp.exp(sink_col - ma)))
        per_head = [o[p * SAMPLE_T:(p + 1) * SAMPLE_T] for p in range(N_HEADS_A)]
        for c, pair in enumerate(_merge_pairs(per_head)):
            oa_ref[rows, c * LANES:(c + 1) * LANES] = pair
        o = _cached_pv(eb_c, eb_n, cvb_ref[j], _new_rows_block(vbn_ref[rows, :]))
        out = None
        for kvh in reversed(range(N_KV_B)):
            sl = [slice((g * N_KV_B + kvh) * SAMPLE_T, (g * N_KV_B + kvh + 1) * SAMPLE_T)
                  for g in range(len(DILATIONS))]
            mj = jnp.maximum(jnp.maximum(m[sl[0]], m[sl[1]]), m[sl[2]])
            ws = [jnp.exp(m[x] - mj) for x in sl]
            den = ws[0] * l[sl[0]] + ws[1] * l[sl[1]] + ws[2] * l[sl[2]]
            num = ws[0] * o[sl[0]] + ws[1] * o[sl[1]] + ws[2] * o[sl[2]]
            okv = num * (1.0 / den)
            out = okv if out is None else jnp.where(lane_b < (kvh + 1) * HEAD_DIM, okv, out)
        ob_ref[rows, :] = out
        yield

    elements = [element(j) for j in range(SAMPLE_ELEMS)]
    for _ in range(3):
        for e in elements:
            next(e)


def _attn_sample_call(sink, q32, ka32, va32, kb32, vb32, cka, cva, ckb, cvb):
    nb = cka.shape[0]
    bt = SAMPLE_ELEMS
    t = q32.shape[0]
    tok = lambda w: pl.BlockSpec((bt * SAMPLE_T, w), lambda i: (i, 0))
    cache = lambda w, n: pl.BlockSpec((bt, w, n), lambda i: (i, 0, 0))
    return pl.pallas_call(
        _attn_sample_kernel,
        grid=(nb // bt,),
        in_specs=[pl.BlockSpec(memory_space=pltpu.SMEM), tok(Q_A_W + Q_B_W),
                  tok(KV_A_W), tok(KV_A_W), tok(KV_B_W), tok(KV_B_W),
                  cache(KV_A_W, CACHE_A), cache(KV_A_W, CACHE_A), cache(KV_B_W, CACHE_B), cache(KV_B_W, CACHE_B)],
        out_specs=[tok(Q_A_W), tok(KV_B_W)],
        out_shape=[jax.ShapeDtypeStruct((t, Q_A_W), F32), jax.ShapeDtypeStruct((t, KV_B_W), F32)],
        compiler_params=pltpu.CompilerParams(dimension_semantics=("parallel",), vmem_limit_bytes=VMEM_LIMIT),
        name="attn_sample",
    )(sink, q32, ka32, va32, kb32, vb32, cka, cva, ckb, cvb)


def _mix_out(h, oa, ob, gm_ref, wgate_ref, wba_ref, wbb_ref, wout_ref):
    n = _rms(h, gm_ref[...]).astype(BF16)
    gates = _dot(n, wgate_ref[...])
    ga = _sigmoid(gates[:, :D_MODEL])
    gb = _sigmoid(gates[:, D_MODEL:])
    mixed = ga * _dot(oa.astype(BF16), wba_ref[...]) + gb * _dot(ob.astype(BF16), wbb_ref[...])
    return h + _dot(mixed.astype(BF16), wout_ref[...])


def _combine_groups(os, ls):
    m = jnp.maximum(jnp.maximum(ls[0], ls[1]), ls[2])
    es = [jnp.exp(x - m) for x in ls]
    den = es[0] + es[1] + es[2]
    num = es[0] * os[0] + es[1] * os[1] + es[2] * os[2]
    return num * (1.0 / den)


def _merge_split(split_ref, field, scr, rb):
    d = split_ref.shape[1]
    if d == 1:
        return split_ref[0, 0, rb, field * KV_B_W:(field + 1) * KV_B_W]
    n = (rb.stop - rb.start) // d
    src = slice(rb.start // d, rb.start // d + n)
    for r in range(d):
        for c in range(scr.shape[0]):
            lanes = slice(field * KV_B_W + c * LANES, field * KV_B_W + (c + 1) * LANES)
            scr[c, pl.ds(rb.start + r, n, stride=d), :] = split_ref[0, r, src, lanes]
    return jnp.concatenate([scr[c, rb, :] for c in range(scr.shape[0])], axis=1)


def _softmax_rows(s):
    m = jnp.max(s, axis=-1, keepdims=True)
    e = jnp.exp(s - m)
    return e * (1.0 / jnp.sum(e, axis=-1, keepdims=True))


def _back_prompt_kernel(h_ref, oa_ref, ol0_ref, ol1_ref, ol2_ref, mk_ref, mv_ref,
                        gm_ref, wgate_ref, wba_ref, wbb_ref, wout_ref, gq_ref, wq_ref, wo_ref,
                        g2_ref, wg_ref, wu_ref, wd_ref, gf_ref, y_ref, o1_s, o2_s, l1_s, l2_s, ob_s):
    blocks = _row_blocks(h_ref, BACK_ROW_PARTS)
    h3s = []
    for rb in blocks:
        ob_s[rb, :] = _combine_groups(
            (_merge_split(ol0_ref, 0, None, rb), _merge_split(ol1_ref, 0, o1_s, rb),
             _merge_split(ol2_ref, 0, o2_s, rb)),
            (_merge_split(ol0_ref, 1, None, rb), _merge_split(ol1_ref, 1, l1_s, rb),
             _merge_split(ol2_ref, 1, l2_s, rb))).astype(BF16)
        h2 = _mix_out(h_ref[rb, :], oa_ref[rb, :], ob_s[rb, :], gm_ref, wgate_ref, wba_ref, wbb_ref, wout_ref)
        q = _dot(_rms(h2, gq_ref[...]).astype(BF16), wq_ref[...])
        heads = [None] * MEM_HEADS

        def head(hd):
            cols = slice(hd * MEM_HEAD_DIM, (hd + 1) * MEM_HEAD_DIM)
            s = _dot_nt(q[:, cols].astype(BF16), mk_ref[0, :, cols]) * MEM_SCALE
            yield
            e = jnp.exp(s - jnp.max(s, axis=-1, keepdims=True))
            inv = 1.0 / jnp.sum(e, axis=-1, keepdims=True)
            yield
            heads[hd] = _dot(e.astype(BF16), mv_ref[0, :, cols]) * inv
            yield

        _issue_skewed([head(hd) for hd in range(MEM_HEADS)])
        oc = jnp.concatenate(heads, axis=1).astype(BF16)
        h3s.append(h2 + _dot(oc, wo_ref[...]))
    for rb, y in zip(blocks, _ffn_half(h3s, g2_ref, wg_ref, wu_ref, wd_ref)):
        y_ref[rb, :] = _rms(y, gf_ref[...])


def _back_prompt_call(h, oa, ols, mk, mv, w, seq):
    t = h.shape[0]
    tm = TOKEN_TILE
    tiles_per_batch = seq // tm
    row = lambda i: (i, 0)
    mem = lambda i: (i // tiles_per_batch, 0, 0)
    tok = lambda width: pl.BlockSpec((tm, width), row)
    split = lambda d: pl.BlockSpec((1, d, tm // d, 2 * KV_B_W),
                                   lambda i: (i // tiles_per_batch, 0, i % tiles_per_batch, 0))
    in_specs = ([tok(D_MODEL), tok(Q_A_W)] + [split(d) for d in DILATIONS]
                + [pl.BlockSpec((1, N_MEM, MEM_W), mem), pl.BlockSpec((1, N_MEM, MEM_W), mem)]
                + [_const_spec((1, D_MODEL)), _const_spec((D_MODEL, 2 * D_MODEL), (0, 1)), _const_spec((Q_A_W, D_MODEL)),
                   _const_spec((KV_B_W, D_MODEL)), _const_spec((D_MODEL, D_MODEL)),
                   _const_spec((1, D_MODEL)), _const_spec((D_MODEL, MEM_W)), _const_spec((MEM_W, D_MODEL)),
                   _const_spec((1, D_MODEL)), _const_spec((D_MODEL, D_FF)), _const_spec((D_MODEL, D_FF)),
                   _const_spec((D_FF, D_MODEL)), _const_spec((1, D_MODEL))])
    return pl.pallas_call(
        _back_prompt_kernel,
        grid=(t // tm,),
        in_specs=in_specs,
        out_specs=tok(D_MODEL),
        out_shape=jax.ShapeDtypeStruct((t, D_MODEL), F32),
        scratch_shapes=[pltpu.VMEM((KV_B_W // LANES, tm, LANES), F32)] * 4 + [pltpu.VMEM((tm, KV_B_W), BF16)],
        compiler_params=pltpu.CompilerParams(dimension_semantics=("parallel",), vmem_limit_bytes=VMEM_LIMIT),
        name="back_prompt",
    )(h, oa, *ols, mk, mv, w["mix_norm"], w["w_in"], w["w_branch_a"], w["w_branch_b"], w["w_out"],
      w["mem_q_norm"], w["w_mem_q"], w["w_mem_o"], w["ffn2_norm"], w["ffn2_wg"], w["ffn2_wu"], w["ffn2_wd"],
      w["final_norm"])


def _back_sample_a_kernel(h_ref, oa_ref, ob_ref, gm_ref, wgate_ref, wba_ref, wbb_ref, wout_ref, gq_ref, wq_ref,
                          h2_ref, q_ref):
    h2 = _mix_out(h_ref[...], oa_ref[...], ob_ref[...], gm_ref, wgate_ref, wba_ref, wbb_ref, wout_ref)
    h2_ref[...] = h2
    q_ref[...] = _dot(_rms(h2, gq_ref[...]).astype(BF16), wq_ref[...])


def _back_sample_a_call(h, oa, ob, w):
    t = h.shape[0]
    tm = TOKEN_TILE
    tok = lambda width: pl.BlockSpec((tm, width), lambda i: (i, 0))
    return pl.pallas_call(
        _back_sample_a_kernel,
        grid=(t // tm,),
        in_specs=[tok(D_MODEL), tok(Q_A_W), tok(KV_B_W),
                  _const_spec((1, D_MODEL)), _const_spec((D_MODEL, 2 * D_MODEL), (0, 1)), _const_spec((Q_A_W, D_MODEL)),
                  _const_spec((KV_B_W, D_MODEL)), _const_spec((D_MODEL, D_MODEL)),
                  _const_spec((1, D_MODEL)), _const_spec((D_MODEL, MEM_W))],
        out_specs=[tok(D_MODEL), tok(MEM_W)],
        out_shape=[jax.ShapeDtypeStruct((t, D_MODEL), F32), jax.ShapeDtypeStruct((t, MEM_W), F32)],
        compiler_params=pltpu.CompilerParams(dimension_semantics=("parallel",), vmem_limit_bytes=VMEM_LIMIT),
        name="back_sample_mix",
    )(h, oa, ob, w["mix_norm"], w["w_in"], w["w_branch_a"], w["w_branch_b"], w["w_out"],
      w["mem_q_norm"], w["w_mem_q"])


CROSS_BATCH_TILE = 8


def _cross_sample_kernel(q_ref, mk_ref, mv_ref, o_ref):
    nrow = MEM_HEADS * SAMPLE_T
    qhead = lax.broadcasted_iota(jnp.int32, (nrow, N_MEM * MEM_HEADS), 0) >> (SAMPLE_T.bit_length() - 1)
    khead = lax.broadcasted_iota(jnp.int32, (nrow, N_MEM * MEM_HEADS), 1) & (MEM_HEADS - 1)
    own_head = qhead == khead
    def element(bi):
        rows = slice(bi * SAMPLE_T, (bi + 1) * SAMPLE_T)
        qs = jnp.concatenate(
            [q_ref[rows, hd * MEM_HEAD_DIM:(hd + 1) * MEM_HEAD_DIM] for hd in range(MEM_HEADS)], axis=0)
        s = _dot_nt(qs, mk_ref[bi]) * MEM_SCALE
        yield
        p = _softmax_rows(jnp.where(own_head, s, NEG_INF))
        yield
        o = _dot(p, mv_ref[bi])
        for hd in range(MEM_HEADS):
            o_ref[rows, hd * MEM_HEAD_DIM:(hd + 1) * MEM_HEAD_DIM] = o[hd * SAMPLE_T:(hd + 1) * SAMPLE_T]
        yield

    _issue_skewed([element(bi) for bi in range(CROSS_BATCH_TILE)])


def _cross_sample_call(q, mk, mv):
    t = q.shape[0]
    bt = CROSS_BATCH_TILE
    tok = pl.BlockSpec((bt * SAMPLE_T, MEM_W), lambda i: (i, 0))
    mem = pl.BlockSpec((bt, N_MEM * MEM_HEADS, MEM_HEAD_DIM), lambda i: (i, 0, 0))
    return pl.pallas_call(
        _cross_sample_kernel,
        grid=(mk.shape[0] // bt,),
        in_specs=[tok, mem, mem],
        out_specs=tok,
        out_shape=jax.ShapeDtypeStruct((t, MEM_W), F32),
        compiler_params=pltpu.CompilerParams(dimension_semantics=("parallel",), vmem_limit_bytes=VMEM_LIMIT),
        name="cross_sample",
    )(q, mk, mv)


def _back_sample_b_kernel(h2_ref, oc_ref, wo_ref, g2_ref, wg_ref, wu_ref, wd_ref, gf_ref, y_ref):
    blocks = _row_blocks(h2_ref, BACK_ROW_PARTS)
    h3s = [h2_ref[rb, :] + _dot(oc_ref[rb, :].astype(BF16), wo_ref[...]) for rb in blocks]
    for rb, y in zip(blocks, _ffn_half(h3s, g2_ref, wg_ref, wu_ref, wd_ref)):
        y_ref[rb, :] = _rms(y, gf_ref[...])


def _back_sample_b_call(h2, oc, w):
    t = h2.shape[0]
    tm = TOKEN_TILE
    tok = lambda width: pl.BlockSpec((tm, width), lambda i: (i, 0))
    return pl.pallas_call(
        _back_sample_b_kernel,
        grid=(t // tm,),
        in_specs=[tok(D_MODEL), tok(MEM_W), _const_spec((MEM_W, D_MODEL)),
                  _const_spec((1, D_MODEL)), _const_spec((D_MODEL, D_FF)), _const_spec((D_MODEL, D_FF)),
                  _const_spec((D_FF, D_MODEL)), _const_spec((1, D_MODEL))],
        out_specs=tok(D_MODEL),
        out_shape=jax.ShapeDtypeStruct((t, D_MODEL), F32),
        compiler_params=pltpu.CompilerParams(dimension_semantics=("parallel",), vmem_limit_bytes=VMEM_LIMIT),
        name="back_sample_ffn",
    )(h2, oc, w["w_mem_o"], w["ffn2_norm"], w["ffn2_wg"], w["ffn2_wu"], w["ffn2_wd"], w["final_norm"])


def _mem_kv_kernel(mem_ref, g_ref, wk_ref, wv_ref, k32_ref, v32_ref, k16_ref, v16_ref):
    u = _rms(mem_ref[...], g_ref[...]).astype(BF16)
    k = _dot(u, wk_ref[...])
    v = _dot(u, wv_ref[...])
    k32_ref[...] = k
    v32_ref[...] = v
    k16_ref[...] = k.astype(BF16)
    v16_ref[...] = v.astype(BF16)


def _mem_kv_call(mem, w):
    rows = mem.shape[0]
    full = lambda width: pl.BlockSpec((rows, width), lambda i: (0, 0))
    return pl.pallas_call(
        _mem_kv_kernel,
        grid=(1,),
        in_specs=[full(D_MODEL), _const_spec((1, D_MODEL)), _const_spec((D_MODEL, MEM_W)),
                  _const_spec((D_MODEL, MEM_W))],
        out_specs=[full(MEM_W)] * 4,
        out_shape=[jax.ShapeDtypeStruct((rows, MEM_W), F32)] * 2 + [jax.ShapeDtypeStruct((rows, MEM_W), BF16)] * 2,
        compiler_params=pltpu.CompilerParams(dimension_semantics=("arbitrary",), vmem_limit_bytes=VMEM_LIMIT),
        name="mem_kv",
    )(mem, w["mem_kv_norm"], w["w_mem_k"], w["w_mem_v"])


def _rope_tables(pos):
    within = np.arange(LANES) % HEAD_DIM
    inv_freq = jnp.power(jnp.float32(ROPE_THETA), -jnp.arange(ROT_HALF, dtype=jnp.float32) / ROT_HALF)
    ang = pos.astype(jnp.float32)[:, None] * inv_freq[None, :]
    cos = jnp.tile(jnp.cos(ang), (1, LANES // ROT_HALF))
    sin = jnp.tile(jnp.sin(ang), (1, LANES // ROT_HALF))
    first = jnp.asarray(within < ROT_HALF)[None, :]
    second = jnp.asarray((within >= ROT_HALF) & (within < 2 * ROT_HALF))[None, :]
    return jnp.where(first | second, cos, 1.0), jnp.where(first, -sin, jnp.where(second, sin, 0.0))


def _prep_weights(p):
    bf = lambda x: x.astype(BF16)
    vec = lambda x: x.reshape(1, -1)
    return {
        "ffn1_norm": vec(p["ffn1_norm"]), "ffn1_wg": bf(p["ffn1_w_gate"]), "ffn1_wu": bf(p["ffn1_w_up"]),
        "ffn1_wd": bf(p["ffn1_w_down"]),
        "mix_norm": vec(p["mix_norm"]), "w_in": bf(p["w_in"]),
        "w_branch_a": bf(p["w_branch_a"].reshape(N_HEADS_A, HEAD_DIM, D_MODEL)[np.array(HEAD_PERM_A)].reshape(
            Q_A_W, D_MODEL)),
        "w_branch_b": bf(p["w_branch_b"]), "w_out": bf(p["w_out"]),
        "mem_q_norm": vec(p["mem_q_norm"]), "mem_kv_norm": vec(p["mem_kv_norm"]),
        "w_mem_q": bf(p["w_mem_q"]), "w_mem_k": bf(p["w_mem_k"]), "w_mem_v": bf(p["w_mem_v"]),
        "w_mem_o": bf(p["w_mem_o"]),
        "ffn2_norm": vec(p["ffn2_norm"]), "ffn2_wg": bf(p["ffn2_w_gate"]), "ffn2_wu": bf(p["ffn2_w_up"]),
        "ffn2_wd": bf(p["ffn2_w_down"]), "final_norm": vec(p["final_norm"]),
    }


def kernel(x_prompt, x_sample, cache_swa_k, cache_swa_v, cache_dil_k, cache_dil_v, cache_mem_k, cache_mem_v, mem_prompt, ffn1_norm, ffn1_w_gate, ffn1_w_up, ffn1_w_down, mix_norm, w_in, attn_sink, w_branch_a, w_branch_b, w_out, mem_q_norm, mem_kv_norm, w_mem_q, w_mem_k, w_mem_v, w_mem_o, ffn2_norm, ffn2_w_gate, ffn2_w_up, ffn2_w_down, final_norm):
    depth = ffn1_norm.shape[0]
    assert depth == 1
    batch, seq, d = x_prompt.shape
    nb, t_new, _ = x_sample.shape
    assert d == D_MODEL and t_new == SAMPLE_T and seq % (TOKEN_TILE * DILATIONS[-1]) == 0 and seq % ATTN_CHUNK == 0
    assert cache_swa_k.shape[2] == CACHE_A and cache_dil_k.shape[2] == CACHE_B
    layer = lambda x: x[0]
    w = _prep_weights(dict(
        ffn1_norm=layer(ffn1_norm), ffn1_w_gate=layer(ffn1_w_gate), ffn1_w_up=layer(ffn1_w_up),
        ffn1_w_down=layer(ffn1_w_down), mix_norm=layer(mix_norm), w_in=layer(w_in),
        w_branch_a=layer(w_branch_a), w_branch_b=layer(w_branch_b), w_out=layer(w_out),
        mem_q_norm=layer(mem_q_norm), mem_kv_norm=layer(mem_kv_norm), w_mem_q=layer(w_mem_q),
        w_mem_k=layer(w_mem_k), w_mem_v=layer(w_mem_v), w_mem_o=layer(w_mem_o), ffn2_norm=layer(ffn2_norm),
        ffn2_w_gate=layer(ffn2_w_gate), ffn2_w_up=layer(ffn2_w_up), ffn2_w_down=layer(ffn2_w_down),
        final_norm=final_norm))
    sink = layer(attn_sink)

    ts = nb * t_new
    pos_s = PAST_LEN + (jnp.arange(TOKEN_TILE, dtype=jnp.int32) % t_new)
    tables_s = _rope_tables(pos_s)
    (h_s, q32, ka32_s, va32_s, kb32_s, vb32_s) = _front_call(x_sample.reshape(ts, d), tables_s, w, sample=True)
    by_pos = lambda c: jnp.transpose(layer(c), (0, 2, 3, 1)).reshape(nb, c.shape[3] * c.shape[4], c.shape[2])
    by_slot_head = lambda c: layer(c).reshape(nb, N_MEM * MEM_HEADS, MEM_HEAD_DIM)

    oa_s, ob_s = _attn_sample_call(
        sink, q32, ka32_s, va32_s, kb32_s, vb32_s,
        by_pos(cache_swa_k), by_pos(cache_swa_v), by_pos(cache_dil_k), by_pos(cache_dil_v))

    tp = batch * seq
    mk32, mv32, mk16, mv16 = _mem_kv_call(mem_prompt.reshape(batch * N_MEM, d), w)
    tables_p = _rope_tables(jnp.arange(seq, dtype=jnp.int32))
    h_p, main, split4, split16, tails = _front_call(x_prompt.reshape(tp, d), tables_p, w, sample=False, batch=batch)
    oa = _attn_a_call(sink, main, batch, seq)
    groups = ((main.reshape(batch, 1, seq, MAIN_W), MAIN_QB_COL), (split4, 0), (split16, 0))
    ols = [_attn_b_call(qkv, qcol, dil) for (qkv, qcol), dil in zip(groups, DILATIONS)]
    y_p = _back_prompt_call(h_p, oa, ols, mk16.reshape(batch, N_MEM, MEM_W),
                            mv16.reshape(batch, N_MEM, MEM_W), w, seq)

    h2_s, qc_s = _back_sample_a_call(h_s, oa_s, ob_s, w)
    oc_s = _cross_sample_call(qc_s, by_slot_head(cache_mem_k), by_slot_head(cache_mem_v))
    y_s = _back_sample_b_call(h2_s, oc_s, w)

    keep_a, keep_b = min(CACHE_A, seq), min(CACHE_B, seq)
    assert max(keep_a, keep_b) <= TAIL_ROWS <= seq and TAIL_ROWS % TOKEN_TILE == 0
    tails = tails.reshape(batch, TAIL_ROWS, TAILS_W)

    def tail(col, keep, heads):
        return tails[:, TAIL_ROWS - keep:, col:col + heads * HEAD_DIM].reshape(1, batch, keep, heads, HEAD_DIM)

    new = lambda x, heads: x.reshape(1, nb, t_new, heads, HEAD_DIM)
    memo = lambda x: x.reshape(1, batch, N_MEM, MEM_HEADS, MEM_HEAD_DIM)
    return (y_p.reshape(batch, seq, d), y_s.reshape(nb, t_new, d),
            tail(0, keep_a, N_KV_A), tail(KV_A_W, keep_a, N_KV_A),
            tail(2 * KV_A_W, keep_b, N_KV_B), tail(2 * KV_A_W + KV_B_W, keep_b, N_KV_B),
            memo(mk32), memo(mv32),
            new(ka32_s, N_KV_A), new(va32_s, N_KV_A), new(kb32_s, N_KV_B), new(vb32_s, N_KV_B))
```

```python
import functools

import jax
import jax.numpy as jnp
import numpy as np
from jax import lax
from jax.experimental import pallas as pl
from jax.experimental.pallas import tpu as pltpu

F32 = jnp.float32
BF16 = jnp.bfloat16

D_MODEL = 1024
D_FF = 2816
HEAD_DIM = 64
ROT_HALF = 8
ROPE_THETA = 500000.0
ATTN_SCALE = HEAD_DIM ** -0.5
RMS_EPS = 1e-6
PAST_LEN = 16384

N_HEADS_A = 8
N_KV_A = 2
N_KV_B = 4
DILATIONS = (1, 4, 16)
BAND_STEPS = 128
Q_A_W = N_HEADS_A * HEAD_DIM
KV_A_W = N_KV_A * HEAD_DIM
Q_B_W = len(DILATIONS) * N_KV_B * HEAD_DIM
KV_B_W = N_KV_B * HEAD_DIM
QKV_W = Q_A_W + 2 * KV_A_W + Q_B_W + 2 * KV_B_W
MEM_HEADS = 4
MEM_HEAD_DIM = 128
MEM_W = MEM_HEADS * MEM_HEAD_DIM
MEM_SCALE = MEM_HEAD_DIM ** -0.5
N_MEM = 256

HEAD_PERM_A = (0, 4, 1, 5, 2, 6, 3, 7)

MAIN_W = Q_A_W + 2 * KV_A_W + 3 * KV_B_W
MAIN_KA_COL, MAIN_VA_COL = Q_A_W // KV_A_W, Q_A_W // KV_A_W + 1
MAIN_QB_COL = (Q_A_W + 2 * KV_A_W) // KV_B_W
SPLIT_W = 3 * KV_B_W
TAILS_W = 2 * KV_A_W + 2 * KV_B_W
TAIL_ROWS = 2048

LANES = 128
BLOCK = 128
TOKEN_TILE = 512
FRONT_ROW_PARTS = 2
BACK_ROW_PARTS = 1
ATTN_CHUNK = 1024
FF_CHUNKS = ((0, 1536), (1536, 2816))
VMEM_LIMIT = 60 * 1024 * 1024
NEG_INF = float("-inf")


def _rms(x, g):
    ms = jnp.mean(x * x, axis=-1, keepdims=True)
    return x * lax.rsqrt(ms + RMS_EPS) * g


def _sigmoid(x):
    return 0.5 * jnp.tanh(0.5 * x) + 0.5


def _dot(a, b):
    return jnp.dot(a, b, preferred_element_type=F32)


def _dot_nt(a, b):
    return lax.dot_general(a, b, (((1,), (1,)), ((), ())), preferred_element_type=F32)


def _ffn_half(xs, g_ref, wg_ref, wu_ref, wd_ref):
    us = [_rms(x, g_ref[...]).astype(BF16) for x in xs]
    accs = [None] * len(xs)
    for lo, hi in FF_CHUNKS:
        gates = [_dot(u, wg_ref[:, lo:hi]) for u in us]
        ups = [_dot(u, wu_ref[:, lo:hi]) for u in us]
        acts = [(gate * _sigmoid(gate) * up).astype(BF16) for gate, up in zip(gates, ups)]
        for i, act in enumerate(acts):
            part = _dot(act, wd_ref[lo:hi, :])
            accs[i] = part if accs[i] is None else accs[i] + part
    return [x + 0.5 * acc for x, acc in zip(xs, accs)]


def _row_blocks(ref, parts):
    n = ref.shape[0] // parts
    return [slice(i * n, (i + 1) * n) for i in range(parts)]


def _swap_halves(x):
    return pltpu.roll(x, HEAD_DIM, 1)


def _low_half():
    return lax.broadcasted_iota(jnp.int32, (1, LANES), 1) < HEAD_DIM


def _pair_kv_groups(chunks):
    low = _low_half()
    c0, c1, c2, c3 = chunks
    return [jnp.where(low, c0, _swap_halves(c2)), jnp.where(low, _swap_halves(c0), c2),
            jnp.where(low, c1, _swap_halves(c3)), jnp.where(low, _swap_halves(c1), c3)]


def _merge_pairs(per_head):
    low = _low_half()
    return [jnp.where(low, per_head[2 * c], per_head[2 * c + 1]) for c in range(N_HEADS_A // 2)]


def _front_kernel(x_ref, cos_ref, sin_ref, g1_ref, wg_ref, wu_ref, wd_ref, gm_ref, wqkv_ref,
                  h_ref, *rest, sample, tiles_per_seq=None, tail_tiles=None):
    blocks = _row_blocks(x_ref, FRONT_ROW_PARTS)
    hs = _ffn_half([x_ref[rb, :] for rb in blocks], g1_ref, wg_ref, wu_ref, wd_ref)
    for rb, h in zip(blocks, hs):
        h_ref[rb, :] = h
    ns = [_rms(h, gm_ref[...]).astype(BF16) for h in hs]
    zs = [_dot(n, wqkv_ref[...]) for n in ns]
    for rb, z in zip(blocks, zs):
        _front_emit(rb, z, cos_ref, sin_ref, rest, sample)
    if not sample:
        tails_ref, kb_s, vb_s, kva_s = rest[3], rest[6], rest[7], rest[8]

        @pl.when(lax.rem(pl.program_id(0), tiles_per_seq) >= tiles_per_seq - tail_tiles)
        def _():
            for c, col in enumerate([kva_s[0], kva_s[1], kb_s[0], kb_s[1], vb_s[0], vb_s[1]]):
                tails_ref[0, c * LANES:(c + 1) * LANES, :] = col.T


def _front_emit(rb, z, cos_ref, sin_ref, rest, sample):
    cos, sin = cos_ref[rb, :], sin_ref[rb, :]
    lane_in_head = lax.broadcasted_iota(jnp.int32, (1, LANES), 1) & (HEAD_DIM - 1)
    first = lane_in_head < ROT_HALF

    def chunk(c):
        return z[:, c * LANES:(c + 1) * LANES]

    def rope(c):
        zc = chunk(c)
        partner = jnp.where(first, pltpu.roll(zc, LANES - ROT_HALF, 1), pltpu.roll(zc, ROT_HALF, 1))
        return zc * cos + partner * sin

    nqa, nka = Q_A_W // LANES, KV_A_W // LANES
    nqb, nkb = Q_B_W // LANES, KV_B_W // LANES
    c0 = 0
    qa = _pair_kv_groups([rope(c0 + c) * ATTN_SCALE for c in range(nqa)])
    c0 += nqa
    ka = [rope(c0 + c) for c in range(nka)]
    c0 += nka
    va = [chunk(c0 + c) for c in range(nka)]
    c0 += nka
    qb = [rope(c0 + c) * ATTN_SCALE for c in range(nqb)]
    c0 += nqb
    kb = [rope(c0 + c) for c in range(nkb)]
    c0 += nkb
    vb = [chunk(c0 + c) for c in range(nkb)]

    def store(ref, parts, dtype):
        for c, p in enumerate(parts):
            ref[rb, c * LANES:(c + 1) * LANES] = p.astype(dtype)

    def fill(scr, parts):
        for c, p in enumerate(parts):
            scr[c, rb, :] = p

    def store_split(ref, slot, scr, d):
        n = (rb.stop - rb.start) // d
        dst = slice(rb.start // d, rb.start // d + n)
        for r in range(d):
            for c in range(scr.shape[0]):
                lanes = slice(slot * KV_B_W + c * LANES, slot * KV_B_W + (c + 1) * LANES)
                ref[0, r, dst, lanes] = scr[c, pl.ds(rb.start + r, n, stride=d), :].astype(BF16)

    if sample:
        q32_ref, ka32_ref, va32_ref, kb32_ref, vb32_ref = rest
        store(q32_ref, qa + qb, F32)
        store(ka32_ref, ka, F32)
        store(va32_ref, va, F32)
        store(kb32_ref, kb, F32)
        store(vb32_ref, vb, F32)
    else:
        main_ref, s4_ref, s16_ref, _, q4_s, q16_s, kb_s, vb_s, kva_s = rest
        gw = KV_B_W // LANES
        store(main_ref, qa + ka + va + qb[0:gw] + kb + vb, BF16)
        fill(q4_s, qb[gw:2 * gw])
        fill(q16_s, qb[2 * gw:3 * gw])
        fill(kb_s, kb)
        fill(vb_s, vb)
        fill(kva_s, ka + va)
        for slot, scr4, scr16 in ((0, q4_s, q16_s), (1, kb_s, kb_s), (2, vb_s, vb_s)):
            store_split(s4_ref, slot, scr4, DILATIONS[1])
            store_split(s16_ref, slot, scr16, DILATIONS[2])


def _const_spec(shape, index=None):
    index = (0,) * len(shape) if index is None else index
    return pl.BlockSpec(shape, lambda *_: index, pipeline_mode=pl.Buffered(1))


def _front_call(x, tables, w, *, sample, batch=1):
    t = x.shape[0]
    tm = TOKEN_TILE
    nsteps = t // tm
    tpb = nsteps // batch
    seq = t // batch
    cos_t, sin_t = tables
    tab_blocks = cos_t.shape[0] // tm
    row = lambda i: (i, 0)
    tab = lambda i: (i % tab_blocks, 0)

    def tok(width, dtype):
        return jax.ShapeDtypeStruct((t, width), dtype), pl.BlockSpec((tm, width), row)

    def split(d):
        return (jax.ShapeDtypeStruct((batch, d, seq // d, SPLIT_W), BF16),
                pl.BlockSpec((1, d, tm // d, SPLIT_W), lambda i: (i // tpb, 0, i % tpb, 0)))

    outs = [tok(D_MODEL, F32)]
    scratch = []
    static = {}
    if sample:
        outs += [tok(Q_A_W + Q_B_W, F32), tok(KV_A_W, F32), tok(KV_A_W, F32), tok(KV_B_W, F32), tok(KV_B_W, F32)]
    else:
        tail_tiles = TAIL_ROWS // tm
        tails = (jax.ShapeDtypeStruct((batch, TAILS_W, TAIL_ROWS), F32),
                 pl.BlockSpec((1, TAILS_W, tm),
                              lambda i: (i // tpb, 0, jnp.maximum(i % tpb - (tpb - tail_tiles), 0))))
        outs += [tok(MAIN_W, BF16), split(DILATIONS[1]), split(DILATIONS[2]), tails]
        scratch = [pltpu.VMEM((KV_B_W // LANES, tm, LANES), F32)] * 4 + [pltpu.VMEM((2, tm, LANES), F32)]
        static = dict(tiles_per_seq=tpb, tail_tiles=tail_tiles)
    out_shape, out_specs = zip(*outs)
    in_specs = [pl.BlockSpec((tm, D_MODEL), row),
                pl.BlockSpec((tm, LANES), tab), pl.BlockSpec((tm, LANES), tab),
                _const_spec((1, D_MODEL)), _const_spec((D_MODEL, D_FF)), _const_spec((D_MODEL, D_FF)),
                _const_spec((D_FF, D_MODEL)), _const_spec((1, D_MODEL)), _const_spec((D_MODEL, QKV_W), (0, 0))]
    return pl.pallas_call(
        functools.partial(_front_kernel, sample=sample, **static),
        grid=(nsteps,),
        in_specs=in_specs,
        out_specs=list(out_specs),
        out_shape=list(out_shape),
        scratch_shapes=scratch,
        compiler_params=pltpu.CompilerParams(dimension_semantics=("parallel" if sample else "arbitrary",),
                                             vmem_limit_bytes=VMEM_LIMIT),
        name="front_sample" if sample else "front_prompt",
    )(x, cos_t, sin_t, w["ffn1_norm"], w["ffn1_wg"], w["ffn1_wu"], w["ffn1_wd"], w["mix_norm"], w["w_in"])


def _lane_block_masks(width, block, dtype):
    lane = lax.broadcasted_iota(jnp.int32, (1, width), 1)
    return [((lane >= i * block) & (lane < (i + 1) * block)).astype(dtype) for i in range(width // block)]


def _band_mask():
    qi = lax.broadcasted_iota(jnp.int32, (BLOCK, 2 * BLOCK), 0)
    kj = lax.broadcasted_iota(jnp.int32, (BLOCK, 2 * BLOCK), 1)
    dist = qi + BLOCK - kj
    return (dist >= 0) & (dist <= BAND_STEPS), kj >= BLOCK


def _band_block(load_qbd, load_k, load_v, mask, nheads, finish):
    s = _dot_nt(load_qbd(), load_k())
    yield
    ps, ms, ls = [], [], []
    for p in range(nheads):
        sp = jnp.where(mask, s[p * BLOCK:(p + 1) * BLOCK], NEG_INF)
        m = jnp.max(sp, axis=-1, keepdims=True)
        e = jnp.exp(sp - m)
        ls.append(jnp.sum(e, axis=-1, keepdims=True))
        ms.append(m)
        ps.append(e.astype(BF16))
    pc = jnp.concatenate(ps, axis=0)
    yield
    finish(_dot(pc, load_v()), ms, ls)
    yield


def _issue_skewed(items, nstages=3):
    for tick in range(len(items) + nstages - 1):
        for s in reversed(range(nstages)):
            i = tick - s
            if 0 <= i < len(items):
                next(items[i])


def _kv_window(cur_ref, prev_ref, qb):
    if qb == 0:
        return jnp.concatenate([prev_ref[...], cur_ref[0:BLOCK]], axis=0)
    return cur_ref[(qb - 1) * BLOCK:(qb + 1) * BLOCK]


def _attn_a_kernel(sink_ref, q_ref, kc_ref, kp_ref, vc_ref, vp_ref, o_ref):
    first_chunk = pl.program_id(1) == 0
    band, in_cur = _band_mask()
    half_bf = _lane_block_masks(LANES, HEAD_DIM, BF16)

    def block(qb):
        mask = band & (in_cur | jnp.logical_not(first_chunk)) if qb == 0 else band
        rows = slice(qb * BLOCK, (qb + 1) * BLOCK)

        def load_qbd():
            return jnp.concatenate(
                [q_ref[rows, (p // 2) * LANES:(p // 2 + 1) * LANES] * half_bf[p % 2] for p in range(N_HEADS_A)],
                axis=0)

        def finish(o, ms, ls):
            normed = []
            for p in range(N_HEADS_A):
                den = ls[p] + jnp.exp(sink_ref[HEAD_PERM_A[p]] - ms[p])
                normed.append(o[p * BLOCK:(p + 1) * BLOCK] * (1.0 / den))
            for c, pair in enumerate(_merge_pairs(normed)):
                o_ref[rows, c * LANES:(c + 1) * LANES] = pair.astype(BF16)

        return _band_block(load_qbd, lambda: _kv_window(kc_ref, kp_ref, qb), lambda: _kv_window(vc_ref, vp_ref, qb),
                           mask, N_HEADS_A, finish)

    for qb in range(q_ref.shape[0] // BLOCK):
        for _ in block(qb):
            pass


def _attn_a_call(sink, main, batch, seq):
    t = main.shape[0]
    cpb = seq // ATTN_CHUNK
    bpc = ATTN_CHUNK // BLOCK
    cur = lambda col: (lambda b, c: (b * cpb + c, col))
    prev = lambda col: (lambda b, c: (b * cpb * bpc + jnp.maximum(c * bpc - 1, 0), col))
    kv = lambda col: [pl.BlockSpec((ATTN_CHUNK, KV_A_W), cur(col)), pl.BlockSpec((BLOCK, KV_A_W), prev(col))]
    return pl.pallas_call(
        _attn_a_kernel,
        grid=(batch, cpb),
        in_specs=[pl.BlockSpec(memory_space=pltpu.SMEM), pl.BlockSpec((ATTN_CHUNK, Q_A_W), cur(0))]
        + kv(MAIN_KA_COL) + kv(MAIN_VA_COL),
        out_specs=pl.BlockSpec((ATTN_CHUNK, Q_A_W), cur(0)),
        out_shape=jax.ShapeDtypeStruct((t, Q_A_W), BF16),
        compiler_params=pltpu.CompilerParams(dimension_semantics=("parallel", "parallel"),
                                             vmem_limit_bytes=VMEM_LIMIT),
        name="attn_a_prompt",
    )(sink, main, main, main, main, main)


def _attn_b_kernel(q_ref, kc_ref, kp_ref, vc_ref, vp_ref, ol_ref):
    first_chunk = pl.program_id(2) == 0
    band, in_cur = _band_mask()
    head_bf = _lane_block_masks(KV_B_W, HEAD_DIM, BF16)
    lane = lax.broadcasted_iota(jnp.int32, (1, KV_B_W), 1)

    def block(r, qb):
        mask = band & (in_cur | jnp.logical_not(first_chunk)) if qb == 0 else band
        rows = slice(qb * BLOCK, (qb + 1) * BLOCK)

        def load_qbd():
            q = q_ref[r, rows, :]
            return jnp.concatenate([q * head_bf[p] for p in range(N_KV_B)], axis=0)

        def finish(o, ms, ls):
            out = lse = None
            for p in reversed(range(N_KV_B)):
                op = o[p * BLOCK:(p + 1) * BLOCK] * (1.0 / ls[p])
                lp = jnp.broadcast_to(ms[p] + jnp.log(ls[p]), (BLOCK, KV_B_W))
                if out is None:
                    out, lse = op, lp
                else:
                    sel = lane < (p + 1) * HEAD_DIM
                    out, lse = jnp.where(sel, op, out), jnp.where(sel, lp, lse)
            ol_ref[r, rows, 0:KV_B_W] = out
            ol_ref[r, rows, KV_B_W:2 * KV_B_W] = lse

        return _band_block(load_qbd, lambda: _kv_window(kc_ref.at[r], kp_ref.at[r], qb),
                           lambda: _kv_window(vc_ref.at[r], vp_ref.at[r], qb), mask, N_KV_B, finish)

    _issue_skewed([block(r, qb) for r in range(q_ref.shape[0]) for qb in range(q_ref.shape[1] // BLOCK)])


def _attn_b_call(qkv, qcol, d):
    batch, _, rows, _ = qkv.shape
    chunk = min(ATTN_CHUNK, rows)
    rps = min(d, ATTN_CHUNK // chunk)
    cpb = rows // chunk
    bpc = chunk // BLOCK
    cur = lambda col: pl.BlockSpec((None, rps, chunk, KV_B_W), lambda b, r, c: (b, r, c, col))
    prev = lambda col: pl.BlockSpec((None, rps, BLOCK, KV_B_W),
                                    lambda b, r, c: (b, r, jnp.maximum(c * bpc - 1, 0), col))
    return pl.pallas_call(
        _attn_b_kernel,
        grid=(batch, d // rps, cpb),
        in_specs=[cur(qcol), cur(qcol + 1), prev(qcol + 1), cur(qcol + 2), prev(qcol + 2)],
        out_specs=pl.BlockSpec((None, rps, chunk, 2 * KV_B_W), lambda b, r, c: (b, r, c, 0)),
        out_shape=jax.ShapeDtypeStruct((batch, d, rows, 2 * KV_B_W), F32),
        compiler_params=pltpu.CompilerParams(dimension_semantics=("parallel", "parallel", "parallel"),
                                             vmem_limit_bytes=VMEM_LIMIT),
        name=f"attn_b_prompt_d{d}",
    )(qkv, qkv, qkv, qkv, qkv)


SAMPLE_T = 8
CACHE_A = 128
CACHE_B = 2048
SAMPLE_ELEMS = 4


def _new_rows_block(new):
    return jnp.concatenate([new, jnp.zeros((BLOCK - SAMPLE_T, new.shape[1]), F32)], axis=0)


def _cached_scores(qbd, kt, knew):
    return _dot(qbd, kt), _dot_nt(qbd, knew)


def _cached_softmax(s_c, s_n, mask_c, mask_n):
    s_c = jnp.where(mask_c, s_c, NEG_INF)
    s_n = jnp.where(mask_n, s_n, NEG_INF)
    m = jnp.maximum(jnp.max(s_c, axis=-1, keepdims=True), jnp.max(s_n, axis=-1, keepdims=True))
    e_c = jnp.exp(s_c - m)
    e_n = jnp.exp(s_n - m)
    l = jnp.sum(e_c, axis=-1, keepdims=True) + jnp.sum(e_n, axis=-1, keepdims=True)
    return e_c, e_n, m, l


def _cached_pv(e_c, e_n, vt, vnew):
    return _dot_nt(e_c, vt) + _dot(e_n, vnew)


def _attn_sample_kernel(sink_ref, q_ref, kan_ref, van_ref, kbn_ref, vbn_ref, cka_ref, cva_ref, ckb_ref, cvb_ref,
                        oa_ref, ob_ref):
    half_f = _lane_block_masks(LANES, HEAD_DIM, F32)
    lane_b = lax.broadcasted_iota(jnp.int32, (1, KV_B_W), 1)

    def dist(nrows, ncols, offset):
        r = lax.broadcasted_iota(jnp.int32, (nrows, ncols), 0)
        i = lax.broadcasted_iota(jnp.int32, (nrows, ncols), 1)
        return r, offset + (r & (SAMPLE_T - 1)) - i

    nrow_a = N_HEADS_A * SAMPLE_T
    _, dac = dist(nrow_a, CACHE_A, CACHE_A)
    _, dan = dist(nrow_a, BLOCK, 0)
    mask_ac = (dac >= 0) & (dac <= BAND_STEPS)
    mask_an = (dan >= 0) & (dan <= BAND_STEPS)
    nrow_b = len(DILATIONS) * N_KV_B * SAMPLE_T
    rows_per_group = N_KV_B * SAMPLE_T

    def mask_b(ncols, offset):
        r, db = dist(nrow_b, ncols, offset)
        dil = jnp.where(r < rows_per_group, DILATIONS[0], jnp.where(r < 2 * rows_per_group, DILATIONS[1], DILATIONS[2]))
        return (db >= 0) & (db <= BAND_STEPS * dil) & ((db & (dil - 1)) == 0)

    mask_bc = mask_b(CACHE_B, CACHE_B)
    mask_bn = mask_b(BLOCK, 0)

    sink_col = jnp.concatenate(
        [jnp.full((SAMPLE_T, 1), sink_ref[HEAD_PERM_A[p]], F32) for p in range(N_HEADS_A)], axis=0)

    def element(j):
        rows = slice(j * SAMPLE_T, (j + 1) * SAMPLE_T)
        qbd_a = jnp.concatenate(
            [q_ref[rows, (p // 2) * LANES:(p // 2 + 1) * LANES] * half_f[p % 2] for p in range(N_HEADS_A)],
            axis=0)
        qrows = []
        for g in range(len(DILATIONS)):
            for kvh in range(N_KV_B):
                c = (Q_A_W // LANES) + 2 * g + kvh // 2
                part = q_ref[rows, c * LANES:(c + 1) * LANES] * half_f[kvh % 2]
                zero = jnp.zeros_like(part)
                qrows.append(jnp.concatenate([part, zero] if kvh // 2 == 0 else [zero, part], axis=1))
        qbd_b = jnp.concatenate(qrows, axis=0)
        scores_a = _cached_scores(qbd_a, cka_ref[j], _new_rows_block(kan_ref[rows, :]))
        scores_b = _cached_scores(qbd_b, ckb_ref[j], _new_rows_block(kbn_ref[rows, :]))
        yield
        ea_c, ea_n, ma, la = _cached_softmax(*scores_a, mask_ac, mask_an)
        eb_c, eb_n, m, l = _cached_softmax(*scores_b, mask_bc, mask_bn)
        yield
        o = _cached_pv(ea_c, ea_n, cva_ref[j], _new_rows_block(van_ref[rows, :]))
        o = o * (1.0 / (la + jnp.exp(sink_col - ma)))
        per_head = [o[p * SAMPLE_T:(p + 1) * SAMPLE_T] for p in range(N_HEADS_A)]
        for c, pair in enumerate(_merge_pairs(per_head)):
            oa_ref[rows, c * LANES:(c + 1) * LANES] = pair
        o = _cached_pv(eb_c, eb_n, cvb_ref[j], _new_rows_block(vbn_ref[rows, :]))
        out = None
        for kvh in reversed(range(N_KV_B)):
            sl = [slice((g * N_KV_B + kvh) * SAMPLE_T, (g * N_KV_B + kvh + 1) * SAMPLE_T)
                  for g in range(len(DILATIONS))]
            mj = jnp.maximum(jnp.maximum(m[sl[0]], m[sl[1]]), m[sl[2]])
            ws = [jnp.exp(m[x] - mj) for x in sl]
            den = ws[0] * l[sl[0]] + ws[1] * l[sl[1]] + ws[2] * l[sl[2]]
            num = ws[0] * o[sl[0]] + ws[1] * o[sl[1]] + ws[2] * o[sl[2]]
            okv = num * (1.0 / den)
            out = okv if out is None else jnp.where(lane_b < (kvh + 1) * HEAD_DIM, okv, out)
        ob_ref[rows, :] = out
        yield

    elements = [element(j) for j in range(SAMPLE_ELEMS)]
    for _ in range(3):
        for e in elements:
            next(e)


def _attn_sample_call(sink, q32, ka32, va32, kb32, vb32, cka, cva, ckb, cvb):
    nb = cka.shape[0]
    bt = SAMPLE_ELEMS
    t = q32.shape[0]
    tok = lambda w: pl.BlockSpec((bt * SAMPLE_T, w), lambda i: (i, 0))
    cache = lambda w, n: pl.BlockSpec((bt, w, n), lambda i: (i, 0, 0))
    return pl.pallas_call(
        _attn_sample_kernel,
        grid=(nb // bt,),
        in_specs=[pl.BlockSpec(memory_space=pltpu.SMEM), tok(Q_A_W + Q_B_W),
                  tok(KV_A_W), tok(KV_A_W), tok(KV_B_W), tok(KV_B_W),
                  cache(KV_A_W, CACHE_A), cache(KV_A_W, CACHE_A), cache(KV_B_W, CACHE_B), cache(KV_B_W, CACHE_B)],
        out_specs=[tok(Q_A_W), tok(KV_B_W)],
        out_shape=[jax.ShapeDtypeStruct((t, Q_A_W), F32), jax.ShapeDtypeStruct((t, KV_B_W), F32)],
        compiler_params=pltpu.CompilerParams(dimension_semantics=("parallel",), vmem_limit_bytes=VMEM_LIMIT),
        name="attn_sample",
    )(sink, q32, ka32, va32, kb32, vb32, cka, cva, ckb, cvb)


def _mix_out(h, oa, ob, gm_ref, wgate_ref, wba_ref, wbb_ref, wout_ref):
    n = _rms(h, gm_ref[...]).astype(BF16)
    gates = _dot(n, wgate_ref[...])
    ga = _sigmoid(gates[:, :D_MODEL])
    gb = _sigmoid(gates[:, D_MODEL:])
    mixed = ga * _dot(oa.astype(BF16), wba_ref[...]) + gb * _dot(ob.astype(BF16), wbb_ref[...])
    return h + _dot(mixed.astype(BF16), wout_ref[...])


def _combine_groups(os, ls):
    m = jnp.maximum(jnp.maximum(ls[0], ls[1]), ls[2])
    es = [jnp.exp(x - m) for x in ls]
    den = es[0] + es[1] + es[2]
    num = es[0] * os[0] + es[1] * os[1] + es[2] * os[2]
    return num * (1.0 / den)


def _merge_split(split_ref, field, scr, rb):
    d = split_ref.shape[1]
    if d == 1:
        return split_ref[0, 0, rb, field * KV_B_W:(field + 1) * KV_B_W]
    n = (rb.stop - rb.start) // d
    src = slice(rb.start // d, rb.start // d + n)
    for r in range(d):
        for c in range(scr.shape[0]):
            lanes = slice(field * KV_B_W + c * LANES, field * KV_B_W + (c + 1) * LANES)
            scr[c, pl.ds(rb.start + r, n, stride=d), :] = split_ref[0, r, src, lanes]
    return jnp.concatenate([scr[c, rb, :] for c in range(scr.shape[0])], axis=1)


def _softmax_rows(s):
    m = jnp.max(s, axis=-1, keepdims=True)
    e = jnp.exp(s - m)
    return e * (1.0 / jnp.sum(e, axis=-1, keepdims=True))


def _back_prompt_kernel(h_ref, oa_ref, ol0_ref, ol1_ref, ol2_ref, mk_ref, mv_ref,
                        gm_ref, wgate_ref, wba_ref, wbb_ref, wout_ref, gq_ref, wq_ref, wo_ref,
                        g2_ref, wg_ref, wu_ref, wd_ref, gf_ref, y_ref, o1_s, o2_s, l1_s, l2_s, ob_s):
    blocks = _row_blocks(h_ref, BACK_ROW_PARTS)
    h3s = []
    for rb in blocks:
        ob_s[rb, :] = _combine_groups(
            (_merge_split(ol0_ref, 0, None, rb), _merge_split(ol1_ref, 0, o1_s, rb),
             _merge_split(ol2_ref, 0, o2_s, rb)),
            (_merge_split(ol0_ref, 1, None, rb), _merge_split(ol1_ref, 1, l1_s, rb),
             _merge_split(ol2_ref, 1, l2_s, rb))).astype(BF16)
        h2 = _mix_out(h_ref[rb, :], oa_ref[rb, :], ob_s[rb, :], gm_ref, wgate_ref, wba_ref, wbb_ref, wout_ref)
        q = _dot(_rms(h2, gq_ref[...]).astype(BF16), wq_ref[...])
        heads = [None] * MEM_HEADS

        def head(hd):
            cols = slice(hd * MEM_HEAD_DIM, (hd + 1) * MEM_HEAD_DIM)
            s = _dot_nt(q[:, cols].astype(BF16), mk_ref[0, :, cols]) * MEM_SCALE
            yield
            e = jnp.exp(s - jnp.max(s, axis=-1, keepdims=True))
            inv = 1.0 / jnp.sum(e, axis=-1, keepdims=True)
            yield
            heads[hd] = _dot(e.astype(BF16), mv_ref[0, :, cols]) * inv
            yield

        _issue_skewed([head(hd) for hd in range(MEM_HEADS)])
        oc = jnp.concatenate(heads, axis=1).astype(BF16)
        h3s.append(h2 + _dot(oc, wo_ref[...]))
    for rb, y in zip(blocks, _ffn_half(h3s, g2_ref, wg_ref, wu_ref, wd_ref)):
        y_ref[rb, :] = _rms(y, gf_ref[...])


def _back_prompt_call(h, oa, ols, mk, mv, w, seq):
    t = h.shape[0]
    tm = TOKEN_TILE
    tiles_per_batch = seq // tm
    row = lambda i: (i, 0)
    mem = lambda i: (i // tiles_per_batch, 0, 0)
    tok = lambda width: pl.BlockSpec((tm, width), row)
    split = lambda d: pl.BlockSpec((1, d, tm // d, 2 * KV_B_W),
                                   lambda i: (i // tiles_per_batch, 0, i % tiles_per_batch, 0))
    in_specs = ([tok(D_MODEL), tok(Q_A_W)] + [split(d) for d in DILATIONS]
                + [pl.BlockSpec((1, N_MEM, MEM_W), mem), pl.BlockSpec((1, N_MEM, MEM_W), mem)]
                + [_const_spec((1, D_MODEL)), _const_spec((D_MODEL, 2 * D_MODEL), (0, 1)), _const_spec((Q_A_W, D_MODEL)),
                   _const_spec((KV_B_W, D_MODEL)), _const_spec((D_MODEL, D_MODEL)),
                   _const_spec((1, D_MODEL)), _const_spec((D_MODEL, MEM_W)), _const_spec((MEM_W, D_MODEL)),
                   _const_spec((1, D_MODEL)), _const_spec((D_MODEL, D_FF)), _const_spec((D_MODEL, D_FF)),
                   _const_spec((D_FF, D_MODEL)), _const_spec((1, D_MODEL))])
    return pl.pallas_call(
        _back_prompt_kernel,
        grid=(t // tm,),
        in_specs=in_specs,
        out_specs=tok(D_MODEL),
        out_shape=jax.ShapeDtypeStruct((t, D_MODEL), F32),
        scratch_shapes=[pltpu.VMEM((KV_B_W // LANES, tm, LANES), F32)] * 4 + [pltpu.VMEM((tm, KV_B_W), BF16)],
        compiler_params=pltpu.CompilerParams(dimension_semantics=("parallel",), vmem_limit_bytes=VMEM_LIMIT),
        name="back_prompt",
    )(h, oa, *ols, mk, mv, w["mix_norm"], w["w_in"], w["w_branch_a"], w["w_branch_b"], w["w_out"],
      w["mem_q_norm"], w["w_mem_q"], w["w_mem_o"], w["ffn2_norm"], w["ffn2_wg"], w["ffn2_wu"], w["ffn2_wd"],
      w["final_norm"])


def _back_sample_a_kernel(h_ref, oa_ref, ob_ref, gm_ref, wgate_ref, wba_ref, wbb_ref, wout_ref, gq_ref, wq_ref,
                          h2_ref, q_ref):
    h2 = _mix_out(h_ref[...], oa_ref[...], ob_ref[...], gm_ref, wgate_ref, wba_ref, wbb_ref, wout_ref)
    h2_ref[...] = h2
    q_ref[...] = _dot(_rms(h2, gq_ref[...]).astype(BF16), wq_ref[...])


def _back_sample_a_call(h, oa, ob, w):
    t = h.shape[0]
    tm = TOKEN_TILE
    tok = lambda width: pl.BlockSpec((tm, width), lambda i: (i, 0))
    return pl.pallas_call(
        _back_sample_a_kernel,
        grid=(t // tm,),
        in_specs=[tok(D_MODEL), tok(Q_A_W), tok(KV_B_W),
                  _const_spec((1, D_MODEL)), _const_spec((D_MODEL, 2 * D_MODEL), (0, 1)), _const_spec((Q_A_W, D_MODEL)),
                  _const_spec((KV_B_W, D_MODEL)), _const_spec((D_MODEL, D_MODEL)),
                  _const_spec((1, D_MODEL)), _const_spec((D_MODEL, MEM_W))],
        out_specs=[tok(D_MODEL), tok(MEM_W)],
        out_shape=[jax.ShapeDtypeStruct((t, D_MODEL), F32), jax.ShapeDtypeStruct((t, MEM_W), F32)],
        compiler_params=pltpu.CompilerParams(dimension_semantics=("parallel",), vmem_limit_bytes=VMEM_LIMIT),
        name="back_sample_mix",
    )(h, oa, ob, w["mix_norm"], w["w_in"], w["w_branch_a"], w["w_branch_b"], w["w_out"],
      w["mem_q_norm"], w["w_mem_q"])


CROSS_BATCH_TILE = 8


def _cross_sample_kernel(q_ref, mk_ref, mv_ref, o_ref):
    nrow = MEM_HEADS * SAMPLE_T
    qhead = lax.broadcasted_iota(jnp.int32, (nrow, N_MEM * MEM_HEADS), 0) >> (SAMPLE_T.bit_length() - 1)
    khead = lax.broadcasted_iota(jnp.int32, (nrow, N_MEM * MEM_HEADS), 1) & (MEM_HEADS - 1)
    own_head = qhead == khead
    def element(bi):
        rows = slice(bi * SAMPLE_T, (bi + 1) * SAMPLE_T)
        qs = jnp.concatenate(
            [q_ref[rows, hd * MEM_HEAD_DIM:(hd + 1) * MEM_HEAD_DIM] for hd in range(MEM_HEADS)], axis=0)
        s = _dot_nt(qs, mk_ref[bi]) * MEM_SCALE
        yield
        p = _softmax_rows(jnp.where(own_head, s, NEG_INF))
        yield
        o = _dot(p, mv_ref[bi])
        for hd in range(MEM_HEADS):
            o_ref[rows, hd * MEM_HEAD_DIM:(hd + 1) * MEM_HEAD_DIM] = o[hd * SAMPLE_T:(hd + 1) * SAMPLE_T]
        yield

    _issue_skewed([element(bi) for bi in range(CROSS_BATCH_TILE)])


def _cross_sample_call(q, mk, mv):
    t = q.shape[0]
    bt = CROSS_BATCH_TILE
    tok = pl.BlockSpec((bt * SAMPLE_T, MEM_W), lambda i: (i, 0))
    mem = pl.BlockSpec((bt, N_MEM * MEM_HEADS, MEM_HEAD_DIM), lambda i: (i, 0, 0))
    return pl.pallas_call(
        _cross_sample_kernel,
        grid=(mk.shape[0] // bt,),
        in_specs=[tok, mem, mem],
        out_specs=tok,
        out_shape=jax.ShapeDtypeStruct((t, MEM_W), F32),
        compiler_params=pltpu.CompilerParams(dimension_semantics=("parallel",), vmem_limit_bytes=VMEM_LIMIT),
        name="cross_sample",
    )(q, mk, mv)


def _back_sample_b_kernel(h2_ref, oc_ref, wo_ref, g2_ref, wg_ref, wu_ref, wd_ref, gf_ref, y_ref):
    blocks = _row_blocks(h2_ref, BACK_ROW_PARTS)
    h3s = [h2_ref[rb, :] + _dot(oc_ref[rb, :].astype(BF16), wo_ref[...]) for rb in blocks]
    for rb, y in zip(blocks, _ffn_half(h3s, g2_ref, wg_ref, wu_ref, wd_ref)):
        y_ref[rb, :] = _rms(y, gf_ref[...])


def _back_sample_b_call(h2, oc, w):
    t = h2.shape[0]
    tm = TOKEN_TILE
    tok = lambda width: pl.BlockSpec((tm, width), lambda i: (i, 0))
    return pl.pallas_call(
        _back_sample_b_kernel,
        grid=(t // tm,),
        in_specs=[tok(D_MODEL), tok(MEM_W), _const_spec((MEM_W, D_MODEL)),
                  _const_spec((1, D_MODEL)), _const_spec((D_MODEL, D_FF)), _const_spec((D_MODEL, D_FF)),
                  _const_spec((D_FF, D_MODEL)), _const_spec((1, D_MODEL))],
        out_specs=tok(D_MODEL),
        out_shape=jax.ShapeDtypeStruct((t, D_MODEL), F32),
        compiler_params=pltpu.CompilerParams(dimension_semantics=("parallel",), vmem_limit_bytes=VMEM_LIMIT),
        name="back_sample_ffn",
    )(h2, oc, w["w_mem_o"], w["ffn2_norm"], w["ffn2_wg"], w["ffn2_wu"], w["ffn2_wd"], w["final_norm"])


def _mem_kv_kernel(mem_ref, g_ref, wk_ref, wv_ref, k32_ref, v32_ref, k16_ref, v16_ref):
    u = _rms(mem_ref[...], g_ref[...]).astype(BF16)
    k = _dot(u, wk_ref[...])
    v = _dot(u, wv_ref[...])
    k32_ref[...] = k
    v32_ref[...] = v
    k16_ref[...] = k.astype(BF16)
    v16_ref[...] = v.astype(BF16)


def _mem_kv_call(mem, w):
    rows = mem.shape[0]
    full = lambda width: pl.BlockSpec((rows, width), lambda i: (0, 0))
    return pl.pallas_call(
        _mem_kv_kernel,
        grid=(1,),
        in_specs=[full(D_MODEL), _const_spec((1, D_MODEL)), _const_spec((D_MODEL, MEM_W)),
                  _const_spec((D_MODEL, MEM_W))],
        out_specs=[full(MEM_W)] * 4,
        out_shape=[jax.ShapeDtypeStruct((rows, MEM_W), F32)] * 2 + [jax.ShapeDtypeStruct((rows, MEM_W), BF16)] * 2,
        compiler_params=pltpu.CompilerParams(dimension_semantics=("arbitrary",), vmem_limit_bytes=VMEM_LIMIT),
        name="mem_kv",
    )(mem, w["mem_kv_norm"], w["w_mem_k"], w["w_mem_v"])


def _rope_tables(pos):
    within = np.arange(LANES) % HEAD_DIM
    inv_freq = jnp.power(jnp.float32(ROPE_THETA), -jnp.arange(ROT_HALF, dtype=jnp.float32) / ROT_HALF)
    ang = pos.astype(jnp.float32)[:, None] * inv_freq[None, :]
    cos = jnp.tile(jnp.cos(ang), (1, LANES // ROT_HALF))
    sin = jnp.tile(jnp.sin(ang), (1, LANES // ROT_HALF))
    first = jnp.asarray(within < ROT_HALF)[None, :]
    second = jnp.asarray((within >= ROT_HALF) & (within < 2 * ROT_HALF))[None, :]
    return jnp.where(first | second, cos, 1.0), jnp.where(first, -sin, jnp.where(second, sin, 0.0))


def _prep_weights(p):
    bf = lambda x: x.astype(BF16)
    vec = lambda x: x.reshape(1, -1)
    return {
        "ffn1_norm": vec(p["ffn1_norm"]), "ffn1_wg": bf(p["ffn1_w_gate"]), "ffn1_wu": bf(p["ffn1_w_up"]),
        "ffn1_wd": bf(p["ffn1_w_down"]),
        "mix_norm": vec(p["mix_norm"]), "w_in": bf(p["w_in"]),
        "w_branch_a": bf(p["w_branch_a"].reshape(N_HEADS_A, HEAD_DIM, D_MODEL)[np.array(HEAD_PERM_A)].reshape(
            Q_A_W, D_MODEL)),
        "w_branch_b": bf(p["w_branch_b"]), "w_out": bf(p["w_out"]),
        "mem_q_norm": vec(p["mem_q_norm"]), "mem_kv_norm": vec(p["mem_kv_norm"]),
        "w_mem_q": bf(p["w_mem_q"]), "w_mem_k": bf(p["w_mem_k"]), "w_mem_v": bf(p["w_mem_v"]),
        "w_mem_o": bf(p["w_mem_o"]),
        "ffn2_norm": vec(p["ffn2_norm"]), "ffn2_wg": bf(p["ffn2_w_gate"]), "ffn2_wu": bf(p["ffn2_w_up"]),
        "ffn2_wd": bf(p["ffn2_w_down"]), "final_norm": vec(p["final_norm"]),
    }


def kernel(x_prompt, x_sample, cache_swa_k, cache_swa_v, cache_dil_k, cache_dil_v, cache_mem_k, cache_mem_v, mem_prompt, ffn1_norm, ffn1_w_gate, ffn1_w_up, ffn1_w_down, mix_norm, w_in, attn_sink, w_branch_a, w_branch_b, w_out, mem_q_norm, mem_kv_norm, w_mem_q, w_mem_k, w_mem_v, w_mem_o, ffn2_norm, ffn2_w_gate, ffn2_w_up, ffn2_w_down, final_norm):
    depth = ffn1_norm.shape[0]
    assert depth == 1
    batch, seq, d = x_prompt.shape
    nb, t_new, _ = x_sample.shape
    assert d == D_MODEL and t_new == SAMPLE_T and seq % (TOKEN_TILE * DILATIONS[-1]) == 0 and seq % ATTN_CHUNK == 0
    assert cache_swa_k.shape[2] == CACHE_A and cache_dil_k.shape[2] == CACHE_B
    layer = lambda x: x[0]
    w = _prep_weights(dict(
        ffn1_norm=layer(ffn1_norm), ffn1_w_gate=layer(ffn1_w_gate), ffn1_w_up=layer(ffn1_w_up),
        ffn1_w_down=layer(ffn1_w_down), mix_norm=layer(mix_norm), w_in=layer(w_in),
        w_branch_a=layer(w_branch_a), w_branch_b=layer(w_branch_b), w_out=layer(w_out),
        mem_q_norm=layer(mem_q_norm), mem_kv_norm=layer(mem_kv_norm), w_mem_q=layer(w_mem_q),
        w_mem_k=layer(w_mem_k), w_mem_v=layer(w_mem_v), w_mem_o=layer(w_mem_o), ffn2_norm=layer(ffn2_norm),
        ffn2_w_gate=layer(ffn2_w_gate), ffn2_w_up=layer(ffn2_w_up), ffn2_w_down=layer(ffn2_w_down),
        final_norm=final_norm))
    sink = layer(attn_sink)

    ts = nb * t_new
    pos_s = PAST_LEN + (jnp.arange(TOKEN_TILE, dtype=jnp.int32) % t_new)
    tables_s = _rope_tables(pos_s)
    (h_s, q32, ka32_s, va32_s, kb32_s, vb32_s) = _front_call(x_sample.reshape(ts, d), tables_s, w, sample=True)
    by_pos = lambda c: jnp.transpose(layer(c), (0, 2, 3, 1)).reshape(nb, c.shape[3] * c.shape[4], c.shape[2])
    by_slot_head = lambda c: layer(c).reshape(nb, N_MEM * MEM_HEADS, MEM_HEAD_DIM)

    oa_s, ob_s = _attn_sample_call(
        sink, q32, ka32_s, va32_s, kb32_s, vb32_s,
        by_pos(cache_swa_k), by_pos(cache_swa_v), by_pos(cache_dil_k), by_pos(cache_dil_v))

    tp = batch * seq
    mk32, mv32, mk16, mv16 = _mem_kv_call(mem_prompt.reshape(batch * N_MEM, d), w)
    tables_p = _rope_tables(jnp.arange(seq, dtype=jnp.int32))
    h_p, main, split4, split16, tails = _front_call(x_prompt.reshape(tp, d), tables_p, w, sample=False, batch=batch)
    oa = _attn_a_call(sink, main, batch, seq)
    groups = ((main.reshape(batch, 1, seq, MAIN_W), MAIN_QB_COL), (split4, 0), (split16, 0))
    ols = [_attn_b_call(qkv, qcol, dil) for (qkv, qcol), dil in zip(groups, DILATIONS)]
    y_p = _back_prompt_call(h_p, oa, ols, mk16.reshape(batch, N_MEM, MEM_W),
                            mv16.reshape(batch, N_MEM, MEM_W), w, seq)

    h2_s, qc_s = _back_sample_a_call(h_s, oa_s, ob_s, w)
    oc_s = _cross_sample_call(qc_s, by_slot_head(cache_mem_k), by_slot_head(cache_mem_v))
    y_s = _back_sample_b_call(h2_s, oc_s, w)

    keep_a, keep_b = min(CACHE_A, seq), min(CACHE_B, seq)
    assert max(keep_a, keep_b) <= TAIL_ROWS <= seq and TAIL_ROWS % TOKEN_TILE == 0

    def tail(col, keep, heads):
        part = tails[:, col:col + heads * HEAD_DIM, TAIL_ROWS - keep:].reshape(batch, heads, HEAD_DIM, keep)
        return jnp.transpose(part, (0, 3, 1, 2))[None]

    new = lambda x, heads: x.reshape(1, nb, t_new, heads, HEAD_DIM)
    memo = lambda x: x.reshape(1, batch, N_MEM, MEM_HEADS, MEM_HEAD_DIM)
    return (y_p.reshape(batch, seq, d), y_s.reshape(nb, t_new, d),
            tail(0, keep_a, N_KV_A), tail(KV_A_W, keep_a, N_KV_A),
            tail(2 * KV_A_W, keep_b, N_KV_B), tail(2 * KV_A_W + KV_B_W, keep_b, N_KV_B),
            memo(mk32), memo(mv32),
            new(ka32_s, N_KV_A), new(va32_s, N_KV_A), new(kb32_s, N_KV_B), new(vb32_s, N_KV_B))
```

```python
import functools

import jax
import jax.numpy as jnp
import numpy as np
from jax import lax
from jax.experimental import pallas as pl
from jax.experimental.pallas import tpu as pltpu

F32 = jnp.float32
BF16 = jnp.bfloat16

D_MODEL = 1024
D_FF = 2816
HEAD_DIM = 64
ROT_HALF = 8
ROPE_THETA = 500000.0
ATTN_SCALE = HEAD_DIM ** -0.5
RMS_EPS = 1e-6
PAST_LEN = 16384

N_HEADS_A = 8
N_KV_A = 2
N_KV_B = 4
DILATIONS = (1, 4, 16)
BAND_STEPS = 128
Q_A_W = N_HEADS_A * HEAD_DIM
KV_A_W = N_KV_A * HEAD_DIM
Q_B_W = len(DILATIONS) * N_KV_B * HEAD_DIM
KV_B_W = N_KV_B * HEAD_DIM
QKV_W = Q_A_W + 2 * KV_A_W + Q_B_W + 2 * KV_B_W
MEM_HEADS = 4
MEM_HEAD_DIM = 128
MEM_W = MEM_HEADS * MEM_HEAD_DIM
MEM_SCALE = MEM_HEAD_DIM ** -0.5
N_MEM = 256

HEAD_PERM_A = (0, 4, 1, 5, 2, 6, 3, 7)

MAIN_W = Q_A_W + 2 * KV_A_W + 3 * KV_B_W
MAIN_KA_COL, MAIN_VA_COL = Q_A_W // KV_A_W, Q_A_W // KV_A_W + 1
MAIN_QB_COL = (Q_A_W + 2 * KV_A_W) // KV_B_W
SPLIT_W = 3 * KV_B_W
TAILS_W = 2 * KV_A_W + 2 * KV_B_W
TAIL_ROWS = 2048

LANES = 128
BLOCK = 128
TOKEN_TILE = 512
FRONT_ROW_PARTS = 2
BACK_ROW_PARTS = 1
ATTN_CHUNK = 1024
FF_CHUNKS = ((0, 1536), (1536, 2816))
VMEM_LIMIT = 60 * 1024 * 1024
NEG_INF = float("-inf")


def _rms(x, g):
    ms = jnp.mean(x * x, axis=-1, keepdims=True)
    return x * lax.rsqrt(ms + RMS_EPS) * g


def _sigmoid(x):
    return 0.5 * jnp.tanh(0.5 * x) + 0.5


def _dot(a, b):
    return jnp.dot(a, b, preferred_element_type=F32)


def _dot_nt(a, b):
    return lax.dot_general(a, b, (((1,), (1,)), ((), ())), preferred_element_type=F32)


def _ffn_half(xs, g_ref, wg_ref, wu_ref, wd_ref):
    us = [_rms(x, g_ref[...]).astype(BF16) for x in xs]
    accs = [None] * len(xs)
    for lo, hi in FF_CHUNKS:
        gates = [_dot(u, wg_ref[:, lo:hi]) for u in us]
        ups = [_dot(u, wu_ref[:, lo:hi]) for u in us]
        acts = [(gate * _sigmoid(gate) * up).astype(BF16) for gate, up in zip(gates, ups)]
        for i, act in enumerate(acts):
            part = _dot(act, wd_ref[lo:hi, :])
            accs[i] = part if accs[i] is None else accs[i] + part
    return [x + 0.5 * acc for x, acc in zip(xs, accs)]


def _row_blocks(ref, parts):
    n = ref.shape[0] // parts
    return [slice(i * n, (i + 1) * n) for i in range(parts)]


def _swap_halves(x):
    return pltpu.roll(x, HEAD_DIM, 1)


def _low_half():
    return lax.broadcasted_iota(jnp.int32, (1, LANES), 1) < HEAD_DIM


def _pair_kv_groups(chunks):
    low = _low_half()
    c0, c1, c2, c3 = chunks
    return [jnp.where(low, c0, _swap_halves(c2)), jnp.where(low, _swap_halves(c0), c2),
            jnp.where(low, c1, _swap_halves(c3)), jnp.where(low, _swap_halves(c1), c3)]


def _merge_pairs(per_head):
    low = _low_half()
    return [jnp.where(low, per_head[2 * c], per_head[2 * c + 1]) for c in range(N_HEADS_A // 2)]


def _front_kernel(x_ref, cos_ref, sin_ref, g1_ref, wg_ref, wu_ref, wd_ref, gm_ref, wqkv_ref,
                  h_ref, *rest, sample, tiles_per_seq=None, tail_tiles=None):
    blocks = _row_blocks(x_ref, FRONT_ROW_PARTS)
    hs = _ffn_half([x_ref[rb, :] for rb in blocks], g1_ref, wg_ref, wu_ref, wd_ref)
    for rb, h in zip(blocks, hs):
        h_ref[rb, :] = h
    ns = [_rms(h, gm_ref[...]).astype(BF16) for h in hs]
    zs = [_dot(n, wqkv_ref[...]) for n in ns]
    for rb, z in zip(blocks, zs):
        _front_emit(rb, z, cos_ref, sin_ref, rest, sample)
    if not sample:
        tails_ref, kb_s, vb_s, kva_s = rest[3], rest[6], rest[7], rest[8]

        @pl.when(lax.rem(pl.program_id(0), tiles_per_seq) >= tiles_per_seq - tail_tiles)
        def _():
            for c, col in enumerate([kva_s[0], kva_s[1], kb_s[0], kb_s[1], vb_s[0], vb_s[1]]):
                tails_ref[0, c * LANES:(c + 1) * LANES, :] = col.T


def _front_emit(rb, z, cos_ref, sin_ref, rest, sample):
    cos, sin = cos_ref[rb, :], sin_ref[rb, :]
    lane_in_head = lax.broadcasted_iota(jnp.int32, (1, LANES), 1) & (HEAD_DIM - 1)
    first = lane_in_head < ROT_HALF

    def chunk(c):
        return z[:, c * LANES:(c + 1) * LANES]

    def rope(c):
        zc = chunk(c)
        partner = jnp.where(first, pltpu.roll(zc, LANES - ROT_HALF, 1), pltpu.roll(zc, ROT_HALF, 1))
        return zc * cos + partner * sin

    nqa, nka = Q_A_W // LANES, KV_A_W // LANES
    nqb, nkb = Q_B_W // LANES, KV_B_W // LANES
    c0 = 0
    qa = _pair_kv_groups([rope(c0 + c) * ATTN_SCALE for c in range(nqa)])
    c0 += nqa
    ka = [rope(c0 + c) for c in range(nka)]
    c0 += nka
    va = [chunk(c0 + c) for c in range(nka)]
    c0 += nka
    qb = [rope(c0 + c) * ATTN_SCALE for c in range(nqb)]
    c0 += nqb
    kb = [rope(c0 + c) for c in range(nkb)]
    c0 += nkb
    vb = [chunk(c0 + c) for c in range(nkb)]

    def store(ref, parts, dtype):
        for c, p in enumerate(parts):
            ref[rb, c * LANES:(c + 1) * LANES] = p.astype(dtype)

    def fill(scr, parts):
        for c, p in enumerate(parts):
            scr[c, rb, :] = p

    def store_split(ref, slot, scr, d):
        n = (rb.stop - rb.start) // d
        dst = slice(rb.start // d, rb.start // d + n)
        for r in range(d):
            for c in range(scr.shape[0]):
                lanes = slice(slot * KV_B_W + c * LANES, slot * KV_B_W + (c + 1) * LANES)
                ref[0, r, dst, lanes] = scr[c, pl.ds(rb.start + r, n, stride=d), :].astype(BF16)

    if sample:
        q32_ref, ka32_ref, va32_ref, kb32_ref, vb32_ref = rest
        store(q32_ref, qa + qb, F32)
        store(ka32_ref, ka, F32)
        store(va32_ref, va, F32)
        store(kb32_ref, kb, F32)
        store(vb32_ref, vb, F32)
    else:
        main_ref, s4_ref, s16_ref, _, q4_s, q16_s, kb_s, vb_s, kva_s = rest
        gw = KV_B_W // LANES
        store(main_ref, qa + ka + va + qb[0:gw] + kb + vb, BF16)
        fill(q4_s, qb[gw:2 * gw])
        fill(q16_s, qb[2 * gw:3 * gw])
        fill(kb_s, kb)
        fill(vb_s, vb)
        fill(kva_s, ka + va)
        for slot, scr4, scr16 in ((0, q4_s, q16_s), (1, kb_s, kb_s), (2, vb_s, vb_s)):
            store_split(s4_ref, slot, scr4, DILATIONS[1])
            store_split(s16_ref, slot, scr16, DILATIONS[2])


def _const_spec(shape, index=None):
    index = (0,) * len(shape) if index is None else index
    return pl.BlockSpec(shape, lambda *_: index, pipeline_mode=pl.Buffered(1))


def _front_call(x, tables, w, *, sample, batch=1):
    t = x.shape[0]
    tm = TOKEN_TILE
    nsteps = t // tm
    tpb = nsteps // batch
    seq = t // batch
    cos_t, sin_t = tables
    tab_blocks = cos_t.shape[0] // tm
    row = lambda i: (i, 0)
    tab = lambda i: (i % tab_blocks, 0)

    def tok(width, dtype):
        return jax.ShapeDtypeStruct((t, width), dtype), pl.BlockSpec((tm, width), row)

    def split(d):
        return (jax.ShapeDtypeStruct((batch, d, seq // d, SPLIT_W), BF16),
                pl.BlockSpec((1, d, tm // d, SPLIT_W), lambda i: (i // tpb, 0, i % tpb, 0)))

    outs = [tok(D_MODEL, F32)]
    scratch = []
    static = {}
    if sample:
        outs += [tok(Q_A_W + Q_B_W, F32), tok(KV_A_W, F32), tok(KV_A_W, F32), tok(KV_B_W, F32), tok(KV_B_W, F32)]
    else:
        tail_tiles = TAIL_ROWS // tm
        tails = (jax.ShapeDtypeStruct((batch, TAILS_W, TAIL_ROWS), F32),
                 pl.BlockSpec((1, TAILS_W, tm),
                              lambda i: (i // tpb, 0, jnp.maximum(i % tpb - (tpb - tail_tiles), 0))))
        outs += [tok(MAIN_W, BF16), split(DILATIONS[1]), split(DILATIONS[2]), tails]
        scratch = [pltpu.VMEM((KV_B_W // LANES, tm, LANES), F32)] * 4 + [pltpu.VMEM((2, tm, LANES), F32)]
        static = dict(tiles_per_seq=tpb, tail_tiles=tail_tiles)
    out_shape, out_specs = zip(*outs)
    in_specs = [pl.BlockSpec((tm, D_MODEL), row),
                pl.BlockSpec((tm, LANES), tab), pl.BlockSpec((tm, LANES), tab),
                _const_spec((1, D_MODEL)), _const_spec((D_MODEL, D_FF)), _const_spec((D_MODEL, D_FF)),
                _const_spec((D_FF, D_MODEL)), _const_spec((1, D_MODEL)), _const_spec((D_MODEL, QKV_W), (0, 0))]
    return pl.pallas_call(
        functools.partial(_front_kernel, sample=sample, **static),
        grid=(nsteps,),
        in_specs=in_specs,
        out_specs=list(out_specs),
        out_shape=list(out_shape),
        scratch_shapes=scratch,
        compiler_params=pltpu.CompilerParams(dimension_semantics=("parallel" if sample else "arbitrary",),
                                             vmem_limit_bytes=VMEM_LIMIT),
        name="front_sample" if sample else "front_prompt",
    )(x, cos_t, sin_t, w["ffn1_norm"], w["ffn1_wg"], w["ffn1_wu"], w["ffn1_wd"], w["mix_norm"], w["w_in"])


def _lane_block_masks(width, block, dtype):
    lane = lax.broadcasted_iota(jnp.int32, (1, width), 1)
    return [((lane >= i * block) & (lane < (i + 1) * block)).astype(dtype) for i in range(width // block)]


def _band_mask():
    qi = lax.broadcasted_iota(jnp.int32, (BLOCK, 2 * BLOCK), 0)
    kj = lax.broadcasted_iota(jnp.int32, (BLOCK, 2 * BLOCK), 1)
    dist = qi + BLOCK - kj
    return (dist >= 0) & (dist <= BAND_STEPS), kj >= BLOCK


def _band_block(load_qbd, load_k, load_v, mask, nheads, finish):
    s = _dot_nt(load_qbd(), load_k())
    yield
    ps, ms, ls = [], [], []
    for p in range(nheads):
        sp = jnp.where(mask, s[p * BLOCK:(p + 1) * BLOCK], NEG_INF)
        m = jnp.max(sp, axis=-1, keepdims=True)
        e = jnp.exp(sp - m)
        ls.append(jnp.sum(e, axis=-1, keepdims=True))
        ms.append(m)
        ps.append(e.astype(BF16))
    pc = jnp.concatenate(ps, axis=0)
    yield
    finish(_dot(pc, load_v()), ms, ls)
    yield


def _issue_skewed(items, nstages=3):
    for tick in range(len(items) + nstages - 1):
        for s in reversed(range(nstages)):
            i = tick - s
            if 0 <= i < len(items):
                next(items[i])


def _kv_window(cur_ref, prev_ref, qb):
    if qb == 0:
        return jnp.concatenate([prev_ref[...], cur_ref[0:BLOCK]], axis=0)
    return cur_ref[(qb - 1) * BLOCK:(qb + 1) * BLOCK]


def _attn_a_kernel(sink_ref, q_ref, kc_ref, kp_ref, vc_ref, vp_ref, o_ref):
    first_chunk = pl.program_id(1) == 0
    band, in_cur = _band_mask()
    half_bf = _lane_block_masks(LANES, HEAD_DIM, BF16)

    def block(qb):
        mask = band & (in_cur | jnp.logical_not(first_chunk)) if qb == 0 else band
        rows = slice(qb * BLOCK, (qb + 1) * BLOCK)

        def load_qbd():
            return jnp.concatenate(
                [q_ref[rows, (p // 2) * LANES:(p // 2 + 1) * LANES] * half_bf[p % 2] for p in range(N_HEADS_A)],
                axis=0)

        def finish(o, ms, ls):
            normed = []
            for p in range(N_HEADS_A):
                den = ls[p] + jnp.exp(sink_ref[HEAD_PERM_A[p]] - ms[p])
                normed.append(o[p * BLOCK:(p + 1) * BLOCK] * (1.0 / den))
            for c, pair in enumerate(_merge_pairs(normed)):
                o_ref[rows, c * LANES:(c + 1) * LANES] = pair.astype(BF16)

        return _band_block(load_qbd, lambda: _kv_window(kc_ref, kp_ref, qb), lambda: _kv_window(vc_ref, vp_ref, qb),
                           mask, N_HEADS_A, finish)

    for qb in range(q_ref.shape[0] // BLOCK):
        for _ in block(qb):
            pass


def _attn_a_call(sink, main, batch, seq):
    t = main.shape[0]
    cpb = seq // ATTN_CHUNK
    bpc = ATTN_CHUNK // BLOCK
    cur = lambda col: (lambda b, c: (b * cpb + c, col))
    prev = lambda col: (lambda b, c: (b * cpb * bpc + jnp.maximum(c * bpc - 1, 0), col))
    kv = lambda col: [pl.BlockSpec((ATTN_CHUNK, KV_A_W), cur(col)), pl.BlockSpec((BLOCK, KV_A_W), prev(col))]
    return pl.pallas_call(
        _attn_a_kernel,
        grid=(batch, cpb),
        in_specs=[pl.BlockSpec(memory_space=pltpu.SMEM), pl.BlockSpec((ATTN_CHUNK, Q_A_W), cur(0))]
        + kv(MAIN_KA_COL) + kv(MAIN_VA_COL),
        out_specs=pl.BlockSpec((ATTN_CHUNK, Q_A_W), cur(0)),
        out_shape=jax.ShapeDtypeStruct((t, Q_A_W), BF16),
        compiler_params=pltpu.CompilerParams(dimension_semantics=("parallel", "parallel"),
                                             vmem_limit_bytes=VMEM_LIMIT),
        name="attn_a_prompt",
    )(sink, main, main, main, main, main)


def _attn_b_kernel(q_ref, kc_ref, kp_ref, vc_ref, vp_ref, ol_ref):
    first_chunk = pl.program_id(2) == 0
    band, in_cur = _band_mask()
    head_bf = _lane_block_masks(KV_B_W, HEAD_DIM, BF16)
    lane = lax.broadcasted_iota(jnp.int32, (1, KV_B_W), 1)

    def block(r, qb):
        mask = band & (in_cur | jnp.logical_not(first_chunk)) if qb == 0 else band
        rows = slice(qb * BLOCK, (qb + 1) * BLOCK)

        def load_qbd():
            q = q_ref[r, rows, :]
            return jnp.concatenate([q * head_bf[p] for p in range(N_KV_B)], axis=0)

        def finish(o, ms, ls):
            out = lse = None
            for p in reversed(range(N_KV_B)):
                op = o[p * BLOCK:(p + 1) * BLOCK] * (1.0 / ls[p])
                lp = jnp.broadcast_to(ms[p] + jnp.log(ls[p]), (BLOCK, KV_B_W))
                if out is None:
                    out, lse = op, lp
                else:
                    sel = lane < (p + 1) * HEAD_DIM
                    out, lse = jnp.where(sel, op, out), jnp.where(sel, lp, lse)
            ol_ref[r, rows, 0:KV_B_W] = out
            ol_ref[r, rows, KV_B_W:2 * KV_B_W] = lse

        return _band_block(load_qbd, lambda: _kv_window(kc_ref.at[r], kp_ref.at[r], qb),
                           lambda: _kv_window(vc_ref.at[r], vp_ref.at[r], qb), mask, N_KV_B, finish)

    _issue_skewed([block(r, qb) for r in range(q_ref.shape[0]) for qb in range(q_ref.shape[1] // BLOCK)])


def _attn_b_call(qkv, qcol, d):
    batch, _, rows, _ = qkv.shape
    chunk = min(ATTN_CHUNK, rows)
    rps = min(d, ATTN_CHUNK // chunk)
    cpb = rows // chunk
    bpc = chunk // BLOCK
    cur = lambda col: pl.BlockSpec((None, rps, chunk, KV_B_W), lambda b, r, c: (b, r, c, col))
    prev = lambda col: pl.BlockSpec((None, rps, BLOCK, KV_B_W),
                                    lambda b, r, c: (b, r, jnp.maximum(c * bpc - 1, 0), col))
    return pl.pallas_call(
        _attn_b_kernel,
        grid=(batch, d // rps, cpb),
        in_specs=[cur(qcol), cur(qcol + 1), prev(qcol + 1), cur(qcol + 2), prev(qcol + 2)],
        out_specs=pl.BlockSpec((None, rps, chunk, 2 * KV_B_W), lambda b, r, c: (b, r, c, 0)),
        out_shape=jax.ShapeDtypeStruct((batch, d, rows, 2 * KV_B_W), F32),
        compiler_params=pltpu.CompilerParams(dimension_semantics=("parallel", "parallel", "parallel"),
                                             vmem_limit_bytes=VMEM_LIMIT),
        name=f"attn_b_prompt_d{d}",
    )(qkv, qkv, qkv, qkv, qkv)


SAMPLE_T = 8
CACHE_A = 128
CACHE_B = 2048
SAMPLE_ELEMS = 4


def _new_rows_block(new):
    return jnp.concatenate([new, jnp.zeros((BLOCK - SAMPLE_T, new.shape[1]), F32)], axis=0)


def _cached_scores(qbd, kt, knew):
    return _dot(qbd, kt), _dot_nt(qbd, knew)


def _cached_softmax(s_c, s_n, mask_c, mask_n):
    s_c = jnp.where(mask_c, s_c, NEG_INF)
    s_n = jnp.where(mask_n, s_n, NEG_INF)
    m = jnp.maximum(jnp.max(s_c, axis=-1, keepdims=True), jnp.max(s_n, axis=-1, keepdims=True))
    e_c = jnp.exp(s_c - m)
    e_n = jnp.exp(s_n - m)
    l = jnp.sum(e_c, axis=-1, keepdims=True) + jnp.sum(e_n, axis=-1, keepdims=True)
    return e_c, e_n, m, l


def _cached_pv(e_c, e_n, vt, vnew):
    return _dot_nt(e_c, vt) + _dot(e_n, vnew)


def _attn_sample_kernel(sink_ref, q_ref, kan_ref, van_ref, kbn_ref, vbn_ref, cka_ref, cva_ref, ckb_ref, cvb_ref,
                        oa_ref, ob_ref):
    half_f = _lane_block_masks(LANES, HEAD_DIM, F32)
    lane_b = lax.broadcasted_iota(jnp.int32, (1, KV_B_W), 1)

    def dist(nrows, ncols, offset):
        r = lax.broadcasted_iota(jnp.int32, (nrows, ncols), 0)
        i = lax.broadcasted_iota(jnp.int32, (nrows, ncols), 1)
        return r, offset + (r & (SAMPLE_T - 1)) - i

    nrow_a = N_HEADS_A * SAMPLE_T
    _, dac = dist(nrow_a, CACHE_A, CACHE_A)
    _, dan = dist(nrow_a, BLOCK, 0)
    mask_ac = (dac >= 0) & (dac <= BAND_STEPS)
    mask_an = (dan >= 0) & (dan <= BAND_STEPS)
    nrow_b = len(DILATIONS) * N_KV_B * SAMPLE_T
    rows_per_group = N_KV_B * SAMPLE_T

    def mask_b(ncols, offset):
        r, db = dist(nrow_b, ncols, offset)
        dil = jnp.where(r < rows_per_group, DILATIONS[0], jnp.where(r < 2 * rows_per_group, DILATIONS[1], DILATIONS[2]))
        return (db >= 0) & (db <= BAND_STEPS * dil) & ((db & (dil - 1)) == 0)

    mask_bc = mask_b(CACHE_B, CACHE_B)
    mask_bn = mask_b(BLOCK, 0)

    sink_col = jnp.concatenate(
        [jnp.full((SAMPLE_T, 1), sink_ref[HEAD_PERM_A[p]], F32) for p in range(N_HEADS_A)], axis=0)

    def element(j):
        rows = slice(j * SAMPLE_T, (j + 1) * SAMPLE_T)
        qbd_a = jnp.concatenate(
            [q_ref[rows, (p // 2) * LANES:(p // 2 + 1) * LANES] * half_f[p % 2] for p in range(N_HEADS_A)],
            axis=0)
        qrows = []
        for g in range(len(DILATIONS)):
            for kvh in range(N_KV_B):
                c = (Q_A_W // LANES) + 2 * g + kvh // 2
                part = q_ref[rows, c * LANES:(c + 1) * LANES] * half_f[kvh % 2]
                zero = jnp.zeros_like(part)
                qrows.append(jnp.concatenate([part, zero] if kvh // 2 == 0 else [zero, part], axis=1))
        qbd_b = jnp.concatenate(qrows, axis=0)
        scores_a = _cached_scores(qbd_a, cka_ref[j], _new_rows_block(kan_ref[rows, :]))
        scores_b = _cached_scores(qbd_b, ckb_ref[j], _new_rows_block(kbn_ref[rows, :]))
        yield
        ea_c, ea_n, ma, la = _cached_softmax(*scores_a, mask_ac, mask_an)
        eb_c, eb_n, m, l = _cached_softmax(*scores_b, mask_bc, mask_bn)
        yield
        o = _cached_pv(ea_c, ea_n, cva_ref[j], _new_rows_block(van_ref[rows, :]))
        o = o * (1.0 / (la + jnp.exp(sink_col - ma)))
        per_head = [o[p * SAMPLE_T:(p + 1) * SAMPLE_T] for p in range(N_HEADS_A)]
        for c, pair in enumerate(_merge_pairs(per_head)):
            oa_ref[rows, c * LANES:(c + 1) * LANES] = pair
        o = _cached_pv(eb_c, eb_n, cvb_ref[j], _new_rows_block(vbn_ref[rows, :]))
        out = None
        for kvh in reversed(range(N_KV_B)):
            sl = [slice((g * N_KV_B + kvh) * SAMPLE_T, (g * N_KV_B + kvh + 1) * SAMPLE_T)
                  for g in range(len(DILATIONS))]
            mj = jnp.maximum(jnp.maximum(m[sl[0]], m[sl[1]]), m[sl[2]])
            ws = [jnp.exp(m[x] - mj) for x in sl]
            den = ws[0] * l[sl[0]] + ws[1] * l[sl[1]] + ws[2] * l[sl[2]]
            num = ws[0] * o[sl[0]] + ws[1] * o[sl[1]] + ws[2] * o[sl[2]]
            okv = num * (1.0 / den)
            out = okv if out is None else jnp.where(lane_b < (kvh + 1) * HEAD_DIM, okv, out)
        ob_ref[rows, :] = out
        yield

    elements = [element(j) for j in range(SAMPLE_ELEMS)]
    for _ in range(3):
        for e in elements:
            next(e)


def _attn_sample_call(sink, q32, ka32, va32, kb32, vb32, cka, cva, ckb, cvb):
    nb = cka.shape[0]
    bt = SAMPLE_ELEMS
    t = q32.shape[0]
    tok = lambda w: pl.BlockSpec((bt * SAMPLE_T, w), lambda i: (i, 0))
    cache = lambda w, n: pl.BlockSpec((bt, w, n), lambda i: (i, 0, 0))
    return pl.pallas_call(
        _attn_sample_kernel,
        grid=(nb // bt,),
        in_specs=[pl.BlockSpec(memory_space=pltpu.SMEM), tok(Q_A_W + Q_B_W),
                  tok(KV_A_W), tok(KV_A_W), tok(KV_B_W), tok(KV_B_W),
                  cache(KV_A_W, CACHE_A), cache(KV_A_W, CACHE_A), cache(KV_B_W, CACHE_B), cache(KV_B_W, CACHE_B)],
        out_specs=[tok(Q_A_W), tok(KV_B_W)],
        out_shape=[jax.ShapeDtypeStruct((t, Q_A_W), F32), jax.ShapeDtypeStruct((t, KV_B_W), F32)],
        compiler_params=pltpu.CompilerParams(dimension_semantics=("parallel",), vmem_limit_bytes=VMEM_LIMIT),
        name="attn_sample",
    )(sink, q32, ka32, va32, kb32, vb32, cka, cva, ckb, cvb)


def _mix_out(h, oa, ob, gm_ref, wgate_ref, wba_ref, wbb_ref, wout_ref):
    n = _rms(h, gm_ref[...]).astype(BF16)
    gates = _dot(n, wgate_ref[...])
    ga = _sigmoid(gates[:, :D_MODEL])
    gb = _sigmoid(gates[:, D_MODEL:])
    mixed = ga * _dot(oa.astype(BF16), wba_ref[...]) + gb * _dot(ob.astype(BF16), wbb_ref[...])
    return h + _dot(mixed.astype(BF16), wout_ref[...])


def _combine_groups(os, ls):
    m = jnp.maximum(jnp.maximum(ls[0], ls[1]), ls[2])
    es = [jnp.exp(x - m) for x in ls]
    den = es[0] + es[1] + es[2]
    num = es[0] * os[0] + es[1] * os[1] + es[2] * os[2]
    return num * (1.0 / den)


def _merge_split(split_ref, field, scr, rb):
    d = split_ref.shape[1]
    if d == 1:
        return split_ref[0, 0, rb, field * KV_B_W:(field + 1) * KV_B_W]
    n = (rb.stop - rb.start) // d
    src = slice(rb.start // d, rb.start // d + n)
    for r in range(d):
        for c in range(scr.shape[0]):
            lanes = slice(field * KV_B_W + c * LANES, field * KV_B_W + (c + 1) * LANES)
            scr[c, pl.ds(rb.start + r, n, stride=d), :] = split_ref[0, r, src, lanes]
    return jnp.concatenate([scr[c, rb, :] for c in range(scr.shape[0])], axis=1)


def _softmax_rows(s):
    m = jnp.max(s, axis=-1, keepdims=True)
    e = jnp.exp(s - m)
    return e * (1.0 / jnp.sum(e, axis=-1, keepdims=True))


def _back_prompt_kernel(h_ref, oa_ref, ol0_ref, ol1_ref, ol2_ref, mk_ref, mv_ref,
                        gm_ref, wgate_ref, wba_ref, wbb_ref, wout_ref, gq_ref, wq_ref, wo_ref,
                        g2_ref, wg_ref, wu_ref, wd_ref, gf_ref, y_ref, o1_s, o2_s, l1_s, l2_s, ob_s):
    blocks = _row_blocks(h_ref, BACK_ROW_PARTS)
    h3s = []
    for rb in blocks:
        ob_s[rb, :] = _combine_groups(
            (_merge_split(ol0_ref, 0, None, rb), _merge_split(ol1_ref, 0, o1_s, rb),
             _merge_split(ol2_ref, 0, o2_s, rb)),
            (_merge_split(ol0_ref, 1, None, rb), _merge_split(ol1_ref, 1, l1_s, rb),
             _merge_split(ol2_ref, 1, l2_s, rb))).astype(BF16)
        h2 = _mix_out(h_ref[rb, :], oa_ref[rb, :], ob_s[rb, :], gm_ref, wgate_ref, wba_ref, wbb_ref, wout_ref)
        q = _dot(_rms(h2, gq_ref[...]).astype(BF16), wq_ref[...])
        heads = [None] * MEM_HEADS

        def head(hd):
            cols = slice(hd * MEM_HEAD_DIM, (hd + 1) * MEM_HEAD_DIM)
            s = _dot_nt(q[:, cols].astype(BF16), mk_ref[0, :, cols]) * MEM_SCALE
            yield
            e = jnp.exp(s - jnp.max(s, axis=-1, keepdims=True))
            inv = 1.0 / jnp.sum(e, axis=-1, keepdims=True)
            yield
            heads[hd] = _dot(e.astype(BF16), mv_ref[0, :, cols]) * inv
            yield

        _issue_skewed([head(hd) for hd in range(MEM_HEADS)])
        oc = jnp.concatenate(heads, axis=1).astype(BF16)
        h3s.append(h2 + _dot(oc, wo_ref[...]))
    for rb, y in zip(blocks, _ffn_half(h3s, g2_ref, wg_ref, wu_ref, wd_ref)):
        y_ref[rb, :] = _rms(y, gf_ref[...])


def _back_prompt_call(h, oa, ols, mk, mv, w, seq):
    t = h.shape[0]
    tm = TOKEN_TILE
    tiles_per_batch = seq // tm
    row = lambda i: (i, 0)
    mem = lambda i: (i // tiles_per_batch, 0, 0)
    tok = lambda width: pl.BlockSpec((tm, width), row)
    split = lambda d: pl.BlockSpec((1, d, tm // d, 2 * KV_B_W),
                                   lambda i: (i // tiles_per_batch, 0, i % tiles_per_batch, 0))
    in_specs = ([tok(D_MODEL), tok(Q_A_W)] + [split(d) for d in DILATIONS]
                + [pl.BlockSpec((1, N_MEM, MEM_W), mem), pl.BlockSpec((1, N_MEM, MEM_W), mem)]
                + [_const_spec((1, D_MODEL)), _const_spec((D_MODEL, 2 * D_MODEL), (0, 1)), _const_spec((Q_A_W, D_MODEL)),
                   _const_spec((KV_B_W, D_MODEL)), _const_spec((D_MODEL, D_MODEL)),
                   _const_spec((1, D_MODEL)), _const_spec((D_MODEL, MEM_W)), _const_spec((MEM_W, D_MODEL)),
                   _const_spec((1, D_MODEL)), _const_spec((D_MODEL, D_FF)), _const_spec((D_MODEL, D_FF)),
                   _const_spec((D_FF, D_MODEL)), _const_spec((1, D_MODEL))])
    return pl.pallas_call(
        _back_prompt_kernel,
        grid=(t // tm,),
        in_specs=in_specs,
        out_specs=tok(D_MODEL),
        out_shape=jax.ShapeDtypeStruct((t, D_MODEL), F32),
        scratch_shapes=[pltpu.VMEM((KV_B_W // LANES, tm, LANES), F32)] * 4 + [pltpu.VMEM((tm, KV_B_W), BF16)],
        compiler_params=pltpu.CompilerParams(dimension_semantics=("parallel",), vmem_limit_bytes=VMEM_LIMIT),
        name="back_prompt",
    )(h, oa, *ols, mk, mv, w["mix_norm"], w["w_in"], w["w_branch_a"], w["w_branch_b"], w["w_out"],
      w["mem_q_norm"], w["w_mem_q"], w["w_mem_o"], w["ffn2_norm"], w["ffn2_wg"], w["ffn2_wu"], w["ffn2_wd"],
      w["final_norm"])


def _back_sample_a_kernel(h_ref, oa_ref, ob_ref, gm_ref, wgate_ref, wba_ref, wbb_ref, wout_ref, gq_ref, wq_ref,
                          h2_ref, q_ref):
    h2 = _mix_out(h_ref[...], oa_ref[...], ob_ref[...], gm_ref, wgate_ref, wba_ref, wbb_ref, wout_ref)
    h2_ref[...] = h2
    q_ref[...] = _dot(_rms(h2, gq_ref[...]).astype(BF16), wq_ref[...])


def _back_sample_a_call(h, oa, ob, w):
    t = h.shape[0]
    tm = TOKEN_TILE
    tok = lambda width: pl.BlockSpec((tm, width), lambda i: (i, 0))
    return pl.pallas_call(
        _back_sample_a_kernel,
        grid=(t // tm,),
        in_specs=[tok(D_MODEL), tok(Q_A_W), tok(KV_B_W),
                  _const_spec((1, D_MODEL)), _const_spec((D_MODEL, 2 * D_MODEL), (0, 1)), _const_spec((Q_A_W, D_MODEL)),
                  _const_spec((KV_B_W, D_MODEL)), _const_spec((D_MODEL, D_MODEL)),
                  _const_spec((1, D_MODEL)), _const_spec((D_MODEL, MEM_W))],
        out_specs=[tok(D_MODEL), tok(MEM_W)],
        out_shape=[jax.ShapeDtypeStruct((t, D_MODEL), F32), jax.ShapeDtypeStruct((t, MEM_W), F32)],
        compiler_params=pltpu.CompilerParams(dimension_semantics=("parallel",), vmem_limit_bytes=VMEM_LIMIT),
        name="back_sample_mix",
    )(h, oa, ob, w["mix_norm"], w["w_in"], w["w_branch_a"], w["w_branch_b"], w["w_out"],
      w["mem_q_norm"], w["w_mem_q"])


CROSS_BATCH_TILE = 8


def _cross_sample_kernel(q_ref, mk_ref, mv_ref, o_ref):
    nrow = MEM_HEADS * SAMPLE_T
    qhead = lax.broadcasted_iota(jnp.int32, (nrow, N_MEM * MEM_HEADS), 0) >> (SAMPLE_T.bit_length() - 1)
    khead = lax.broadcasted_iota(jnp.int32, (nrow, N_MEM * MEM_HEADS), 1) & (MEM_HEADS - 1)
    own_head = qhead == khead
    def element(bi):
        rows = slice(bi * SAMPLE_T, (bi + 1) * SAMPLE_T)
        qs = jnp.concatenate(
            [q_ref[rows, hd * MEM_HEAD_DIM:(hd + 1) * MEM_HEAD_DIM] for hd in range(MEM_HEADS)], axis=0)
        s = _dot_nt(qs, mk_ref[bi]) * MEM_SCALE
        yield
        p = _softmax_rows(jnp.where(own_head, s, NEG_INF))
        yield
        o = _dot(p, mv_ref[bi])
        for hd in range(MEM_HEADS):
            o_ref[rows, hd * MEM_HEAD_DIM:(hd + 1) * MEM_HEAD_DIM] = o[hd * SAMPLE_T:(hd + 1) * SAMPLE_T]
        yield

    _issue_skewed([element(bi) for bi in range(CROSS_BATCH_TILE)])


def _cross_sample_call(q, mk, mv):
    t = q.shape[0]
    bt = CROSS_BATCH_TILE
    tok = pl.BlockSpec((bt * SAMPLE_T, MEM_W), lambda i: (i, 0))
    mem = pl.BlockSpec((bt, N_MEM * MEM_HEADS, MEM_HEAD_DIM), lambda i: (i, 0, 0))
    return pl.pallas_call(
        _cross_sample_kernel,
        grid=(mk.shape[0] // bt,),
        in_specs=[tok, mem, mem],
        out_specs=tok,
        out_shape=jax.ShapeDtypeStruct((t, MEM_W), F32),
        compiler_params=pltpu.CompilerParams(dimension_semantics=("parallel",), vmem_limit_bytes=VMEM_LIMIT),
        name="cross_sample",
    )(q, mk, mv)


def _back_sample_b_kernel(h2_ref, oc_ref, wo_ref, g2_ref, wg_ref, wu_ref, wd_ref, gf_ref, y_ref):
    blocks = _row_blocks(h2_ref, BACK_ROW_PARTS)
    h3s = [h2_ref[rb, :] + _dot(oc_ref[rb, :].astype(BF16), wo_ref[...]) for rb in blocks]
    for rb, y in zip(blocks, _ffn_half(h3s, g2_ref, wg_ref, wu_ref, wd_ref)):
        y_ref[rb, :] = _rms(y, gf_ref[...])


def _back_sample_b_call(h2, oc, w):
    t = h2.shape[0]
    tm = TOKEN_TILE
    tok = lambda width: pl.BlockSpec((tm, width), lambda i: (i, 0))
    return pl.pallas_call(
        _back_sample_b_kernel,
        grid=(t // tm,),
        in_specs=[tok(D_MODEL), tok(MEM_W), _const_spec((MEM_W, D_MODEL)),
                  _const_spec((1, D_MODEL)), _const_spec((D_MODEL, D_FF)), _const_spec((D_MODEL, D_FF)),
                  _const_spec((D_FF, D_MODEL)), _const_spec((1, D_MODEL))],
        out_specs=tok(D_MODEL),
        out_shape=jax.ShapeDtypeStruct((t, D_MODEL), F32),
        compiler_params=pltpu.CompilerParams(dimension_semantics=("parallel",), vmem_limit_bytes=VMEM_LIMIT),
        name="back_sample_ffn",
    )(h2, oc, w["w_mem_o"], w["ffn2_norm"], w["ffn2_wg"], w["ffn2_wu"], w["ffn2_wd"], w["final_norm"])


def _mem_kv_kernel(mem_ref, g_ref, wk_ref, wv_ref, k32_ref, v32_ref, k16_ref, v16_ref):
    u = _rms(mem_ref[...], g_ref[...]).astype(BF16)
    k = _dot(u, wk_ref[...])
    v = _dot(u, wv_ref[...])
    rows = mem_ref.shape[0]
    for hd in range(MEM_HEADS):
        cols = slice(hd * MEM_HEAD_DIM, (hd + 1) * MEM_HEAD_DIM)
        k32_ref[pl.ds(hd, rows, stride=MEM_HEADS), :] = k[:, cols]
        v32_ref[pl.ds(hd, rows, stride=MEM_HEADS), :] = v[:, cols]
    k16_ref[...] = k.astype(BF16)
    v16_ref[...] = v.astype(BF16)


def _mem_kv_call(mem, w):
    rows = mem.shape[0]
    full = lambda width: pl.BlockSpec((rows, width), lambda i: (0, 0))
    return pl.pallas_call(
        _mem_kv_kernel,
        grid=(1,),
        in_specs=[full(D_MODEL), _const_spec((1, D_MODEL)), _const_spec((D_MODEL, MEM_W)),
                  _const_spec((D_MODEL, MEM_W))],
        out_specs=[pl.BlockSpec((rows * MEM_HEADS, MEM_HEAD_DIM), lambda i: (0, 0))] * 2 + [full(MEM_W)] * 2,
        out_shape=([jax.ShapeDtypeStruct((rows * MEM_HEADS, MEM_HEAD_DIM), F32)] * 2
                   + [jax.ShapeDtypeStruct((rows, MEM_W), BF16)] * 2),
        compiler_params=pltpu.CompilerParams(dimension_semantics=("arbitrary",), vmem_limit_bytes=VMEM_LIMIT),
        name="mem_kv",
    )(mem, w["mem_kv_norm"], w["w_mem_k"], w["w_mem_v"])


def _rope_tables(pos):
    inv_freq = jnp.power(jnp.float32(ROPE_THETA), -jnp.arange(ROT_HALF, dtype=jnp.float32) / ROT_HALF)
    ang = pos.astype(jnp.float32)[:, None] * inv_freq[None, :]
    lane = np.arange(LANES)
    within = lane % HEAD_DIM
    first = within < ROT_HALF
    second = (within >= ROT_HALF) & (within < 2 * ROT_HALF)
    pick_cos = np.zeros((ROT_HALF, LANES), np.float32)
    pick_sin = np.zeros((ROT_HALF, LANES), np.float32)
    pick_cos[within[first | second] % ROT_HALF, lane[first | second]] = 1.0
    pick_sin[within[first] % ROT_HALF, lane[first]] = -1.0
    pick_sin[within[second] % ROT_HALF, lane[second]] = 1.0
    spread = functools.partial(jnp.dot, precision=lax.Precision.HIGHEST)
    return (spread(jnp.cos(ang), pick_cos) + np.where(first | second, 0.0, 1.0).astype(np.float32)[None, :],
            spread(jnp.sin(ang), pick_sin))


def _prep_weights(p):
    bf = lambda x: x.astype(BF16)
    vec = lambda x: x.reshape(1, -1)
    return {
        "ffn1_norm": vec(p["ffn1_norm"]), "ffn1_wg": bf(p["ffn1_w_gate"]), "ffn1_wu": bf(p["ffn1_w_up"]),
        "ffn1_wd": bf(p["ffn1_w_down"]),
        "mix_norm": vec(p["mix_norm"]), "w_in": bf(p["w_in"]),
        "w_branch_a": bf(p["w_branch_a"].reshape(N_HEADS_A, HEAD_DIM, D_MODEL)[np.array(HEAD_PERM_A)].reshape(
            Q_A_W, D_MODEL)),
        "w_branch_b": bf(p["w_branch_b"]), "w_out": bf(p["w_out"]),
        "mem_q_norm": vec(p["mem_q_norm"]), "mem_kv_norm": vec(p["mem_kv_norm"]),
        "w_mem_q": bf(p["w_mem_q"]), "w_mem_k": bf(p["w_mem_k"]), "w_mem_v": bf(p["w_mem_v"]),
        "w_mem_o": bf(p["w_mem_o"]),
        "ffn2_norm": vec(p["ffn2_norm"]), "ffn2_wg": bf(p["ffn2_w_gate"]), "ffn2_wu": bf(p["ffn2_w_up"]),
        "ffn2_wd": bf(p["ffn2_w_down"]), "final_norm": vec(p["final_norm"]),
    }


def kernel(x_prompt, x_sample, cache_swa_k, cache_swa_v, cache_dil_k, cache_dil_v, cache_mem_k, cache_mem_v, mem_prompt, ffn1_norm, ffn1_w_gate, ffn1_w_up, ffn1_w_down, mix_norm, w_in, attn_sink, w_branch_a, w_branch_b, w_out, mem_q_norm, mem_kv_norm, w_mem_q, w_mem_k, w_mem_v, w_mem_o, ffn2_norm, ffn2_w_gate, ffn2_w_up, ffn2_w_down, final_norm):
    depth = ffn1_norm.shape[0]
    assert depth == 1
    batch, seq, d = x_prompt.shape
    nb, t_new, _ = x_sample.shape
    assert d == D_MODEL and t_new == SAMPLE_T and seq % (TOKEN_TILE * DILATIONS[-1]) == 0 and seq % ATTN_CHUNK == 0
    assert cache_swa_k.shape[2] == CACHE_A and cache_dil_k.shape[2] == CACHE_B
    layer = lambda x: x[0]
    w = _prep_weights(dict(
        ffn1_norm=layer(ffn1_norm), ffn1_w_gate=layer(ffn1_w_gate), ffn1_w_up=layer(ffn1_w_up),
        ffn1_w_down=layer(ffn1_w_down), mix_norm=layer(mix_norm), w_in=layer(w_in),
        w_branch_a=layer(w_branch_a), w_branch_b=layer(w_branch_b), w_out=layer(w_out),
        mem_q_norm=layer(mem_q_norm), mem_kv_norm=layer(mem_kv_norm), w_mem_q=layer(w_mem_q),
        w_mem_k=layer(w_mem_k), w_mem_v=layer(w_mem_v), w_mem_o=layer(w_mem_o), ffn2_norm=layer(ffn2_norm),
        ffn2_w_gate=layer(ffn2_w_gate), ffn2_w_up=layer(ffn2_w_up), ffn2_w_down=layer(ffn2_w_down),
        final_norm=final_norm))
    sink = layer(attn_sink)

    ts = nb * t_new
    pos_s = PAST_LEN + (jnp.arange(TOKEN_TILE, dtype=jnp.int32) % t_new)
    tables_s = _rope_tables(pos_s)
    (h_s, q32, ka32_s, va32_s, kb32_s, vb32_s) = _front_call(x_sample.reshape(ts, d), tables_s, w, sample=True)
    by_pos = lambda c: jnp.transpose(layer(c), (0, 2, 3, 1)).reshape(nb, c.shape[3] * c.shape[4], c.shape[2])
    by_slot_head = lambda c: layer(c).reshape(nb, N_MEM * MEM_HEADS, MEM_HEAD_DIM)

    oa_s, ob_s = _attn_sample_call(
        sink, q32, ka32_s, va32_s, kb32_s, vb32_s,
        by_pos(cache_swa_k), by_pos(cache_swa_v), by_pos(cache_dil_k), by_pos(cache_dil_v))

    tp = batch * seq
    mk32, mv32, mk16, mv16 = _mem_kv_call(mem_prompt.reshape(batch * N_MEM, d), w)
    tables_p = _rope_tables(jnp.arange(seq, dtype=jnp.int32))
    h_p, main, split4, split16, tails = _front_call(x_prompt.reshape(tp, d), tables_p, w, sample=False, batch=batch)
    oa = _attn_a_call(sink, main, batch, seq)
    groups = ((main.reshape(batch, 1, seq, MAIN_W), MAIN_QB_COL), (split4, 0), (split16, 0))
    ols = [_attn_b_call(qkv, qcol, dil) for (qkv, qcol), dil in zip(groups, DILATIONS)]
    y_p = _back_prompt_call(h_p, oa, ols, mk16.reshape(batch, N_MEM, MEM_W),
                            mv16.reshape(batch, N_MEM, MEM_W), w, seq)

    h2_s, qc_s = _back_sample_a_call(h_s, oa_s, ob_s, w)
    oc_s = _cross_sample_call(qc_s, by_slot_head(cache_mem_k), by_slot_head(cache_mem_v))
    y_s = _back_sample_b_call(h2_s, oc_s, w)

    keep_a, keep_b = min(CACHE_A, seq), min(CACHE_B, seq)
    assert max(keep_a, keep_b) <= TAIL_ROWS <= seq and TAIL_ROWS % TOKEN_TILE == 0

    def tail(col, keep, heads):
        part = tails[:, col:col + heads * HEAD_DIM, TAIL_ROWS - keep:].reshape(batch, heads, HEAD_DIM, keep)
        return jnp.transpose(part, (0, 3, 1, 2))[None]

    new = lambda x, heads: x.reshape(1, nb, t_new, heads, HEAD_DIM)
    memo = lambda x: x.reshape(1, batch, N_MEM, MEM_HEADS, MEM_HEAD_DIM)
    return (y_p.reshape(batch, seq, d), y_s.reshape(nb, t_new, d),
            tail(0, keep_a, N_KV_A), tail(KV_A_W, keep_a, N_KV_A),
            tail(2 * KV_A_W, keep_b, N_KV_B), tail(2 * KV_A_W + KV_B_W, keep_b, N_KV_B),
            memo(mk32), memo(mv32),
            new(ka32_s, N_KV_A), new(va32_s, N_KV_A), new(kb32_s, N_KV_B), new(vb32_s, N_KV_B))
```

```python
import functools

import jax
import jax.numpy as jnp
import numpy as np
from jax import lax
from jax.experimental import pallas as pl
from jax.experimental.pallas import tpu as pltpu

F32 = jnp.float32
BF16 = jnp.bfloat16

D_MODEL = 1024
D_FF = 2816
HEAD_DIM = 64
ROT_HALF = 8
ROPE_THETA = 500000.0
ATTN_SCALE = HEAD_DIM ** -0.5
RMS_EPS = 1e-6
PAST_LEN = 16384

N_HEADS_A = 8
N_KV_A = 2
N_KV_B = 4
DILATIONS = (1, 4, 16)
BAND_STEPS = 128
Q_A_W = N_HEADS_A * HEAD_DIM
KV_A_W = N_KV_A * HEAD_DIM
Q_B_W = len(DILATIONS) * N_KV_B * HEAD_DIM
KV_B_W = N_KV_B * HEAD_DIM
QKV_W = Q_A_W + 2 * KV_A_W + Q_B_W + 2 * KV_B_W
MEM_HEADS = 4
MEM_HEAD_DIM = 128
MEM_W = MEM_HEADS * MEM_HEAD_DIM
MEM_SCALE = MEM_HEAD_DIM ** -0.5
N_MEM = 256

HEAD_PERM_A = (0, 4, 1, 5, 2, 6, 3, 7)

MAIN_W = Q_A_W + 2 * KV_A_W + 3 * KV_B_W
MAIN_KA_COL, MAIN_VA_COL = Q_A_W // KV_A_W, Q_A_W // KV_A_W + 1
MAIN_QB_COL = (Q_A_W + 2 * KV_A_W) // KV_B_W
SPLIT_W = 3 * KV_B_W
TAIL_ROWS = 2048
assert 2 * KV_A_W == KV_B_W

LANES = 128
BLOCK = 128
TOKEN_TILE = 512
FRONT_ROW_PARTS = 2
BACK_ROW_PARTS = 1
ATTN_CHUNK = 1024
FF_CHUNKS = ((0, 1536), (1536, 2816))
VMEM_LIMIT = 60 * 1024 * 1024
NEG_INF = float("-inf")


def _rms(x, g):
    ms = jnp.mean(x * x, axis=-1, keepdims=True)
    return x * lax.rsqrt(ms + RMS_EPS) * g


def _sigmoid(x):
    return 0.5 * jnp.tanh(0.5 * x) + 0.5


def _dot(a, b):
    return jnp.dot(a, b, preferred_element_type=F32)


def _dot_nt(a, b):
    return lax.dot_general(a, b, (((1,), (1,)), ((), ())), preferred_element_type=F32)


def _ffn_half(xs, g_ref, wg_ref, wu_ref, wd_ref):
    us = [_rms(x, g_ref[...]).astype(BF16) for x in xs]
    accs = [None] * len(xs)
    for lo, hi in FF_CHUNKS:
        gates = [_dot(u, wg_ref[:, lo:hi]) for u in us]
        ups = [_dot(u, wu_ref[:, lo:hi]) for u in us]
        acts = [(gate * _sigmoid(gate) * up).astype(BF16) for gate, up in zip(gates, ups)]
        for i, act in enumerate(acts):
            part = _dot(act, wd_ref[lo:hi, :])
            accs[i] = part if accs[i] is None else accs[i] + part
    return [x + 0.5 * acc for x, acc in zip(xs, accs)]


def _row_blocks(ref, parts):
    n = ref.shape[0] // parts
    return [slice(i * n, (i + 1) * n) for i in range(parts)]


def _swap_halves(x):
    return pltpu.roll(x, HEAD_DIM, 1)


def _low_half():
    return lax.broadcasted_iota(jnp.int32, (1, LANES), 1) < HEAD_DIM


def _pair_kv_groups(chunks):
    low = _low_half()
    c0, c1, c2, c3 = chunks
    return [jnp.where(low, c0, _swap_halves(c2)), jnp.where(low, _swap_halves(c0), c2),
            jnp.where(low, c1, _swap_halves(c3)), jnp.where(low, _swap_halves(c1), c3)]


def _merge_pairs(per_head):
    low = _low_half()
    return [jnp.where(low, per_head[2 * c], per_head[2 * c + 1]) for c in range(N_HEADS_A // 2)]


def _front_kernel(x_ref, cos_ref, sin_ref, g1_ref, wg_ref, wu_ref, wd_ref, gm_ref, wqkv_ref,
                  h_ref, *rest, sample, tiles_per_seq=None, tail_tiles=None):
    blocks = _row_blocks(x_ref, FRONT_ROW_PARTS)
    hs = _ffn_half([x_ref[rb, :] for rb in blocks], g1_ref, wg_ref, wu_ref, wd_ref)
    for rb, h in zip(blocks, hs):
        h_ref[rb, :] = h
    ns = [_rms(h, gm_ref[...]).astype(BF16) for h in hs]
    zs = [_dot(n, wqkv_ref[...]) for n in ns]
    for rb, z in zip(blocks, zs):
        _front_emit(rb, z, cos_ref, sin_ref, rest, sample)
    if not sample:
        tail_refs, tail_scratch = rest[3:6], (rest[10], rest[8], rest[9])

        @pl.when(lax.rem(pl.program_id(0), tiles_per_seq) >= tiles_per_seq - tail_tiles)
        def _():
            for ref, scr in zip(tail_refs, tail_scratch):
                for c in range(scr.shape[0]):
                    ref[0, c * LANES:(c + 1) * LANES, :] = scr[c].T


def _front_emit(rb, z, cos_ref, sin_ref, rest, sample):
    cos, sin = cos_ref[rb, :], sin_ref[rb, :]
    lane_in_head = lax.broadcasted_iota(jnp.int32, (1, LANES), 1) & (HEAD_DIM - 1)
    first = lane_in_head < ROT_HALF

    def chunk(c):
        return z[:, c * LANES:(c + 1) * LANES]

    def rope(c):
        zc = chunk(c)
        partner = jnp.where(first, pltpu.roll(zc, LANES - ROT_HALF, 1), pltpu.roll(zc, ROT_HALF, 1))
        return zc * cos + partner * sin

    nqa, nka = Q_A_W // LANES, KV_A_W // LANES
    nqb, nkb = Q_B_W // LANES, KV_B_W // LANES
    c0 = 0
    qa = _pair_kv_groups([rope(c0 + c) * ATTN_SCALE for c in range(nqa)])
    c0 += nqa
    ka = [rope(c0 + c) for c in range(nka)]
    c0 += nka
    va = [chunk(c0 + c) for c in range(nka)]
    c0 += nka
    qb = [rope(c0 + c) * ATTN_SCALE for c in range(nqb)]
    c0 += nqb
    kb = [rope(c0 + c) for c in range(nkb)]
    c0 += nkb
    vb = [chunk(c0 + c) for c in range(nkb)]

    def store(ref, parts, dtype):
        for c, p in enumerate(parts):
            ref[rb, c * LANES:(c + 1) * LANES] = p.astype(dtype)

    def fill(scr, parts):
        for c, p in enumerate(parts):
            scr[c, rb, :] = p

    def store_split(ref, slot, scr, d):
        n = (rb.stop - rb.start) // d
        dst = slice(rb.start // d, rb.start // d + n)
        for r in range(d):
            for c in range(scr.shape[0]):
                lanes = slice(slot * KV_B_W + c * LANES, slot * KV_B_W + (c + 1) * LANES)
                ref[0, r, dst, lanes] = scr[c, pl.ds(rb.start + r, n, stride=d), :].astype(BF16)

    if sample:
        q32_ref, ka32_ref, va32_ref, kb32_ref, vb32_ref = rest
        store(q32_ref, qa + qb, F32)
        store(ka32_ref, ka, F32)
        store(va32_ref, va, F32)
        store(kb32_ref, kb, F32)
        store(vb32_ref, vb, F32)
    else:
        main_ref, s4_ref, s16_ref, _, _, _, q4_s, q16_s, kb_s, vb_s, kva_s = rest
        gw = KV_B_W // LANES
        store(main_ref, qa + ka + va + qb[0:gw] + kb + vb, BF16)
        fill(q4_s, qb[gw:2 * gw])
        fill(q16_s, qb[2 * gw:3 * gw])
        fill(kb_s, kb)
        fill(vb_s, vb)
        fill(kva_s, ka + va)
        for slot, scr4, scr16 in ((0, q4_s, q16_s), (1, kb_s, kb_s), (2, vb_s, vb_s)):
            store_split(s4_ref, slot, scr4, DILATIONS[1])
            store_split(s16_ref, slot, scr16, DILATIONS[2])


def _const_spec(shape, index=None):
    index = (0,) * len(shape) if index is None else index
    return pl.BlockSpec(shape, lambda *_: index, pipeline_mode=pl.Buffered(1))


def _front_call(x, tables, w, *, sample, batch=1):
    t = x.shape[0]
    tm = TOKEN_TILE
    nsteps = t // tm
    tpb = nsteps // batch
    seq = t // batch
    cos_t, sin_t = tables
    tab_blocks = cos_t.shape[0] // tm
    row = lambda i: (i, 0)
    tab = lambda i: (i % tab_blocks, 0)

    def tok(width, dtype):
        return jax.ShapeDtypeStruct((t, width), dtype), pl.BlockSpec((tm, width), row)

    def split(d):
        return (jax.ShapeDtypeStruct((batch, d, seq // d, SPLIT_W), BF16),
                pl.BlockSpec((1, d, tm // d, SPLIT_W), lambda i: (i // tpb, 0, i % tpb, 0)))

    outs = [tok(D_MODEL, F32)]
    scratch = []
    static = {}
    if sample:
        outs += [tok(Q_A_W + Q_B_W, F32), tok(KV_A_W, F32), tok(KV_A_W, F32), tok(KV_B_W, F32), tok(KV_B_W, F32)]
    else:
        tail_tiles = TAIL_ROWS // tm
        tails = (jax.ShapeDtypeStruct((batch, KV_B_W, TAIL_ROWS), F32),
                 pl.BlockSpec((1, KV_B_W, tm),
                              lambda i: (i // tpb, 0, jnp.maximum(i % tpb - (tpb - tail_tiles), 0))))
        outs += [tok(MAIN_W, BF16), split(DILATIONS[1]), split(DILATIONS[2])] + [tails] * 3
        scratch = [pltpu.VMEM((KV_B_W // LANES, tm, LANES), F32)] * 4 + [pltpu.VMEM((2, tm, LANES), F32)]
        static = dict(tiles_per_seq=tpb, tail_tiles=tail_tiles)
    out_shape, out_specs = zip(*outs)
    in_specs = [pl.BlockSpec((tm, D_MODEL), row),
                pl.BlockSpec((tm, LANES), tab), pl.BlockSpec((tm, LANES), tab),
                _const_spec((1, D_MODEL)), _const_spec((D_MODEL, D_FF)), _const_spec((D_MODEL, D_FF)),
                _const_spec((D_FF, D_MODEL)), _const_spec((1, D_MODEL)), _const_spec((D_MODEL, QKV_W), (0, 0))]
    return pl.pallas_call(
        functools.partial(_front_kernel, sample=sample, **static),
        grid=(nsteps,),
        in_specs=in_specs,
        out_specs=list(out_specs),
        out_shape=list(out_shape),
        scratch_shapes=scratch,
        compiler_params=pltpu.CompilerParams(dimension_semantics=("parallel" if sample else "arbitrary",),
                                             vmem_limit_bytes=VMEM_LIMIT),
        name="front_sample" if sample else "front_prompt",
    )(x, cos_t, sin_t, w["ffn1_norm"], w["ffn1_wg"], w["ffn1_wu"], w["ffn1_wd"], w["mix_norm"], w["w_in"])


def _lane_block_masks(width, block, dtype):
    lane = lax.broadcasted_iota(jnp.int32, (1, width), 1)
    return [((lane >= i * block) & (lane < (i + 1) * block)).astype(dtype) for i in range(width // block)]


def _band_mask():
    qi = lax.broadcasted_iota(jnp.int32, (BLOCK, 2 * BLOCK), 0)
    kj = lax.broadcasted_iota(jnp.int32, (BLOCK, 2 * BLOCK), 1)
    dist = qi + BLOCK - kj
    return (dist >= 0) & (dist <= BAND_STEPS), kj >= BLOCK


def _band_block(load_qbd, load_k, load_v, mask, nheads, finish):
    s = _dot_nt(load_qbd(), load_k())
    yield
    ps, ms, ls = [], [], []
    for p in range(nheads):
        sp = jnp.where(mask, s[p * BLOCK:(p + 1) * BLOCK], NEG_INF)
        m = jnp.max(sp, axis=-1, keepdims=True)
        e = jnp.exp(sp - m)
        ls.append(jnp.sum(e, axis=-1, keepdims=True))
        ms.append(m)
        ps.append(e.astype(BF16))
    pc = jnp.concatenate(ps, axis=0)
    yield
    finish(_dot(pc, load_v()), ms, ls)
    yield


def _issue_skewed(items, nstages=3):
    for tick in range(len(items) + nstages - 1):
        for s in reversed(range(nstages)):
            i = tick - s
            if 0 <= i < len(items):
                next(items[i])


def _kv_window(cur_ref, prev_ref, qb):
    if qb == 0:
        return jnp.concatenate([prev_ref[...], cur_ref[0:BLOCK]], axis=0)
    return cur_ref[(qb - 1) * BLOCK:(qb + 1) * BLOCK]


def _attn_a_kernel(sink_ref, q_ref, kc_ref, kp_ref, vc_ref, vp_ref, o_ref):
    first_chunk = pl.program_id(1) == 0
    band, in_cur = _band_mask()
    half_bf = _lane_block_masks(LANES, HEAD_DIM, BF16)

    def block(qb):
        mask = band & (in_cur | jnp.logical_not(first_chunk)) if qb == 0 else band
        rows = slice(qb * BLOCK, (qb + 1) * BLOCK)

        def load_qbd():
            return jnp.concatenate(
                [q_ref[rows, (p // 2) * LANES:(p // 2 + 1) * LANES] * half_bf[p % 2] for p in range(N_HEADS_A)],
                axis=0)

        def finish(o, ms, ls):
            normed = []
            for p in range(N_HEADS_A):
                den = ls[p] + jnp.exp(sink_ref[HEAD_PERM_A[p]] - ms[p])
                normed.append(o[p * BLOCK:(p + 1) * BLOCK] * (1.0 / den))
            for c, pair in enumerate(_merge_pairs(normed)):
                o_ref[rows, c * LANES:(c + 1) * LANES] = pair.astype(BF16)

        return _band_block(load_qbd, lambda: _kv_window(kc_ref, kp_ref, qb), lambda: _kv_window(vc_ref, vp_ref, qb),
                           mask, N_HEADS_A, finish)

    for qb in range(q_ref.shape[0] // BLOCK):
        for _ in block(qb):
            pass


def _attn_a_call(sink, main, batch, seq):
    t = main.shape[0]
    cpb = seq // ATTN_CHUNK
    bpc = ATTN_CHUNK // BLOCK
    cur = lambda col: (lambda b, c: (b * cpb + c, col))
    prev = lambda col: (lambda b, c: (b * cpb * bpc + jnp.maximum(c * bpc - 1, 0), col))
    kv = lambda col: [pl.BlockSpec((ATTN_CHUNK, KV_A_W), cur(col)), pl.BlockSpec((BLOCK, KV_A_W), prev(col))]
    return pl.pallas_call(
        _attn_a_kernel,
        grid=(batch, cpb),
        in_specs=[pl.BlockSpec(memory_space=pltpu.SMEM), pl.BlockSpec((ATTN_CHUNK, Q_A_W), cur(0))]
        + kv(MAIN_KA_COL) + kv(MAIN_VA_COL),
        out_specs=pl.BlockSpec((ATTN_CHUNK, Q_A_W), cur(0)),
        out_shape=jax.ShapeDtypeStruct((t, Q_A_W), BF16),
        compiler_params=pltpu.CompilerParams(dimension_semantics=("parallel", "parallel"),
                                             vmem_limit_bytes=VMEM_LIMIT),
        name="attn_a_prompt",
    )(sink, main, main, main, main, main)


def _attn_b_kernel(q_ref, kc_ref, kp_ref, vc_ref, vp_ref, ol_ref):
    first_chunk = pl.program_id(2) == 0
    band, in_cur = _band_mask()
    head_bf = _lane_block_masks(KV_B_W, HEAD_DIM, BF16)
    lane = lax.broadcasted_iota(jnp.int32, (1, KV_B_W), 1)

    def block(r, qb):
        mask = band & (in_cur | jnp.logical_not(first_chunk)) if qb == 0 else band
        rows = slice(qb * BLOCK, (qb + 1) * BLOCK)

        def load_qbd():
            q = q_ref[r, rows, :]
            return jnp.concatenate([q * head_bf[p] for p in range(N_KV_B)], axis=0)

        def finish(o, ms, ls):
            out = lse = None
            for p in reversed(range(N_KV_B)):
                op = o[p * BLOCK:(p + 1) * BLOCK] * (1.0 / ls[p])
                lp = jnp.broadcast_to(ms[p] + jnp.log(ls[p]), (BLOCK, KV_B_W))
                if out is None:
                    out, lse = op, lp
                else:
                    sel = lane < (p + 1) * HEAD_DIM
                    out, lse = jnp.where(sel, op, out), jnp.where(sel, lp, lse)
            ol_ref[r, rows, 0:KV_B_W] = out
            ol_ref[r, rows, KV_B_W:2 * KV_B_W] = lse

        return _band_block(load_qbd, lambda: _kv_window(kc_ref.at[r], kp_ref.at[r], qb),
                           lambda: _kv_window(vc_ref.at[r], vp_ref.at[r], qb), mask, N_KV_B, finish)

    _issue_skewed([block(r, qb) for r in range(q_ref.shape[0]) for qb in range(q_ref.shape[1] // BLOCK)])


def _attn_b_call(qkv, qcol, d):
    batch, _, rows, _ = qkv.shape
    chunk = min(ATTN_CHUNK, rows)
    rps = min(d, ATTN_CHUNK // chunk)
    cpb = rows // chunk
    bpc = chunk // BLOCK
    cur = lambda col: pl.BlockSpec((None, rps, chunk, KV_B_W), lambda b, r, c: (b, r, c, col))
    prev = lambda col: pl.BlockSpec((None, rps, BLOCK, KV_B_W),
                                    lambda b, r, c: (b, r, jnp.maximum(c * bpc - 1, 0), col))
    return pl.pallas_call(
        _attn_b_kernel,
        grid=(batch, d // rps, cpb),
        in_specs=[cur(qcol), cur(qcol + 1), prev(qcol + 1), cur(qcol + 2), prev(qcol + 2)],
        out_specs=pl.BlockSpec((None, rps, chunk, 2 * KV_B_W), lambda b, r, c: (b, r, c, 0)),
        out_shape=jax.ShapeDtypeStruct((batch, d, rows, 2 * KV_B_W), F32),
        compiler_params=pltpu.CompilerParams(dimension_semantics=("parallel", "parallel", "parallel"),
                                             vmem_limit_bytes=VMEM_LIMIT),
        name=f"attn_b_prompt_d{d}",
    )(qkv, qkv, qkv, qkv, qkv)


SAMPLE_T = 8
CACHE_A = 128
CACHE_B = 2048
SAMPLE_ELEMS = 4


def _new_rows_block(new):
    return jnp.concatenate([new, jnp.zeros((BLOCK - SAMPLE_T, new.shape[1]), F32)], axis=0)


def _cached_scores(qbd, kt, knew):
    return _dot(qbd, kt), _dot_nt(qbd, knew)


def _cached_softmax(s_c, s_n, mask_c, mask_n):
    s_c = jnp.where(mask_c, s_c, NEG_INF)
    s_n = jnp.where(mask_n, s_n, NEG_INF)
    m = jnp.maximum(jnp.max(s_c, axis=-1, keepdims=True), jnp.max(s_n, axis=-1, keepdims=True))
    e_c = jnp.exp(s_c - m)
    e_n = jnp.exp(s_n - m)
    l = jnp.sum(e_c, axis=-1, keepdims=True) + jnp.sum(e_n, axis=-1, keepdims=True)
    return e_c, e_n, m, l


def _cached_pv(e_c, e_n, vt, vnew):
    return _dot_nt(e_c, vt) + _dot(e_n, vnew)


def _attn_sample_kernel(sink_ref, q_ref, kan_ref, van_ref, kbn_ref, vbn_ref, cka_ref, cva_ref, ckb_ref, cvb_ref,
                        oa_ref, ob_ref):
    half_f = _lane_block_masks(LANES, HEAD_DIM, F32)
    lane_b = lax.broadcasted_iota(jnp.int32, (1, KV_B_W), 1)

    def dist(nrows, ncols, offset):
        r = lax.broadcasted_iota(jnp.int32, (nrows, ncols), 0)
        i = lax.broadcasted_iota(jnp.int32, (nrows, ncols), 1)
        return r, offset + (r & (SAMPLE_T - 1)) - i

    nrow_a = N_HEADS_A * SAMPLE_T
    _, dac = dist(nrow_a, CACHE_A, CACHE_A)
    _, dan = dist(nrow_a, BLOCK, 0)
    mask_ac = (dac >= 0) & (dac <= BAND_STEPS)
    mask_an = (dan >= 0) & (dan <= BAND_STEPS)
    nrow_b = len(DILATIONS) * N_KV_B * SAMPLE_T
    rows_per_group = N_KV_B * SAMPLE_T

    def mask_b(ncols, offset):
        r, db = dist(nrow_b, ncols, offset)
        dil = jnp.where(r < rows_per_group, DILATIONS[0], jnp.where(r < 2 * rows_per_group, DILATIONS[1], DILATIONS[2]))
        return (db >= 0) & (db <= BAND_STEPS * dil) & ((db & (dil - 1)) == 0)

    mask_bc = mask_b(CACHE_B, CACHE_B)
    mask_bn = mask_b(BLOCK, 0)

    sink_col = jnp.concatenate(
        [jnp.full((SAMPLE_T, 1), sink_ref[HEAD_PERM_A[p]], F32) for p in range(N_HEADS_A)], axis=0)

    def element(j):
        rows = slice(j * SAMPLE_T, (j + 1) * SAMPLE_T)
        qbd_a = jnp.concatenate(
            [q_ref[rows, (p // 2) * LANES:(p // 2 + 1) * LANES] * half_f[p % 2] for p in range(N_HEADS_A)],
            axis=0)
        qrows = []
        for g in range(len(DILATIONS)):
            for kvh in range(N_KV_B):
                c = (Q_A_W // LANES) + 2 * g + kvh // 2
                part = q_ref[rows, c * LANES:(c + 1) * LANES] * half_f[kvh % 2]
                zero = jnp.zeros_like(part)
                qrows.append(jnp.concatenate([part, zero] if kvh // 2 == 0 else [zero, part], axis=1))
        qbd_b = jnp.concatenate(qrows, axis=0)
        scores_a = _cached_scores(qbd_a, cka_ref[j], _new_rows_block(kan_ref[rows, :]))
        scores_b = _cached_scores(qbd_b, ckb_ref[j], _new_rows_block(kbn_ref[rows, :]))
        yield
        ea_c, ea_n, ma, la = _cached_softmax(*scores_a, mask_ac, mask_an)
        eb_c, eb_n, m, l = _cached_softmax(*scores_b, mask_bc, mask_bn)
        yield
        o = _cached_pv(ea_c, ea_n, cva_ref[j], _new_rows_block(van_ref[rows, :]))
        o = o * (1.0 / (la + jnp.exp(sink_col - ma)))
        per_head = [o[p * SAMPLE_T:(p + 1) * SAMPLE_T] for p in range(N_HEADS_A)]
        for c, pair in enumerate(_merge_pairs(per_head)):
            oa_ref[rows, c * LANES:(c + 1) * LANES] = pair
        o = _cached_pv(eb_c, eb_n, cvb_ref[j], _new_rows_block(vbn_ref[rows, :]))
        out = None
        for kvh in reversed(range(N_KV_B)):
            sl = [slice((g * N_KV_B + kvh) * SAMPLE_T, (g * N_KV_B + kvh + 1) * SAMPLE_T)
                  for g in range(len(DILATIONS))]
            mj = jnp.maximum(jnp.maximum(m[sl[0]], m[sl[1]]), m[sl[2]])
            ws = [jnp.exp(m[x] - mj) for x in sl]
            den = ws[0] * l[sl[0]] + ws[1] * l[sl[1]] + ws[2] * l[sl[2]]
            num = ws[0] * o[sl[0]] + ws[1] * o[sl[1]] + ws[2] * o[sl[2]]
            okv = num * (1.0 / den)
            out = okv if out is None else jnp.where(lane_b < (kvh + 1) * HEAD_DIM, okv, out)
        ob_ref[rows, :] = out
        yield

    elements = [element(j) for j in range(SAMPLE_ELEMS)]
    for _ in range(3):
        for e in elements:
            next(e)


def _attn_sample_call(sink, q32, ka32, va32, kb32, vb32, cka, cva, ckb, cvb):
    nb = cka.shape[0]
    bt = SAMPLE_ELEMS
    t = q32.shape[0]
    tok = lambda w: pl.BlockSpec((bt * SAMPLE_T, w), lambda i: (i, 0))
    cache = lambda w, n: pl.BlockSpec((bt, w, n), lambda i: (i, 0, 0))
    return pl.pallas_call(
        _attn_sample_kernel,
        grid=(nb // bt,),
        in_specs=[pl.BlockSpec(memory_space=pltpu.SMEM), tok(Q_A_W + Q_B_W),
                  tok(KV_A_W), tok(KV_A_W), tok(KV_B_W), tok(KV_B_W),
                  cache(KV_A_W, CACHE_A), cache(KV_A_W, CACHE_A), cache(KV_B_W, CACHE_B), cache(KV_B_W, CACHE_B)],
        out_specs=[tok(Q_A_W), tok(KV_B_W)],
        out_shape=[jax.ShapeDtypeStruct((t, Q_A_W), F32), jax.ShapeDtypeStruct((t, KV_B_W), F32)],
        compiler_params=pltpu.CompilerParams(dimension_semantics=("parallel",), vmem_limit_bytes=VMEM_LIMIT),
        name="attn_sample",
    )(sink, q32, ka32, va32, kb32, vb32, cka, cva, ckb, cvb)


def _mix_out(h, oa, ob, gm_ref, wgate_ref, wba_ref, wbb_ref, wout_ref):
    n = _rms(h, gm_ref[...]).astype(BF16)
    gates = _dot(n, wgate_ref[...])
    ga = _sigmoid(gates[:, :D_MODEL])
    gb = _sigmoid(gates[:, D_MODEL:])
    mixed = ga * _dot(oa.astype(BF16), wba_ref[...]) + gb * _dot(ob.astype(BF16), wbb_ref[...])
    return h + _dot(mixed.astype(BF16), wout_ref[...])


def _combine_groups(os, ls):
    m = jnp.maximum(jnp.maximum(ls[0], ls[1]), ls[2])
    es = [jnp.exp(x - m) for x in ls]
    den = es[0] + es[1] + es[2]
    num = es[0] * os[0] + es[1] * os[1] + es[2] * os[2]
    return num * (1.0 / den)


def _merge_split(split_ref, field, scr, rb):
    d = split_ref.shape[1]
    if d == 1:
        return split_ref[0, 0, rb, field * KV_B_W:(field + 1) * KV_B_W]
    n = (rb.stop - rb.start) // d
    src = slice(rb.start // d, rb.start // d + n)
    for r in range(d):
        for c in range(scr.shape[0]):
            lanes = slice(field * KV_B_W + c * LANES, field * KV_B_W + (c + 1) * LANES)
            scr[c, pl.ds(rb.start + r, n, stride=d), :] = split_ref[0, r, src, lanes]
    return jnp.concatenate([scr[c, rb, :] for c in range(scr.shape[0])], axis=1)


def _softmax_rows(s):
    m = jnp.max(s, axis=-1, keepdims=True)
    e = jnp.exp(s - m)
    return e * (1.0 / jnp.sum(e, axis=-1, keepdims=True))


def _back_prompt_kernel(h_ref, oa_ref, ol0_ref, ol1_ref, ol2_ref, mk_ref, mv_ref,
                        gm_ref, wgate_ref, wba_ref, wbb_ref, wout_ref, gq_ref, wq_ref, wo_ref,
                        g2_ref, wg_ref, wu_ref, wd_ref, gf_ref, y_ref, o1_s, o2_s, l1_s, l2_s, ob_s):
    blocks = _row_blocks(h_ref, BACK_ROW_PARTS)
    h3s = []
    for rb in blocks:
        ob_s[rb, :] = _combine_groups(
            (_merge_split(ol0_ref, 0, None, rb), _merge_split(ol1_ref, 0, o1_s, rb),
             _merge_split(ol2_ref, 0, o2_s, rb)),
            (_merge_split(ol0_ref, 1, None, rb), _merge_split(ol1_ref, 1, l1_s, rb),
             _merge_split(ol2_ref, 1, l2_s, rb))).astype(BF16)
        h2 = _mix_out(h_ref[rb, :], oa_ref[rb, :], ob_s[rb, :], gm_ref, wgate_ref, wba_ref, wbb_ref, wout_ref)
        q = _dot(_rms(h2, gq_ref[...]).astype(BF16), wq_ref[...])
        heads = [None] * MEM_HEADS

        def head(hd):
            cols = slice(hd * MEM_HEAD_DIM, (hd + 1) * MEM_HEAD_DIM)
            s = _dot_nt(q[:, cols].astype(BF16), mk_ref[0, :, cols]) * MEM_SCALE
            yield
            e = jnp.exp(s - jnp.max(s, axis=-1, keepdims=True))
            inv = 1.0 / jnp.sum(e, axis=-1, keepdims=True)
            yield
            heads[hd] = _dot(e.astype(BF16), mv_ref[0, :, cols]) * inv
            yield

        _issue_skewed([head(hd) for hd in range(MEM_HEADS)])
        oc = jnp.concatenate(heads, axis=1).astype(BF16)
        h3s.append(h2 + _dot(oc, wo_ref[...]))
    for rb, y in zip(blocks, _ffn_half(h3s, g2_ref, wg_ref, wu_ref, wd_ref)):
        y_ref[rb, :] = _rms(y, gf_ref[...])


def _back_prompt_call(h, oa, ols, mk, mv, w, seq):
    t = h.shape[0]
    tm = TOKEN_TILE
    tiles_per_batch = seq // tm
    row = lambda i: (i, 0)
    mem = lambda i: (i // tiles_per_batch, 0, 0)
    tok = lambda width: pl.BlockSpec((tm, width), row)
    split = lambda d: pl.BlockSpec((1, d, tm // d, 2 * KV_B_W),
                                   lambda i: (i // tiles_per_batch, 0, i % tiles_per_batch, 0))
    in_specs = ([tok(D_MODEL), tok(Q_A_W)] + [split(d) for d in DILATIONS]
                + [pl.BlockSpec((1, N_MEM, MEM_W), mem), pl.BlockSpec((1, N_MEM, MEM_W), mem)]
                + [_const_spec((1, D_MODEL)), _const_spec((D_MODEL, 2 * D_MODEL), (0, 1)), _const_spec((Q_A_W, D_MODEL)),
                   _const_spec((KV_B_W, D_MODEL)), _const_spec((D_MODEL, D_MODEL)),
                   _const_spec((1, D_MODEL)), _const_spec((D_MODEL, MEM_W)), _const_spec((MEM_W, D_MODEL)),
                   _const_spec((1, D_MODEL)), _const_spec((D_MODEL, D_FF)), _const_spec((D_MODEL, D_FF)),
                   _const_spec((D_FF, D_MODEL)), _const_spec((1, D_MODEL))])
    return pl.pallas_call(
        _back_prompt_kernel,
        grid=(t // tm,),
        in_specs=in_specs,
        out_specs=tok(D_MODEL),
        out_shape=jax.ShapeDtypeStruct((t, D_MODEL), F32),
        scratch_shapes=[pltpu.VMEM((KV_B_W // LANES, tm, LANES), F32)] * 4 + [pltpu.VMEM((tm, KV_B_W), BF16)],
        compiler_params=pltpu.CompilerParams(dimension_semantics=("parallel",), vmem_limit_bytes=VMEM_LIMIT),
        name="back_prompt",
    )(h, oa, *ols, mk, mv, w["mix_norm"], w["w_in"], w["w_branch_a"], w["w_branch_b"], w["w_out"],
      w["mem_q_norm"], w["w_mem_q"], w["w_mem_o"], w["ffn2_norm"], w["ffn2_wg"], w["ffn2_wu"], w["ffn2_wd"],
      w["final_norm"])


def _back_sample_a_kernel(h_ref, oa_ref, ob_ref, gm_ref, wgate_ref, wba_ref, wbb_ref, wout_ref, gq_ref, wq_ref,
                          h2_ref, q_ref):
    h2 = _mix_out(h_ref[...], oa_ref[...], ob_ref[...], gm_ref, wgate_ref, wba_ref, wbb_ref, wout_ref)
    h2_ref[...] = h2
    q_ref[...] = _dot(_rms(h2, gq_ref[...]).astype(BF16), wq_ref[...])


def _back_sample_a_call(h, oa, ob, w):
    t = h.shape[0]
    tm = TOKEN_TILE
    tok = lambda width: pl.BlockSpec((tm, width), lambda i: (i, 0))
    return pl.pallas_call(
        _back_sample_a_kernel,
        grid=(t // tm,),
        in_specs=[tok(D_MODEL), tok(Q_A_W), tok(KV_B_W),
                  _const_spec((1, D_MODEL)), _const_spec((D_MODEL, 2 * D_MODEL), (0, 1)), _const_spec((Q_A_W, D_MODEL)),
                  _const_spec((KV_B_W, D_MODEL)), _const_spec((D_MODEL, D_MODEL)),
                  _const_spec((1, D_MODEL)), _const_spec((D_MODEL, MEM_W))],
        out_specs=[tok(D_MODEL), tok(MEM_W)],
        out_shape=[jax.ShapeDtypeStruct((t, D_MODEL), F32), jax.ShapeDtypeStruct((t, MEM_W), F32)],
        compiler_params=pltpu.CompilerParams(dimension_semantics=("parallel",), vmem_limit_bytes=VMEM_LIMIT),
        name="back_sample_mix",
    )(h, oa, ob, w["mix_norm"], w["w_in"], w["w_branch_a"], w["w_branch_b"], w["w_out"],
      w["mem_q_norm"], w["w_mem_q"])


CROSS_BATCH_TILE = 8


def _cross_sample_kernel(q_ref, mk_ref, mv_ref, o_ref):
    nrow = MEM_HEADS * SAMPLE_T
    qhead = lax.broadcasted_iota(jnp.int32, (nrow, N_MEM * MEM_HEADS), 0) >> (SAMPLE_T.bit_length() - 1)
    khead = lax.broadcasted_iota(jnp.int32, (nrow, N_MEM * MEM_HEADS), 1) & (MEM_HEADS - 1)
    own_head = qhead == khead
    def element(bi):
        rows = slice(bi * SAMPLE_T, (bi + 1) * SAMPLE_T)
        qs = jnp.concatenate(
            [q_ref[rows, hd * MEM_HEAD_DIM:(hd + 1) * MEM_HEAD_DIM] for hd in range(MEM_HEADS)], axis=0)
        s = _dot_nt(qs, mk_ref[bi]) * MEM_SCALE
        yield
        p = _softmax_rows(jnp.where(own_head, s, NEG_INF))
        yield
        o = _dot(p, mv_ref[bi])
        for hd in range(MEM_HEADS):
            o_ref[rows, hd * MEM_HEAD_DIM:(hd + 1) * MEM_HEAD_DIM] = o[hd * SAMPLE_T:(hd + 1) * SAMPLE_T]
        yield

    _issue_skewed([element(bi) for bi in range(CROSS_BATCH_TILE)])


def _cross_sample_call(q, mk, mv):
    t = q.shape[0]
    bt = CROSS_BATCH_TILE
    tok = pl.BlockSpec((bt * SAMPLE_T, MEM_W), lambda i: (i, 0))
    mem = pl.BlockSpec((bt, N_MEM * MEM_HEADS, MEM_HEAD_DIM), lambda i: (i, 0, 0))
    return pl.pallas_call(
        _cross_sample_kernel,
        grid=(mk.shape[0] // bt,),
        in_specs=[tok, mem, mem],
        out_specs=tok,
        out_shape=jax.ShapeDtypeStruct((t, MEM_W), F32),
        compiler_params=pltpu.CompilerParams(dimension_semantics=("parallel",), vmem_limit_bytes=VMEM_LIMIT),
        name="cross_sample",
    )(q, mk, mv)


def _back_sample_b_kernel(h2_ref, oc_ref, wo_ref, g2_ref, wg_ref, wu_ref, wd_ref, gf_ref, y_ref):
    blocks = _row_blocks(h2_ref, BACK_ROW_PARTS)
    h3s = [h2_ref[rb, :] + _dot(oc_ref[rb, :].astype(BF16), wo_ref[...]) for rb in blocks]
    for rb, y in zip(blocks, _ffn_half(h3s, g2_ref, wg_ref, wu_ref, wd_ref)):
        y_ref[rb, :] = _rms(y, gf_ref[...])


def _back_sample_b_call(h2, oc, w):
    t = h2.shape[0]
    tm = TOKEN_TILE
    tok = lambda width: pl.BlockSpec((tm, width), lambda i: (i, 0))
    return pl.pallas_call(
        _back_sample_b_kernel,
        grid=(t // tm,),
        in_specs=[tok(D_MODEL), tok(MEM_W), _const_spec((MEM_W, D_MODEL)),
                  _const_spec((1, D_MODEL)), _const_spec((D_MODEL, D_FF)), _const_spec((D_MODEL, D_FF)),
                  _const_spec((D_FF, D_MODEL)), _const_spec((1, D_MODEL))],
        out_specs=tok(D_MODEL),
        out_shape=jax.ShapeDtypeStruct((t, D_MODEL), F32),
        compiler_params=pltpu.CompilerParams(dimension_semantics=("parallel",), vmem_limit_bytes=VMEM_LIMIT),
        name="back_sample_ffn",
    )(h2, oc, w["w_mem_o"], w["ffn2_norm"], w["ffn2_wg"], w["ffn2_wu"], w["ffn2_wd"], w["final_norm"])


def _mem_kv_kernel(mem_ref, g_ref, wk_ref, wv_ref, k32_ref, v32_ref, k16_ref, v16_ref):
    u = _rms(mem_ref[...], g_ref[...]).astype(BF16)
    k = _dot(u, wk_ref[...])
    v = _dot(u, wv_ref[...])
    rows = mem_ref.shape[0]
    for hd in range(MEM_HEADS):
        cols = slice(hd * MEM_HEAD_DIM, (hd + 1) * MEM_HEAD_DIM)
        k32_ref[pl.ds(hd, rows, stride=MEM_HEADS), :] = k[:, cols]
        v32_ref[pl.ds(hd, rows, stride=MEM_HEADS), :] = v[:, cols]
    k16_ref[...] = k.astype(BF16)
    v16_ref[...] = v.astype(BF16)


def _mem_kv_call(mem, w):
    rows = mem.shape[0]
    full = lambda width: pl.BlockSpec((rows, width), lambda i: (0, 0))
    return pl.pallas_call(
        _mem_kv_kernel,
        grid=(1,),
        in_specs=[full(D_MODEL), _const_spec((1, D_MODEL)), _const_spec((D_MODEL, MEM_W)),
                  _const_spec((D_MODEL, MEM_W))],
        out_specs=[pl.BlockSpec((rows * MEM_HEADS, MEM_HEAD_DIM), lambda i: (0, 0))] * 2 + [full(MEM_W)] * 2,
        out_shape=([jax.ShapeDtypeStruct((rows * MEM_HEADS, MEM_HEAD_DIM), F32)] * 2
                   + [jax.ShapeDtypeStruct((rows, MEM_W), BF16)] * 2),
        compiler_params=pltpu.CompilerParams(dimension_semantics=("arbitrary",), vmem_limit_bytes=VMEM_LIMIT),
        name="mem_kv",
    )(mem, w["mem_kv_norm"], w["w_mem_k"], w["w_mem_v"])


def _rope_tables(pos):
    inv_freq = jnp.power(jnp.float32(ROPE_THETA), -jnp.arange(ROT_HALF, dtype=jnp.float32) / ROT_HALF)
    ang = pos.astype(jnp.float32)[:, None] * inv_freq[None, :]
    lane = np.arange(LANES)
    within = lane % HEAD_DIM
    first = within < ROT_HALF
    second = (within >= ROT_HALF) & (within < 2 * ROT_HALF)
    pick_cos = np.zeros((ROT_HALF, LANES), np.float32)
    pick_sin = np.zeros((ROT_HALF, LANES), np.float32)
    pick_cos[within[first | second] % ROT_HALF, lane[first | second]] = 1.0
    pick_sin[within[first] % ROT_HALF, lane[first]] = -1.0
    pick_sin[within[second] % ROT_HALF, lane[second]] = 1.0
    base = np.where(first | second, 0.0, 1.0).astype(np.float32)[None, :]
    n = pos.shape[0]
    narrow = pl.BlockSpec((n, ROT_HALF), lambda i: (0, 0))
    pick = pl.BlockSpec((ROT_HALF, LANES), lambda i: (0, 0))
    wide = pl.BlockSpec((n, LANES), lambda i: (0, 0))
    return pl.pallas_call(
        _spread_tables_kernel,
        grid=(1,),
        in_specs=[narrow, narrow, pick, pick, pl.BlockSpec((1, LANES), lambda i: (0, 0))],
        out_specs=[wide, wide],
        out_shape=[jax.ShapeDtypeStruct((n, LANES), F32)] * 2,
        compiler_params=pltpu.CompilerParams(dimension_semantics=("arbitrary",), vmem_limit_bytes=VMEM_LIMIT),
        name="rope_tables",
    )(jnp.cos(ang), jnp.sin(ang), pick_cos, pick_sin, base)


def _spread_tables_kernel(cos8_ref, sin8_ref, pick_cos_ref, pick_sin_ref, base_ref, cos_ref, sin_ref):
    spread = functools.partial(jnp.dot, precision=lax.Precision.HIGHEST, preferred_element_type=F32)
    cos_ref[...] = spread(cos8_ref[...], pick_cos_ref[...]) + base_ref[...]
    sin_ref[...] = spread(sin8_ref[...], pick_sin_ref[...])


def _prep_weights(p):
    bf = lambda x: x.astype(BF16)
    vec = lambda x: x.reshape(1, -1)
    return {
        "ffn1_norm": vec(p["ffn1_norm"]), "ffn1_wg": bf(p["ffn1_w_gate"]), "ffn1_wu": bf(p["ffn1_w_up"]),
        "ffn1_wd": bf(p["ffn1_w_down"]),
        "mix_norm": vec(p["mix_norm"]), "w_in": bf(p["w_in"]),
        "w_branch_a": bf(p["w_branch_a"].reshape(N_HEADS_A, HEAD_DIM, D_MODEL)[np.array(HEAD_PERM_A)].reshape(
            Q_A_W, D_MODEL)),
        "w_branch_b": bf(p["w_branch_b"]), "w_out": bf(p["w_out"]),
        "mem_q_norm": vec(p["mem_q_norm"]), "mem_kv_norm": vec(p["mem_kv_norm"]),
        "w_mem_q": bf(p["w_mem_q"]), "w_mem_k": bf(p["w_mem_k"]), "w_mem_v": bf(p["w_mem_v"]),
        "w_mem_o": bf(p["w_mem_o"]),
        "ffn2_norm": vec(p["ffn2_norm"]), "ffn2_wg": bf(p["ffn2_w_gate"]), "ffn2_wu": bf(p["ffn2_w_up"]),
        "ffn2_wd": bf(p["ffn2_w_down"]), "final_norm": vec(p["final_norm"]),
    }


def kernel(x_prompt, x_sample, cache_swa_k, cache_swa_v, cache_dil_k, cache_dil_v, cache_mem_k, cache_mem_v, mem_prompt, ffn1_norm, ffn1_w_gate, ffn1_w_up, ffn1_w_down, mix_norm, w_in, attn_sink, w_branch_a, w_branch_b, w_out, mem_q_norm, mem_kv_norm, w_mem_q, w_mem_k, w_mem_v, w_mem_o, ffn2_norm, ffn2_w_gate, ffn2_w_up, ffn2_w_down, final_norm):
    depth = ffn1_norm.shape[0]
    assert depth == 1
    batch, seq, d = x_prompt.shape
    nb, t_new, _ = x_sample.shape
    assert d == D_MODEL and t_new == SAMPLE_T and seq % (TOKEN_TILE * DILATIONS[-1]) == 0 and seq % ATTN_CHUNK == 0
    assert cache_swa_k.shape[2] == CACHE_A and cache_dil_k.shape[2] == CACHE_B
    layer = lambda x: x[0]
    w = _prep_weights(dict(
        ffn1_norm=layer(ffn1_norm), ffn1_w_gate=layer(ffn1_w_gate), ffn1_w_up=layer(ffn1_w_up),
        ffn1_w_down=layer(ffn1_w_down), mix_norm=layer(mix_norm), w_in=layer(w_in),
        w_branch_a=layer(w_branch_a), w_branch_b=layer(w_branch_b), w_out=layer(w_out),
        mem_q_norm=layer(mem_q_norm), mem_kv_norm=layer(mem_kv_norm), w_mem_q=layer(w_mem_q),
        w_mem_k=layer(w_mem_k), w_mem_v=layer(w_mem_v), w_mem_o=layer(w_mem_o), ffn2_norm=layer(ffn2_norm),
        ffn2_w_gate=layer(ffn2_w_gate), ffn2_w_up=layer(ffn2_w_up), ffn2_w_down=layer(ffn2_w_down),
        final_norm=final_norm))
    sink = layer(attn_sink)

    ts = nb * t_new
    pos_s = PAST_LEN + (jnp.arange(TOKEN_TILE, dtype=jnp.int32) % t_new)
    tables_s = _rope_tables(pos_s)
    (h_s, q32, ka32_s, va32_s, kb32_s, vb32_s) = _front_call(x_sample.reshape(ts, d), tables_s, w, sample=True)
    by_pos = lambda c: jnp.transpose(layer(c), (0, 2, 3, 1)).reshape(nb, c.shape[3] * c.shape[4], c.shape[2])
    by_slot_head = lambda c: layer(c).reshape(nb, N_MEM * MEM_HEADS, MEM_HEAD_DIM)

    oa_s, ob_s = _attn_sample_call(
        sink, q32, ka32_s, va32_s, kb32_s, vb32_s,
        by_pos(cache_swa_k), by_pos(cache_swa_v), by_pos(cache_dil_k), by_pos(cache_dil_v))

    tp = batch * seq
    mk32, mv32, mk16, mv16 = _mem_kv_call(mem_prompt.reshape(batch * N_MEM, d), w)
    tables_p = _rope_tables(jnp.arange(seq, dtype=jnp.int32))
    h_p, main, split4, split16, tails_kva, tails_kb, tails_vb = _front_call(
        x_prompt.reshape(tp, d), tables_p, w, sample=False, batch=batch)
    oa = _attn_a_call(sink, main, batch, seq)
    groups = ((main.reshape(batch, 1, seq, MAIN_W), MAIN_QB_COL), (split4, 0), (split16, 0))
    ols = [_attn_b_call(qkv, qcol, dil) for (qkv, qcol), dil in zip(groups, DILATIONS)]
    y_p = _back_prompt_call(h_p, oa, ols, mk16.reshape(batch, N_MEM, MEM_W),
                            mv16.reshape(batch, N_MEM, MEM_W), w, seq)

    h2_s, qc_s = _back_sample_a_call(h_s, oa_s, ob_s, w)
    oc_s = _cross_sample_call(qc_s, by_slot_head(cache_mem_k), by_slot_head(cache_mem_v))
    y_s = _back_sample_b_call(h2_s, oc_s, w)

    keep_a, keep_b = min(CACHE_A, seq), min(CACHE_B, seq)
    assert max(keep_a, keep_b) <= TAIL_ROWS <= seq and TAIL_ROWS % TOKEN_TILE == 0

    def tail(x, col, keep, heads):
        part = x[:, col:col + heads * HEAD_DIM, TAIL_ROWS - keep:].reshape(batch, heads, HEAD_DIM, keep)
        return jnp.transpose(part, (0, 3, 1, 2))[None]

    new = lambda x, heads: x.reshape(1, nb, t_new, heads, HEAD_DIM)
    memo = lambda x: x.reshape(1, batch, N_MEM, MEM_HEADS, MEM_HEAD_DIM)
    return (y_p.reshape(batch, seq, d), y_s.reshape(nb, t_new, d),
            tail(tails_kva, 0, keep_a, N_KV_A), tail(tails_kva, KV_A_W, keep_a, N_KV_A),
            tail(tails_kb, 0, keep_b, N_KV_B), tail(tails_vb, 0, keep_b, N_KV_B),
            memo(mk32), memo(mv32),
            new(ka32_s, N_KV_A), new(va32_s, N_KV_A), new(kb32_s, N_KV_B), new(vb32_s, N_KV_B))
```

```python
import functools

import jax
import jax.numpy as jnp
import numpy as np
from jax import lax
from jax.experimental import pallas as pl
from jax.experimental.pallas import tpu as pltpu

F32 = jnp.float32
BF16 = jnp.bfloat16

D_MODEL = 1024
D_FF = 2816
HEAD_DIM = 64
ROT_HALF = 8
ROPE_THETA = 500000.0
ATTN_SCALE = HEAD_DIM ** -0.5
RMS_EPS = 1e-6
PAST_LEN = 16384

N_HEADS_A = 8
N_KV_A = 2
N_KV_B = 4
DILATIONS = (1, 4, 16)
BAND_STEPS = 128
Q_A_W = N_HEADS_A * HEAD_DIM
KV_A_W = N_KV_A * HEAD_DIM
Q_B_W = len(DILATIONS) * N_KV_B * HEAD_DIM
KV_B_W = N_KV_B * HEAD_DIM
QKV_W = Q_A_W + 2 * KV_A_W + Q_B_W + 2 * KV_B_W
MEM_HEADS = 4
MEM_HEAD_DIM = 128
MEM_W = MEM_HEADS * MEM_HEAD_DIM
MEM_SCALE = MEM_HEAD_DIM ** -0.5
N_MEM = 256

HEAD_PERM_A = (0, 4, 1, 5, 2, 6, 3, 7)

MAIN_W = Q_A_W + 2 * KV_A_W + 3 * KV_B_W
MAIN_KA_COL, MAIN_VA_COL = Q_A_W // KV_A_W, Q_A_W // KV_A_W + 1
MAIN_QB_COL = (Q_A_W + 2 * KV_A_W) // KV_B_W
SPLIT_W = 3 * KV_B_W
TAIL_ROWS = 2048
assert 2 * KV_A_W == KV_B_W

LANES = 128
BLOCK = 128
TOKEN_TILE = 512
FRONT_ROW_PARTS = 2
BACK_ROW_PARTS = 1
ATTN_CHUNK = 2048
FF_CHUNKS = ((0, 1536), (1536, 2816))
VMEM_LIMIT = 60 * 1024 * 1024
NEG_INF = float("-inf")


def _rms(x, g):
    ms = jnp.mean(x * x, axis=-1, keepdims=True)
    return x * lax.rsqrt(ms + RMS_EPS) * g


def _sigmoid(x):
    return 0.5 * jnp.tanh(0.5 * x) + 0.5


def _dot(a, b):
    return jnp.dot(a, b, preferred_element_type=F32)


def _dot_nt(a, b):
    return lax.dot_general(a, b, (((1,), (1,)), ((), ())), preferred_element_type=F32)


def _ffn_half(xs, g_ref, wg_ref, wu_ref, wd_ref):
    us = [_rms(x, g_ref[...]).astype(BF16) for x in xs]
    accs = [None] * len(xs)
    for lo, hi in FF_CHUNKS:
        gates = [_dot(u, wg_ref[:, lo:hi]) for u in us]
        ups = [_dot(u, wu_ref[:, lo:hi]) for u in us]
        acts = [(gate * _sigmoid(gate) * up).astype(BF16) for gate, up in zip(gates, ups)]
        for i, act in enumerate(acts):
            part = _dot(act, wd_ref[lo:hi, :])
            accs[i] = part if accs[i] is None else accs[i] + part
    return [x + 0.5 * acc for x, acc in zip(xs, accs)]


def _row_blocks(ref, parts):
    n = ref.shape[0] // parts
    return [slice(i * n, (i + 1) * n) for i in range(parts)]


def _swap_halves(x):
    return pltpu.roll(x, HEAD_DIM, 1)


def _low_half():
    return lax.broadcasted_iota(jnp.int32, (1, LANES), 1) < HEAD_DIM


def _pair_kv_groups(chunks):
    low = _low_half()
    c0, c1, c2, c3 = chunks
    return [jnp.where(low, c0, _swap_halves(c2)), jnp.where(low, _swap_halves(c0), c2),
            jnp.where(low, c1, _swap_halves(c3)), jnp.where(low, _swap_halves(c1), c3)]


def _merge_pairs(per_head):
    low = _low_half()
    return [jnp.where(low, per_head[2 * c], per_head[2 * c + 1]) for c in range(N_HEADS_A // 2)]


def _front_kernel(x_ref, cos_ref, sin_ref, g1_ref, wg_ref, wu_ref, wd_ref, gm_ref, wqkv_ref,
                  h_ref, *rest, sample, tiles_per_seq=None, tail_tiles=None):
    blocks = _row_blocks(x_ref, FRONT_ROW_PARTS)
    hs = _ffn_half([x_ref[rb, :] for rb in blocks], g1_ref, wg_ref, wu_ref, wd_ref)
    for rb, h in zip(blocks, hs):
        h_ref[rb, :] = h
    ns = [_rms(h, gm_ref[...]).astype(BF16) for h in hs]
    zs = [_dot(n, wqkv_ref[...]) for n in ns]
    for rb, z in zip(blocks, zs):
        _front_emit(rb, z, cos_ref, sin_ref, rest, sample)
    if not sample:
        tail_refs, tail_scratch = rest[3:6], (rest[10], rest[8], rest[9])

        @pl.when(lax.rem(pl.program_id(0), tiles_per_seq) >= tiles_per_seq - tail_tiles)
        def _():
            for ref, scr in zip(tail_refs, tail_scratch):
                for c in range(scr.shape[0]):
                    ref[0, c * LANES:(c + 1) * LANES, :] = scr[c].T


def _front_emit(rb, z, cos_ref, sin_ref, rest, sample):
    cos, sin = cos_ref[rb, :], sin_ref[rb, :]
    lane_in_head = lax.broadcasted_iota(jnp.int32, (1, LANES), 1) & (HEAD_DIM - 1)
    first = lane_in_head < ROT_HALF

    def chunk(c):
        return z[:, c * LANES:(c + 1) * LANES]

    def rope(c):
        zc = chunk(c)
        partner = jnp.where(first, pltpu.roll(zc, LANES - ROT_HALF, 1), pltpu.roll(zc, ROT_HALF, 1))
        return zc * cos + partner * sin

    nqa, nka = Q_A_W // LANES, KV_A_W // LANES
    nqb, nkb = Q_B_W // LANES, KV_B_W // LANES
    c0 = 0
    qa = _pair_kv_groups([rope(c0 + c) * ATTN_SCALE for c in range(nqa)])
    c0 += nqa
    ka = [rope(c0 + c) for c in range(nka)]
    c0 += nka
    va = [chunk(c0 + c) for c in range(nka)]
    c0 += nka
    qb = [rope(c0 + c) * ATTN_SCALE for c in range(nqb)]
    c0 += nqb
    kb = [rope(c0 + c) for c in range(nkb)]
    c0 += nkb
    vb = [chunk(c0 + c) for c in range(nkb)]

    def store(ref, parts, dtype):
        for c, p in enumerate(parts):
            ref[rb, c * LANES:(c + 1) * LANES] = p.astype(dtype)

    def fill(scr, parts):
        for c, p in enumerate(parts):
            scr[c, rb, :] = p

    def store_split(ref, slot, scr, d):
        n = (rb.stop - rb.start) // d
        dst = slice(rb.start // d, rb.start // d + n)
        for r in range(d):
            for c in range(scr.shape[0]):
                lanes = slice(slot * KV_B_W + c * LANES, slot * KV_B_W + (c + 1) * LANES)
                ref[0, r, dst, lanes] = scr[c, pl.ds(rb.start + r, n, stride=d), :].astype(BF16)

    if sample:
        q32_ref, ka32_ref, va32_ref, kb32_ref, vb32_ref = rest
        store(q32_ref, qa + qb, F32)
        store(ka32_ref, ka, F32)
        store(va32_ref, va, F32)
        store(kb32_ref, kb, F32)
        store(vb32_ref, vb, F32)
    else:
        main_ref, s4_ref, s16_ref, _, _, _, q4_s, q16_s, kb_s, vb_s, kva_s = rest
        gw = KV_B_W // LANES
        store(main_ref, qa + ka + va + qb[0:gw] + kb + vb, BF16)
        fill(q4_s, qb[gw:2 * gw])
        fill(q16_s, qb[2 * gw:3 * gw])
        fill(kb_s, kb)
        fill(vb_s, vb)
        fill(kva_s, ka + va)
        for slot, scr4, scr16 in ((0, q4_s, q16_s), (1, kb_s, kb_s), (2, vb_s, vb_s)):
            store_split(s4_ref, slot, scr4, DILATIONS[1])
            store_split(s16_ref, slot, scr16, DILATIONS[2])


def _const_spec(shape, index=None):
    index = (0,) * len(shape) if index is None else index
    return pl.BlockSpec(shape, lambda *_: index, pipeline_mode=pl.Buffered(1))


def _front_call(x, tables, w, *, sample, batch=1):
    t = x.shape[0]
    tm = TOKEN_TILE
    nsteps = t // tm
    tpb = nsteps // batch
    seq = t // batch
    cos_t, sin_t = tables
    tab_blocks = cos_t.shape[0] // tm
    row = lambda i: (i, 0)
    tab = lambda i: (i % tab_blocks, 0)

    def tok(width, dtype):
        return jax.ShapeDtypeStruct((t, width), dtype), pl.BlockSpec((tm, width), row)

    def split(d):
        return (jax.ShapeDtypeStruct((batch, d, seq // d, SPLIT_W), BF16),
                pl.BlockSpec((1, d, tm // d, SPLIT_W), lambda i: (i // tpb, 0, i % tpb, 0)))

    outs = [tok(D_MODEL, F32)]
    scratch = []
    static = {}
    if sample:
        outs += [tok(Q_A_W + Q_B_W, F32), tok(KV_A_W, F32), tok(KV_A_W, F32), tok(KV_B_W, F32), tok(KV_B_W, F32)]
    else:
        tail_tiles = TAIL_ROWS // tm
        tails = (jax.ShapeDtypeStruct((batch, KV_B_W, TAIL_ROWS), F32),
                 pl.BlockSpec((1, KV_B_W, tm),
                              lambda i: (i // tpb, 0, jnp.maximum(i % tpb - (tpb - tail_tiles), 0))))
        outs += [tok(MAIN_W, BF16), split(DILATIONS[1]), split(DILATIONS[2])] + [tails] * 3
        scratch = [pltpu.VMEM((KV_B_W // LANES, tm, LANES), F32)] * 4 + [pltpu.VMEM((2, tm, LANES), F32)]
        static = dict(tiles_per_seq=tpb, tail_tiles=tail_tiles)
    out_shape, out_specs = zip(*outs)
    in_specs = [pl.BlockSpec((tm, D_MODEL), row),
                pl.BlockSpec((tm, LANES), tab), pl.BlockSpec((tm, LANES), tab),
                _const_spec((1, D_MODEL)), _const_spec((D_MODEL, D_FF)), _const_spec((D_MODEL, D_FF)),
                _const_spec((D_FF, D_MODEL)), _const_spec((1, D_MODEL)), _const_spec((D_MODEL, QKV_W), (0, 0))]
    return pl.pallas_call(
        functools.partial(_front_kernel, sample=sample, **static),
        grid=(nsteps,),
        in_specs=in_specs,
        out_specs=list(out_specs),
        out_shape=list(out_shape),
        scratch_shapes=scratch,
        compiler_params=pltpu.CompilerParams(dimension_semantics=("parallel" if sample else "arbitrary",),
                                             vmem_limit_bytes=VMEM_LIMIT),
        name="front_sample" if sample else "front_prompt",
    )(x, cos_t, sin_t, w["ffn1_norm"], w["ffn1_wg"], w["ffn1_wu"], w["ffn1_wd"], w["mix_norm"], w["w_in"])


def _lane_block_masks(width, block, dtype):
    lane = lax.broadcasted_iota(jnp.int32, (1, width), 1)
    return [((lane >= i * block) & (lane < (i + 1) * block)).astype(dtype) for i in range(width // block)]


def _band_mask():
    qi = lax.broadcasted_iota(jnp.int32, (BLOCK, 2 * BLOCK), 0)
    kj = lax.broadcasted_iota(jnp.int32, (BLOCK, 2 * BLOCK), 1)
    dist = qi + BLOCK - kj
    return (dist >= 0) & (dist <= BAND_STEPS), kj >= BLOCK


def _band_block(load_qbd, load_k, load_v, mask, nheads, finish):
    s = _dot_nt(load_qbd(), load_k())
    yield
    ps, ms, ls = [], [], []
    for p in range(nheads):
        sp = jnp.where(mask, s[p * BLOCK:(p + 1) * BLOCK], NEG_INF)
        m = jnp.max(sp, axis=-1, keepdims=True)
        e = jnp.exp(sp - m)
        ls.append(jnp.sum(e, axis=-1, keepdims=True))
        ms.append(m)
        ps.append(e.astype(BF16))
    pc = jnp.concatenate(ps, axis=0)
    yield
    finish(_dot(pc, load_v()), ms, ls)
    yield


def _issue_skewed(items, nstages=3):
    for tick in range(len(items) + nstages - 1):
        for s in reversed(range(nstages)):
            i = tick - s
            if 0 <= i < len(items):
                next(items[i])


def _kv_window(cur_ref, prev_ref, qb):
    if qb == 0:
        return jnp.concatenate([prev_ref[...], cur_ref[0:BLOCK]], axis=0)
    return cur_ref[(qb - 1) * BLOCK:(qb + 1) * BLOCK]


def _attn_a_kernel(sink_ref, q_ref, kc_ref, kp_ref, vc_ref, vp_ref, o_ref):
    first_chunk = pl.program_id(1) == 0
    band, in_cur = _band_mask()
    half_bf = _lane_block_masks(LANES, HEAD_DIM, BF16)

    def block(qb):
        mask = band & (in_cur | jnp.logical_not(first_chunk)) if qb == 0 else band
        rows = slice(qb * BLOCK, (qb + 1) * BLOCK)

        def load_qbd():
            return jnp.concatenate(
                [q_ref[rows, (p // 2) * LANES:(p // 2 + 1) * LANES] * half_bf[p % 2] for p in range(N_HEADS_A)],
                axis=0)

        def finish(o, ms, ls):
            normed = []
            for p in range(N_HEADS_A):
                den = ls[p] + jnp.exp(sink_ref[HEAD_PERM_A[p]] - ms[p])
                normed.append(o[p * BLOCK:(p + 1) * BLOCK] * (1.0 / den))
            for c, pair in enumerate(_merge_pairs(normed)):
                o_ref[rows, c * LANES:(c + 1) * LANES] = pair.astype(BF16)

        return _band_block(load_qbd, lambda: _kv_window(kc_ref, kp_ref, qb), lambda: _kv_window(vc_ref, vp_ref, qb),
                           mask, N_HEADS_A, finish)

    for qb in range(q_ref.shape[0] // BLOCK):
        for _ in block(qb):
            pass


def _attn_a_call(sink, main, batch, seq):
    t = main.shape[0]
    cpb = seq // ATTN_CHUNK
    bpc = ATTN_CHUNK // BLOCK
    cur = lambda col: (lambda b, c: (b * cpb + c, col))
    prev = lambda col: (lambda b, c: (b * cpb * bpc + jnp.maximum(c * bpc - 1, 0), col))
    kv = lambda col: [pl.BlockSpec((ATTN_CHUNK, KV_A_W), cur(col)), pl.BlockSpec((BLOCK, KV_A_W), prev(col))]
    return pl.pallas_call(
        _attn_a_kernel,
        grid=(batch, cpb),
        in_specs=[pl.BlockSpec(memory_space=pltpu.SMEM), pl.BlockSpec((ATTN_CHUNK, Q_A_W), cur(0))]
        + kv(MAIN_KA_COL) + kv(MAIN_VA_COL),
        out_specs=pl.BlockSpec((ATTN_CHUNK, Q_A_W), cur(0)),
        out_shape=jax.ShapeDtypeStruct((t, Q_A_W), BF16),
        compiler_params=pltpu.CompilerParams(dimension_semantics=("parallel", "parallel"),
                                             vmem_limit_bytes=VMEM_LIMIT),
        name="attn_a_prompt",
    )(sink, main, main, main, main, main)


def _attn_b_kernel(q_ref, kc_ref, kp_ref, vc_ref, vp_ref, ol_ref):
    first_chunk = pl.program_id(2) == 0
    band, in_cur = _band_mask()
    head_bf = _lane_block_masks(KV_B_W, HEAD_DIM, BF16)
    lane = lax.broadcasted_iota(jnp.int32, (1, KV_B_W), 1)

    def block(r, qb):
        mask = band & (in_cur | jnp.logical_not(first_chunk)) if qb == 0 else band
        rows = slice(qb * BLOCK, (qb + 1) * BLOCK)

        def load_qbd():
            q = q_ref[r, rows, :]
            return jnp.concatenate([q * head_bf[p] for p in range(N_KV_B)], axis=0)

        def finish(o, ms, ls):
            out = lse = None
            for p in reversed(range(N_KV_B)):
                op = o[p * BLOCK:(p + 1) * BLOCK] * (1.0 / ls[p])
                lp = jnp.broadcast_to(ms[p] + jnp.log(ls[p]), (BLOCK, KV_B_W))
                if out is None:
                    out, lse = op, lp
                else:
                    sel = lane < (p + 1) * HEAD_DIM
                    out, lse = jnp.where(sel, op, out), jnp.where(sel, lp, lse)
            ol_ref[r, rows, 0:KV_B_W] = out
            ol_ref[r, rows, KV_B_W:2 * KV_B_W] = lse

        return _band_block(load_qbd, lambda: _kv_window(kc_ref.at[r], kp_ref.at[r], qb),
                           lambda: _kv_window(vc_ref.at[r], vp_ref.at[r], qb), mask, N_KV_B, finish)

    _issue_skewed([block(r, qb) for r in range(q_ref.shape[0]) for qb in range(q_ref.shape[1] // BLOCK)])


def _attn_b_call(qkv, qcol, d):
    batch, _, rows, _ = qkv.shape
    chunk = min(ATTN_CHUNK, rows)
    rps = min(d, ATTN_CHUNK // chunk)
    cpb = rows // chunk
    bpc = chunk // BLOCK
    cur = lambda col: pl.BlockSpec((None, rps, chunk, KV_B_W), lambda b, r, c: (b, r, c, col))
    prev = lambda col: pl.BlockSpec((None, rps, BLOCK, KV_B_W),
                                    lambda b, r, c: (b, r, jnp.maximum(c * bpc - 1, 0), col))
    return pl.pallas_call(
        _attn_b_kernel,
        grid=(batch, d // rps, cpb),
        in_specs=[cur(qcol), cur(qcol + 1), prev(qcol + 1), cur(qcol + 2), prev(qcol + 2)],
        out_specs=pl.BlockSpec((None, rps, chunk, 2 * KV_B_W), lambda b, r, c: (b, r, c, 0)),
        out_shape=jax.ShapeDtypeStruct((batch, d, rows, 2 * KV_B_W), F32),
        compiler_params=pltpu.CompilerParams(dimension_semantics=("parallel", "parallel", "parallel"),
                                             vmem_limit_bytes=VMEM_LIMIT),
        name=f"attn_b_prompt_d{d}",
    )(qkv, qkv, qkv, qkv, qkv)


SAMPLE_T = 8
CACHE_A = 128
CACHE_B = 2048
SAMPLE_ELEMS = 4


def _new_rows_block(new):
    return jnp.concatenate([new, jnp.zeros((BLOCK - SAMPLE_T, new.shape[1]), F32)], axis=0)


def _cached_scores(qbd, kt, knew):
    return _dot(qbd, kt), _dot_nt(qbd, knew)


def _cached_softmax(s_c, s_n, mask_c, mask_n):
    s_c = jnp.where(mask_c, s_c, NEG_INF)
    s_n = jnp.where(mask_n, s_n, NEG_INF)
    m = jnp.maximum(jnp.max(s_c, axis=-1, keepdims=True), jnp.max(s_n, axis=-1, keepdims=True))
    e_c = jnp.exp(s_c - m)
    e_n = jnp.exp(s_n - m)
    l = jnp.sum(e_c, axis=-1, keepdims=True) + jnp.sum(e_n, axis=-1, keepdims=True)
    return e_c, e_n, m, l


def _cached_pv(e_c, e_n, vt, vnew):
    return _dot_nt(e_c, vt) + _dot(e_n, vnew)


def _attn_sample_kernel(sink_ref, q_ref, kan_ref, van_ref, kbn_ref, vbn_ref, cka_ref, cva_ref, ckb_ref, cvb_ref,
                        oa_ref, ob_ref):
    half_f = _lane_block_masks(LANES, HEAD_DIM, F32)
    lane_b = lax.broadcasted_iota(jnp.int32, (1, KV_B_W), 1)

    def dist(nrows, ncols, offset):
        r = lax.broadcasted_iota(jnp.int32, (nrows, ncols), 0)
        i = lax.broadcasted_iota(jnp.int32, (nrows, ncols), 1)
        return r, offset + (r & (SAMPLE_T - 1)) - i

    nrow_a = N_HEADS_A * SAMPLE_T
    _, dac = dist(nrow_a, CACHE_A, CACHE_A)
    _, dan = dist(nrow_a, BLOCK, 0)
    mask_ac = (dac >= 0) & (dac <= BAND_STEPS)
    mask_an = (dan >= 0) & (dan <= BAND_STEPS)
    nrow_b = len(DILATIONS) * N_KV_B * SAMPLE_T
    rows_per_group = N_KV_B * SAMPLE_T

    def mask_b(ncols, offset):
        r, db = dist(nrow_b, ncols, offset)
        dil = jnp.where(r < rows_per_group, DILATIONS[0], jnp.where(r < 2 * rows_per_group, DILATIONS[1], DILATIONS[2]))
        return (db >= 0) & (db <= BAND_STEPS * dil) & ((db & (dil - 1)) == 0)

    mask_bc = mask_b(CACHE_B, CACHE_B)
    mask_bn = mask_b(BLOCK, 0)

    sink_col = jnp.concatenate(
        [jnp.full((SAMPLE_T, 1), sink_ref[HEAD_PERM_A[p]], F32) for p in range(N_HEADS_A)], axis=0)

    def element(j):
        rows = slice(j * SAMPLE_T, (j + 1) * SAMPLE_T)
        qbd_a = jnp.concatenate(
            [q_ref[rows, (p // 2) * LANES:(p // 2 + 1) * LANES] * half_f[p % 2] for p in range(N_HEADS_A)],
            axis=0)
        qrows = []
        for g in range(len(DILATIONS)):
            for kvh in range(N_KV_B):
                c = (Q_A_W // LANES) + 2 * g + kvh // 2
                part = q_ref[rows, c * LANES:(c + 1) * LANES] * half_f[kvh % 2]
                zero = jnp.zeros_like(part)
                qrows.append(jnp.concatenate([part, zero] if kvh // 2 == 0 else [zero, part], axis=1))
        qbd_b = jnp.concatenate(qrows, axis=0)
        scores_a = _cached_scores(qbd_a, cka_ref[j], _new_rows_block(kan_ref[rows, :]))
        scores_b = _cached_scores(qbd_b, ckb_ref[j], _new_rows_block(kbn_ref[rows, :]))
        yield
        ea_c, ea_n, ma, la = _cached_softmax(*scores_a, mask_ac, mask_an)
        eb_c, eb_n, m, l = _cached_softmax(*scores_b, mask_bc, mask_bn)
        yield
        o = _cached_pv(ea_c, ea_n, cva_ref[j], _new_rows_block(van_ref[rows, :]))
        o = o * (1.0 / (la + jnp.exp(sink_col - ma)))
        per_head = [o[p * SAMPLE_T:(p + 1) * SAMPLE_T] for p in range(N_HEADS_A)]
        for c, pair in enumerate(_merge_pairs(per_head)):
            oa_ref[rows, c * LANES:(c + 1) * LANES] = pair
        o = _cached_pv(eb_c, eb_n, cvb_ref[j], _new_rows_block(vbn_ref[rows, :]))
        out = None
        for kvh in reversed(range(N_KV_B)):
            sl = [slice((g * N_KV_B + kvh) * SAMPLE_T, (g * N_KV_B + kvh + 1) * SAMPLE_T)
                  for g in range(len(DILATIONS))]
            mj = jnp.maximum(jnp.maximum(m[sl[0]], m[sl[1]]), m[sl[2]])
            ws = [jnp.exp(m[x] - mj) for x in sl]
            den = ws[0] * l[sl[0]] + ws[1] * l[sl[1]] + ws[2] * l[sl[2]]
            num = ws[0] * o[sl[0]] + ws[1] * o[sl[1]] + ws[2] * o[sl[2]]
            okv = num * (1.0 / den)
            out = okv if out is None else jnp.where(lane_b < (kvh + 1) * HEAD_DIM, okv, out)
        ob_ref[rows, :] = out
        yield

    elements = [element(j) for j in range(SAMPLE_ELEMS)]
    for _ in range(3):
        for e in elements:
            next(e)


def _attn_sample_call(sink, q32, ka32, va32, kb32, vb32, cka, cva, ckb, cvb):
    nb = cka.shape[0]
    bt = SAMPLE_ELEMS
    t = q32.shape[0]
    tok = lambda w: pl.BlockSpec((bt * SAMPLE_T, w), lambda i: (i, 0))
    cache = lambda w, n: pl.BlockSpec((bt, w, n), lambda i: (i, 0, 0))
    return pl.pallas_call(
        _attn_sample_kernel,
        grid=(nb // bt,),
        in_specs=[pl.BlockSpec(memory_space=pltpu.SMEM), tok(Q_A_W + Q_B_W),
                  tok(KV_A_W), tok(KV_A_W), tok(KV_B_W), tok(KV_B_W),
                  cache(KV_A_W, CACHE_A), cache(KV_A_W, CACHE_A), cache(KV_B_W, CACHE_B), cache(KV_B_W, CACHE_B)],
        out_specs=[tok(Q_A_W), tok(KV_B_W)],
        out_shape=[jax.ShapeDtypeStruct((t, Q_A_W), F32), jax.ShapeDtypeStruct((t, KV_B_W), F32)],
        compiler_params=pltpu.CompilerParams(dimension_semantics=("parallel",), vmem_limit_bytes=VMEM_LIMIT),
        name="attn_sample",
    )(sink, q32, ka32, va32, kb32, vb32, cka, cva, ckb, cvb)


def _mix_out(h, oa, ob, gm_ref, wgate_ref, wba_ref, wbb_ref, wout_ref):
    n = _rms(h, gm_ref[...]).astype(BF16)
    gates = _dot(n, wgate_ref[...])
    ga = _sigmoid(gates[:, :D_MODEL])
    gb = _sigmoid(gates[:, D_MODEL:])
    mixed = ga * _dot(oa.astype(BF16), wba_ref[...]) + gb * _dot(ob.astype(BF16), wbb_ref[...])
    return h + _dot(mixed.astype(BF16), wout_ref[...])


def _combine_groups(os, ls):
    m = jnp.maximum(jnp.maximum(ls[0], ls[1]), ls[2])
    es = [jnp.exp(x - m) for x in ls]
    den = es[0] + es[1] + es[2]
    num = es[0] * os[0] + es[1] * os[1] + es[2] * os[2]
    return num * (1.0 / den)


def _merge_split(split_ref, field, scr, rb):
    d = split_ref.shape[1]
    if d == 1:
        return split_ref[0, 0, rb, field * KV_B_W:(field + 1) * KV_B_W]
    n = (rb.stop - rb.start) // d
    src = slice(rb.start // d, rb.start // d + n)
    for r in range(d):
        for c in range(scr.shape[0]):
            lanes = slice(field * KV_B_W + c * LANES, field * KV_B_W + (c + 1) * LANES)
            scr[c, pl.ds(rb.start + r, n, stride=d), :] = split_ref[0, r, src, lanes]
    return jnp.concatenate([scr[c, rb, :] for c in range(scr.shape[0])], axis=1)


def _softmax_rows(s):
    m = jnp.max(s, axis=-1, keepdims=True)
    e = jnp.exp(s - m)
    return e * (1.0 / jnp.sum(e, axis=-1, keepdims=True))


def _back_prompt_kernel(h_ref, oa_ref, ol0_ref, ol1_ref, ol2_ref, mk_ref, mv_ref,
                        gm_ref, wgate_ref, wba_ref, wbb_ref, wout_ref, gq_ref, wq_ref, wo_ref,
                        g2_ref, wg_ref, wu_ref, wd_ref, gf_ref, y_ref, o1_s, o2_s, l1_s, l2_s, ob_s):
    blocks = _row_blocks(h_ref, BACK_ROW_PARTS)
    h3s = []
    for rb in blocks:
        ob_s[rb, :] = _combine_groups(
            (_merge_split(ol0_ref, 0, None, rb), _merge_split(ol1_ref, 0, o1_s, rb),
             _merge_split(ol2_ref, 0, o2_s, rb)),
            (_merge_split(ol0_ref, 1, None, rb), _merge_split(ol1_ref, 1, l1_s, rb),
             _merge_split(ol2_ref, 1, l2_s, rb))).astype(BF16)
        h2 = _mix_out(h_ref[rb, :], oa_ref[rb, :], ob_s[rb, :], gm_ref, wgate_ref, wba_ref, wbb_ref, wout_ref)
        q = _dot(_rms(h2, gq_ref[...]).astype(BF16), wq_ref[...])
        heads = [None] * MEM_HEADS

        def head(hd):
            cols = slice(hd * MEM_HEAD_DIM, (hd + 1) * MEM_HEAD_DIM)
            s = _dot_nt(q[:, cols].astype(BF16), mk_ref[0, :, cols]) * MEM_SCALE
            yield
            e = jnp.exp(s - jnp.max(s, axis=-1, keepdims=True))
            inv = 1.0 / jnp.sum(e, axis=-1, keepdims=True)
            yield
            heads[hd] = _dot(e.astype(BF16), mv_ref[0, :, cols]) * inv
            yield

        _issue_skewed([head(hd) for hd in range(MEM_HEADS)])
        oc = jnp.concatenate(heads, axis=1).astype(BF16)
        h3s.append(h2 + _dot(oc, wo_ref[...]))
    for rb, y in zip(blocks, _ffn_half(h3s, g2_ref, wg_ref, wu_ref, wd_ref)):
        y_ref[rb, :] = _rms(y, gf_ref[...])


def _back_prompt_call(h, oa, ols, mk, mv, w, seq):
    t = h.shape[0]
    tm = TOKEN_TILE
    tiles_per_batch = seq // tm
    row = lambda i: (i, 0)
    mem = lambda i: (i // tiles_per_batch, 0, 0)
    tok = lambda width: pl.BlockSpec((tm, width), row)
    split = lambda d: pl.BlockSpec((1, d, tm // d, 2 * KV_B_W),
                                   lambda i: (i // tiles_per_batch, 0, i % tiles_per_batch, 0))
    in_specs = ([tok(D_MODEL), tok(Q_A_W)] + [split(d) for d in DILATIONS]
                + [pl.BlockSpec((1, N_MEM, MEM_W), mem), pl.BlockSpec((1, N_MEM, MEM_W), mem)]
                + [_const_spec((1, D_MODEL)), _const_spec((D_MODEL, 2 * D_MODEL), (0, 1)), _const_spec((Q_A_W, D_MODEL)),
                   _const_spec((KV_B_W, D_MODEL)), _const_spec((D_MODEL, D_MODEL)),
                   _const_spec((1, D_MODEL)), _const_spec((D_MODEL, MEM_W)), _const_spec((MEM_W, D_MODEL)),
                   _const_spec((1, D_MODEL)), _const_spec((D_MODEL, D_FF)), _const_spec((D_MODEL, D_FF)),
                   _const_spec((D_FF, D_MODEL)), _const_spec((1, D_MODEL))])
    return pl.pallas_call(
        _back_prompt_kernel,
        grid=(t // tm,),
        in_specs=in_specs,
        out_specs=tok(D_MODEL),
        out_shape=jax.ShapeDtypeStruct((t, D_MODEL), F32),
        scratch_shapes=[pltpu.VMEM((KV_B_W // LANES, tm, LANES), F32)] * 4 + [pltpu.VMEM((tm, KV_B_W), BF16)],
        compiler_params=pltpu.CompilerParams(dimension_semantics=("parallel",), vmem_limit_bytes=VMEM_LIMIT),
        name="back_prompt",
    )(h, oa, *ols, mk, mv, w["mix_norm"], w["w_in"], w["w_branch_a"], w["w_branch_b"], w["w_out"],
      w["mem_q_norm"], w["w_mem_q"], w["w_mem_o"], w["ffn2_norm"], w["ffn2_wg"], w["ffn2_wu"], w["ffn2_wd"],
      w["final_norm"])


def _back_sample_a_kernel(h_ref, oa_ref, ob_ref, gm_ref, wgate_ref, wba_ref, wbb_ref, wout_ref, gq_ref, wq_ref,
                          h2_ref, q_ref):
    h2 = _mix_out(h_ref[...], oa_ref[...], ob_ref[...], gm_ref, wgate_ref, wba_ref, wbb_ref, wout_ref)
    h2_ref[...] = h2
    q_ref[...] = _dot(_rms(h2, gq_ref[...]).astype(BF16), wq_ref[...])


def _back_sample_a_call(h, oa, ob, w):
    t = h.shape[0]
    tm = TOKEN_TILE
    tok = lambda width: pl.BlockSpec((tm, width), lambda i: (i, 0))
    return pl.pallas_call(
        _back_sample_a_kernel,
        grid=(t // tm,),
        in_specs=[tok(D_MODEL), tok(Q_A_W), tok(KV_B_W),
                  _const_spec((1, D_MODEL)), _const_spec((D_MODEL, 2 * D_MODEL), (0, 1)), _const_spec((Q_A_W, D_MODEL)),
                  _const_spec((KV_B_W, D_MODEL)), _const_spec((D_MODEL, D_MODEL)),
                  _const_spec((1, D_MODEL)), _const_spec((D_MODEL, MEM_W))],
        out_specs=[tok(D_MODEL), tok(MEM_W)],
        out_shape=[jax.ShapeDtypeStruct((t, D_MODEL), F32), jax.ShapeDtypeStruct((t, MEM_W), F32)],
        compiler_params=pltpu.CompilerParams(dimension_semantics=("parallel",), vmem_limit_bytes=VMEM_LIMIT),
        name="back_sample_mix",
    )(h, oa, ob, w["mix_norm"], w["w_in"], w["w_branch_a"], w["w_branch_b"], w["w_out"],
      w["mem_q_norm"], w["w_mem_q"])


CROSS_BATCH_TILE = 8


def _cross_sample_kernel(q_ref, mk_ref, mv_ref, o_ref):
    nrow = MEM_HEADS * SAMPLE_T
    qhead = lax.broadcasted_iota(jnp.int32, (nrow, N_MEM * MEM_HEADS), 0) >> (SAMPLE_T.bit_length() - 1)
    khead = lax.broadcasted_iota(jnp.int32, (nrow, N_MEM * MEM_HEADS), 1) & (MEM_HEADS - 1)
    own_head = qhead == khead
    def element(bi):
        rows = slice(bi * SAMPLE_T, (bi + 1) * SAMPLE_T)
        qs = jnp.concatenate(
            [q_ref[rows, hd * MEM_HEAD_DIM:(hd + 1) * MEM_HEAD_DIM] for hd in range(MEM_HEADS)], axis=0)
        s = _dot_nt(qs, mk_ref[bi]) * MEM_SCALE
        yield
        p = _softmax_rows(jnp.where(own_head, s, NEG_INF))
        yield
        o = _dot(p, mv_ref[bi])
        for hd in range(MEM_HEADS):
            o_ref[rows, hd * MEM_HEAD_DIM:(hd + 1) * MEM_HEAD_DIM] = o[hd * SAMPLE_T:(hd + 1) * SAMPLE_T]
        yield

    _issue_skewed([element(bi) for bi in range(CROSS_BATCH_TILE)])


def _cross_sample_call(q, mk, mv):
    t = q.shape[0]
    bt = CROSS_BATCH_TILE
    tok = pl.BlockSpec((bt * SAMPLE_T, MEM_W), lambda i: (i, 0))
    mem = pl.BlockSpec((bt, N_MEM * MEM_HEADS, MEM_HEAD_DIM), lambda i: (i, 0, 0))
    return pl.pallas_call(
        _cross_sample_kernel,
        grid=(mk.shape[0] // bt,),
        in_specs=[tok, mem, mem],
        out_specs=tok,
        out_shape=jax.ShapeDtypeStruct((t, MEM_W), F32),
        compiler_params=pltpu.CompilerParams(dimension_semantics=("parallel",), vmem_limit_bytes=VMEM_LIMIT),
        name="cross_sample",
    )(q, mk, mv)


def _back_sample_b_kernel(h2_ref, oc_ref, wo_ref, g2_ref, wg_ref, wu_ref, wd_ref, gf_ref, y_ref):
    blocks = _row_blocks(h2_ref, BACK_ROW_PARTS)
    h3s = [h2_ref[rb, :] + _dot(oc_ref[rb, :].astype(BF16), wo_ref[...]) for rb in blocks]
    for rb, y in zip(blocks, _ffn_half(h3s, g2_ref, wg_ref, wu_ref, wd_ref)):
        y_ref[rb, :] = _rms(y, gf_ref[...])


def _back_sample_b_call(h2, oc, w):
    t = h2.shape[0]
    tm = TOKEN_TILE
    tok = lambda width: pl.BlockSpec((tm, width), lambda i: (i, 0))
    return pl.pallas_call(
        _back_sample_b_kernel,
        grid=(t // tm,),
        in_specs=[tok(D_MODEL), tok(MEM_W), _const_spec((MEM_W, D_MODEL)),
                  _const_spec((1, D_MODEL)), _const_spec((D_MODEL, D_FF)), _const_spec((D_MODEL, D_FF)),
                  _const_spec((D_FF, D_MODEL)), _const_spec((1, D_MODEL))],
        out_specs=tok(D_MODEL),
        out_shape=jax.ShapeDtypeStruct((t, D_MODEL), F32),
        compiler_params=pltpu.CompilerParams(dimension_semantics=("parallel",), vmem_limit_bytes=VMEM_LIMIT),
        name="back_sample_ffn",
    )(h2, oc, w["w_mem_o"], w["ffn2_norm"], w["ffn2_wg"], w["ffn2_wu"], w["ffn2_wd"], w["final_norm"])


def _mem_kv_kernel(mem_ref, g_ref, wk_ref, wv_ref, k32_ref, v32_ref, k16_ref, v16_ref):
    u = _rms(mem_ref[...], g_ref[...]).astype(BF16)
    k = _dot(u, wk_ref[...])
    v = _dot(u, wv_ref[...])
    rows = mem_ref.shape[0]
    for hd in range(MEM_HEADS):
        cols = slice(hd * MEM_HEAD_DIM, (hd + 1) * MEM_HEAD_DIM)
        k32_ref[pl.ds(hd, rows, stride=MEM_HEADS), :] = k[:, cols]
        v32_ref[pl.ds(hd, rows, stride=MEM_HEADS), :] = v[:, cols]
    k16_ref[...] = k.astype(BF16)
    v16_ref[...] = v.astype(BF16)


def _mem_kv_call(mem, w):
    rows = mem.shape[0]
    full = lambda width: pl.BlockSpec((rows, width), lambda i: (0, 0))
    return pl.pallas_call(
        _mem_kv_kernel,
        grid=(1,),
        in_specs=[full(D_MODEL), _const_spec((1, D_MODEL)), _const_spec((D_MODEL, MEM_W)),
                  _const_spec((D_MODEL, MEM_W))],
        out_specs=[pl.BlockSpec((rows * MEM_HEADS, MEM_HEAD_DIM), lambda i: (0, 0))] * 2 + [full(MEM_W)] * 2,
        out_shape=([jax.ShapeDtypeStruct((rows * MEM_HEADS, MEM_HEAD_DIM), F32)] * 2
                   + [jax.ShapeDtypeStruct((rows, MEM_W), BF16)] * 2),
        compiler_params=pltpu.CompilerParams(dimension_semantics=("arbitrary",), vmem_limit_bytes=VMEM_LIMIT),
        name="mem_kv",
    )(mem, w["mem_kv_norm"], w["w_mem_k"], w["w_mem_v"])


def _rope_tables(pos):
    inv_freq = jnp.power(jnp.float32(ROPE_THETA), -jnp.arange(ROT_HALF, dtype=jnp.float32) / ROT_HALF)
    ang = pos.astype(jnp.float32)[:, None] * inv_freq[None, :]
    lane = np.arange(LANES)
    within = lane % HEAD_DIM
    first = within < ROT_HALF
    second = (within >= ROT_HALF) & (within < 2 * ROT_HALF)
    pick_cos = np.zeros((ROT_HALF, LANES), np.float32)
    pick_sin = np.zeros((ROT_HALF, LANES), np.float32)
    pick_cos[within[first | second] % ROT_HALF, lane[first | second]] = 1.0
    pick_sin[within[first] % ROT_HALF, lane[first]] = -1.0
    pick_sin[within[second] % ROT_HALF, lane[second]] = 1.0
    spread = functools.partial(jnp.dot, precision=lax.Precision.HIGHEST)
    return (spread(jnp.cos(ang), pick_cos) + np.where(first | second, 0.0, 1.0).astype(np.float32)[None, :],
            spread(jnp.sin(ang), pick_sin))


def _prep_weights(p):
    bf = lambda x: x.astype(BF16)
    vec = lambda x: x.reshape(1, -1)
    return {
        "ffn1_norm": vec(p["ffn1_norm"]), "ffn1_wg": bf(p["ffn1_w_gate"]), "ffn1_wu": bf(p["ffn1_w_up"]),
        "ffn1_wd": bf(p["ffn1_w_down"]),
        "mix_norm": vec(p["mix_norm"]), "w_in": bf(p["w_in"]),
        "w_branch_a": bf(p["w_branch_a"].reshape(N_HEADS_A, HEAD_DIM, D_MODEL)[np.array(HEAD_PERM_A)].reshape(
            Q_A_W, D_MODEL)),
        "w_branch_b": bf(p["w_branch_b"]), "w_out": bf(p["w_out"]),
        "mem_q_norm": vec(p["mem_q_norm"]), "mem_kv_norm": vec(p["mem_kv_norm"]),
        "w_mem_q": bf(p["w_mem_q"]), "w_mem_k": bf(p["w_mem_k"]), "w_mem_v": bf(p["w_mem_v"]),
        "w_mem_o": bf(p["w_mem_o"]),
        "ffn2_norm": vec(p["ffn2_norm"]), "ffn2_wg": bf(p["ffn2_w_gate"]), "ffn2_wu": bf(p["ffn2_w_up"]),
        "ffn2_wd": bf(p["ffn2_w_down"]), "final_norm": vec(p["final_norm"]),
    }


def kernel(x_prompt, x_sample, cache_swa_k, cache_swa_v, cache_dil_k, cache_dil_v, cache_mem_k, cache_mem_v, mem_prompt, ffn1_norm, ffn1_w_gate, ffn1_w_up, ffn1_w_down, mix_norm, w_in, attn_sink, w_branch_a, w_branch_b, w_out, mem_q_norm, mem_kv_norm, w_mem_q, w_mem_k, w_mem_v, w_mem_o, ffn2_norm, ffn2_w_gate, ffn2_w_up, ffn2_w_down, final_norm):
    depth = ffn1_norm.shape[0]
    assert depth == 1
    batch, seq, d = x_prompt.shape
    nb, t_new, _ = x_sample.shape
    assert d == D_MODEL and t_new == SAMPLE_T and seq % (TOKEN_TILE * DILATIONS[-1]) == 0 and seq % ATTN_CHUNK == 0
    assert cache_swa_k.shape[2] == CACHE_A and cache_dil_k.shape[2] == CACHE_B
    layer = lambda x: x[0]
    w = _prep_weights(dict(
        ffn1_norm=layer(ffn1_norm), ffn1_w_gate=layer(ffn1_w_gate), ffn1_w_up=layer(ffn1_w_up),
        ffn1_w_down=layer(ffn1_w_down), mix_norm=layer(mix_norm), w_in=layer(w_in),
        w_branch_a=layer(w_branch_a), w_branch_b=layer(w_branch_b), w_out=layer(w_out),
        mem_q_norm=layer(mem_q_norm), mem_kv_norm=layer(mem_kv_norm), w_mem_q=layer(w_mem_q),
        w_mem_k=layer(w_mem_k), w_mem_v=layer(w_mem_v), w_mem_o=layer(w_mem_o), ffn2_norm=layer(ffn2_norm),
        ffn2_w_gate=layer(ffn2_w_gate), ffn2_w_up=layer(ffn2_w_up), ffn2_w_down=layer(ffn2_w_down),
        final_norm=final_norm))
    sink = layer(attn_sink)

    ts = nb * t_new
    pos_s = PAST_LEN + (jnp.arange(TOKEN_TILE, dtype=jnp.int32) % t_new)
    tables_s = _rope_tables(pos_s)
    (h_s, q32, ka32_s, va32_s, kb32_s, vb32_s) = _front_call(x_sample.reshape(ts, d), tables_s, w, sample=True)
    by_pos = lambda c: jnp.transpose(layer(c), (0, 2, 3, 1)).reshape(nb, c.shape[3] * c.shape[4], c.shape[2])
    by_slot_head = lambda c: layer(c).reshape(nb, N_MEM * MEM_HEADS, MEM_HEAD_DIM)

    oa_s, ob_s = _attn_sample_call(
        sink, q32, ka32_s, va32_s, kb32_s, vb32_s,
        by_pos(cache_swa_k), by_pos(cache_swa_v), by_pos(cache_dil_k), by_pos(cache_dil_v))

    tp = batch * seq
    mk32, mv32, mk16, mv16 = _mem_kv_call(mem_prompt.reshape(batch * N_MEM, d), w)
    tables_p = _rope_tables(jnp.arange(seq, dtype=jnp.int32))
    h_p, main, split4, split16, tails_kva, tails_kb, tails_vb = _front_call(
        x_prompt.reshape(tp, d), tables_p, w, sample=False, batch=batch)
    oa = _attn_a_call(sink, main, batch, seq)
    groups = ((main.reshape(batch, 1, seq, MAIN_W), MAIN_QB_COL), (split4, 0), (split16, 0))
    ols = [_attn_b_call(qkv, qcol, dil) for (qkv, qcol), dil in zip(groups, DILATIONS)]
    y_p = _back_prompt_call(h_p, oa, ols, mk16.reshape(batch, N_MEM, MEM_W),
                            mv16.reshape(batch, N_MEM, MEM_W), w, seq)

    h2_s, qc_s = _back_sample_a_call(h_s, oa_s, ob_s, w)
    oc_s = _cross_sample_call(qc_s, by_slot_head(cache_mem_k), by_slot_head(cache_mem_v))
    y_s = _back_sample_b_call(h2_s, oc_s, w)

    keep_a, keep_b = min(CACHE_A, seq), min(CACHE_B, seq)
    assert max(keep_a, keep_b) <= TAIL_ROWS <= seq and TAIL_ROWS % TOKEN_TILE == 0

    def tail(x, col, keep, heads):
        part = x[:, col:col + heads * HEAD_DIM, TAIL_ROWS - keep:].reshape(batch, heads, HEAD_DIM, keep)
        return jnp.transpose(part, (0, 3, 1, 2))[None]

    new = lambda x, heads: x.reshape(1, nb, t_new, heads, HEAD_DIM)
    memo = lambda x: x.reshape(1, batch, N_MEM, MEM_HEADS, MEM_HEAD_DIM)
    return (y_p.reshape(batch, seq, d), y_s.reshape(nb, t_new, d),
            tail(tails_kva, 0, keep_a, N_KV_A), tail(tails_kva, KV_A_W, keep_a, N_KV_A),
            tail(tails_kb, 0, keep_b, N_KV_B), tail(tails_vb, 0, keep_b, N_KV_B),
            memo(mk32), memo(mv32),
            new(ka32_s, N_KV_A), new(va32_s, N_KV_A), new(kb32_s, N_KV_B), new(vb32_s, N_KV_B))
```

```python
import functools

import jax
import jax.numpy as jnp
import numpy as np
from jax import lax
from jax.experimental import pallas as pl
from jax.experimental.pallas import tpu as pltpu

F32 = jnp.float32
BF16 = jnp.bfloat16

D_MODEL = 1024
D_FF = 2816
HEAD_DIM = 64
ROT_HALF = 8
ROPE_THETA = 500000.0
ATTN_SCALE = HEAD_DIM ** -0.5
RMS_EPS = 1e-6
PAST_LEN = 16384

N_HEADS_A = 8
N_KV_A = 2
N_KV_B = 4
DILATIONS = (1, 4, 16)
BAND_STEPS = 128
Q_A_W = N_HEADS_A * HEAD_DIM
KV_A_W = N_KV_A * HEAD_DIM
Q_B_W = len(DILATIONS) * N_KV_B * HEAD_DIM
KV_B_W = N_KV_B * HEAD_DIM
QKV_W = Q_A_W + 2 * KV_A_W + Q_B_W + 2 * KV_B_W
MEM_HEADS = 4
MEM_HEAD_DIM = 128
MEM_W = MEM_HEADS * MEM_HEAD_DIM
MEM_SCALE = MEM_HEAD_DIM ** -0.5
N_MEM = 256

HEAD_PERM_A = (0, 4, 1, 5, 2, 6, 3, 7)

MAIN_W = Q_A_W + 2 * KV_A_W + 3 * KV_B_W
MAIN_KA_COL, MAIN_VA_COL = Q_A_W // KV_A_W, Q_A_W // KV_A_W + 1
MAIN_QB_COL = (Q_A_W + 2 * KV_A_W) // KV_B_W
SPLIT_W = 3 * KV_B_W
TAIL_ROWS = 2048
assert 2 * KV_A_W == KV_B_W

LANES = 128
BLOCK = 128
TOKEN_TILE = 512
FRONT_ROW_PARTS = 2
BACK_ROW_PARTS = 1
ATTN_CHUNK = 4096
FF_CHUNKS = ((0, 1536), (1536, 2816))
VMEM_LIMIT = 60 * 1024 * 1024
NEG_INF = float("-inf")


def _rms(x, g):
    ms = jnp.mean(x * x, axis=-1, keepdims=True)
    return x * lax.rsqrt(ms + RMS_EPS) * g


def _sigmoid(x):
    return 0.5 * jnp.tanh(0.5 * x) + 0.5


def _dot(a, b):
    return jnp.dot(a, b, preferred_element_type=F32)


def _dot_nt(a, b):
    return lax.dot_general(a, b, (((1,), (1,)), ((), ())), preferred_element_type=F32)


def _ffn_half(xs, g_ref, wg_ref, wu_ref, wd_ref):
    us = [_rms(x, g_ref[...]).astype(BF16) for x in xs]
    accs = [None] * len(xs)
    for lo, hi in FF_CHUNKS:
        gates = [_dot(u, wg_ref[:, lo:hi]) for u in us]
        ups = [_dot(u, wu_ref[:, lo:hi]) for u in us]
        acts = [(gate * _sigmoid(gate) * up).astype(BF16) for gate, up in zip(gates, ups)]
        for i, act in enumerate(acts):
            part = _dot(act, wd_ref[lo:hi, :])
            accs[i] = part if accs[i] is None else accs[i] + part
    return [x + 0.5 * acc for x, acc in zip(xs, accs)]


def _row_blocks(ref, parts):
    n = ref.shape[0] // parts
    return [slice(i * n, (i + 1) * n) for i in range(parts)]


def _swap_halves(x):
    return pltpu.roll(x, HEAD_DIM, 1)


def _low_half():
    return lax.broadcasted_iota(jnp.int32, (1, LANES), 1) < HEAD_DIM


def _pair_kv_groups(chunks):
    low = _low_half()
    c0, c1, c2, c3 = chunks
    return [jnp.where(low, c0, _swap_halves(c2)), jnp.where(low, _swap_halves(c0), c2),
            jnp.where(low, c1, _swap_halves(c3)), jnp.where(low, _swap_halves(c1), c3)]


def _merge_pairs(per_head):
    low = _low_half()
    return [jnp.where(low, per_head[2 * c], per_head[2 * c + 1]) for c in range(N_HEADS_A // 2)]


def _front_kernel(x_ref, cos_ref, sin_ref, g1_ref, wg_ref, wu_ref, wd_ref, gm_ref, wqkv_ref,
                  h_ref, *rest, sample, tiles_per_seq=None, tail_tiles=None):
    blocks = _row_blocks(x_ref, FRONT_ROW_PARTS)
    hs = _ffn_half([x_ref[rb, :] for rb in blocks], g1_ref, wg_ref, wu_ref, wd_ref)
    for rb, h in zip(blocks, hs):
        h_ref[rb, :] = h
    ns = [_rms(h, gm_ref[...]).astype(BF16) for h in hs]
    zs = [_dot(n, wqkv_ref[...]) for n in ns]
    for rb, z in zip(blocks, zs):
        _front_emit(rb, z, cos_ref, sin_ref, rest, sample)
    if not sample:
        tail_refs, tail_scratch = rest[3:6], (rest[10], rest[8], rest[9])

        @pl.when(lax.rem(pl.program_id(0), tiles_per_seq) >= tiles_per_seq - tail_tiles)
        def _():
            for ref, scr in zip(tail_refs, tail_scratch):
                for c in range(scr.shape[0]):
                    ref[0, c * LANES:(c + 1) * LANES, :] = scr[c].T


def _front_emit(rb, z, cos_ref, sin_ref, rest, sample):
    cos, sin = cos_ref[rb, :], sin_ref[rb, :]
    lane_in_head = lax.broadcasted_iota(jnp.int32, (1, LANES), 1) & (HEAD_DIM - 1)
    first = lane_in_head < ROT_HALF

    def chunk(c):
        return z[:, c * LANES:(c + 1) * LANES]

    def rope(c):
        zc = chunk(c)
        partner = jnp.where(first, pltpu.roll(zc, LANES - ROT_HALF, 1), pltpu.roll(zc, ROT_HALF, 1))
        return zc * cos + partner * sin

    nqa, nka = Q_A_W // LANES, KV_A_W // LANES
    nqb, nkb = Q_B_W // LANES, KV_B_W // LANES
    c0 = 0
    qa = _pair_kv_groups([rope(c0 + c) * ATTN_SCALE for c in range(nqa)])
    c0 += nqa
    ka = [rope(c0 + c) for c in range(nka)]
    c0 += nka
    va = [chunk(c0 + c) for c in range(nka)]
    c0 += nka
    qb = [rope(c0 + c) * ATTN_SCALE for c in range(nqb)]
    c0 += nqb
    kb = [rope(c0 + c) for c in range(nkb)]
    c0 += nkb
    vb = [chunk(c0 + c) for c in range(nkb)]

    def store(ref, parts, dtype):
        for c, p in enumerate(parts):
            ref[rb, c * LANES:(c + 1) * LANES] = p.astype(dtype)

    def fill(scr, parts):
        for c, p in enumerate(parts):
            scr[c, rb, :] = p

    def store_split(ref, slot, scr, d):
        n = (rb.stop - rb.start) // d
        dst = slice(rb.start // d, rb.start // d + n)
        for r in range(d):
            for c in range(scr.shape[0]):
                lanes = slice(slot * KV_B_W + c * LANES, slot * KV_B_W + (c + 1) * LANES)
                ref[0, r, dst, lanes] = scr[c, pl.ds(rb.start + r, n, stride=d), :].astype(BF16)

    if sample:
        q32_ref, ka32_ref, va32_ref, kb32_ref, vb32_ref = rest
        store(q32_ref, qa + qb, F32)
        store(ka32_ref, ka, F32)
        store(va32_ref, va, F32)
        store(kb32_ref, kb, F32)
        store(vb32_ref, vb, F32)
    else:
        main_ref, s4_ref, s16_ref, _, _, _, q4_s, q16_s, kb_s, vb_s, kva_s = rest
        gw = KV_B_W // LANES
        store(main_ref, qa + ka + va + qb[0:gw] + kb + vb, BF16)
        fill(q4_s, qb[gw:2 * gw])
        fill(q16_s, qb[2 * gw:3 * gw])
        fill(kb_s, kb)
        fill(vb_s, vb)
        fill(kva_s, ka + va)
        for slot, scr4, scr16 in ((0, q4_s, q16_s), (1, kb_s, kb_s), (2, vb_s, vb_s)):
            store_split(s4_ref, slot, scr4, DILATIONS[1])
            store_split(s16_ref, slot, scr16, DILATIONS[2])


def _const_spec(shape, index=None):
    index = (0,) * len(shape) if index is None else index
    return pl.BlockSpec(shape, lambda *_: index, pipeline_mode=pl.Buffered(1))


def _front_call(x, tables, w, *, sample, batch=1):
    t = x.shape[0]
    tm = TOKEN_TILE
    nsteps = t // tm
    tpb = nsteps // batch
    seq = t // batch
    cos_t, sin_t = tables
    tab_blocks = cos_t.shape[0] // tm
    row = lambda i: (i, 0)
    tab = lambda i: (i % tab_blocks, 0)

    def tok(width, dtype):
        return jax.ShapeDtypeStruct((t, width), dtype), pl.BlockSpec((tm, width), row)

    def split(d):
        return (jax.ShapeDtypeStruct((batch, d, seq // d, SPLIT_W), BF16),
                pl.BlockSpec((1, d, tm // d, SPLIT_W), lambda i: (i // tpb, 0, i % tpb, 0)))

    outs = [tok(D_MODEL, F32)]
    scratch = []
    static = {}
    if sample:
        outs += [tok(Q_A_W + Q_B_W, F32), tok(KV_A_W, F32), tok(KV_A_W, F32), tok(KV_B_W, F32), tok(KV_B_W, F32)]
    else:
        tail_tiles = TAIL_ROWS // tm
        tails = (jax.ShapeDtypeStruct((batch, KV_B_W, TAIL_ROWS), F32),
                 pl.BlockSpec((1, KV_B_W, tm),
                              lambda i: (i // tpb, 0, jnp.maximum(i % tpb - (tpb - tail_tiles), 0))))
        outs += [tok(MAIN_W, BF16), split(DILATIONS[1]), split(DILATIONS[2])] + [tails] * 3
        scratch = [pltpu.VMEM((KV_B_W // LANES, tm, LANES), F32)] * 4 + [pltpu.VMEM((2, tm, LANES), F32)]
        static = dict(tiles_per_seq=tpb, tail_tiles=tail_tiles)
    out_shape, out_specs = zip(*outs)
    in_specs = [pl.BlockSpec((tm, D_MODEL), row),
                pl.BlockSpec((tm, LANES), tab), pl.BlockSpec((tm, LANES), tab),
                _const_spec((1, D_MODEL)), _const_spec((D_MODEL, D_FF)), _const_spec((D_MODEL, D_FF)),
                _const_spec((D_FF, D_MODEL)), _const_spec((1, D_MODEL)), _const_spec((D_MODEL, QKV_W), (0, 0))]
    return pl.pallas_call(
        functools.partial(_front_kernel, sample=sample, **static),
        grid=(nsteps,),
        in_specs=in_specs,
        out_specs=list(out_specs),
        out_shape=list(out_shape),
        scratch_shapes=scratch,
        compiler_params=pltpu.CompilerParams(dimension_semantics=("parallel" if sample else "arbitrary",),
                                             vmem_limit_bytes=VMEM_LIMIT),
        name="front_sample" if sample else "front_prompt",
    )(x, cos_t, sin_t, w["ffn1_norm"], w["ffn1_wg"], w["ffn1_wu"], w["ffn1_wd"], w["mix_norm"], w["w_in"])


def _lane_block_masks(width, block, dtype):
    lane = lax.broadcasted_iota(jnp.int32, (1, width), 1)
    return [((lane >= i * block) & (lane < (i + 1) * block)).astype(dtype) for i in range(width // block)]


def _band_mask():
    qi = lax.broadcasted_iota(jnp.int32, (BLOCK, 2 * BLOCK), 0)
    kj = lax.broadcasted_iota(jnp.int32, (BLOCK, 2 * BLOCK), 1)
    dist = qi + BLOCK - kj
    return (dist >= 0) & (dist <= BAND_STEPS), kj >= BLOCK


def _band_block(load_qbd, load_k, load_v, mask, nheads, finish):
    s = _dot_nt(load_qbd(), load_k())
    yield
    ps, ms, ls = [], [], []
    for p in range(nheads):
        sp = jnp.where(mask, s[p * BLOCK:(p + 1) * BLOCK], NEG_INF)
        m = jnp.max(sp, axis=-1, keepdims=True)
        e = jnp.exp(sp - m)
        ls.append(jnp.sum(e, axis=-1, keepdims=True))
        ms.append(m)
        ps.append(e.astype(BF16))
    pc = jnp.concatenate(ps, axis=0)
    yield
    finish(_dot(pc, load_v()), ms, ls)
    yield


def _issue_skewed(items, nstages=3):
    for tick in range(len(items) + nstages - 1):
        for s in reversed(range(nstages)):
            i = tick - s
            if 0 <= i < len(items):
                next(items[i])


def _kv_window(cur_ref, prev_ref, qb):
    if qb == 0:
        return jnp.concatenate([prev_ref[...], cur_ref[0:BLOCK]], axis=0)
    return cur_ref[(qb - 1) * BLOCK:(qb + 1) * BLOCK]


def _attn_a_kernel(sink_ref, q_ref, kc_ref, kp_ref, vc_ref, vp_ref, o_ref):
    first_chunk = pl.program_id(1) == 0
    band, in_cur = _band_mask()
    half_bf = _lane_block_masks(LANES, HEAD_DIM, BF16)

    def block(qb):
        mask = band & (in_cur | jnp.logical_not(first_chunk)) if qb == 0 else band
        rows = slice(qb * BLOCK, (qb + 1) * BLOCK)

        def load_qbd():
            return jnp.concatenate(
                [q_ref[rows, (p // 2) * LANES:(p // 2 + 1) * LANES] * half_bf[p % 2] for p in range(N_HEADS_A)],
                axis=0)

        def finish(o, ms, ls):
            normed = []
            for p in range(N_HEADS_A):
                den = ls[p] + jnp.exp(sink_ref[HEAD_PERM_A[p]] - ms[p])
                normed.append(o[p * BLOCK:(p + 1) * BLOCK] * (1.0 / den))
            for c, pair in enumerate(_merge_pairs(normed)):
                o_ref[rows, c * LANES:(c + 1) * LANES] = pair.astype(BF16)

        return _band_block(load_qbd, lambda: _kv_window(kc_ref, kp_ref, qb), lambda: _kv_window(vc_ref, vp_ref, qb),
                           mask, N_HEADS_A, finish)

    for qb in range(q_ref.shape[0] // BLOCK):
        for _ in block(qb):
            pass


def _attn_a_call(sink, main, batch, seq):
    t = main.shape[0]
    cpb = seq // ATTN_CHUNK
    bpc = ATTN_CHUNK // BLOCK
    cur = lambda col: (lambda b, c: (b * cpb + c, col))
    prev = lambda col: (lambda b, c: (b * cpb * bpc + jnp.maximum(c * bpc - 1, 0), col))
    kv = lambda col: [pl.BlockSpec((ATTN_CHUNK, KV_A_W), cur(col)), pl.BlockSpec((BLOCK, KV_A_W), prev(col))]
    return pl.pallas_call(
        _attn_a_kernel,
        grid=(batch, cpb),
        in_specs=[pl.BlockSpec(memory_space=pltpu.SMEM), pl.BlockSpec((ATTN_CHUNK, Q_A_W), cur(0))]
        + kv(MAIN_KA_COL) + kv(MAIN_VA_COL),
        out_specs=pl.BlockSpec((ATTN_CHUNK, Q_A_W), cur(0)),
        out_shape=jax.ShapeDtypeStruct((t, Q_A_W), BF16),
        compiler_params=pltpu.CompilerParams(dimension_semantics=("parallel", "parallel"),
                                             vmem_limit_bytes=VMEM_LIMIT),
        name="attn_a_prompt",
    )(sink, main, main, main, main, main)


def _attn_b_kernel(q_ref, kc_ref, kp_ref, vc_ref, vp_ref, ol_ref):
    first_chunk = pl.program_id(2) == 0
    band, in_cur = _band_mask()
    head_bf = _lane_block_masks(KV_B_W, HEAD_DIM, BF16)
    lane = lax.broadcasted_iota(jnp.int32, (1, KV_B_W), 1)

    def block(r, qb):
        mask = band & (in_cur | jnp.logical_not(first_chunk)) if qb == 0 else band
        rows = slice(qb * BLOCK, (qb + 1) * BLOCK)

        def load_qbd():
            q = q_ref[r, rows, :]
            return jnp.concatenate([q * head_bf[p] for p in range(N_KV_B)], axis=0)

        def finish(o, ms, ls):
            out = lse = None
            for p in reversed(range(N_KV_B)):
                op = o[p * BLOCK:(p + 1) * BLOCK] * (1.0 / ls[p])
                lp = jnp.broadcast_to(ms[p] + jnp.log(ls[p]), (BLOCK, KV_B_W))
                if out is None:
                    out, lse = op, lp
                else:
                    sel = lane < (p + 1) * HEAD_DIM
                    out, lse = jnp.where(sel, op, out), jnp.where(sel, lp, lse)
            ol_ref[r, rows, 0:KV_B_W] = out
            ol_ref[r, rows, KV_B_W:2 * KV_B_W] = lse

        return _band_block(load_qbd, lambda: _kv_window(kc_ref.at[r], kp_ref.at[r], qb),
                           lambda: _kv_window(vc_ref.at[r], vp_ref.at[r], qb), mask, N_KV_B, finish)

    _issue_skewed([block(r, qb) for r in range(q_ref.shape[0]) for qb in range(q_ref.shape[1] // BLOCK)])


def _attn_b_call(qkv, qcol, d):
    batch, _, rows, _ = qkv.shape
    chunk = min(ATTN_CHUNK, rows)
    rps = min(d, ATTN_CHUNK // chunk)
    cpb = rows // chunk
    bpc = chunk // BLOCK
    cur = lambda col: pl.BlockSpec((None, rps, chunk, KV_B_W), lambda b, r, c: (b, r, c, col))
    prev = lambda col: pl.BlockSpec((None, rps, BLOCK, KV_B_W),
                                    lambda b, r, c: (b, r, jnp.maximum(c * bpc - 1, 0), col))
    return pl.pallas_call(
        _attn_b_kernel,
        grid=(batch, d // rps, cpb),
        in_specs=[cur(qcol), cur(qcol + 1), prev(qcol + 1), cur(qcol + 2), prev(qcol + 2)],
        out_specs=pl.BlockSpec((None, rps, chunk, 2 * KV_B_W), lambda b, r, c: (b, r, c, 0)),
        out_shape=jax.ShapeDtypeStruct((batch, d, rows, 2 * KV_B_W), F32),
        compiler_params=pltpu.CompilerParams(dimension_semantics=("parallel", "parallel", "parallel"),
                                             vmem_limit_bytes=VMEM_LIMIT),
        name=f"attn_b_prompt_d{d}",
    )(qkv, qkv, qkv, qkv, qkv)


SAMPLE_T = 8
CACHE_A = 128
CACHE_B = 2048
SAMPLE_ELEMS = 4


def _new_rows_block(new):
    return jnp.concatenate([new, jnp.zeros((BLOCK - SAMPLE_T, new.shape[1]), F32)], axis=0)


def _cached_scores(qbd, kt, knew):
    return _dot(qbd, kt), _dot_nt(qbd, knew)


def _cached_softmax(s_c, s_n, mask_c, mask_n):
    s_c = jnp.where(mask_c, s_c, NEG_INF)
    s_n = jnp.where(mask_n, s_n, NEG_INF)
    m = jnp.maximum(jnp.max(s_c, axis=-1, keepdims=True), jnp.max(s_n, axis=-1, keepdims=True))
    e_c = jnp.exp(s_c - m)
    e_n = jnp.exp(s_n - m)
    l = jnp.sum(e_c, axis=-1, keepdims=True) + jnp.sum(e_n, axis=-1, keepdims=True)
    return e_c, e_n, m, l


def _cached_pv(e_c, e_n, vt, vnew):
    return _dot_nt(e_c, vt) + _dot(e_n, vnew)


def _attn_sample_kernel(sink_ref, q_ref, kan_ref, van_ref, kbn_ref, vbn_ref, cka_ref, cva_ref, ckb_ref, cvb_ref,
                        oa_ref, ob_ref):
    half_f = _lane_block_masks(LANES, HEAD_DIM, F32)
    lane_b = lax.broadcasted_iota(jnp.int32, (1, KV_B_W), 1)

    def dist(nrows, ncols, offset):
        r = lax.broadcasted_iota(jnp.int32, (nrows, ncols), 0)
        i = lax.broadcasted_iota(jnp.int32, (nrows, ncols), 1)
        return r, offset + (r & (SAMPLE_T - 1)) - i

    nrow_a = N_HEADS_A * SAMPLE_T
    _, dac = dist(nrow_a, CACHE_A, CACHE_A)
    _, dan = dist(nrow_a, BLOCK, 0)
    mask_ac = (dac >= 0) & (dac <= BAND_STEPS)
    mask_an = (dan >= 0) & (dan <= BAND_STEPS)
    nrow_b = len(DILATIONS) * N_KV_B * SAMPLE_T
    rows_per_group = N_KV_B * SAMPLE_T

    def mask_b(ncols, offset):
        r, db = dist(nrow_b, ncols, offset)
        dil = jnp.where(r < rows_per_group, DILATIONS[0], jnp.where(r < 2 * rows_per_group, DILATIONS[1], DILATIONS[2]))
        return (db >= 0) & (db <= BAND_STEPS * dil) & ((db & (dil - 1)) == 0)

    mask_bc = mask_b(CACHE_B, CACHE_B)
    mask_bn = mask_b(BLOCK, 0)

    sink_col = jnp.concatenate(
        [jnp.full((SAMPLE_T, 1), sink_ref[HEAD_PERM_A[p]], F32) for p in range(N_HEADS_A)], axis=0)

    def element(j):
        rows = slice(j * SAMPLE_T, (j + 1) * SAMPLE_T)
        qbd_a = jnp.concatenate(
            [q_ref[rows, (p // 2) * LANES:(p // 2 + 1) * LANES] * half_f[p % 2] for p in range(N_HEADS_A)],
            axis=0)
        qrows = []
        for g in range(len(DILATIONS)):
            for kvh in range(N_KV_B):
                c = (Q_A_W // LANES) + 2 * g + kvh // 2
                part = q_ref[rows, c * LANES:(c + 1) * LANES] * half_f[kvh % 2]
                zero = jnp.zeros_like(part)
                qrows.append(jnp.concatenate([part, zero] if kvh // 2 == 0 else [zero, part], axis=1))
        qbd_b = jnp.concatenate(qrows, axis=0)
        scores_a = _cached_scores(qbd_a, cka_ref[j], _new_rows_block(kan_ref[rows, :]))
        scores_b = _cached_scores(qbd_b, ckb_ref[j], _new_rows_block(kbn_ref[rows, :]))
        yield
        ea_c, ea_n, ma, la = _cached_softmax(*scores_a, mask_ac, mask_an)
        eb_c, eb_n, m, l = _cached_softmax(*scores_b, mask_bc, mask_bn)
        yield
        o = _cached_pv(ea_c, ea_n, cva_ref[j], _new_rows_block(van_ref[rows, :]))
        o = o * (1.0 / (la + jnp.exp(sink_col - ma)))
        per_head = [o[p * SAMPLE_T:(p + 1) * SAMPLE_T] for p in range(N_HEADS_A)]
        for c, pair in enumerate(_merge_pairs(per_head)):
            oa_ref[rows, c * LANES:(c + 1) * LANES] = pair
        o = _cached_pv(eb_c, eb_n, cvb_ref[j], _new_rows_block(vbn_ref[rows, :]))
        out = None
        for kvh in reversed(range(N_KV_B)):
            sl = [slice((g * N_KV_B + kvh) * SAMPLE_T, (g * N_KV_B + kvh + 1) * SAMPLE_T)
                  for g in range(len(DILATIONS))]
            mj = jnp.maximum(jnp.maximum(m[sl[0]], m[sl[1]]), m[sl[2]])
            ws = [jnp.exp(m[x] - mj) for x in sl]
            den = ws[0] * l[sl[0]] + ws[1] * l[sl[1]] + ws[2] * l[sl[2]]
            num = ws[0] * o[sl[0]] + ws[1] * o[sl[1]] + ws[2] * o[sl[2]]
            okv = num * (1.0 / den)
            out = okv if out is None else jnp.where(lane_b < (kvh + 1) * HEAD_DIM, okv, out)
        ob_ref[rows, :] = out
        yield

    elements = [element(j) for j in range(SAMPLE_ELEMS)]
    for _ in range(3):
        for e in elements:
            next(e)


def _attn_sample_call(sink, q32, ka32, va32, kb32, vb32, cka, cva, ckb, cvb):
    nb = cka.shape[0]
    bt = SAMPLE_ELEMS
    t = q32.shape[0]
    tok = lambda w: pl.BlockSpec((bt * SAMPLE_T, w), lambda i: (i, 0))
    cache = lambda w, n: pl.BlockSpec((bt, w, n), lambda i: (i, 0, 0))
    return pl.pallas_call(
        _attn_sample_kernel,
        grid=(nb // bt,),
        in_specs=[pl.BlockSpec(memory_space=pltpu.SMEM), tok(Q_A_W + Q_B_W),
                  tok(KV_A_W), tok(KV_A_W), tok(KV_B_W), tok(KV_B_W),
                  cache(KV_A_W, CACHE_A), cache(KV_A_W, CACHE_A), cache(KV_B_W, CACHE_B), cache(KV_B_W, CACHE_B)],
        out_specs=[tok(Q_A_W), tok(KV_B_W)],
        out_shape=[jax.ShapeDtypeStruct((t, Q_A_W), F32), jax.ShapeDtypeStruct((t, KV_B_W), F32)],
        compiler_params=pltpu.CompilerParams(dimension_semantics=("parallel",), vmem_limit_bytes=VMEM_LIMIT),
        name="attn_sample",
    )(sink, q32, ka32, va32, kb32, vb32, cka, cva, ckb, cvb)


def _mix_out(h, oa, ob, gm_ref, wgate_ref, wba_ref, wbb_ref, wout_ref):
    n = _rms(h, gm_ref[...]).astype(BF16)
    gates = _dot(n, wgate_ref[...])
    ga = _sigmoid(gates[:, :D_MODEL])
    gb = _sigmoid(gates[:, D_MODEL:])
    mixed = ga * _dot(oa.astype(BF16), wba_ref[...]) + gb * _dot(ob.astype(BF16), wbb_ref[...])
    return h + _dot(mixed.astype(BF16), wout_ref[...])


def _combine_groups(os, ls):
    m = jnp.maximum(jnp.maximum(ls[0], ls[1]), ls[2])
    es = [jnp.exp(x - m) for x in ls]
    den = es[0] + es[1] + es[2]
    num = es[0] * os[0] + es[1] * os[1] + es[2] * os[2]
    return num * (1.0 / den)


def _merge_split(split_ref, field, scr, rb):
    d = split_ref.shape[1]
    if d == 1:
        return split_ref[0, 0, rb, field * KV_B_W:(field + 1) * KV_B_W]
    n = (rb.stop - rb.start) // d
    src = slice(rb.start // d, rb.start // d + n)
    for r in range(d):
        for c in range(scr.shape[0]):
            lanes = slice(field * KV_B_W + c * LANES, field * KV_B_W + (c + 1) * LANES)
            scr[c, pl.ds(rb.start + r, n, stride=d), :] = split_ref[0, r, src, lanes]
    return jnp.concatenate([scr[c, rb, :] for c in range(scr.shape[0])], axis=1)


def _softmax_rows(s):
    m = jnp.max(s, axis=-1, keepdims=True)
    e = jnp.exp(s - m)
    return e * (1.0 / jnp.sum(e, axis=-1, keepdims=True))


def _back_prompt_kernel(h_ref, oa_ref, ol0_ref, ol1_ref, ol2_ref, mk_ref, mv_ref,
                        gm_ref, wgate_ref, wba_ref, wbb_ref, wout_ref, gq_ref, wq_ref, wo_ref,
                        g2_ref, wg_ref, wu_ref, wd_ref, gf_ref, y_ref, o1_s, o2_s, l1_s, l2_s, ob_s):
    blocks = _row_blocks(h_ref, BACK_ROW_PARTS)
    h3s = []
    for rb in blocks:
        ob_s[rb, :] = _combine_groups(
            (_merge_split(ol0_ref, 0, None, rb), _merge_split(ol1_ref, 0, o1_s, rb),
             _merge_split(ol2_ref, 0, o2_s, rb)),
            (_merge_split(ol0_ref, 1, None, rb), _merge_split(ol1_ref, 1, l1_s, rb),
             _merge_split(ol2_ref, 1, l2_s, rb))).astype(BF16)
        h2 = _mix_out(h_ref[rb, :], oa_ref[rb, :], ob_s[rb, :], gm_ref, wgate_ref, wba_ref, wbb_ref, wout_ref)
        q = _dot(_rms(h2, gq_ref[...]).astype(BF16), wq_ref[...])
        heads = [None] * MEM_HEADS

        def head(hd):
            cols = slice(hd * MEM_HEAD_DIM, (hd + 1) * MEM_HEAD_DIM)
            s = _dot_nt(q[:, cols].astype(BF16), mk_ref[0, :, cols]) * MEM_SCALE
            yield
            e = jnp.exp(s - jnp.max(s, axis=-1, keepdims=True))
            inv = 1.0 / jnp.sum(e, axis=-1, keepdims=True)
            yield
            heads[hd] = _dot(e.astype(BF16), mv_ref[0, :, cols]) * inv
            yield

        _issue_skewed([head(hd) for hd in range(MEM_HEADS)])
        oc = jnp.concatenate(heads, axis=1).astype(BF16)
        h3s.append(h2 + _dot(oc, wo_ref[...]))
    for rb, y in zip(blocks, _ffn_half(h3s, g2_ref, wg_ref, wu_ref, wd_ref)):
        y_ref[rb, :] = _rms(y, gf_ref[...])


def _back_prompt_call(h, oa, ols, mk, mv, w, seq):
    t = h.shape[0]
    tm = TOKEN_TILE
    tiles_per_batch = seq // tm
    row = lambda i: (i, 0)
    mem = lambda i: (i // tiles_per_batch, 0, 0)
    tok = lambda width: pl.BlockSpec((tm, width), row)
    split = lambda d: pl.BlockSpec((1, d, tm // d, 2 * KV_B_W),
                                   lambda i: (i // tiles_per_batch, 0, i % tiles_per_batch, 0))
    in_specs = ([tok(D_MODEL), tok(Q_A_W)] + [split(d) for d in DILATIONS]
                + [pl.BlockSpec((1, N_MEM, MEM_W), mem), pl.BlockSpec((1, N_MEM, MEM_W), mem)]
                + [_const_spec((1, D_MODEL)), _const_spec((D_MODEL, 2 * D_MODEL), (0, 1)), _const_spec((Q_A_W, D_MODEL)),
                   _const_spec((KV_B_W, D_MODEL)), _const_spec((D_MODEL, D_MODEL)),
                   _const_spec((1, D_MODEL)), _const_spec((D_MODEL, MEM_W)), _const_spec((MEM_W, D_MODEL)),
                   _const_spec((1, D_MODEL)), _const_spec((D_MODEL, D_FF)), _const_spec((D_MODEL, D_FF)),
                   _const_spec((D_FF, D_MODEL)), _const_spec((1, D_MODEL))])
    return pl.pallas_call(
        _back_prompt_kernel,
        grid=(t // tm,),
        in_specs=in_specs,
        out_specs=tok(D_MODEL),
        out_shape=jax.ShapeDtypeStruct((t, D_MODEL), F32),
        scratch_shapes=[pltpu.VMEM((KV_B_W // LANES, tm, LANES), F32)] * 4 + [pltpu.VMEM((tm, KV_B_W), BF16)],
        compiler_params=pltpu.CompilerParams(dimension_semantics=("parallel",), vmem_limit_bytes=VMEM_LIMIT),
        name="back_prompt",
    )(h, oa, *ols, mk, mv, w["mix_norm"], w["w_in"], w["w_branch_a"], w["w_branch_b"], w["w_out"],
      w["mem_q_norm"], w["w_mem_q"], w["w_mem_o"], w["ffn2_norm"], w["ffn2_wg"], w["ffn2_wu"], w["ffn2_wd"],
      w["final_norm"])


def _back_sample_a_kernel(h_ref, oa_ref, ob_ref, gm_ref, wgate_ref, wba_ref, wbb_ref, wout_ref, gq_ref, wq_ref,
                          h2_ref, q_ref):
    h2 = _mix_out(h_ref[...], oa_ref[...], ob_ref[...], gm_ref, wgate_ref, wba_ref, wbb_ref, wout_ref)
    h2_ref[...] = h2
    q_ref[...] = _dot(_rms(h2, gq_ref[...]).astype(BF16), wq_ref[...])


def _back_sample_a_call(h, oa, ob, w):
    t = h.shape[0]
    tm = TOKEN_TILE
    tok = lambda width: pl.BlockSpec((tm, width), lambda i: (i, 0))
    return pl.pallas_call(
        _back_sample_a_kernel,
        grid=(t // tm,),
        in_specs=[tok(D_MODEL), tok(Q_A_W), tok(KV_B_W),
                  _const_spec((1, D_MODEL)), _const_spec((D_MODEL, 2 * D_MODEL), (0, 1)), _const_spec((Q_A_W, D_MODEL)),
                  _const_spec((KV_B_W, D_MODEL)), _const_spec((D_MODEL, D_MODEL)),
                  _const_spec((1, D_MODEL)), _const_spec((D_MODEL, MEM_W))],
        out_specs=[tok(D_MODEL), tok(MEM_W)],
        out_shape=[jax.ShapeDtypeStruct((t, D_MODEL), F32), jax.ShapeDtypeStruct((t, MEM_W), F32)],
        compiler_params=pltpu.CompilerParams(dimension_semantics=("parallel",), vmem_limit_bytes=VMEM_LIMIT),
        name="back_sample_mix",
    )(h, oa, ob, w["mix_norm"], w["w_in"], w["w_branch_a"], w["w_branch_b"], w["w_out"],
      w["mem_q_norm"], w["w_mem_q"])


CROSS_BATCH_TILE = 8


def _cross_sample_kernel(q_ref, mk_ref, mv_ref, o_ref):
    nrow = MEM_HEADS * SAMPLE_T
    qhead = lax.broadcasted_iota(jnp.int32, (nrow, N_MEM * MEM_HEADS), 0) >> (SAMPLE_T.bit_length() - 1)
    khead = lax.broadcasted_iota(jnp.int32, (nrow, N_MEM * MEM_HEADS), 1) & (MEM_HEADS - 1)
    own_head = qhead == khead
    def element(bi):
        rows = slice(bi * SAMPLE_T, (bi + 1) * SAMPLE_T)
        qs = jnp.concatenate(
            [q_ref[rows, hd * MEM_HEAD_DIM:(hd + 1) * MEM_HEAD_DIM] for hd in range(MEM_HEADS)], axis=0)
        s = _dot_nt(qs, mk_ref[bi]) * MEM_SCALE
        yield
        p = _softmax_rows(jnp.where(own_head, s, NEG_INF))
        yield
        o = _dot(p, mv_ref[bi])
        for hd in range(MEM_HEADS):
            o_ref[rows, hd * MEM_HEAD_DIM:(hd + 1) * MEM_HEAD_DIM] = o[hd * SAMPLE_T:(hd + 1) * SAMPLE_T]
        yield

    _issue_skewed([element(bi) for bi in range(CROSS_BATCH_TILE)])


def _cross_sample_call(q, mk, mv):
    t = q.shape[0]
    bt = CROSS_BATCH_TILE
    tok = pl.BlockSpec((bt * SAMPLE_T, MEM_W), lambda i: (i, 0))
    mem = pl.BlockSpec((bt, N_MEM * MEM_HEADS, MEM_HEAD_DIM), lambda i: (i, 0, 0))
    return pl.pallas_call(
        _cross_sample_kernel,
        grid=(mk.shape[0] // bt,),
        in_specs=[tok, mem, mem],
        out_specs=tok,
        out_shape=jax.ShapeDtypeStruct((t, MEM_W), F32),
        compiler_params=pltpu.CompilerParams(dimension_semantics=("parallel",), vmem_limit_bytes=VMEM_LIMIT),
        name="cross_sample",
    )(q, mk, mv)


def _back_sample_b_kernel(h2_ref, oc_ref, wo_ref, g2_ref, wg_ref, wu_ref, wd_ref, gf_ref, y_ref):
    blocks = _row_blocks(h2_ref, BACK_ROW_PARTS)
    h3s = [h2_ref[rb, :] + _dot(oc_ref[rb, :].astype(BF16), wo_ref[...]) for rb in blocks]
    for rb, y in zip(blocks, _ffn_half(h3s, g2_ref, wg_ref, wu_ref, wd_ref)):
        y_ref[rb, :] = _rms(y, gf_ref[...])


def _back_sample_b_call(h2, oc, w):
    t = h2.shape[0]
    tm = TOKEN_TILE
    tok = lambda width: pl.BlockSpec((tm, width), lambda i: (i, 0))
    return pl.pallas_call(
        _back_sample_b_kernel,
        grid=(t // tm,),
        in_specs=[tok(D_MODEL), tok(MEM_W), _const_spec((MEM_W, D_MODEL)),
                  _const_spec((1, D_MODEL)), _const_spec((D_MODEL, D_FF)), _const_spec((D_MODEL, D_FF)),
                  _const_spec((D_FF, D_MODEL)), _const_spec((1, D_MODEL))],
        out_specs=tok(D_MODEL),
        out_shape=jax.ShapeDtypeStruct((t, D_MODEL), F32),
        compiler_params=pltpu.CompilerParams(dimension_semantics=("parallel",), vmem_limit_bytes=VMEM_LIMIT),
        name="back_sample_ffn",
    )(h2, oc, w["w_mem_o"], w["ffn2_norm"], w["ffn2_wg"], w["ffn2_wu"], w["ffn2_wd"], w["final_norm"])


def _mem_kv_kernel(mem_ref, g_ref, wk_ref, wv_ref, k32_ref, v32_ref, k16_ref, v16_ref):
    u = _rms(mem_ref[...], g_ref[...]).astype(BF16)
    k = _dot(u, wk_ref[...])
    v = _dot(u, wv_ref[...])
    rows = mem_ref.shape[0]
    for hd in range(MEM_HEADS):
        cols = slice(hd * MEM_HEAD_DIM, (hd + 1) * MEM_HEAD_DIM)
        k32_ref[pl.ds(hd, rows, stride=MEM_HEADS), :] = k[:, cols]
        v32_ref[pl.ds(hd, rows, stride=MEM_HEADS), :] = v[:, cols]
    k16_ref[...] = k.astype(BF16)
    v16_ref[...] = v.astype(BF16)


def _mem_kv_call(mem, w):
    rows = mem.shape[0]
    full = lambda width: pl.BlockSpec((rows, width), lambda i: (0, 0))
    return pl.pallas_call(
        _mem_kv_kernel,
        grid=(1,),
        in_specs=[full(D_MODEL), _const_spec((1, D_MODEL)), _const_spec((D_MODEL, MEM_W)),
                  _const_spec((D_MODEL, MEM_W))],
        out_specs=[pl.BlockSpec((rows * MEM_HEADS, MEM_HEAD_DIM), lambda i: (0, 0))] * 2 + [full(MEM_W)] * 2,
        out_shape=([jax.ShapeDtypeStruct((rows * MEM_HEADS, MEM_HEAD_DIM), F32)] * 2
                   + [jax.ShapeDtypeStruct((rows, MEM_W), BF16)] * 2),
        compiler_params=pltpu.CompilerParams(dimension_semantics=("arbitrary",), vmem_limit_bytes=VMEM_LIMIT),
        name="mem_kv",
    )(mem, w["mem_kv_norm"], w["w_mem_k"], w["w_mem_v"])


def _rope_tables(pos):
    inv_freq = jnp.power(jnp.float32(ROPE_THETA), -jnp.arange(ROT_HALF, dtype=jnp.float32) / ROT_HALF)
    ang = pos.astype(jnp.float32)[:, None] * inv_freq[None, :]
    lane = np.arange(LANES)
    within = lane % HEAD_DIM
    first = within < ROT_HALF
    second = (within >= ROT_HALF) & (within < 2 * ROT_HALF)
    pick_cos = np.zeros((ROT_HALF, LANES), np.float32)
    pick_sin = np.zeros((ROT_HALF, LANES), np.float32)
    pick_cos[within[first | second] % ROT_HALF, lane[first | second]] = 1.0
    pick_sin[within[first] % ROT_HALF, lane[first]] = -1.0
    pick_sin[within[second] % ROT_HALF, lane[second]] = 1.0
    spread = functools.partial(jnp.dot, precision=lax.Precision.HIGHEST)
    return (spread(jnp.cos(ang), pick_cos) + np.where(first | second, 0.0, 1.0).astype(np.float32)[None, :],
            spread(jnp.sin(ang), pick_sin))


def _prep_weights(p):
    bf = lambda x: x.astype(BF16)
    vec = lambda x: x.reshape(1, -1)
    return {
        "ffn1_norm": vec(p["ffn1_norm"]), "ffn1_wg": bf(p["ffn1_w_gate"]), "ffn1_wu": bf(p["ffn1_w_up"]),
        "ffn1_wd": bf(p["ffn1_w_down"]),
        "mix_norm": vec(p["mix_norm"]), "w_in": bf(p["w_in"]),
        "w_branch_a": bf(p["w_branch_a"].reshape(N_HEADS_A, HEAD_DIM, D_MODEL)[np.array(HEAD_PERM_A)].reshape(
            Q_A_W, D_MODEL)),
        "w_branch_b": bf(p["w_branch_b"]), "w_out": bf(p["w_out"]),
        "mem_q_norm": vec(p["mem_q_norm"]), "mem_kv_norm": vec(p["mem_kv_norm"]),
        "w_mem_q": bf(p["w_mem_q"]), "w_mem_k": bf(p["w_mem_k"]), "w_mem_v": bf(p["w_mem_v"]),
        "w_mem_o": bf(p["w_mem_o"]),
        "ffn2_norm": vec(p["ffn2_norm"]), "ffn2_wg": bf(p["ffn2_w_gate"]), "ffn2_wu": bf(p["ffn2_w_up"]),
        "ffn2_wd": bf(p["ffn2_w_down"]), "final_norm": vec(p["final_norm"]),
    }


def kernel(x_prompt, x_sample, cache_swa_k, cache_swa_v, cache_dil_k, cache_dil_v, cache_mem_k, cache_mem_v, mem_prompt, ffn1_norm, ffn1_w_gate, ffn1_w_up, ffn1_w_down, mix_norm, w_in, attn_sink, w_branch_a, w_branch_b, w_out, mem_q_norm, mem_kv_norm, w_mem_q, w_mem_k, w_mem_v, w_mem_o, ffn2_norm, ffn2_w_gate, ffn2_w_up, ffn2_w_down, final_norm):
    depth = ffn1_norm.shape[0]
    assert depth == 1
    batch, seq, d = x_prompt.shape
    nb, t_new, _ = x_sample.shape
    assert d == D_MODEL and t_new == SAMPLE_T and seq % (TOKEN_TILE * DILATIONS[-1]) == 0 and seq % ATTN_CHUNK == 0
    assert cache_swa_k.shape[2] == CACHE_A and cache_dil_k.shape[2] == CACHE_B
    layer = lambda x: x[0]
    w = _prep_weights(dict(
        ffn1_norm=layer(ffn1_norm), ffn1_w_gate=layer(ffn1_w_gate), ffn1_w_up=layer(ffn1_w_up),
        ffn1_w_down=layer(ffn1_w_down), mix_norm=layer(mix_norm), w_in=layer(w_in),
        w_branch_a=layer(w_branch_a), w_branch_b=layer(w_branch_b), w_out=layer(w_out),
        mem_q_norm=layer(mem_q_norm), mem_kv_norm=layer(mem_kv_norm), w_mem_q=layer(w_mem_q),
        w_mem_k=layer(w_mem_k), w_mem_v=layer(w_mem_v), w_mem_o=layer(w_mem_o), ffn2_norm=layer(ffn2_norm),
        ffn2_w_gate=layer(ffn2_w_gate), ffn2_w_up=layer(ffn2_w_up), ffn2_w_down=layer(ffn2_w_down),
        final_norm=final_norm))
    sink = layer(attn_sink)

    ts = nb * t_new
    pos_s = PAST_LEN + (jnp.arange(TOKEN_TILE, dtype=jnp.int32) % t_new)
    tables_s = _rope_tables(pos_s)
    (h_s, q32, ka32_s, va32_s, kb32_s, vb32_s) = _front_call(x_sample.reshape(ts, d), tables_s, w, sample=True)
    by_pos = lambda c: jnp.transpose(layer(c), (0, 2, 3, 1)).reshape(nb, c.shape[3] * c.shape[4], c.shape[2])
    by_slot_head = lambda c: layer(c).reshape(nb, N_MEM * MEM_HEADS, MEM_HEAD_DIM)

    oa_s, ob_s = _attn_sample_call(
        sink, q32, ka32_s, va32_s, kb32_s, vb32_s,
        by_pos(cache_swa_k), by_pos(cache_swa_v), by_pos(cache_dil_k), by_pos(cache_dil_v))

    tp = batch * seq
    mk32, mv32, mk16, mv16 = _mem_kv_call(mem_prompt.reshape(batch * N_MEM, d), w)
    tables_p = _rope_tables(jnp.arange(seq, dtype=jnp.int32))
    h_p, main, split4, split16, tails_kva, tails_kb, tails_vb = _front_call(
        x_prompt.reshape(tp, d), tables_p, w, sample=False, batch=batch)
    oa = _attn_a_call(sink, main, batch, seq)
    groups = ((main.reshape(batch, 1, seq, MAIN_W), MAIN_QB_COL), (split4, 0), (split16, 0))
    ols = [_attn_b_call(qkv, qcol, dil) for (qkv, qcol), dil in zip(groups, DILATIONS)]
    y_p = _back_prompt_call(h_p, oa, ols, mk16.reshape(batch, N_MEM, MEM_W),
                            mv16.reshape(batch, N_MEM, MEM_W), w, seq)

    h2_s, qc_s = _back_sample_a_call(h_s, oa_s, ob_s, w)
    oc_s = _cross_sample_call(qc_s, by_slot_head(cache_mem_k), by_slot_head(cache_mem_v))
    y_s = _back_sample_b_call(h2_s, oc_s, w)

    keep_a, keep_b = min(CACHE_A, seq), min(CACHE_B, seq)
    assert max(keep_a, keep_b) <= TAIL_ROWS <= seq and TAIL_ROWS % TOKEN_TILE == 0

    def tail(x, col, keep, heads):
        part = x[:, col:col + heads * HEAD_DIM, TAIL_ROWS - keep:].reshape(batch, heads, HEAD_DIM, keep)
        return jnp.transpose(part, (0, 3, 1, 2))[None]

    new = lambda x, heads: x.reshape(1, nb, t_new, heads, HEAD_DIM)
    memo = lambda x: x.reshape(1, batch, N_MEM, MEM_HEADS, MEM_HEAD_DIM)
    return (y_p.reshape(batch, seq, d), y_s.reshape(nb, t_new, d),
            tail(tails_kva, 0, keep_a, N_KV_A), tail(tails_kva, KV_A_W, keep_a, N_KV_A),
            tail(tails_kb, 0, keep_b, N_KV_B), tail(tails_vb, 0, keep_b, N_KV_B),
            memo(mk32), memo(mv32),
            new(ka32_s, N_KV_A), new(va32_s, N_KV_A), new(kb32_s, N_KV_B), new(vb32_s, N_KV_B))
```

```python
import functools

import jax
import jax.numpy as jnp
import numpy as np
from jax import lax
from jax.experimental import pallas as pl
from jax.experimental.pallas import tpu as pltpu

F32 = jnp.float32
BF16 = jnp.bfloat16

D_MODEL = 1024
D_FF = 2816
HEAD_DIM = 64
ROT_HALF = 8
ROPE_THETA = 500000.0
ATTN_SCALE = HEAD_DIM ** -0.5
RMS_EPS = 1e-6
PAST_LEN = 16384

N_HEADS_A = 8
N_KV_A = 2
N_KV_B = 4
DILATIONS = (1, 4, 16)
BAND_STEPS = 128
Q_A_W = N_HEADS_A * HEAD_DIM
KV_A_W = N_KV_A * HEAD_DIM
Q_B_W = len(DILATIONS) * N_KV_B * HEAD_DIM
KV_B_W = N_KV_B * HEAD_DIM
QKV_W = Q_A_W + 2 * KV_A_W + Q_B_W + 2 * KV_B_W
MEM_HEADS = 4
MEM_HEAD_DIM = 128
MEM_W = MEM_HEADS * MEM_HEAD_DIM
MEM_SCALE = MEM_HEAD_DIM ** -0.5
N_MEM = 256

HEAD_PERM_A = (0, 4, 1, 5, 2, 6, 3, 7)

MAIN_W = Q_A_W + 2 * KV_A_W + 3 * KV_B_W
MAIN_KA_COL, MAIN_VA_COL = Q_A_W // KV_A_W, Q_A_W // KV_A_W + 1
MAIN_QB_COL = (Q_A_W + 2 * KV_A_W) // KV_B_W
SPLIT_W = 3 * KV_B_W
TAIL_ROWS = 2048
assert 2 * KV_A_W == KV_B_W

LANES = 128
BLOCK = 128
TOKEN_TILE = 512
FRONT_ROW_PARTS = 2
BACK_ROW_PARTS = 1
ATTN_CHUNK = 2048
FF_CHUNKS = ((0, 1536), (1536, 2816))
VMEM_LIMIT = 60 * 1024 * 1024
NEG_INF = float("-inf")


def _rms(x, g):
    ms = jnp.mean(x * x, axis=-1, keepdims=True)
    return x * lax.rsqrt(ms + RMS_EPS) * g


def _sigmoid(x):
    return 0.5 * jnp.tanh(0.5 * x) + 0.5


def _dot(a, b):
    return jnp.dot(a, b, preferred_element_type=F32)


def _dot_nt(a, b):
    return lax.dot_general(a, b, (((1,), (1,)), ((), ())), preferred_element_type=F32)


def _ffn_half(xs, g_ref, wg_ref, wu_ref, wd_ref):
    us = [_rms(x, g_ref[...]).astype(BF16) for x in xs]
    accs = [None] * len(xs)
    for lo, hi in FF_CHUNKS:
        gates = [_dot(u, wg_ref[:, lo:hi]) for u in us]
        ups = [_dot(u, wu_ref[:, lo:hi]) for u in us]
        acts = [(gate * _sigmoid(gate) * up).astype(BF16) for gate, up in zip(gates, ups)]
        for i, act in enumerate(acts):
            part = _dot(act, wd_ref[lo:hi, :])
            accs[i] = part if accs[i] is None else accs[i] + part
    return [x + 0.5 * acc for x, acc in zip(xs, accs)]


def _row_blocks(ref, parts):
    n = ref.shape[0] // parts
    return [slice(i * n, (i + 1) * n) for i in range(parts)]


def _swap_halves(x):
    return pltpu.roll(x, HEAD_DIM, 1)


def _low_half():
    return lax.broadcasted_iota(jnp.int32, (1, LANES), 1) < HEAD_DIM


def _pair_kv_groups(chunks):
    low = _low_half()
    c0, c1, c2, c3 = chunks
    return [jnp.where(low, c0, _swap_halves(c2)), jnp.where(low, _swap_halves(c0), c2),
            jnp.where(low, c1, _swap_halves(c3)), jnp.where(low, _swap_halves(c1), c3)]


def _merge_pairs(per_head):
    low = _low_half()
    return [jnp.where(low, per_head[2 * c], per_head[2 * c + 1]) for c in range(N_HEADS_A // 2)]


def _front_kernel(x_ref, cos_ref, sin_ref, g1_ref, wg_ref, wu_ref, wd_ref, gm_ref, wqkv_ref,
                  h_ref, *rest, sample, tiles_per_seq=None, tail_tiles=None):
    blocks = _row_blocks(x_ref, FRONT_ROW_PARTS)
    hs = _ffn_half([x_ref[rb, :] for rb in blocks], g1_ref, wg_ref, wu_ref, wd_ref)
    for rb, h in zip(blocks, hs):
        h_ref[rb, :] = h
    ns = [_rms(h, gm_ref[...]).astype(BF16) for h in hs]
    zs = [_dot(n, wqkv_ref[...]) for n in ns]
    for rb, z in zip(blocks, zs):
        _front_emit(rb, z, cos_ref, sin_ref, rest, sample)
    if not sample:
        tail_refs, tail_scratch = rest[3:6], (rest[10], rest[8], rest[9])

        @pl.when(lax.rem(pl.program_id(0), tiles_per_seq) >= tiles_per_seq - tail_tiles)
        def _():
            for ref, scr in zip(tail_refs, tail_scratch):
                for c in range(scr.shape[0]):
                    ref[0, c * LANES:(c + 1) * LANES, :] = scr[c].T


def _front_emit(rb, z, cos_ref, sin_ref, rest, sample):
    cos, sin = cos_ref[rb, :], sin_ref[rb, :]
    lane_in_head = lax.broadcasted_iota(jnp.int32, (1, LANES), 1) & (HEAD_DIM - 1)
    first = lane_in_head < ROT_HALF

    def chunk(c):
        return z[:, c * LANES:(c + 1) * LANES]

    def rope(c):
        zc = chunk(c)
        partner = jnp.where(first, pltpu.roll(zc, LANES - ROT_HALF, 1), pltpu.roll(zc, ROT_HALF, 1))
        return zc * cos + partner * sin

    nqa, nka = Q_A_W // LANES, KV_A_W // LANES
    nqb, nkb = Q_B_W // LANES, KV_B_W // LANES
    c0 = 0
    qa = _pair_kv_groups([rope(c0 + c) * ATTN_SCALE for c in range(nqa)])
    c0 += nqa
    ka = [rope(c0 + c) for c in range(nka)]
    c0 += nka
    va = [chunk(c0 + c) for c in range(nka)]
    c0 += nka
    qb = [rope(c0 + c) * ATTN_SCALE for c in range(nqb)]
    c0 += nqb
    kb = [rope(c0 + c) for c in range(nkb)]
    c0 += nkb
    vb = [chunk(c0 + c) for c in range(nkb)]

    def store(ref, parts, dtype):
        for c, p in enumerate(parts):
            ref[rb, c * LANES:(c + 1) * LANES] = p.astype(dtype)

    def fill(scr, parts):
        for c, p in enumerate(parts):
            scr[c, rb, :] = p

    def store_split(ref, slot, scr, d):
        n = (rb.stop - rb.start) // d
        dst = slice(rb.start // d, rb.start // d + n)
        for r in range(d):
            for c in range(scr.shape[0]):
                lanes = slice(slot * KV_B_W + c * LANES, slot * KV_B_W + (c + 1) * LANES)
                ref[0, r, dst, lanes] = scr[c, pl.ds(rb.start + r, n, stride=d), :].astype(BF16)

    if sample:
        q32_ref, ka32_ref, va32_ref, kb32_ref, vb32_ref = rest
        store(q32_ref, qa + qb, F32)
        store(ka32_ref, ka, F32)
        store(va32_ref, va, F32)
        store(kb32_ref, kb, F32)
        store(vb32_ref, vb, F32)
    else:
        main_ref, s4_ref, s16_ref, _, _, _, q4_s, q16_s, kb_s, vb_s, kva_s = rest
        gw = KV_B_W // LANES
        store(main_ref, qa + ka + va + qb[0:gw] + kb + vb, BF16)
        fill(q4_s, qb[gw:2 * gw])
        fill(q16_s, qb[2 * gw:3 * gw])
        fill(kb_s, kb)
        fill(vb_s, vb)
        fill(kva_s, ka + va)
        for slot, scr4, scr16 in ((0, q4_s, q16_s), (1, kb_s, kb_s), (2, vb_s, vb_s)):
            store_split(s4_ref, slot, scr4, DILATIONS[1])
            store_split(s16_ref, slot, scr16, DILATIONS[2])


def _const_spec(shape, index=None):
    index = (0,) * len(shape) if index is None else index
    return pl.BlockSpec(shape, lambda *_: index, pipeline_mode=pl.Buffered(1))


def _front_call(x, tables, w, *, sample, batch=1):
    t = x.shape[0]
    tm = TOKEN_TILE
    nsteps = t // tm
    tpb = nsteps // batch
    seq = t // batch
    cos_t, sin_t = tables
    tab_blocks = cos_t.shape[0] // tm
    row = lambda i: (i, 0)
    tab = lambda i: (i % tab_blocks, 0)

    def tok(width, dtype):
        return jax.ShapeDtypeStruct((t, width), dtype), pl.BlockSpec((tm, width), row)

    def split(d):
        return (jax.ShapeDtypeStruct((batch, d, seq // d, SPLIT_W), BF16),
                pl.BlockSpec((1, d, tm // d, SPLIT_W), lambda i: (i // tpb, 0, i % tpb, 0)))

    outs = [tok(D_MODEL, F32)]
    scratch = []
    static = {}
    if sample:
        outs += [tok(Q_A_W + Q_B_W, F32), tok(KV_A_W, F32), tok(KV_A_W, F32), tok(KV_B_W, F32), tok(KV_B_W, F32)]
    else:
        tail_tiles = TAIL_ROWS // tm
        tails = (jax.ShapeDtypeStruct((batch, KV_B_W, TAIL_ROWS), F32),
                 pl.BlockSpec((1, KV_B_W, tm),
                              lambda i: (i // tpb, 0, jnp.maximum(i % tpb - (tpb - tail_tiles), 0))))
        outs += [tok(MAIN_W, BF16), split(DILATIONS[1]), split(DILATIONS[2])] + [tails] * 3
        scratch = [pltpu.VMEM((KV_B_W // LANES, tm, LANES), F32)] * 4 + [pltpu.VMEM((2, tm, LANES), F32)]
        static = dict(tiles_per_seq=tpb, tail_tiles=tail_tiles)
    out_shape, out_specs = zip(*outs)
    in_specs = [pl.BlockSpec((tm, D_MODEL), row),
                pl.BlockSpec((tm, LANES), tab), pl.BlockSpec((tm, LANES), tab),
                _const_spec((1, D_MODEL)), _const_spec((D_MODEL, D_FF)), _const_spec((D_MODEL, D_FF)),
                _const_spec((D_FF, D_MODEL)), _const_spec((1, D_MODEL)), _const_spec((D_MODEL, QKV_W), (0, 0))]
    return pl.pallas_call(
        functools.partial(_front_kernel, sample=sample, **static),
        grid=(nsteps,),
        in_specs=in_specs,
        out_specs=list(out_specs),
        out_shape=list(out_shape),
        scratch_shapes=scratch,
        compiler_params=pltpu.CompilerParams(dimension_semantics=("parallel" if sample else "arbitrary",),
                                             vmem_limit_bytes=VMEM_LIMIT),
        name="front_sample" if sample else "front_prompt",
    )(x, cos_t, sin_t, w["ffn1_norm"], w["ffn1_wg"], w["ffn1_wu"], w["ffn1_wd"], w["mix_norm"], w["w_in"])


def _lane_block_masks(width, block, dtype):
    lane = lax.broadcasted_iota(jnp.int32, (1, width), 1)
    return [((lane >= i * block) & (lane < (i + 1) * block)).astype(dtype) for i in range(width // block)]


def _band_mask():
    qi = lax.broadcasted_iota(jnp.int32, (BLOCK, 2 * BLOCK), 0)
    kj = lax.broadcasted_iota(jnp.int32, (BLOCK, 2 * BLOCK), 1)
    dist = qi + BLOCK - kj
    return (dist >= 0) & (dist <= BAND_STEPS), kj >= BLOCK


def _band_block(load_qbd, load_k, load_v, mask, nheads, finish):
    s = _dot_nt(load_qbd(), load_k())
    yield
    ps, ms, ls = [], [], []
    for p in range(nheads):
        sp = jnp.where(mask, s[p * BLOCK:(p + 1) * BLOCK], NEG_INF)
        m = jnp.max(sp, axis=-1, keepdims=True)
        e = jnp.exp(sp - m)
        ls.append(jnp.sum(e, axis=-1, keepdims=True))
        ms.append(m)
        ps.append(e.astype(BF16))
    pc = jnp.concatenate(ps, axis=0)
    yield
    finish(_dot(pc, load_v()), ms, ls)
    yield


def _issue_skewed(items, nstages=3):
    for tick in range(len(items) + nstages - 1):
        for s in reversed(range(nstages)):
            i = tick - s
            if 0 <= i < len(items):
                next(items[i])


def _kv_window(cur_ref, prev_ref, qb):
    if qb == 0:
        return jnp.concatenate([prev_ref[...], cur_ref[0:BLOCK]], axis=0)
    return cur_ref[(qb - 1) * BLOCK:(qb + 1) * BLOCK]


def _attn_a_kernel(sink_ref, q_ref, kc_ref, kp_ref, vc_ref, vp_ref, o_ref):
    first_chunk = pl.program_id(1) == 0
    band, in_cur = _band_mask()
    half_bf = _lane_block_masks(LANES, HEAD_DIM, BF16)

    def block(qb):
        mask = band & (in_cur | jnp.logical_not(first_chunk)) if qb == 0 else band
        rows = slice(qb * BLOCK, (qb + 1) * BLOCK)

        def load_qbd():
            return jnp.concatenate(
                [q_ref[rows, (p // 2) * LANES:(p // 2 + 1) * LANES] * half_bf[p % 2] for p in range(N_HEADS_A)],
                axis=0)

        def finish(o, ms, ls):
            normed = []
            for p in range(N_HEADS_A):
                den = ls[p] + jnp.exp(sink_ref[HEAD_PERM_A[p]] - ms[p])
                normed.append(o[p * BLOCK:(p + 1) * BLOCK] * (1.0 / den))
            for c, pair in enumerate(_merge_pairs(normed)):
                o_ref[rows, c * LANES:(c + 1) * LANES] = pair.astype(BF16)

        return _band_block(load_qbd, lambda: _kv_window(kc_ref, kp_ref, qb), lambda: _kv_window(vc_ref, vp_ref, qb),
                           mask, N_HEADS_A, finish)

    for qb in range(q_ref.shape[0] // BLOCK):
        for _ in block(qb):
            pass


def _attn_a_call(sink, main, batch, seq):
    t = main.shape[0]
    cpb = seq // ATTN_CHUNK
    bpc = ATTN_CHUNK // BLOCK
    cur = lambda col: (lambda b, c: (b * cpb + c, col))
    prev = lambda col: (lambda b, c: (b * cpb * bpc + jnp.maximum(c * bpc - 1, 0), col))
    kv = lambda col: [pl.BlockSpec((ATTN_CHUNK, KV_A_W), cur(col)), pl.BlockSpec((BLOCK, KV_A_W), prev(col))]
    return pl.pallas_call(
        _attn_a_kernel,
        grid=(batch, cpb),
        in_specs=[pl.BlockSpec(memory_space=pltpu.SMEM), pl.BlockSpec((ATTN_CHUNK, Q_A_W), cur(0))]
        + kv(MAIN_KA_COL) + kv(MAIN_VA_COL),
        out_specs=pl.BlockSpec((ATTN_CHUNK, Q_A_W), cur(0)),
        out_shape=jax.ShapeDtypeStruct((t, Q_A_W), BF16),
        compiler_params=pltpu.CompilerParams(dimension_semantics=("parallel", "parallel"),
                                             vmem_limit_bytes=VMEM_LIMIT),
        name="attn_a_prompt",
    )(sink, main, main, main, main, main)


def _attn_b_kernel(q_ref, kc_ref, kp_ref, vc_ref, vp_ref, ol_ref):
    first_chunk = pl.program_id(2) == 0
    band, in_cur = _band_mask()
    head_bf = _lane_block_masks(KV_B_W, HEAD_DIM, BF16)
    lane = lax.broadcasted_iota(jnp.int32, (1, KV_B_W), 1)

    def block(r, qb):
        mask = band & (in_cur | jnp.logical_not(first_chunk)) if qb == 0 else band
        rows = slice(qb * BLOCK, (qb + 1) * BLOCK)

        def load_qbd():
            q = q_ref[r, rows, :]
            return jnp.concatenate([q * head_bf[p] for p in range(N_KV_B)], axis=0)

        def finish(o, ms, ls):
            out = lse = None
            for p in reversed(range(N_KV_B)):
                op = o[p * BLOCK:(p + 1) * BLOCK] * (1.0 / ls[p])
                lp = jnp.broadcast_to(ms[p] + jnp.log(ls[p]), (BLOCK, KV_B_W))
                if out is None:
                    out, lse = op, lp
                else:
                    sel = lane < (p + 1) * HEAD_DIM
                    out, lse = jnp.where(sel, op, out), jnp.where(sel, lp, lse)
            ol_ref[r, rows, 0:KV_B_W] = out
            ol_ref[r, rows, KV_B_W:2 * KV_B_W] = lse

        return _band_block(load_qbd, lambda: _kv_window(kc_ref.at[r], kp_ref.at[r], qb),
                           lambda: _kv_window(vc_ref.at[r], vp_ref.at[r], qb), mask, N_KV_B, finish)

    _issue_skewed([block(r, qb) for r in range(q_ref.shape[0]) for qb in range(q_ref.shape[1] // BLOCK)])


def _attn_b_call(qkv, qcol, d):
    batch, _, rows, _ = qkv.shape
    chunk = min(ATTN_CHUNK, rows)
    rps = min(d, ATTN_CHUNK // chunk)
    cpb = rows // chunk
    bpc = chunk // BLOCK
    cur = lambda col: pl.BlockSpec((None, rps, chunk, KV_B_W), lambda b, r, c: (b, r, c, col))
    prev = lambda col: pl.BlockSpec((None, rps, BLOCK, KV_B_W),
                                    lambda b, r, c: (b, r, jnp.maximum(c * bpc - 1, 0), col))
    return pl.pallas_call(
        _attn_b_kernel,
        grid=(batch, d // rps, cpb),
        in_specs=[cur(qcol), cur(qcol + 1), prev(qcol + 1), cur(qcol + 2), prev(qcol + 2)],
        out_specs=pl.BlockSpec((None, rps, chunk, 2 * KV_B_W), lambda b, r, c: (b, r, c, 0)),
        out_shape=jax.ShapeDtypeStruct((batch, d, rows, 2 * KV_B_W), F32),
        compiler_params=pltpu.CompilerParams(dimension_semantics=("parallel", "parallel", "parallel"),
                                             vmem_limit_bytes=VMEM_LIMIT),
        name=f"attn_b_prompt_d{d}",
    )(qkv, qkv, qkv, qkv, qkv)


SAMPLE_T = 8
CACHE_A = 128
CACHE_B = 2048
SAMPLE_ELEMS = 4


def _new_rows_block(new):
    return jnp.concatenate([new, jnp.zeros((BLOCK - SAMPLE_T, new.shape[1]), F32)], axis=0)


def _cached_scores(qbd, kt, knew):
    return _dot(qbd, kt), _dot_nt(qbd, knew)


def _cached_softmax(s_c, s_n, mask_c, mask_n):
    s_c = jnp.where(mask_c, s_c, NEG_INF)
    s_n = jnp.where(mask_n, s_n, NEG_INF)
    m = jnp.maximum(jnp.max(s_c, axis=-1, keepdims=True), jnp.max(s_n, axis=-1, keepdims=True))
    e_c = jnp.exp(s_c - m)
    e_n = jnp.exp(s_n - m)
    l = jnp.sum(e_c, axis=-1, keepdims=True) + jnp.sum(e_n, axis=-1, keepdims=True)
    return e_c, e_n, m, l


def _cached_pv(e_c, e_n, vt, vnew):
    return _dot_nt(e_c, vt) + _dot(e_n, vnew)


def _attn_sample_kernel(sink_ref, q_ref, kan_ref, van_ref, kbn_ref, vbn_ref, cka_ref, cva_ref, ckb_ref, cvb_ref,
                        oa_ref, ob_ref):
    half_f = _lane_block_masks(LANES, HEAD_DIM, F32)
    lane_b = lax.broadcasted_iota(jnp.int32, (1, KV_B_W), 1)

    def dist(nrows, ncols, offset):
        r = lax.broadcasted_iota(jnp.int32, (nrows, ncols), 0)
        i = lax.broadcasted_iota(jnp.int32, (nrows, ncols), 1)
        return r, offset + (r & (SAMPLE_T - 1)) - i

    nrow_a = N_HEADS_A * SAMPLE_T
    _, dac = dist(nrow_a, CACHE_A, CACHE_A)
    _, dan = dist(nrow_a, BLOCK, 0)
    mask_ac = (dac >= 0) & (dac <= BAND_STEPS)
    mask_an = (dan >= 0) & (dan <= BAND_STEPS)
    nrow_b = len(DILATIONS) * N_KV_B * SAMPLE_T
    rows_per_group = N_KV_B * SAMPLE_T

    def mask_b(ncols, offset):
        r, db = dist(nrow_b, ncols, offset)
        dil = jnp.where(r < rows_per_group, DILATIONS[0], jnp.where(r < 2 * rows_per_group, DILATIONS[1], DILATIONS[2]))
        return (db >= 0) & (db <= BAND_STEPS * dil) & ((db & (dil - 1)) == 0)

    mask_bc = mask_b(CACHE_B, CACHE_B)
    mask_bn = mask_b(BLOCK, 0)

    sink_col = jnp.concatenate(
        [jnp.full((SAMPLE_T, 1), sink_ref[HEAD_PERM_A[p]], F32) for p in range(N_HEADS_A)], axis=0)

    def element(j):
        rows = slice(j * SAMPLE_T, (j + 1) * SAMPLE_T)
        qbd_a = jnp.concatenate(
            [q_ref[rows, (p // 2) * LANES:(p // 2 + 1) * LANES] * half_f[p % 2] for p in range(N_HEADS_A)],
            axis=0)
        qrows = []
        for g in range(len(DILATIONS)):
            for kvh in range(N_KV_B):
                c = (Q_A_W // LANES) + 2 * g + kvh // 2
                part = q_ref[rows, c * LANES:(c + 1) * LANES] * half_f[kvh % 2]
                zero = jnp.zeros_like(part)
                qrows.append(jnp.concatenate([part, zero] if kvh // 2 == 0 else [zero, part], axis=1))
        qbd_b = jnp.concatenate(qrows, axis=0)
        scores_a = _cached_scores(qbd_a, cka_ref[j], _new_rows_block(kan_ref[rows, :]))
        scores_b = _cached_scores(qbd_b, ckb_ref[j], _new_rows_block(kbn_ref[rows, :]))
        yield
        ea_c, ea_n, ma, la = _cached_softmax(*scores_a, mask_ac, mask_an)
        eb_c, eb_n, m, l = _cached_softmax(*scores_b, mask_bc, mask_bn)
        yield
        o = _cached_pv(ea_c, ea_n, cva_ref[j], _new_rows_block(van_ref[rows, :]))
        o = o * (1.0 / (la + jnp.exp(sink_col - ma)))
        per_head = [o[p * SAMPLE_T:(p + 1) * SAMPLE_T] for p in range(N_HEADS_A)]
        for c, pair in enumerate(_merge_pairs(per_head)):
            oa_ref[rows, c * LANES:(c + 1) * LANES] = pair
        o = _cached_pv(eb_c, eb_n, cvb_ref[j], _new_rows_block(vbn_ref[rows, :]))
        out = None
        for kvh in reversed(range(N_KV_B)):
            sl = [slice((g * N_KV_B + kvh) * SAMPLE_T, (g * N_KV_B + kvh + 1) * SAMPLE_T)
                  for g in range(len(DILATIONS))]
            mj = jnp.maximum(jnp.maximum(m[sl[0]], m[sl[1]]), m[sl[2]])
            ws = [jnp.exp(m[x] - mj) for x in sl]
            den = ws[0] * l[sl[0]] + ws[1] * l[sl[1]] + ws[2] * l[sl[2]]
            num = ws[0] * o[sl[0]] + ws[1] * o[sl[1]] + ws[2] * o[sl[2]]
            okv = num * (1.0 / den)
            out = okv if out is None else jnp.where(lane_b < (kvh + 1) * HEAD_DIM, okv, out)
        ob_ref[rows, :] = out
        yield

    elements = [element(j) for j in range(SAMPLE_ELEMS)]
    for _ in range(3):
        for e in elements:
            next(e)


def _attn_sample_call(sink, q32, ka32, va32, kb32, vb32, cka, cva, ckb, cvb):
    nb = cka.shape[0]
    bt = SAMPLE_ELEMS
    t = q32.shape[0]
    tok = lambda w: pl.BlockSpec((bt * SAMPLE_T, w), lambda i: (i, 0))
    cache = lambda w, n: pl.BlockSpec((bt, w, n), lambda i: (i, 0, 0))
    return pl.pallas_call(
        _attn_sample_kernel,
        grid=(nb // bt,),
        in_specs=[pl.BlockSpec(memory_space=pltpu.SMEM), tok(Q_A_W + Q_B_W),
                  tok(KV_A_W), tok(KV_A_W), tok(KV_B_W), tok(KV_B_W),
                  cache(KV_A_W, CACHE_A), cache(KV_A_W, CACHE_A), cache(KV_B_W, CACHE_B), cache(KV_B_W, CACHE_B)],
        out_specs=[tok(Q_A_W), tok(KV_B_W)],
        out_shape=[jax.ShapeDtypeStruct((t, Q_A_W), F32), jax.ShapeDtypeStruct((t, KV_B_W), F32)],
        compiler_params=pltpu.CompilerParams(dimension_semantics=("parallel",), vmem_limit_bytes=VMEM_LIMIT),
        name="attn_sample",
    )(sink, q32, ka32, va32, kb32, vb32, cka, cva, ckb, cvb)


def _mix_out(h, oa, ob, gm_ref, wgate_ref, wba_ref, wbb_ref, wout_ref):
    n = _rms(h, gm_ref[...]).astype(BF16)
    gates = _dot(n, wgate_ref[...])
    ga = _sigmoid(gates[:, :D_MODEL])
    gb = _sigmoid(gates[:, D_MODEL:])
    mixed = ga * _dot(oa.astype(BF16), wba_ref[...]) + gb * _dot(ob.astype(BF16), wbb_ref[...])
    return h + _dot(mixed.astype(BF16), wout_ref[...])


def _combine_groups(os, ls):
    m = jnp.maximum(jnp.maximum(ls[0], ls[1]), ls[2])
    es = [jnp.exp(x - m) for x in ls]
    den = es[0] + es[1] + es[2]
    num = es[0] * os[0] + es[1] * os[1] + es[2] * os[2]
    return num * (1.0 / den)


def _merge_split(split_ref, field, scr, rb):
    d = split_ref.shape[1]
    if d == 1:
        return split_ref[0, 0, rb, field * KV_B_W:(field + 1) * KV_B_W]
    n = (rb.stop - rb.start) // d
    src = slice(rb.start // d, rb.start // d + n)
    for r in range(d):
        for c in range(scr.shape[0]):
            lanes = slice(field * KV_B_W + c * LANES, field * KV_B_W + (c + 1) * LANES)
            scr[c, pl.ds(rb.start + r, n, stride=d), :] = split_ref[0, r, src, lanes]
    return jnp.concatenate([scr[c, rb, :] for c in range(scr.shape[0])], axis=1)


def _softmax_rows(s):
    m = jnp.max(s, axis=-1, keepdims=True)
    e = jnp.exp(s - m)
    return e * (1.0 / jnp.sum(e, axis=-1, keepdims=True))


def _back_prompt_kernel(h_ref, oa_ref, ol0_ref, ol1_ref, ol2_ref, mk_ref, mv_ref,
                        gm_ref, wgate_ref, wba_ref, wbb_ref, wout_ref, gq_ref, wq_ref, wo_ref,
                        g2_ref, wg_ref, wu_ref, wd_ref, gf_ref, y_ref, o1_s, o2_s, l1_s, l2_s, ob_s):
    blocks = _row_blocks(h_ref, BACK_ROW_PARTS)
    h3s = []
    for rb in blocks:
        ob_s[rb, :] = _combine_groups(
            (_merge_split(ol0_ref, 0, None, rb), _merge_split(ol1_ref, 0, o1_s, rb),
             _merge_split(ol2_ref, 0, o2_s, rb)),
            (_merge_split(ol0_ref, 1, None, rb), _merge_split(ol1_ref, 1, l1_s, rb),
             _merge_split(ol2_ref, 1, l2_s, rb))).astype(BF16)
        h2 = _mix_out(h_ref[rb, :], oa_ref[rb, :], ob_s[rb, :], gm_ref, wgate_ref, wba_ref, wbb_ref, wout_ref)
        q = _dot(_rms(h2, gq_ref[...]).astype(BF16), wq_ref[...])
        heads = [None] * MEM_HEADS

        def head(hd):
            cols = slice(hd * MEM_HEAD_DIM, (hd + 1) * MEM_HEAD_DIM)
            s = _dot_nt(q[:, cols].astype(BF16), mk_ref[0, :, cols]) * MEM_SCALE
            yield
            e = jnp.exp(s - jnp.max(s, axis=-1, keepdims=True))
            inv = 1.0 / jnp.sum(e, axis=-1, keepdims=True)
            yield
            heads[hd] = _dot(e.astype(BF16), mv_ref[0, :, cols]) * inv
            yield

        _issue_skewed([head(hd) for hd in range(MEM_HEADS)])
        oc = jnp.concatenate(heads, axis=1).astype(BF16)
        h3s.append(h2 + _dot(oc, wo_ref[...]))
    for rb, y in zip(blocks, _ffn_half(h3s, g2_ref, wg_ref, wu_ref, wd_ref)):
        y_ref[rb, :] = _rms(y, gf_ref[...])


def _back_prompt_call(h, oa, ols, mk, mv, w, seq):
    t = h.shape[0]
    tm = TOKEN_TILE
    tiles_per_batch = seq // tm
    row = lambda i: (i, 0)
    mem = lambda i: (i // tiles_per_batch, 0, 0)
    tok = lambda width: pl.BlockSpec((tm, width), row)
    split = lambda d: pl.BlockSpec((1, d, tm // d, 2 * KV_B_W),
                                   lambda i: (i // tiles_per_batch, 0, i % tiles_per_batch, 0))
    in_specs = ([tok(D_MODEL), tok(Q_A_W)] + [split(d) for d in DILATIONS]
                + [pl.BlockSpec((1, N_MEM, MEM_W), mem), pl.BlockSpec((1, N_MEM, MEM_W), mem)]
                + [_const_spec((1, D_MODEL)), _const_spec((D_MODEL, 2 * D_MODEL), (0, 1)), _const_spec((Q_A_W, D_MODEL)),
                   _const_spec((KV_B_W, D_MODEL)), _const_spec((D_MODEL, D_MODEL)),
                   _const_spec((1, D_MODEL)), _const_spec((D_MODEL, MEM_W)), _const_spec((MEM_W, D_MODEL)),
                   _const_spec((1, D_MODEL)), _const_spec((D_MODEL, D_FF)), _const_spec((D_MODEL, D_FF)),
                   _const_spec((D_FF, D_MODEL)), _const_spec((1, D_MODEL))])
    return pl.pallas_call(
        _back_prompt_kernel,
        grid=(t // tm,),
        in_specs=in_specs,
        out_specs=tok(D_MODEL),
        out_shape=jax.ShapeDtypeStruct((t, D_MODEL), F32),
        scratch_shapes=[pltpu.VMEM((KV_B_W // LANES, tm, LANES), F32)] * 4 + [pltpu.VMEM((tm, KV_B_W), BF16)],
        compiler_params=pltpu.CompilerParams(dimension_semantics=("parallel",), vmem_limit_bytes=VMEM_LIMIT),
        name="back_prompt",
    )(h, oa, *ols, mk, mv, w["mix_norm"], w["w_in"], w["w_branch_a"], w["w_branch_b"], w["w_out"],
      w["mem_q_norm"], w["w_mem_q"], w["w_mem_o"], w["ffn2_norm"], w["ffn2_wg"], w["ffn2_wu"], w["ffn2_wd"],
      w["final_norm"])


def _back_sample_a_kernel(h_ref, oa_ref, ob_ref, gm_ref, wgate_ref, wba_ref, wbb_ref, wout_ref, gq_ref, wq_ref,
                          h2_ref, q_ref):
    h2 = _mix_out(h_ref[...], oa_ref[...], ob_ref[...], gm_ref, wgate_ref, wba_ref, wbb_ref, wout_ref)
    h2_ref[...] = h2
    q_ref[...] = _dot(_rms(h2, gq_ref[...]).astype(BF16), wq_ref[...])


def _back_sample_a_call(h, oa, ob, w):
    t = h.shape[0]
    tm = TOKEN_TILE
    tok = lambda width: pl.BlockSpec((tm, width), lambda i: (i, 0))
    return pl.pallas_call(
        _back_sample_a_kernel,
        grid=(t // tm,),
        in_specs=[tok(D_MODEL), tok(Q_A_W), tok(KV_B_W),
                  _const_spec((1, D_MODEL)), _const_spec((D_MODEL, 2 * D_MODEL), (0, 1)), _const_spec((Q_A_W, D_MODEL)),
                  _const_spec((KV_B_W, D_MODEL)), _const_spec((D_MODEL, D_MODEL)),
                  _const_spec((1, D_MODEL)), _const_spec((D_MODEL, MEM_W))],
        out_specs=[tok(D_MODEL), tok(MEM_W)],
        out_shape=[jax.ShapeDtypeStruct((t, D_MODEL), F32), jax.ShapeDtypeStruct((t, MEM_W), F32)],
        compiler_params=pltpu.CompilerParams(dimension_semantics=("parallel",), vmem_limit_bytes=VMEM_LIMIT),
        name="back_sample_mix",
    )(h, oa, ob, w["mix_norm"], w["w_in"], w["w_branch_a"], w["w_branch_b"], w["w_out"],
      w["mem_q_norm"], w["w_mem_q"])


CROSS_BATCH_TILE = 8


def _cross_sample_kernel(q_ref, mk_ref, mv_ref, o_ref):
    nrow = MEM_HEADS * SAMPLE_T
    qhead = lax.broadcasted_iota(jnp.int32, (nrow, N_MEM * MEM_HEADS), 0) >> (SAMPLE_T.bit_length() - 1)
    khead = lax.broadcasted_iota(jnp.int32, (nrow, N_MEM * MEM_HEADS), 1) & (MEM_HEADS - 1)
    own_head = qhead == khead
    def element(bi):
        rows = slice(bi * SAMPLE_T, (bi + 1) * SAMPLE_T)
        qs = jnp.concatenate(
            [q_ref[rows, hd * MEM_HEAD_DIM:(hd + 1) * MEM_HEAD_DIM] for hd in range(MEM_HEADS)], axis=0)
        s = _dot_nt(qs, mk_ref[bi]) * MEM_SCALE
        yield
        p = _softmax_rows(jnp.where(own_head, s, NEG_INF))
        yield
        o = _dot(p, mv_ref[bi])
        for hd in range(MEM_HEADS):
            o_ref[rows, hd * MEM_HEAD_DIM:(hd + 1) * MEM_HEAD_DIM] = o[hd * SAMPLE_T:(hd + 1) * SAMPLE_T]
        yield

    _issue_skewed([element(bi) for bi in range(CROSS_BATCH_TILE)])


def _cross_sample_call(q, mk, mv):
    t = q.shape[0]
    bt = CROSS_BATCH_TILE
    tok = pl.BlockSpec((bt * SAMPLE_T, MEM_W), lambda i: (i, 0))
    mem = pl.BlockSpec((bt, N_MEM * MEM_HEADS, MEM_HEAD_DIM), lambda i: (i, 0, 0))
    return pl.pallas_call(
        _cross_sample_kernel,
        grid=(mk.shape[0] // bt,),
        in_specs=[tok, mem, mem],
        out_specs=tok,
        out_shape=jax.ShapeDtypeStruct((t, MEM_W), F32),
        compiler_params=pltpu.CompilerParams(dimension_semantics=("parallel",), vmem_limit_bytes=VMEM_LIMIT),
        name="cross_sample",
    )(q, mk, mv)


def _back_sample_b_kernel(h2_ref, oc_ref, wo_ref, g2_ref, wg_ref, wu_ref, wd_ref, gf_ref, y_ref):
    blocks = _row_blocks(h2_ref, BACK_ROW_PARTS)
    h3s = [h2_ref[rb, :] + _dot(oc_ref[rb, :].astype(BF16), wo_ref[...]) for rb in blocks]
    for rb, y in zip(blocks, _ffn_half(h3s, g2_ref, wg_ref, wu_ref, wd_ref)):
        y_ref[rb, :] = _rms(y, gf_ref[...])


def _back_sample_b_call(h2, oc, w):
    t = h2.shape[0]
    tm = TOKEN_TILE
    tok = lambda width: pl.BlockSpec((tm, width), lambda i: (i, 0))
    return pl.pallas_call(
        _back_sample_b_kernel,
        grid=(t // tm,),
        in_specs=[tok(D_MODEL), tok(MEM_W), _const_spec((MEM_W, D_MODEL)),
                  _const_spec((1, D_MODEL)), _const_spec((D_MODEL, D_FF)), _const_spec((D_MODEL, D_FF)),
                  _const_spec((D_FF, D_MODEL)), _const_spec((1, D_MODEL))],
        out_specs=tok(D_MODEL),
        out_shape=jax.ShapeDtypeStruct((t, D_MODEL), F32),
        compiler_params=pltpu.CompilerParams(dimension_semantics=("parallel",), vmem_limit_bytes=VMEM_LIMIT),
        name="back_sample_ffn",
    )(h2, oc, w["w_mem_o"], w["ffn2_norm"], w["ffn2_wg"], w["ffn2_wu"], w["ffn2_wd"], w["final_norm"])


def _mem_kv_kernel(mem_ref, g_ref, wk_ref, wv_ref, k32_ref, v32_ref, k16_ref, v16_ref):
    u = _rms(mem_ref[...], g_ref[...]).astype(BF16)
    k = _dot(u, wk_ref[...])
    v = _dot(u, wv_ref[...])
    rows = mem_ref.shape[0]
    for hd in range(MEM_HEADS):
        cols = slice(hd * MEM_HEAD_DIM, (hd + 1) * MEM_HEAD_DIM)
        k32_ref[pl.ds(hd, rows, stride=MEM_HEADS), :] = k[:, cols]
        v32_ref[pl.ds(hd, rows, stride=MEM_HEADS), :] = v[:, cols]
    k16_ref[...] = k.astype(BF16)
    v16_ref[...] = v.astype(BF16)


def _mem_kv_call(mem, w):
    rows = mem.shape[0]
    full = lambda width: pl.BlockSpec((rows, width), lambda i: (0, 0))
    return pl.pallas_call(
        _mem_kv_kernel,
        grid=(1,),
        in_specs=[full(D_MODEL), _const_spec((1, D_MODEL)), _const_spec((D_MODEL, MEM_W)),
                  _const_spec((D_MODEL, MEM_W))],
        out_specs=[pl.BlockSpec((rows * MEM_HEADS, MEM_HEAD_DIM), lambda i: (0, 0))] * 2 + [full(MEM_W)] * 2,
        out_shape=([jax.ShapeDtypeStruct((rows * MEM_HEADS, MEM_HEAD_DIM), F32)] * 2
                   + [jax.ShapeDtypeStruct((rows, MEM_W), BF16)] * 2),
        compiler_params=pltpu.CompilerParams(dimension_semantics=("arbitrary",), vmem_limit_bytes=VMEM_LIMIT),
        name="mem_kv",
    )(mem, w["mem_kv_norm"], w["w_mem_k"], w["w_mem_v"])


def _rope_tables(pos):
    inv_freq = jnp.power(jnp.float32(ROPE_THETA), -jnp.arange(ROT_HALF, dtype=jnp.float32) / ROT_HALF)
    ang = pos.astype(jnp.float32)[:, None] * inv_freq[None, :]
    lane = np.arange(LANES)
    within = lane % HEAD_DIM
    first = within < ROT_HALF
    second = (within >= ROT_HALF) & (within < 2 * ROT_HALF)
    pick_cos = np.zeros((ROT_HALF, LANES), np.float32)
    pick_sin = np.zeros((ROT_HALF, LANES), np.float32)
    pick_cos[within[first | second] % ROT_HALF, lane[first | second]] = 1.0
    pick_sin[within[first] % ROT_HALF, lane[first]] = -1.0
    pick_sin[within[second] % ROT_HALF, lane[second]] = 1.0
    def spread(t8, pick):
        out = t8[:, 0:1] * pick[0][None, :]
        for j in range(1, ROT_HALF):
            out = out + t8[:, j:j + 1] * pick[j][None, :]
        return out

    return (spread(jnp.cos(ang), pick_cos) + np.where(first | second, 0.0, 1.0).astype(np.float32)[None, :],
            spread(jnp.sin(ang), pick_sin))


def _prep_weights(p):
    bf = lambda x: x.astype(BF16)
    vec = lambda x: x.reshape(1, -1)
    return {
        "ffn1_norm": vec(p["ffn1_norm"]), "ffn1_wg": bf(p["ffn1_w_gate"]), "ffn1_wu": bf(p["ffn1_w_up"]),
        "ffn1_wd": bf(p["ffn1_w_down"]),
        "mix_norm": vec(p["mix_norm"]), "w_in": bf(p["w_in"]),
        "w_branch_a": bf(p["w_branch_a"].reshape(N_HEADS_A, HEAD_DIM, D_MODEL)[np.array(HEAD_PERM_A)].reshape(
            Q_A_W, D_MODEL)),
        "w_branch_b": bf(p["w_branch_b"]), "w_out": bf(p["w_out"]),
        "mem_q_norm": vec(p["mem_q_norm"]), "mem_kv_norm": vec(p["mem_kv_norm"]),
        "w_mem_q": bf(p["w_mem_q"]), "w_mem_k": bf(p["w_mem_k"]), "w_mem_v": bf(p["w_mem_v"]),
        "w_mem_o": bf(p["w_mem_o"]),
        "ffn2_norm": vec(p["ffn2_norm"]), "ffn2_wg": bf(p["ffn2_w_gate"]), "ffn2_wu": bf(p["ffn2_w_up"]),
        "ffn2_wd": bf(p["ffn2_w_down"]), "final_norm": vec(p["final_norm"]),
    }


def kernel(x_prompt, x_sample, cache_swa_k, cache_swa_v, cache_dil_k, cache_dil_v, cache_mem_k, cache_mem_v, mem_prompt, ffn1_norm, ffn1_w_gate, ffn1_w_up, ffn1_w_down, mix_norm, w_in, attn_sink, w_branch_a, w_branch_b, w_out, mem_q_norm, mem_kv_norm, w_mem_q, w_mem_k, w_mem_v, w_mem_o, ffn2_norm, ffn2_w_gate, ffn2_w_up, ffn2_w_down, final_norm):
    depth = ffn1_norm.shape[0]
    assert depth == 1
    batch, seq, d = x_prompt.shape
    nb, t_new, _ = x_sample.shape
    assert d == D_MODEL and t_new == SAMPLE_T and seq % (TOKEN_TILE * DILATIONS[-1]) == 0 and seq % ATTN_CHUNK == 0
    assert cache_swa_k.shape[2] == CACHE_A and cache_dil_k.shape[2] == CACHE_B
    layer = lambda x: x[0]
    w = _prep_weights(dict(
        ffn1_norm=layer(ffn1_norm), ffn1_w_gate=layer(ffn1_w_gate), ffn1_w_up=layer(ffn1_w_up),
        ffn1_w_down=layer(ffn1_w_down), mix_norm=layer(mix_norm), w_in=layer(w_in),
        w_branch_a=layer(w_branch_a), w_branch_b=layer(w_branch_b), w_out=layer(w_out),
        mem_q_norm=layer(mem_q_norm), mem_kv_norm=layer(mem_kv_norm), w_mem_q=layer(w_mem_q),
        w_mem_k=layer(w_mem_k), w_mem_v=layer(w_mem_v), w_mem_o=layer(w_mem_o), ffn2_norm=layer(ffn2_norm),
        ffn2_w_gate=layer(ffn2_w_gate), ffn2_w_up=layer(ffn2_w_up), ffn2_w_down=layer(ffn2_w_down),
        final_norm=final_norm))
    sink = layer(attn_sink)

    ts = nb * t_new
    pos_s = PAST_LEN + (jnp.arange(TOKEN_TILE, dtype=jnp.int32) % t_new)
    tables_s = _rope_tables(pos_s)
    (h_s, q32, ka32_s, va32_s, kb32_s, vb32_s) = _front_call(x_sample.reshape(ts, d), tables_s, w, sample=True)
    by_pos = lambda c: jnp.transpose(layer(c), (0, 2, 3, 1)).reshape(nb, c.shape[3] * c.shape[4], c.shape[2])
    by_slot_head = lambda c: layer(c).reshape(nb, N_MEM * MEM_HEADS, MEM_HEAD_DIM)

    oa_s, ob_s = _attn_sample_call(
        sink, q32, ka32_s, va32_s, kb32_s, vb32_s,
        by_pos(cache_swa_k), by_pos(cache_swa_v), by_pos(cache_dil_k), by_pos(cache_dil_v))

    tp = batch * seq
    mk32, mv32, mk16, mv16 = _mem_kv_call(mem_prompt.reshape(batch * N_MEM, d), w)
    tables_p = _rope_tables(jnp.arange(seq, dtype=jnp.int32))
    h_p, main, split4, split16, tails_kva, tails_kb, tails_vb = _front_call(
        x_prompt.reshape(tp, d), tables_p, w, sample=False, batch=batch)
    oa = _attn_a_call(sink, main, batch, seq)
    groups = ((main.reshape(batch, 1, seq, MAIN_W), MAIN_QB_COL), (split4, 0), (split16, 0))
    ols = [_attn_b_call(qkv, qcol, dil) for (qkv, qcol), dil in zip(groups, DILATIONS)]
    y_p = _back_prompt_call(h_p, oa, ols, mk16.reshape(batch, N_MEM, MEM_W),
                            mv16.reshape(batch, N_MEM, MEM_W), w, seq)

    h2_s, qc_s = _back_sample_a_call(h_s, oa_s, ob_s, w)
    oc_s = _cross_sample_call(qc_s, by_slot_head(cache_mem_k), by_slot_head(cache_mem_v))
    y_s = _back_sample_b_call(h2_s, oc_s, w)

    keep_a, keep_b = min(CACHE_A, seq), min(CACHE_B, seq)
    assert max(keep_a, keep_b) <= TAIL_ROWS <= seq and TAIL_ROWS % TOKEN_TILE == 0

    def tail(x, col, keep, heads):
        part = x[:, col:col + heads * HEAD_DIM, TAIL_ROWS - keep:].reshape(batch, heads, HEAD_DIM, keep)
        return jnp.transpose(part, (0, 3, 1, 2))[None]

    new = lambda x, heads: x.reshape(1, nb, t_new, heads, HEAD_DIM)
    memo = lambda x: x.reshape(1, batch, N_MEM, MEM_HEADS, MEM_HEAD_DIM)
    return (y_p.reshape(batch, seq, d), y_s.reshape(nb, t_new, d),
            tail(tails_kva, 0, keep_a, N_KV_A), tail(tails_kva, KV_A_W, keep_a, N_KV_A),
            tail(tails_kb, 0, keep_b, N_KV_B), tail(tails_vb, 0, keep_b, N_KV_B),
            memo(mk32), memo(mv32),
            new(ka32_s, N_KV_A), new(va32_s, N_KV_A), new(kb32_s, N_KV_B), new(vb32_s, N_KV_B))
```

```python
import functools

import jax
import jax.numpy as jnp
import numpy as np
from jax import lax
from jax.experimental import pallas as pl
from jax.experimental.pallas import tpu as pltpu

F32 = jnp.float32
BF16 = jnp.bfloat16

D_MODEL = 1024
D_FF = 2816
HEAD_DIM = 64
ROT_HALF = 8
ROPE_THETA = 500000.0
ATTN_SCALE = HEAD_DIM ** -0.5
RMS_EPS = 1e-6
PAST_LEN = 16384

N_HEADS_A = 8
N_KV_A = 2
N_KV_B = 4
DILATIONS = (1, 4, 16)
BAND_STEPS = 128
Q_A_W = N_HEADS_A * HEAD_DIM
KV_A_W = N_KV_A * HEAD_DIM
Q_B_W = len(DILATIONS) * N_KV_B * HEAD_DIM
KV_B_W = N_KV_B * HEAD_DIM
QKV_W = Q_A_W + 2 * KV_A_W + Q_B_W + 2 * KV_B_W
MEM_HEADS = 4
MEM_HEAD_DIM = 128
MEM_W = MEM_HEADS * MEM_HEAD_DIM
MEM_SCALE = MEM_HEAD_DIM ** -0.5
N_MEM = 256

HEAD_PERM_A = (0, 4, 1, 5, 2, 6, 3, 7)

MAIN_W = Q_A_W + 2 * KV_A_W + 3 * KV_B_W
MAIN_KA_COL, MAIN_VA_COL = Q_A_W // KV_A_W, Q_A_W // KV_A_W + 1
MAIN_QB_COL = (Q_A_W + 2 * KV_A_W) // KV_B_W
SPLIT_W = 3 * KV_B_W
TAIL_ROWS = 2048
assert 2 * KV_A_W == KV_B_W

LANES = 128
BLOCK = 128
TOKEN_TILE = 512
FRONT_ROW_PARTS = 2
BACK_ROW_PARTS = 1
FFN_ROW_PARTS = 2
ATTN_CHUNK = 2048
FF_CHUNKS = ((0, 1536), (1536, 2816))
VMEM_LIMIT = 60 * 1024 * 1024
NEG_INF = float("-inf")


def _rms(x, g):
    ms = jnp.mean(x * x, axis=-1, keepdims=True)
    return x * lax.rsqrt(ms + RMS_EPS) * g


def _sigmoid(x):
    return 0.5 * jnp.tanh(0.5 * x) + 0.5


def _dot(a, b):
    return jnp.dot(a, b, preferred_element_type=F32)


def _dot_nt(a, b):
    return lax.dot_general(a, b, (((1,), (1,)), ((), ())), preferred_element_type=F32)


def _ffn_half(xs, g_ref, wg_ref, wu_ref, wd_ref):
    us = [_rms(x, g_ref[...]).astype(BF16) for x in xs]
    accs = [None] * len(xs)
    for lo, hi in FF_CHUNKS:
        gates = [_dot(u, wg_ref[:, lo:hi]) for u in us]
        ups = [_dot(u, wu_ref[:, lo:hi]) for u in us]
        acts = [(gate * _sigmoid(gate) * up).astype(BF16) for gate, up in zip(gates, ups)]
        for i, act in enumerate(acts):
            part = _dot(act, wd_ref[lo:hi, :])
            accs[i] = part if accs[i] is None else accs[i] + part
    return [x + 0.5 * acc for x, acc in zip(xs, accs)]


def _ffn2_and_norm(h3, g2_ref, wg_ref, wu_ref, wd_ref, gf_ref, y_ref):
    blocks = _row_blocks(y_ref, FFN_ROW_PARTS)
    for rb, y in zip(blocks, _ffn_half([h3[rb, :] for rb in blocks], g2_ref, wg_ref, wu_ref, wd_ref)):
        y_ref[rb, :] = _rms(y, gf_ref[...])


def _row_blocks(ref, parts):
    n = ref.shape[0] // parts
    return [slice(i * n, (i + 1) * n) for i in range(parts)]


def _swap_halves(x):
    return pltpu.roll(x, HEAD_DIM, 1)


def _low_half():
    return lax.broadcasted_iota(jnp.int32, (1, LANES), 1) < HEAD_DIM


def _pair_kv_groups(chunks):
    low = _low_half()
    c0, c1, c2, c3 = chunks
    return [jnp.where(low, c0, _swap_halves(c2)), jnp.where(low, _swap_halves(c0), c2),
            jnp.where(low, c1, _swap_halves(c3)), jnp.where(low, _swap_halves(c1), c3)]


def _merge_pairs(per_head):
    low = _low_half()
    return [jnp.where(low, per_head[2 * c], per_head[2 * c + 1]) for c in range(N_HEADS_A // 2)]


def _front_kernel(x_ref, cos_ref, sin_ref, g1_ref, wg_ref, wu_ref, wd_ref, gm_ref, wqkv_ref,
                  h_ref, *rest, sample, tiles_per_seq=None, tail_tiles=None):
    blocks = _row_blocks(x_ref, FRONT_ROW_PARTS)
    hs = _ffn_half([x_ref[rb, :] for rb in blocks], g1_ref, wg_ref, wu_ref, wd_ref)
    for rb, h in zip(blocks, hs):
        h_ref[rb, :] = h
    ns = [_rms(h, gm_ref[...]).astype(BF16) for h in hs]
    zs = [_dot(n, wqkv_ref[...]) for n in ns]
    for rb, z in zip(blocks, zs):
        _front_emit(rb, z, cos_ref, sin_ref, rest, sample)
    if not sample:
        tail_refs, tail_scratch = rest[3:6], (rest[10], rest[8], rest[9])

        @pl.when(lax.rem(pl.program_id(0), tiles_per_seq) >= tiles_per_seq - tail_tiles)
        def _():
            for ref, scr in zip(tail_refs, tail_scratch):
                for c in range(scr.shape[0]):
                    ref[0, c * LANES:(c + 1) * LANES, :] = scr[c].T


def _front_emit(rb, z, cos_ref, sin_ref, rest, sample):
    cos, sin = cos_ref[rb, :], sin_ref[rb, :]
    lane_in_head = lax.broadcasted_iota(jnp.int32, (1, LANES), 1) & (HEAD_DIM - 1)
    first = lane_in_head < ROT_HALF

    def chunk(c):
        return z[:, c * LANES:(c + 1) * LANES]

    def rope(c):
        zc = chunk(c)
        partner = jnp.where(first, pltpu.roll(zc, LANES - ROT_HALF, 1), pltpu.roll(zc, ROT_HALF, 1))
        return zc * cos + partner * sin

    nqa, nka = Q_A_W // LANES, KV_A_W // LANES
    nqb, nkb = Q_B_W // LANES, KV_B_W // LANES
    c0 = 0
    qa = _pair_kv_groups([rope(c0 + c) * ATTN_SCALE for c in range(nqa)])
    c0 += nqa
    ka = [rope(c0 + c) for c in range(nka)]
    c0 += nka
    va = [chunk(c0 + c) for c in range(nka)]
    c0 += nka
    qb = [rope(c0 + c) * ATTN_SCALE for c in range(nqb)]
    c0 += nqb
    kb = [rope(c0 + c) for c in range(nkb)]
    c0 += nkb
    vb = [chunk(c0 + c) for c in range(nkb)]

    def store(ref, parts, dtype):
        for c, p in enumerate(parts):
            ref[rb, c * LANES:(c + 1) * LANES] = p.astype(dtype)

    def fill(scr, parts):
        for c, p in enumerate(parts):
            scr[c, rb, :] = p

    def store_split(ref, slot, scr, d):
        n = (rb.stop - rb.start) // d
        dst = slice(rb.start // d, rb.start // d + n)
        for r in range(d):
            for c in range(scr.shape[0]):
                lanes = slice(slot * KV_B_W + c * LANES, slot * KV_B_W + (c + 1) * LANES)
                ref[0, r, dst, lanes] = scr[c, pl.ds(rb.start + r, n, stride=d), :].astype(BF16)

    if sample:
        q32_ref, ka32_ref, va32_ref, kb32_ref, vb32_ref = rest
        store(q32_ref, qa + qb, F32)
        store(ka32_ref, ka, F32)
        store(va32_ref, va, F32)
        store(kb32_ref, kb, F32)
        store(vb32_ref, vb, F32)
    else:
        main_ref, s4_ref, s16_ref, _, _, _, q4_s, q16_s, kb_s, vb_s, kva_s = rest
        gw = KV_B_W // LANES
        store(main_ref, qa + ka + va + qb[0:gw] + kb + vb, BF16)
        fill(q4_s, qb[gw:2 * gw])
        fill(q16_s, qb[2 * gw:3 * gw])
        fill(kb_s, kb)
        fill(vb_s, vb)
        fill(kva_s, ka + va)
        for slot, scr4, scr16 in ((0, q4_s, q16_s), (1, kb_s, kb_s), (2, vb_s, vb_s)):
            store_split(s4_ref, slot, scr4, DILATIONS[1])
            store_split(s16_ref, slot, scr16, DILATIONS[2])


def _const_spec(shape, index=None):
    index = (0,) * len(shape) if index is None else index
    return pl.BlockSpec(shape, lambda *_: index, pipeline_mode=pl.Buffered(1))


def _front_call(x, tables, w, *, sample, batch=1):
    t = x.shape[0]
    tm = TOKEN_TILE
    nsteps = t // tm
    tpb = nsteps // batch
    seq = t // batch
    cos_t, sin_t = tables
    tab_blocks = cos_t.shape[0] // tm
    row = lambda i: (i, 0)
    tab = lambda i: (i % tab_blocks, 0)

    def tok(width, dtype):
        return jax.ShapeDtypeStruct((t, width), dtype), pl.BlockSpec((tm, width), row)

    def split(d):
        return (jax.ShapeDtypeStruct((batch, d, seq // d, SPLIT_W), BF16),
                pl.BlockSpec((1, d, tm // d, SPLIT_W), lambda i: (i // tpb, 0, i % tpb, 0)))

    outs = [tok(D_MODEL, F32)]
    scratch = []
    static = {}
    if sample:
        outs += [tok(Q_A_W + Q_B_W, F32), tok(KV_A_W, F32), tok(KV_A_W, F32), tok(KV_B_W, F32), tok(KV_B_W, F32)]
    else:
        tail_tiles = TAIL_ROWS // tm
        tails = (jax.ShapeDtypeStruct((batch, KV_B_W, TAIL_ROWS), F32),
                 pl.BlockSpec((1, KV_B_W, tm),
                              lambda i: (i // tpb, 0, jnp.maximum(i % tpb - (tpb - tail_tiles), 0))))
        outs += [tok(MAIN_W, BF16), split(DILATIONS[1]), split(DILATIONS[2])] + [tails] * 3
        scratch = [pltpu.VMEM((KV_B_W // LANES, tm, LANES), F32)] * 4 + [pltpu.VMEM((2, tm, LANES), F32)]
        static = dict(tiles_per_seq=tpb, tail_tiles=tail_tiles)
    out_shape, out_specs = zip(*outs)
    in_specs = [pl.BlockSpec((tm, D_MODEL), row),
                pl.BlockSpec((tm, LANES), tab), pl.BlockSpec((tm, LANES), tab),
                _const_spec((1, D_MODEL)), _const_spec((D_MODEL, D_FF)), _const_spec((D_MODEL, D_FF)),
                _const_spec((D_FF, D_MODEL)), _const_spec((1, D_MODEL)), _const_spec((D_MODEL, QKV_W), (0, 0))]
    return pl.pallas_call(
        functools.partial(_front_kernel, sample=sample, **static),
        grid=(nsteps,),
        in_specs=in_specs,
        out_specs=list(out_specs),
        out_shape=list(out_shape),
        scratch_shapes=scratch,
        compiler_params=pltpu.CompilerParams(dimension_semantics=("parallel" if sample else "arbitrary",),
                                             vmem_limit_bytes=VMEM_LIMIT),
        name="front_sample" if sample else "front_prompt",
    )(x, cos_t, sin_t, w["ffn1_norm"], w["ffn1_wg"], w["ffn1_wu"], w["ffn1_wd"], w["mix_norm"], w["w_in"])


def _lane_block_masks(width, block, dtype):
    lane = lax.broadcasted_iota(jnp.int32, (1, width), 1)
    return [((lane >= i * block) & (lane < (i + 1) * block)).astype(dtype) for i in range(width // block)]


def _band_mask():
    qi = lax.broadcasted_iota(jnp.int32, (BLOCK, 2 * BLOCK), 0)
    kj = lax.broadcasted_iota(jnp.int32, (BLOCK, 2 * BLOCK), 1)
    dist = qi + BLOCK - kj
    return (dist >= 0) & (dist <= BAND_STEPS), kj >= BLOCK


def _band_block(load_qbd, load_k, load_v, mask, nheads, finish):
    s = _dot_nt(load_qbd(), load_k())
    yield
    ps, ms, ls = [], [], []
    for p in range(nheads):
        sp = jnp.where(mask, s[p * BLOCK:(p + 1) * BLOCK], NEG_INF)
        m = jnp.max(sp, axis=-1, keepdims=True)
        e = jnp.exp(sp - m)
        ls.append(jnp.sum(e, axis=-1, keepdims=True))
        ms.append(m)
        ps.append(e.astype(BF16))
    pc = jnp.concatenate(ps, axis=0)
    yield
    finish(_dot(pc, load_v()), ms, ls)
    yield


def _issue_skewed(items, nstages=3):
    for tick in range(len(items) + nstages - 1):
        for s in reversed(range(nstages)):
            i = tick - s
            if 0 <= i < len(items):
                next(items[i])


def _kv_window(cur_ref, prev_ref, qb):
    if qb == 0:
        return jnp.concatenate([prev_ref[...], cur_ref[0:BLOCK]], axis=0)
    return cur_ref[(qb - 1) * BLOCK:(qb + 1) * BLOCK]


def _attn_a_kernel(sink_ref, q_ref, kc_ref, kp_ref, vc_ref, vp_ref, o_ref):
    first_chunk = pl.program_id(1) == 0
    band, in_cur = _band_mask()
    half_bf = _lane_block_masks(LANES, HEAD_DIM, BF16)

    def block(qb):
        mask = band & (in_cur | jnp.logical_not(first_chunk)) if qb == 0 else band
        rows = slice(qb * BLOCK, (qb + 1) * BLOCK)

        def load_qbd():
            return jnp.concatenate(
                [q_ref[rows, (p // 2) * LANES:(p // 2 + 1) * LANES] * half_bf[p % 2] for p in range(N_HEADS_A)],
                axis=0)

        def finish(o, ms, ls):
            normed = []
            for p in range(N_HEADS_A):
                den = ls[p] + jnp.exp(sink_ref[HEAD_PERM_A[p]] - ms[p])
                normed.append(o[p * BLOCK:(p + 1) * BLOCK] * (1.0 / den))
            for c, pair in enumerate(_merge_pairs(normed)):
                o_ref[rows, c * LANES:(c + 1) * LANES] = pair.astype(BF16)

        return _band_block(load_qbd, lambda: _kv_window(kc_ref, kp_ref, qb), lambda: _kv_window(vc_ref, vp_ref, qb),
                           mask, N_HEADS_A, finish)

    for qb in range(q_ref.shape[0] // BLOCK):
        for _ in block(qb):
            pass


def _attn_a_call(sink, main, batch, seq):
    t = main.shape[0]
    cpb = seq // ATTN_CHUNK
    bpc = ATTN_CHUNK // BLOCK
    cur = lambda col: (lambda b, c: (b * cpb + c, col))
    prev = lambda col: (lambda b, c: (b * cpb * bpc + jnp.maximum(c * bpc - 1, 0), col))
    kv = lambda col: [pl.BlockSpec((ATTN_CHUNK, KV_A_W), cur(col)), pl.BlockSpec((BLOCK, KV_A_W), prev(col))]
    return pl.pallas_call(
        _attn_a_kernel,
        grid=(batch, cpb),
        in_specs=[pl.BlockSpec(memory_space=pltpu.SMEM), pl.BlockSpec((ATTN_CHUNK, Q_A_W), cur(0))]
        + kv(MAIN_KA_COL) + kv(MAIN_VA_COL),
        out_specs=pl.BlockSpec((ATTN_CHUNK, Q_A_W), cur(0)),
        out_shape=jax.ShapeDtypeStruct((t, Q_A_W), BF16),
        compiler_params=pltpu.CompilerParams(dimension_semantics=("parallel", "parallel"),
                                             vmem_limit_bytes=VMEM_LIMIT),
        name="attn_a_prompt",
    )(sink, main, main, main, main, main)


def _attn_b_kernel(q_ref, kc_ref, kp_ref, vc_ref, vp_ref, ol_ref):
    first_chunk = pl.program_id(2) == 0
    band, in_cur = _band_mask()
    head_bf = _lane_block_masks(KV_B_W, HEAD_DIM, BF16)
    lane = lax.broadcasted_iota(jnp.int32, (1, KV_B_W), 1)

    def block(r, qb):
        mask = band & (in_cur | jnp.logical_not(first_chunk)) if qb == 0 else band
        rows = slice(qb * BLOCK, (qb + 1) * BLOCK)

        def load_qbd():
            q = q_ref[r, rows, :]
            return jnp.concatenate([q * head_bf[p] for p in range(N_KV_B)], axis=0)

        def finish(o, ms, ls):
            out = lse = None
            for p in reversed(range(N_KV_B)):
                op = o[p * BLOCK:(p + 1) * BLOCK] * (1.0 / ls[p])
                lp = jnp.broadcast_to(ms[p] + jnp.log(ls[p]), (BLOCK, KV_B_W))
                if out is None:
                    out, lse = op, lp
                else:
                    sel = lane < (p + 1) * HEAD_DIM
                    out, lse = jnp.where(sel, op, out), jnp.where(sel, lp, lse)
            ol_ref[r, rows, 0:KV_B_W] = out
            ol_ref[r, rows, KV_B_W:2 * KV_B_W] = lse

        return _band_block(load_qbd, lambda: _kv_window(kc_ref.at[r], kp_ref.at[r], qb),
                           lambda: _kv_window(vc_ref.at[r], vp_ref.at[r], qb), mask, N_KV_B, finish)

    _issue_skewed([block(r, qb) for r in range(q_ref.shape[0]) for qb in range(q_ref.shape[1] // BLOCK)])


def _attn_b_call(qkv, qcol, d):
    batch, _, rows, _ = qkv.shape
    chunk = min(ATTN_CHUNK, rows)
    rps = min(d, ATTN_CHUNK // chunk)
    cpb = rows // chunk
    bpc = chunk // BLOCK
    cur = lambda col: pl.BlockSpec((None, rps, chunk, KV_B_W), lambda b, r, c: (b, r, c, col))
    prev = lambda col: pl.BlockSpec((None, rps, BLOCK, KV_B_W),
                                    lambda b, r, c: (b, r, jnp.maximum(c * bpc - 1, 0), col))
    return pl.pallas_call(
        _attn_b_kernel,
        grid=(batch, d // rps, cpb),
        in_specs=[cur(qcol), cur(qcol + 1), prev(qcol + 1), cur(qcol + 2), prev(qcol + 2)],
        out_specs=pl.BlockSpec((None, rps, chunk, 2 * KV_B_W), lambda b, r, c: (b, r, c, 0)),
        out_shape=jax.ShapeDtypeStruct((batch, d, rows, 2 * KV_B_W), F32),
        compiler_params=pltpu.CompilerParams(dimension_semantics=("parallel", "parallel", "parallel"),
                                             vmem_limit_bytes=VMEM_LIMIT),
        name=f"attn_b_prompt_d{d}",
    )(qkv, qkv, qkv, qkv, qkv)


SAMPLE_T = 8
CACHE_A = 128
CACHE_B = 2048
SAMPLE_ELEMS = 4


def _new_rows_block(new):
    return jnp.concatenate([new, jnp.zeros((BLOCK - SAMPLE_T, new.shape[1]), F32)], axis=0)


def _cached_scores(qbd, kt, knew):
    return _dot(qbd, kt), _dot_nt(qbd, knew)


def _cached_softmax(s_c, s_n, mask_c, mask_n):
    s_c = jnp.where(mask_c, s_c, NEG_INF)
    s_n = jnp.where(mask_n, s_n, NEG_INF)
    m = jnp.maximum(jnp.max(s_c, axis=-1, keepdims=True), jnp.max(s_n, axis=-1, keepdims=True))
    e_c = jnp.exp(s_c - m)
    e_n = jnp.exp(s_n - m)
    l = jnp.sum(e_c, axis=-1, keepdims=True) + jnp.sum(e_n, axis=-1, keepdims=True)
    return e_c, e_n, m, l


def _cached_pv(e_c, e_n, vt, vnew):
    return _dot_nt(e_c, vt) + _dot(e_n, vnew)


def _attn_sample_kernel(sink_ref, q_ref, kan_ref, van_ref, kbn_ref, vbn_ref, cka_ref, cva_ref, ckb_ref, cvb_ref,
                        oa_ref, ob_ref):
    half_f = _lane_block_masks(LANES, HEAD_DIM, F32)
    lane_b = lax.broadcasted_iota(jnp.int32, (1, KV_B_W), 1)

    def dist(nrows, ncols, offset):
        r = lax.broadcasted_iota(jnp.int32, (nrows, ncols), 0)
        i = lax.broadcasted_iota(jnp.int32, (nrows, ncols), 1)
        return r, offset + (r & (SAMPLE_T - 1)) - i

    nrow_a = N_HEADS_A * SAMPLE_T
    _, dac = dist(nrow_a, CACHE_A, CACHE_A)
    _, dan = dist(nrow_a, BLOCK, 0)
    mask_ac = (dac >= 0) & (dac <= BAND_STEPS)
    mask_an = (dan >= 0) & (dan <= BAND_STEPS)
    nrow_b = len(DILATIONS) * N_KV_B * SAMPLE_T
    rows_per_group = N_KV_B * SAMPLE_T

    def mask_b(ncols, offset):
        r, db = dist(nrow_b, ncols, offset)
        dil = jnp.where(r < rows_per_group, DILATIONS[0], jnp.where(r < 2 * rows_per_group, DILATIONS[1], DILATIONS[2]))
        return (db >= 0) & (db <= BAND_STEPS * dil) & ((db & (dil - 1)) == 0)

    mask_bc = mask_b(CACHE_B, CACHE_B)
    mask_bn = mask_b(BLOCK, 0)

    sink_col = jnp.concatenate(
        [jnp.full((SAMPLE_T, 1), sink_ref[HEAD_PERM_A[p]], F32) for p in range(N_HEADS_A)], axis=0)

    def element(j):
        rows = slice(j * SAMPLE_T, (j + 1) * SAMPLE_T)
        qbd_a = jnp.concatenate(
            [q_ref[rows, (p // 2) * LANES:(p // 2 + 1) * LANES] * half_f[p % 2] for p in range(N_HEADS_A)],
            axis=0)
        qrows = []
        for g in range(len(DILATIONS)):
            for kvh in range(N_KV_B):
                c = (Q_A_W // LANES) + 2 * g + kvh // 2
                part = q_ref[rows, c * LANES:(c + 1) * LANES] * half_f[kvh % 2]
                zero = jnp.zeros_like(part)
                qrows.append(jnp.concatenate([part, zero] if kvh // 2 == 0 else [zero, part], axis=1))
        qbd_b = jnp.concatenate(qrows, axis=0)
        scores_a = _cached_scores(qbd_a, cka_ref[j], _new_rows_block(kan_ref[rows, :]))
        scores_b = _cached_scores(qbd_b, ckb_ref[j], _new_rows_block(kbn_ref[rows, :]))
        yield
        ea_c, ea_n, ma, la = _cached_softmax(*scores_a, mask_ac, mask_an)
        eb_c, eb_n, m, l = _cached_softmax(*scores_b, mask_bc, mask_bn)
        yield
        o = _cached_pv(ea_c, ea_n, cva_ref[j], _new_rows_block(van_ref[rows, :]))
        o = o * (1.0 / (la + jnp.exp(sink_col - ma)))
        per_head = [o[p * SAMPLE_T:(p + 1) * SAMPLE_T] for p in range(N_HEADS_A)]
        for c, pair in enumerate(_merge_pairs(per_head)):
            oa_ref[rows, c * LANES:(c + 1) * LANES] = pair
        o = _cached_pv(eb_c, eb_n, cvb_ref[j], _new_rows_block(vbn_ref[rows, :]))
        out = None
        for kvh in reversed(range(N_KV_B)):
            sl = [slice((g * N_KV_B + kvh) * SAMPLE_T, (g * N_KV_B + kvh + 1) * SAMPLE_T)
                  for g in range(len(DILATIONS))]
            mj = jnp.maximum(jnp.maximum(m[sl[0]], m[sl[1]]), m[sl[2]])
            ws = [jnp.exp(m[x] - mj) for x in sl]
            den = ws[0] * l[sl[0]] + ws[1] * l[sl[1]] + ws[2] * l[sl[2]]
            num = ws[0] * o[sl[0]] + ws[1] * o[sl[1]] + ws[2] * o[sl[2]]
            okv = num * (1.0 / den)
            out = okv if out is None else jnp.where(lane_b < (kvh + 1) * HEAD_DIM, okv, out)
        ob_ref[rows, :] = out
        yield

    elements = [element(j) for j in range(SAMPLE_ELEMS)]
    for _ in range(3):
        for e in elements:
            next(e)


def _attn_sample_call(sink, q32, ka32, va32, kb32, vb32, cka, cva, ckb, cvb):
    nb = cka.shape[0]
    bt = SAMPLE_ELEMS
    t = q32.shape[0]
    tok = lambda w: pl.BlockSpec((bt * SAMPLE_T, w), lambda i: (i, 0))
    cache = lambda w, n: pl.BlockSpec((bt, w, n), lambda i: (i, 0, 0))
    return pl.pallas_call(
        _attn_sample_kernel,
        grid=(nb // bt,),
        in_specs=[pl.BlockSpec(memory_space=pltpu.SMEM), tok(Q_A_W + Q_B_W),
                  tok(KV_A_W), tok(KV_A_W), tok(KV_B_W), tok(KV_B_W),
                  cache(KV_A_W, CACHE_A), cache(KV_A_W, CACHE_A), cache(KV_B_W, CACHE_B), cache(KV_B_W, CACHE_B)],
        out_specs=[tok(Q_A_W), tok(KV_B_W)],
        out_shape=[jax.ShapeDtypeStruct((t, Q_A_W), F32), jax.ShapeDtypeStruct((t, KV_B_W), F32)],
        compiler_params=pltpu.CompilerParams(dimension_semantics=("parallel",), vmem_limit_bytes=VMEM_LIMIT),
        name="attn_sample",
    )(sink, q32, ka32, va32, kb32, vb32, cka, cva, ckb, cvb)


def _mix_out(h, oa, ob, gm_ref, wgate_ref, wba_ref, wbb_ref, wout_ref):
    branch_a = _dot(oa.astype(BF16), wba_ref[...])
    n = _rms(h, gm_ref[...]).astype(BF16)
    gates = _dot(n, wgate_ref[...])
    branch_b = _dot(ob.astype(BF16), wbb_ref[...])
    mixed = _sigmoid(gates[:, :D_MODEL]) * branch_a + _sigmoid(gates[:, D_MODEL:]) * branch_b
    return h + _dot(mixed.astype(BF16), wout_ref[...])


def _combine_groups(os, ls):
    m = jnp.maximum(jnp.maximum(ls[0], ls[1]), ls[2])
    es = [jnp.exp(x - m) for x in ls]
    den = es[0] + es[1] + es[2]
    num = es[0] * os[0] + es[1] * os[1] + es[2] * os[2]
    return num * (1.0 / den)


def _merge_split(split_ref, field, scr, rb):
    d = split_ref.shape[1]
    if d == 1:
        return split_ref[0, 0, rb, field * KV_B_W:(field + 1) * KV_B_W]
    n = (rb.stop - rb.start) // d
    src = slice(rb.start // d, rb.start // d + n)
    for r in range(d):
        for c in range(scr.shape[0]):
            lanes = slice(field * KV_B_W + c * LANES, field * KV_B_W + (c + 1) * LANES)
            scr[c, pl.ds(rb.start + r, n, stride=d), :] = split_ref[0, r, src, lanes]
    return jnp.concatenate([scr[c, rb, :] for c in range(scr.shape[0])], axis=1)


def _softmax_rows(s):
    m = jnp.max(s, axis=-1, keepdims=True)
    e = jnp.exp(s - m)
    return e * (1.0 / jnp.sum(e, axis=-1, keepdims=True))


def _back_prompt_kernel(h_ref, oa_ref, ol0_ref, ol1_ref, ol2_ref, mk_ref, mv_ref,
                        gm_ref, wgate_ref, wba_ref, wbb_ref, wout_ref, gq_ref, wq_ref, wo_ref,
                        g2_ref, wg_ref, wu_ref, wd_ref, gf_ref, y_ref, o1_s, o2_s, l1_s, l2_s, ob_s):
    blocks = _row_blocks(h_ref, BACK_ROW_PARTS)
    h3s = []
    for rb in blocks:
        ob_s[rb, :] = _combine_groups(
            (_merge_split(ol0_ref, 0, None, rb), _merge_split(ol1_ref, 0, o1_s, rb),
             _merge_split(ol2_ref, 0, o2_s, rb)),
            (_merge_split(ol0_ref, 1, None, rb), _merge_split(ol1_ref, 1, l1_s, rb),
             _merge_split(ol2_ref, 1, l2_s, rb))).astype(BF16)
        h2 = _mix_out(h_ref[rb, :], oa_ref[rb, :], ob_s[rb, :], gm_ref, wgate_ref, wba_ref, wbb_ref, wout_ref)
        q = _dot(_rms(h2, gq_ref[...]).astype(BF16), wq_ref[...])
        heads = [None] * MEM_HEADS

        def head(hd):
            cols = slice(hd * MEM_HEAD_DIM, (hd + 1) * MEM_HEAD_DIM)
            s = _dot_nt(q[:, cols].astype(BF16), mk_ref[0, :, cols]) * MEM_SCALE
            yield
            e = jnp.exp(s - jnp.max(s, axis=-1, keepdims=True))
            inv = 1.0 / jnp.sum(e, axis=-1, keepdims=True)
            yield
            heads[hd] = _dot(e.astype(BF16), mv_ref[0, :, cols]) * inv
            yield

        _issue_skewed([head(hd) for hd in range(MEM_HEADS)])
        oc = jnp.concatenate(heads, axis=1).astype(BF16)
        h3s.append(h2 + _dot(oc, wo_ref[...]))
    _ffn2_and_norm(jnp.concatenate(h3s, axis=0) if len(h3s) > 1 else h3s[0], g2_ref, wg_ref, wu_ref, wd_ref, gf_ref,
                   y_ref)


def _back_prompt_call(h, oa, ols, mk, mv, w, seq):
    t = h.shape[0]
    tm = TOKEN_TILE
    tiles_per_batch = seq // tm
    row = lambda i: (i, 0)
    mem = lambda i: (i // tiles_per_batch, 0, 0)
    tok = lambda width: pl.BlockSpec((tm, width), row)
    split = lambda d: pl.BlockSpec((1, d, tm // d, 2 * KV_B_W),
                                   lambda i: (i // tiles_per_batch, 0, i % tiles_per_batch, 0))
    in_specs = ([tok(D_MODEL), tok(Q_A_W)] + [split(d) for d in DILATIONS]
                + [pl.BlockSpec((1, N_MEM, MEM_W), mem), pl.BlockSpec((1, N_MEM, MEM_W), mem)]
                + [_const_spec((1, D_MODEL)), _const_spec((D_MODEL, 2 * D_MODEL), (0, 1)), _const_spec((Q_A_W, D_MODEL)),
                   _const_spec((KV_B_W, D_MODEL)), _const_spec((D_MODEL, D_MODEL)),
                   _const_spec((1, D_MODEL)), _const_spec((D_MODEL, MEM_W)), _const_spec((MEM_W, D_MODEL)),
                   _const_spec((1, D_MODEL)), _const_spec((D_MODEL, D_FF)), _const_spec((D_MODEL, D_FF)),
                   _const_spec((D_FF, D_MODEL)), _const_spec((1, D_MODEL))])
    return pl.pallas_call(
        _back_prompt_kernel,
        grid=(t // tm,),
        in_specs=in_specs,
        out_specs=tok(D_MODEL),
        out_shape=jax.ShapeDtypeStruct((t, D_MODEL), F32),
        scratch_shapes=[pltpu.VMEM((KV_B_W // LANES, tm, LANES), F32)] * 4 + [pltpu.VMEM((tm, KV_B_W), BF16)],
        compiler_params=pltpu.CompilerParams(dimension_semantics=("parallel",), vmem_limit_bytes=VMEM_LIMIT),
        name="back_prompt",
    )(h, oa, *ols, mk, mv, w["mix_norm"], w["w_in"], w["w_branch_a"], w["w_branch_b"], w["w_out"],
      w["mem_q_norm"], w["w_mem_q"], w["w_mem_o"], w["ffn2_norm"], w["ffn2_wg"], w["ffn2_wu"], w["ffn2_wd"],
      w["final_norm"])


def _back_sample_a_kernel(h_ref, oa_ref, ob_ref, gm_ref, wgate_ref, wba_ref, wbb_ref, wout_ref, gq_ref, wq_ref,
                          h2_ref, q_ref):
    h2 = _mix_out(h_ref[...], oa_ref[...], ob_ref[...], gm_ref, wgate_ref, wba_ref, wbb_ref, wout_ref)
    h2_ref[...] = h2
    q_ref[...] = _dot(_rms(h2, gq_ref[...]).astype(BF16), wq_ref[...])


def _back_sample_a_call(h, oa, ob, w):
    t = h.shape[0]
    tm = TOKEN_TILE
    tok = lambda width: pl.BlockSpec((tm, width), lambda i: (i, 0))
    return pl.pallas_call(
        _back_sample_a_kernel,
        grid=(t // tm,),
        in_specs=[tok(D_MODEL), tok(Q_A_W), tok(KV_B_W),
                  _const_spec((1, D_MODEL)), _const_spec((D_MODEL, 2 * D_MODEL), (0, 1)), _const_spec((Q_A_W, D_MODEL)),
                  _const_spec((KV_B_W, D_MODEL)), _const_spec((D_MODEL, D_MODEL)),
                  _const_spec((1, D_MODEL)), _const_spec((D_MODEL, MEM_W))],
        out_specs=[tok(D_MODEL), tok(MEM_W)],
        out_shape=[jax.ShapeDtypeStruct((t, D_MODEL), F32), jax.ShapeDtypeStruct((t, MEM_W), F32)],
        compiler_params=pltpu.CompilerParams(dimension_semantics=("parallel",), vmem_limit_bytes=VMEM_LIMIT),
        name="back_sample_mix",
    )(h, oa, ob, w["mix_norm"], w["w_in"], w["w_branch_a"], w["w_branch_b"], w["w_out"],
      w["mem_q_norm"], w["w_mem_q"])


CROSS_BATCH_TILE = 8


def _cross_sample_kernel(q_ref, mk_ref, mv_ref, o_ref):
    nrow = MEM_HEADS * SAMPLE_T
    qhead = lax.broadcasted_iota(jnp.int32, (nrow, N_MEM * MEM_HEADS), 0) >> (SAMPLE_T.bit_length() - 1)
    khead = lax.broadcasted_iota(jnp.int32, (nrow, N_MEM * MEM_HEADS), 1) & (MEM_HEADS - 1)
    own_head = qhead == khead
    def element(bi):
        rows = slice(bi * SAMPLE_T, (bi + 1) * SAMPLE_T)
        qs = jnp.concatenate(
            [q_ref[rows, hd * MEM_HEAD_DIM:(hd + 1) * MEM_HEAD_DIM] for hd in range(MEM_HEADS)], axis=0)
        s = _dot_nt(qs, mk_ref[bi]) * MEM_SCALE
        yield
        p = _softmax_rows(jnp.where(own_head, s, NEG_INF))
        yield
        o = _dot(p, mv_ref[bi])
        for hd in range(MEM_HEADS):
            o_ref[rows, hd * MEM_HEAD_DIM:(hd + 1) * MEM_HEAD_DIM] = o[hd * SAMPLE_T:(hd + 1) * SAMPLE_T]
        yield

    _issue_skewed([element(bi) for bi in range(CROSS_BATCH_TILE)])


def _cross_sample_call(q, mk, mv):
    t = q.shape[0]
    bt = CROSS_BATCH_TILE
    tok = pl.BlockSpec((bt * SAMPLE_T, MEM_W), lambda i: (i, 0))
    mem = pl.BlockSpec((bt, N_MEM * MEM_HEADS, MEM_HEAD_DIM), lambda i: (i, 0, 0))
    return pl.pallas_call(
        _cross_sample_kernel,
        grid=(mk.shape[0] // bt,),
        in_specs=[tok, mem, mem],
        out_specs=tok,
        out_shape=jax.ShapeDtypeStruct((t, MEM_W), F32),
        compiler_params=pltpu.CompilerParams(dimension_semantics=("parallel",), vmem_limit_bytes=VMEM_LIMIT),
        name="cross_sample",
    )(q, mk, mv)


def _back_sample_b_kernel(h2_ref, oc_ref, wo_ref, g2_ref, wg_ref, wu_ref, wd_ref, gf_ref, y_ref):
    h3 = h2_ref[...] + _dot(oc_ref[...].astype(BF16), wo_ref[...])
    _ffn2_and_norm(h3, g2_ref, wg_ref, wu_ref, wd_ref, gf_ref, y_ref)


def _back_sample_b_call(h2, oc, w):
    t = h2.shape[0]
    tm = TOKEN_TILE
    tok = lambda width: pl.BlockSpec((tm, width), lambda i: (i, 0))
    return pl.pallas_call(
        _back_sample_b_kernel,
        grid=(t // tm,),
        in_specs=[tok(D_MODEL), tok(MEM_W), _const_spec((MEM_W, D_MODEL)),
                  _const_spec((1, D_MODEL)), _const_spec((D_MODEL, D_FF)), _const_spec((D_MODEL, D_FF)),
                  _const_spec((D_FF, D_MODEL)), _const_spec((1, D_MODEL))],
        out_specs=tok(D_MODEL),
        out_shape=jax.ShapeDtypeStruct((t, D_MODEL), F32),
        compiler_params=pltpu.CompilerParams(dimension_semantics=("parallel",), vmem_limit_bytes=VMEM_LIMIT),
        name="back_sample_ffn",
    )(h2, oc, w["w_mem_o"], w["ffn2_norm"], w["ffn2_wg"], w["ffn2_wu"], w["ffn2_wd"], w["final_norm"])


def _mem_kv_kernel(mem_ref, g_ref, wk_ref, wv_ref, k32_ref, v32_ref, k16_ref, v16_ref):
    u = _rms(mem_ref[...], g_ref[...]).astype(BF16)
    k = _dot(u, wk_ref[...])
    v = _dot(u, wv_ref[...])
    rows = mem_ref.shape[0]
    for hd in range(MEM_HEADS):
        cols = slice(hd * MEM_HEAD_DIM, (hd + 1) * MEM_HEAD_DIM)
        k32_ref[pl.ds(hd, rows, stride=MEM_HEADS), :] = k[:, cols]
        v32_ref[pl.ds(hd, rows, stride=MEM_HEADS), :] = v[:, cols]
    k16_ref[...] = k.astype(BF16)
    v16_ref[...] = v.astype(BF16)


def _mem_kv_call(mem, w):
    rows = mem.shape[0]
    full = lambda width: pl.BlockSpec((rows, width), lambda i: (0, 0))
    return pl.pallas_call(
        _mem_kv_kernel,
        grid=(1,),
        in_specs=[full(D_MODEL), _const_spec((1, D_MODEL)), _const_spec((D_MODEL, MEM_W)),
                  _const_spec((D_MODEL, MEM_W))],
        out_specs=[pl.BlockSpec((rows * MEM_HEADS, MEM_HEAD_DIM), lambda i: (0, 0))] * 2 + [full(MEM_W)] * 2,
        out_shape=([jax.ShapeDtypeStruct((rows * MEM_HEADS, MEM_HEAD_DIM), F32)] * 2
                   + [jax.ShapeDtypeStruct((rows, MEM_W), BF16)] * 2),
        compiler_params=pltpu.CompilerParams(dimension_semantics=("arbitrary",), vmem_limit_bytes=VMEM_LIMIT),
        name="mem_kv",
    )(mem, w["mem_kv_norm"], w["w_mem_k"], w["w_mem_v"])


def _rope_tables(pos):
    inv_freq = jnp.power(jnp.float32(ROPE_THETA), -jnp.arange(ROT_HALF, dtype=jnp.float32) / ROT_HALF)
    ang = pos.astype(jnp.float32)[:, None] * inv_freq[None, :]
    lane = np.arange(LANES)
    within = lane % HEAD_DIM
    first = within < ROT_HALF
    second = (within >= ROT_HALF) & (within < 2 * ROT_HALF)
    pick_cos = np.zeros((ROT_HALF, LANES), np.float32)
    pick_sin = np.zeros((ROT_HALF, LANES), np.float32)
    pick_cos[within[first | second] % ROT_HALF, lane[first | second]] = 1.0
    pick_sin[within[first] % ROT_HALF, lane[first]] = -1.0
    pick_sin[within[second] % ROT_HALF, lane[second]] = 1.0
    spread = functools.partial(jnp.dot, precision=lax.Precision.HIGHEST)
    return (spread(jnp.cos(ang), pick_cos) + np.where(first | second, 0.0, 1.0).astype(np.float32)[None, :],
            spread(jnp.sin(ang), pick_sin))


def _prep_weights(p):
    bf = lambda x: x.astype(BF16)
    vec = lambda x: x.reshape(1, -1)
    return {
        "ffn1_norm": vec(p["ffn1_norm"]), "ffn1_wg": bf(p["ffn1_w_gate"]), "ffn1_wu": bf(p["ffn1_w_up"]),
        "ffn1_wd": bf(p["ffn1_w_down"]),
        "mix_norm": vec(p["mix_norm"]), "w_in": bf(p["w_in"]),
        "w_branch_a": bf(p["w_branch_a"].reshape(N_HEADS_A, HEAD_DIM, D_MODEL)[np.array(HEAD_PERM_A)].reshape(
            Q_A_W, D_MODEL)),
        "w_branch_b": bf(p["w_branch_b"]), "w_out": bf(p["w_out"]),
        "mem_q_norm": vec(p["mem_q_norm"]), "mem_kv_norm": vec(p["mem_kv_norm"]),
        "w_mem_q": bf(p["w_mem_q"]), "w_mem_k": bf(p["w_mem_k"]), "w_mem_v": bf(p["w_mem_v"]),
        "w_mem_o": bf(p["w_mem_o"]),
        "ffn2_norm": vec(p["ffn2_norm"]), "ffn2_wg": bf(p["ffn2_w_gate"]), "ffn2_wu": bf(p["ffn2_w_up"]),
        "ffn2_wd": bf(p["ffn2_w_down"]), "final_norm": vec(p["final_norm"]),
    }


def kernel(x_prompt, x_sample, cache_swa_k, cache_swa_v, cache_dil_k, cache_dil_v, cache_mem_k, cache_mem_v, mem_prompt, ffn1_norm, ffn1_w_gate, ffn1_w_up, ffn1_w_down, mix_norm, w_in, attn_sink, w_branch_a, w_branch_b, w_out, mem_q_norm, mem_kv_norm, w_mem_q, w_mem_k, w_mem_v, w_mem_o, ffn2_norm, ffn2_w_gate, ffn2_w_up, ffn2_w_down, final_norm):
    depth = ffn1_norm.shape[0]
    assert depth == 1
    batch, seq, d = x_prompt.shape
    nb, t_new, _ = x_sample.shape
    assert d == D_MODEL and t_new == SAMPLE_T and seq % (TOKEN_TILE * DILATIONS[-1]) == 0 and seq % ATTN_CHUNK == 0
    assert cache_swa_k.shape[2] == CACHE_A and cache_dil_k.shape[2] == CACHE_B
    layer = lambda x: x[0]
    w = _prep_weights(dict(
        ffn1_norm=layer(ffn1_norm), ffn1_w_gate=layer(ffn1_w_gate), ffn1_w_up=layer(ffn1_w_up),
        ffn1_w_down=layer(ffn1_w_down), mix_norm=layer(mix_norm), w_in=layer(w_in),
        w_branch_a=layer(w_branch_a), w_branch_b=layer(w_branch_b), w_out=layer(w_out),
        mem_q_norm=layer(mem_q_norm), mem_kv_norm=layer(mem_kv_norm), w_mem_q=layer(w_mem_q),
        w_mem_k=layer(w_mem_k), w_mem_v=layer(w_mem_v), w_mem_o=layer(w_mem_o), ffn2_norm=layer(ffn2_norm),
        ffn2_w_gate=layer(ffn2_w_gate), ffn2_w_up=layer(ffn2_w_up), ffn2_w_down=layer(ffn2_w_down),
        final_norm=final_norm))
    sink = layer(attn_sink)

    ts = nb * t_new
    pos_s = PAST_LEN + (jnp.arange(TOKEN_TILE, dtype=jnp.int32) % t_new)
    tables_s = _rope_tables(pos_s)
    (h_s, q32, ka32_s, va32_s, kb32_s, vb32_s) = _front_call(x_sample.reshape(ts, d), tables_s, w, sample=True)
    by_pos = lambda c: jnp.transpose(layer(c), (0, 2, 3, 1)).reshape(nb, c.shape[3] * c.shape[4], c.shape[2])
    by_slot_head = lambda c: layer(c).reshape(nb, N_MEM * MEM_HEADS, MEM_HEAD_DIM)

    oa_s, ob_s = _attn_sample_call(
        sink, q32, ka32_s, va32_s, kb32_s, vb32_s,
        by_pos(cache_swa_k), by_pos(cache_swa_v), by_pos(cache_dil_k), by_pos(cache_dil_v))

    tp = batch * seq
    mk32, mv32, mk16, mv16 = _mem_kv_call(mem_prompt.reshape(batch * N_MEM, d), w)
    tables_p = _rope_tables(jnp.arange(seq, dtype=jnp.int32))
    h_p, main, split4, split16, tails_kva, tails_kb, tails_vb = _front_call(
        x_prompt.reshape(tp, d), tables_p, w, sample=False, batch=batch)
    oa = _attn_a_call(sink, main, batch, seq)
    groups = ((main.reshape(batch, 1, seq, MAIN_W), MAIN_QB_COL), (split4, 0), (split16, 0))
    ols = [_attn_b_call(qkv, qcol, dil) for (qkv, qcol), dil in zip(groups, DILATIONS)]
    y_p = _back_prompt_call(h_p, oa, ols, mk16.reshape(batch, N_MEM, MEM_W),
                            mv16.reshape(batch, N_MEM, MEM_W), w, seq)

    h2_s, qc_s = _back_sample_a_call(h_s, oa_s, ob_s, w)
    oc_s = _cross_sample_call(qc_s, by_slot_head(cache_mem_k), by_slot_head(cache_mem_v))
    y_s = _back_sample_b_call(h2_s, oc_s, w)

    keep_a, keep_b = min(CACHE_A, seq), min(CACHE_B, seq)
    assert max(keep_a, keep_b) <= TAIL_ROWS <= seq and TAIL_ROWS % TOKEN_TILE == 0

    def tail(x, col, keep, heads):
        part = x[:, col:col + heads * HEAD_DIM, TAIL_ROWS - keep:].reshape(batch, heads, HEAD_DIM, keep)
        return jnp.transpose(part, (0, 3, 1, 2))[None]

    new = lambda x, heads: x.reshape(1, nb, t_new, heads, HEAD_DIM)
    memo = lambda x: x.reshape(1, batch, N_MEM, MEM_HEADS, MEM_HEAD_DIM)
    return (y_p.reshape(batch, seq, d), y_s.reshape(nb, t_new, d),
            tail(tails_kva, 0, keep_a, N_KV_A), tail(tails_kva, KV_A_W, keep_a, N_KV_A),
            tail(tails_kb, 0, keep_b, N_KV_B), tail(tails_vb, 0, keep_b, N_KV_B),
            memo(mk32), memo(mv32),
            new(ka32_s, N_KV_A), new(va32_s, N_KV_A), new(kb32_s, N_KV_B), new(vb32_s, N_KV_B))
```

```python
import functools

import jax
import jax.numpy as jnp
import numpy as np
from jax import lax
from jax.experimental import pallas as pl
from jax.experimental.pallas import tpu as pltpu

F32 = jnp.float32
BF16 = jnp.bfloat16

D_MODEL = 1024
D_FF = 2816
HEAD_DIM = 64
ROT_HALF = 8
ROPE_THETA = 500000.0
ATTN_SCALE = HEAD_DIM ** -0.5
RMS_EPS = 1e-6
PAST_LEN = 16384

N_HEADS_A = 8
N_KV_A = 2
N_KV_B = 4
DILATIONS = (1, 4, 16)
BAND_STEPS = 128
Q_A_W = N_HEADS_A * HEAD_DIM
KV_A_W = N_KV_A * HEAD_DIM
Q_B_W = len(DILATIONS) * N_KV_B * HEAD_DIM
KV_B_W = N_KV_B * HEAD_DIM
QKV_W = Q_A_W + 2 * KV_A_W + Q_B_W + 2 * KV_B_W
MEM_HEADS = 4
MEM_HEAD_DIM = 128
MEM_W = MEM_HEADS * MEM_HEAD_DIM
MEM_SCALE = MEM_HEAD_DIM ** -0.5
N_MEM = 256

HEAD_PERM_A = (0, 4, 1, 5, 2, 6, 3, 7)

MAIN_W = Q_A_W + 2 * KV_A_W + 3 * KV_B_W
MAIN_KA_COL, MAIN_VA_COL = Q_A_W // KV_A_W, Q_A_W // KV_A_W + 1
MAIN_QB_COL = (Q_A_W + 2 * KV_A_W) // KV_B_W
SPLIT_W = 3 * KV_B_W
TAIL_ROWS = 2048
assert 2 * KV_A_W == KV_B_W

LANES = 128
BLOCK = 128
TOKEN_TILE = 512
FRONT_ROW_PARTS = 2
BACK_ROW_PARTS = 1
FFN_ROW_PARTS = 2
ATTN_CHUNK = 2048
FF_CHUNKS = ((0, 1536), (1536, 2816))
VMEM_LIMIT = 60 * 1024 * 1024
NEG_INF = float("-inf")


def _rms(x, g):
    ms = jnp.mean(x * x, axis=-1, keepdims=True)
    return x * lax.rsqrt(ms + RMS_EPS) * g


def _sigmoid(x):
    return 0.5 * jnp.tanh(0.5 * x) + 0.5


def _dot(a, b):
    return jnp.dot(a, b, preferred_element_type=F32)


def _dot_nt(a, b):
    return lax.dot_general(a, b, (((1,), (1,)), ((), ())), preferred_element_type=F32)


def _ffn_half(xs, g_ref, wg_ref, wu_ref, wd_ref):
    us = [_rms(x, g_ref[...]).astype(BF16) for x in xs]
    accs = [None] * len(xs)
    for lo, hi in FF_CHUNKS:
        gates = [_dot(u, wg_ref[:, lo:hi]) for u in us]
        ups = [_dot(u, wu_ref[:, lo:hi]) for u in us]
        acts = [(gate * _sigmoid(gate) * up).astype(BF16) for gate, up in zip(gates, ups)]
        for i, act in enumerate(acts):
            part = _dot(act, wd_ref[lo:hi, :])
            accs[i] = part if accs[i] is None else accs[i] + part
    return [x + 0.5 * acc for x, acc in zip(xs, accs)]


def _ffn2_and_norm(h3, g2_ref, wg_ref, wu_ref, wd_ref, gf_ref, y_ref):
    blocks = _row_blocks(y_ref, FFN_ROW_PARTS)
    for rb, y in zip(blocks, _ffn_half([h3[rb, :] for rb in blocks], g2_ref, wg_ref, wu_ref, wd_ref)):
        y_ref[rb, :] = _rms(y, gf_ref[...])


def _row_blocks(ref, parts):
    n = ref.shape[0] // parts
    return [slice(i * n, (i + 1) * n) for i in range(parts)]


def _swap_halves(x):
    return pltpu.roll(x, HEAD_DIM, 1)


def _low_half():
    return lax.broadcasted_iota(jnp.int32, (1, LANES), 1) < HEAD_DIM


def _pair_kv_groups(chunks):
    low = _low_half()
    c0, c1, c2, c3 = chunks
    return [jnp.where(low, c0, _swap_halves(c2)), jnp.where(low, _swap_halves(c0), c2),
            jnp.where(low, c1, _swap_halves(c3)), jnp.where(low, _swap_halves(c1), c3)]


def _merge_pairs(per_head):
    low = _low_half()
    return [jnp.where(low, per_head[2 * c], per_head[2 * c + 1]) for c in range(N_HEADS_A // 2)]


def _front_kernel(x_ref, cos_ref, sin_ref, g1_ref, wg_ref, wu_ref, wd_ref, gm_ref, wqkv_ref,
                  h_ref, *rest, sample, tiles_per_seq=None, tail_tiles=None):
    blocks = _row_blocks(x_ref, FRONT_ROW_PARTS)
    hs = _ffn_half([x_ref[rb, :] for rb in blocks], g1_ref, wg_ref, wu_ref, wd_ref)
    for rb, h in zip(blocks, hs):
        h_ref[rb, :] = h
    ns = [_rms(h, gm_ref[...]).astype(BF16) for h in hs]
    zs = [_dot(n, wqkv_ref[...]) for n in ns]
    for rb, z in zip(blocks, zs):
        _front_emit(rb, z, cos_ref, sin_ref, rest, sample)
    if not sample:
        tail_refs, tail_scratch = rest[3:6], (rest[10], rest[8], rest[9])

        @pl.when(lax.rem(pl.program_id(0), tiles_per_seq) >= tiles_per_seq - tail_tiles)
        def _():
            for ref, scr in zip(tail_refs, tail_scratch):
                for c in range(scr.shape[0]):
                    ref[0, c * LANES:(c + 1) * LANES, :] = scr[c].T


def _front_emit(rb, z, cos_ref, sin_ref, rest, sample):
    cos, sin = cos_ref[rb, :], sin_ref[rb, :]
    lane_in_head = lax.broadcasted_iota(jnp.int32, (1, LANES), 1) & (HEAD_DIM - 1)
    first = lane_in_head < ROT_HALF

    def chunk(c):
        return z[:, c * LANES:(c + 1) * LANES]

    def rope(c):
        zc = chunk(c)
        partner = jnp.where(first, pltpu.roll(zc, LANES - ROT_HALF, 1), pltpu.roll(zc, ROT_HALF, 1))
        return zc * cos + partner * sin

    nqa, nka = Q_A_W // LANES, KV_A_W // LANES
    nqb, nkb = Q_B_W // LANES, KV_B_W // LANES
    c0 = 0
    qa = _pair_kv_groups([rope(c0 + c) * ATTN_SCALE for c in range(nqa)])
    c0 += nqa
    ka = [rope(c0 + c) for c in range(nka)]
    c0 += nka
    va = [chunk(c0 + c) for c in range(nka)]
    c0 += nka
    qb = [rope(c0 + c) * ATTN_SCALE for c in range(nqb)]
    c0 += nqb
    kb = [rope(c0 + c) for c in range(nkb)]
    c0 += nkb
    vb = [chunk(c0 + c) for c in range(nkb)]

    def store(ref, parts, dtype):
        for c, p in enumerate(parts):
            ref[rb, c * LANES:(c + 1) * LANES] = p.astype(dtype)

    def fill(scr, parts):
        for c, p in enumerate(parts):
            scr[c, rb, :] = p

    def store_split(ref, slot, scr, d):
        n = (rb.stop - rb.start) // d
        dst = slice(rb.start // d, rb.start // d + n)
        for r in range(d):
            for c in range(scr.shape[0]):
                lanes = slice(slot * KV_B_W + c * LANES, slot * KV_B_W + (c + 1) * LANES)
                ref[0, r, dst, lanes] = scr[c, pl.ds(rb.start + r, n, stride=d), :].astype(BF16)

    if sample:
        q32_ref, ka32_ref, va32_ref, kb32_ref, vb32_ref = rest
        store(q32_ref, qa + qb, F32)
        store(ka32_ref, ka, F32)
        store(va32_ref, va, F32)
        store(kb32_ref, kb, F32)
        store(vb32_ref, vb, F32)
    else:
        main_ref, s4_ref, s16_ref, _, _, _, q4_s, q16_s, kb_s, vb_s, kva_s = rest
        gw = KV_B_W // LANES
        store(main_ref, qa + ka + va + qb[0:gw] + kb + vb, BF16)
        fill(q4_s, qb[gw:2 * gw])
        fill(q16_s, qb[2 * gw:3 * gw])
        fill(kb_s, kb)
        fill(vb_s, vb)
        fill(kva_s, ka + va)
        for slot, scr4, scr16 in ((0, q4_s, q16_s), (1, kb_s, kb_s), (2, vb_s, vb_s)):
            store_split(s4_ref, slot, scr4, DILATIONS[1])
            store_split(s16_ref, slot, scr16, DILATIONS[2])


def _const_spec(shape, index=None):
    index = (0,) * len(shape) if index is None else index
    return pl.BlockSpec(shape, lambda *_: index, pipeline_mode=pl.Buffered(1))


def _front_call(x, tables, w, *, sample, batch=1):
    t = x.shape[0]
    tm = TOKEN_TILE
    nsteps = t // tm
    tpb = nsteps // batch
    seq = t // batch
    cos_t, sin_t = tables
    tab_blocks = cos_t.shape[0] // tm
    row = lambda i: (i, 0)
    tab = lambda i: (i % tab_blocks, 0)

    def tok(width, dtype):
        return jax.ShapeDtypeStruct((t, width), dtype), pl.BlockSpec((tm, width), row)

    def split(d):
        return (jax.ShapeDtypeStruct((batch, d, seq // d, SPLIT_W), BF16),
                pl.BlockSpec((1, d, tm // d, SPLIT_W), lambda i: (i // tpb, 0, i % tpb, 0)))

    outs = [tok(D_MODEL, F32)]
    scratch = []
    static = {}
    if sample:
        outs += [tok(Q_A_W + Q_B_W, F32), tok(KV_A_W, F32), tok(KV_A_W, F32), tok(KV_B_W, F32), tok(KV_B_W, F32)]
    else:
        tail_tiles = TAIL_ROWS // tm
        tails = (jax.ShapeDtypeStruct((batch, KV_B_W, TAIL_ROWS), F32),
                 pl.BlockSpec((1, KV_B_W, tm),
                              lambda i: (i // tpb, 0, jnp.maximum(i % tpb - (tpb - tail_tiles), 0))))
        outs += [tok(MAIN_W, BF16), split(DILATIONS[1]), split(DILATIONS[2])] + [tails] * 3
        scratch = [pltpu.VMEM((KV_B_W // LANES, tm, LANES), F32)] * 4 + [pltpu.VMEM((2, tm, LANES), F32)]
        static = dict(tiles_per_seq=tpb, tail_tiles=tail_tiles)
    out_shape, out_specs = zip(*outs)
    in_specs = [pl.BlockSpec((tm, D_MODEL), row),
                pl.BlockSpec((tm, LANES), tab), pl.BlockSpec((tm, LANES), tab),
                _const_spec((1, D_MODEL)), _const_spec((D_MODEL, D_FF)), _const_spec((D_MODEL, D_FF)),
                _const_spec((D_FF, D_MODEL)), _const_spec((1, D_MODEL)), _const_spec((D_MODEL, QKV_W), (0, 0))]
    return pl.pallas_call(
        functools.partial(_front_kernel, sample=sample, **static),
        grid=(nsteps,),
        in_specs=in_specs,
        out_specs=list(out_specs),
        out_shape=list(out_shape),
        scratch_shapes=scratch,
        compiler_params=pltpu.CompilerParams(dimension_semantics=("parallel" if sample else "arbitrary",),
                                             vmem_limit_bytes=VMEM_LIMIT),
        name="front_sample" if sample else "front_prompt",
    )(x, cos_t, sin_t, w["ffn1_norm"], w["ffn1_wg"], w["ffn1_wu"], w["ffn1_wd"], w["mix_norm"], w["w_in"])


def _lane_block_masks(width, block, dtype):
    lane = lax.broadcasted_iota(jnp.int32, (1, width), 1)
    return [((lane >= i * block) & (lane < (i + 1) * block)).astype(dtype) for i in range(width // block)]


def _band_mask():
    qi = lax.broadcasted_iota(jnp.int32, (BLOCK, 2 * BLOCK), 0)
    kj = lax.broadcasted_iota(jnp.int32, (BLOCK, 2 * BLOCK), 1)
    dist = qi + BLOCK - kj
    return (dist >= 0) & (dist <= BAND_STEPS), kj >= BLOCK


def _band_block(load_qbd, load_k, load_v, mask, nheads, finish):
    s = _dot_nt(load_qbd(), load_k())
    yield
    ps, ms, ls = [], [], []
    for p in range(nheads):
        sp = jnp.where(mask, s[p * BLOCK:(p + 1) * BLOCK], NEG_INF)
        m = jnp.max(sp, axis=-1, keepdims=True)
        e = jnp.exp(sp - m)
        ls.append(jnp.sum(e, axis=-1, keepdims=True))
        ms.append(m)
        ps.append(e.astype(BF16))
    pc = jnp.concatenate(ps, axis=0)
    yield
    finish(_dot(pc, load_v()), ms, ls)
    yield


def _issue_skewed(items, nstages=3):
    for tick in range(len(items) + nstages - 1):
        for s in reversed(range(nstages)):
            i = tick - s
            if 0 <= i < len(items):
                next(items[i])


def _kv_window(cur_ref, prev_ref, qb):
    if qb == 0:
        return jnp.concatenate([prev_ref[...], cur_ref[0:BLOCK]], axis=0)
    return cur_ref[(qb - 1) * BLOCK:(qb + 1) * BLOCK]


def _attn_a_kernel(sink_ref, q_ref, kc_ref, kp_ref, vc_ref, vp_ref, o_ref):
    first_chunk = pl.program_id(1) == 0
    band, in_cur = _band_mask()
    half_bf = _lane_block_masks(LANES, HEAD_DIM, BF16)

    def block(qb):
        mask = band & (in_cur | jnp.logical_not(first_chunk)) if qb == 0 else band
        rows = slice(qb * BLOCK, (qb + 1) * BLOCK)

        def load_qbd():
            return jnp.concatenate(
                [q_ref[rows, (p // 2) * LANES:(p // 2 + 1) * LANES] * half_bf[p % 2] for p in range(N_HEADS_A)],
                axis=0)

        def finish(o, ms, ls):
            normed = []
            for p in range(N_HEADS_A):
                den = ls[p] + jnp.exp(sink_ref[HEAD_PERM_A[p]] - ms[p])
                normed.append(o[p * BLOCK:(p + 1) * BLOCK] * (1.0 / den))
            for c, pair in enumerate(_merge_pairs(normed)):
                o_ref[rows, c * LANES:(c + 1) * LANES] = pair.astype(BF16)

        return _band_block(load_qbd, lambda: _kv_window(kc_ref, kp_ref, qb), lambda: _kv_window(vc_ref, vp_ref, qb),
                           mask, N_HEADS_A, finish)

    for qb in range(q_ref.shape[0] // BLOCK):
        for _ in block(qb):
            pass


def _attn_a_call(sink, main, batch, seq):
    t = main.shape[0]
    cpb = seq // ATTN_CHUNK
    bpc = ATTN_CHUNK // BLOCK
    cur = lambda col: (lambda b, c: (b * cpb + c, col))
    prev = lambda col: (lambda b, c: (b * cpb * bpc + jnp.maximum(c * bpc - 1, 0), col))
    kv = lambda col: [pl.BlockSpec((ATTN_CHUNK, KV_A_W), cur(col)), pl.BlockSpec((BLOCK, KV_A_W), prev(col))]
    return pl.pallas_call(
        _attn_a_kernel,
        grid=(batch, cpb),
        in_specs=[pl.BlockSpec(memory_space=pltpu.SMEM), pl.BlockSpec((ATTN_CHUNK, Q_A_W), cur(0))]
        + kv(MAIN_KA_COL) + kv(MAIN_VA_COL),
        out_specs=pl.BlockSpec((ATTN_CHUNK, Q_A_W), cur(0)),
        out_shape=jax.ShapeDtypeStruct((t, Q_A_W), BF16),
        compiler_params=pltpu.CompilerParams(dimension_semantics=("parallel", "parallel"),
                                             vmem_limit_bytes=VMEM_LIMIT),
        name="attn_a_prompt",
    )(sink, main, main, main, main, main)


def _attn_b_kernel(q_ref, kc_ref, kp_ref, vc_ref, vp_ref, ol_ref):
    first_chunk = pl.program_id(2) == 0
    band, in_cur = _band_mask()
    head_bf = _lane_block_masks(KV_B_W, HEAD_DIM, BF16)
    lane = lax.broadcasted_iota(jnp.int32, (1, KV_B_W), 1)

    def block(r, qb):
        mask = band & (in_cur | jnp.logical_not(first_chunk)) if qb == 0 else band
        rows = slice(qb * BLOCK, (qb + 1) * BLOCK)

        def load_qbd():
            q = q_ref[r, rows, :]
            return jnp.concatenate([q * head_bf[p] for p in range(N_KV_B)], axis=0)

        def finish(o, ms, ls):
            out = lse = None
            for p in reversed(range(N_KV_B)):
                op = o[p * BLOCK:(p + 1) * BLOCK] * (1.0 / ls[p])
                lp = jnp.broadcast_to(ms[p] + jnp.log(ls[p]), (BLOCK, KV_B_W))
                if out is None:
                    out, lse = op, lp
                else:
                    sel = lane < (p + 1) * HEAD_DIM
                    out, lse = jnp.where(sel, op, out), jnp.where(sel, lp, lse)
            ol_ref[r, rows, 0:KV_B_W] = out
            ol_ref[r, rows, KV_B_W:2 * KV_B_W] = lse

        return _band_block(load_qbd, lambda: _kv_window(kc_ref.at[r], kp_ref.at[r], qb),
                           lambda: _kv_window(vc_ref.at[r], vp_ref.at[r], qb), mask, N_KV_B, finish)

    _issue_skewed([block(r, qb) for r in range(q_ref.shape[0]) for qb in range(q_ref.shape[1] // BLOCK)])


def _attn_b_call(qkv, qcol, d):
    batch, _, rows, _ = qkv.shape
    chunk = min(ATTN_CHUNK, rows)
    rps = min(d, ATTN_CHUNK // chunk)
    cpb = rows // chunk
    bpc = chunk // BLOCK
    cur = lambda col: pl.BlockSpec((None, rps, chunk, KV_B_W), lambda b, r, c: (b, r, c, col))
    prev = lambda col: pl.BlockSpec((None, rps, BLOCK, KV_B_W),
                                    lambda b, r, c: (b, r, jnp.maximum(c * bpc - 1, 0), col))
    return pl.pallas_call(
        _attn_b_kernel,
        grid=(batch, d // rps, cpb),
        in_specs=[cur(qcol), cur(qcol + 1), prev(qcol + 1), cur(qcol + 2), prev(qcol + 2)],
        out_specs=pl.BlockSpec((None, rps, chunk, 2 * KV_B_W), lambda b, r, c: (b, r, c, 0)),
        out_shape=jax.ShapeDtypeStruct((batch, d, rows, 2 * KV_B_W), F32),
        compiler_params=pltpu.CompilerParams(dimension_semantics=("parallel", "parallel", "parallel"),
                                             vmem_limit_bytes=VMEM_LIMIT),
        name=f"attn_b_prompt_d{d}",
    )(qkv, qkv, qkv, qkv, qkv)


SAMPLE_T = 8
CACHE_A = 128
CACHE_B = 2048
SAMPLE_ELEMS = 4


def _new_rows_block(new):
    return jnp.concatenate([new, jnp.zeros((BLOCK - SAMPLE_T, new.shape[1]), F32)], axis=0)


def _cached_scores(qbd, kt, knew):
    return _dot(qbd, kt), _dot_nt(qbd, knew)


def _cached_softmax(s_c, s_n, mask_c, mask_n):
    s_c = jnp.where(mask_c, s_c, NEG_INF)
    s_n = jnp.where(mask_n, s_n, NEG_INF)
    m = jnp.maximum(jnp.max(s_c, axis=-1, keepdims=True), jnp.max(s_n, axis=-1, keepdims=True))
    e_c = jnp.exp(s_c - m)
    e_n = jnp.exp(s_n - m)
    l = jnp.sum(e_c, axis=-1, keepdims=True) + jnp.sum(e_n, axis=-1, keepdims=True)
    return e_c, e_n, m, l


def _cached_pv(e_c, e_n, vt, vnew):
    return _dot_nt(e_c, vt) + _dot(e_n, vnew)


def _attn_sample_kernel(sink_ref, q_ref, kan_ref, van_ref, kbn_ref, vbn_ref, cka_ref, cva_ref, ckb_ref, cvb_ref,
                        oa_ref, ob_ref):
    half_f = _lane_block_masks(LANES, HEAD_DIM, F32)
    lane_b = lax.broadcasted_iota(jnp.int32, (1, KV_B_W), 1)

    def dist(nrows, ncols, offset):
        r = lax.broadcasted_iota(jnp.int32, (nrows, ncols), 0)
        i = lax.broadcasted_iota(jnp.int32, (nrows, ncols), 1)
        return r, offset + (r & (SAMPLE_T - 1)) - i

    nrow_a = N_HEADS_A * SAMPLE_T
    _, dac = dist(nrow_a, CACHE_A, CACHE_A)
    _, dan = dist(nrow_a, BLOCK, 0)
    mask_ac = (dac >= 0) & (dac <= BAND_STEPS)
    mask_an = (dan >= 0) & (dan <= BAND_STEPS)
    nrow_b = len(DILATIONS) * N_KV_B * SAMPLE_T
    rows_per_group = N_KV_B * SAMPLE_T

    def mask_b(ncols, offset):
        r, db = dist(nrow_b, ncols, offset)
        dil = jnp.where(r < rows_per_group, DILATIONS[0], jnp.where(r < 2 * rows_per_group, DILATIONS[1], DILATIONS[2]))
        return (db >= 0) & (db <= BAND_STEPS * dil) & ((db & (dil - 1)) == 0)

    mask_bc = mask_b(CACHE_B, CACHE_B)
    mask_bn = mask_b(BLOCK, 0)

    sink_col = jnp.concatenate(
        [jnp.full((SAMPLE_T, 1), sink_ref[HEAD_PERM_A[p]], F32) for p in range(N_HEADS_A)], axis=0)

    def element(j):
        rows = slice(j * SAMPLE_T, (j + 1) * SAMPLE_T)
        qbd_a = jnp.concatenate(
            [q_ref[rows, (p // 2) * LANES:(p // 2 + 1) * LANES] * half_f[p % 2] for p in range(N_HEADS_A)],
            axis=0)
        qrows = []
        for g in range(len(DILATIONS)):
            for kvh in range(N_KV_B):
                c = (Q_A_W // LANES) + 2 * g + kvh // 2
                part = q_ref[rows, c * LANES:(c + 1) * LANES] * half_f[kvh % 2]
                zero = jnp.zeros_like(part)
                qrows.append(jnp.concatenate([part, zero] if kvh // 2 == 0 else [zero, part], axis=1))
        qbd_b = jnp.concatenate(qrows, axis=0)
        scores_a = _cached_scores(qbd_a, cka_ref[j], _new_rows_block(kan_ref[rows, :]))
        scores_b = _cached_scores(qbd_b, ckb_ref[j], _new_rows_block(kbn_ref[rows, :]))
        yield
        ea_c, ea_n, ma, la = _cached_softmax(*scores_a, mask_ac, mask_an)
        eb_c, eb_n, m, l = _cached_softmax(*scores_b, mask_bc, mask_bn)
        yield
        o = _cached_pv(ea_c, ea_n, cva_ref[j], _new_rows_block(van_ref[rows, :]))
        o = o * (1.0 / (la + jnp.exp(sink_col - ma)))
        per_head = [o[p * SAMPLE_T:(p + 1) * SAMPLE_T] for p in range(N_HEADS_A)]
        for c, pair in enumerate(_merge_pairs(per_head)):
            oa_ref[rows, c * LANES:(c + 1) * LANES] = pair
        o = _cached_pv(eb_c, eb_n, cvb_ref[j], _new_rows_block(vbn_ref[rows, :]))
        out = None
        for kvh in reversed(range(N_KV_B)):
            sl = [slice((g * N_KV_B + kvh) * SAMPLE_T, (g * N_KV_B + kvh + 1) * SAMPLE_T)
                  for g in range(len(DILATIONS))]
            mj = jnp.maximum(jnp.maximum(m[sl[0]], m[sl[1]]), m[sl[2]])
            ws = [jnp.exp(m[x] - mj) for x in sl]
            den = ws[0] * l[sl[0]] + ws[1] * l[sl[1]] + ws[2] * l[sl[2]]
            num = ws[0] * o[sl[0]] + ws[1] * o[sl[1]] + ws[2] * o[sl[2]]
            okv = num * (1.0 / den)
            out = okv if out is None else jnp.where(lane_b < (kvh + 1) * HEAD_DIM, okv, out)
        ob_ref[rows, :] = out
        yield

    elements = [element(j) for j in range(SAMPLE_ELEMS)]
    for _ in range(3):
        for e in elements:
            next(e)


def _attn_sample_call(sink, q32, ka32, va32, kb32, vb32, cka, cva, ckb, cvb):
    nb = cka.shape[0]
    bt = SAMPLE_ELEMS
    t = q32.shape[0]
    tok = lambda w: pl.BlockSpec((bt * SAMPLE_T, w), lambda i: (i, 0))
    cache = lambda w, n: pl.BlockSpec((bt, w, n), lambda i: (i, 0, 0))
    return pl.pallas_call(
        _attn_sample_kernel,
        grid=(nb // bt,),
        in_specs=[pl.BlockSpec(memory_space=pltpu.SMEM), tok(Q_A_W + Q_B_W),
                  tok(KV_A_W), tok(KV_A_W), tok(KV_B_W), tok(KV_B_W),
                  cache(KV_A_W, CACHE_A), cache(KV_A_W, CACHE_A), cache(KV_B_W, CACHE_B), cache(KV_B_W, CACHE_B)],
        out_specs=[tok(Q_A_W), tok(KV_B_W)],
        out_shape=[jax.ShapeDtypeStruct((t, Q_A_W), F32), jax.ShapeDtypeStruct((t, KV_B_W), F32)],
        compiler_params=pltpu.CompilerParams(dimension_semantics=("parallel",), vmem_limit_bytes=VMEM_LIMIT),
        name="attn_sample",
    )(sink, q32, ka32, va32, kb32, vb32, cka, cva, ckb, cvb)


def _mix_out(h, oa, ob, gm_ref, wgate_ref, wba_ref, wbb_ref, wout_ref):
    branch_a = _dot(oa.astype(BF16), wba_ref[...])
    n = _rms(h, gm_ref[...]).astype(BF16)
    gates = _dot(n, wgate_ref[...])
    branch_b = _dot(ob.astype(BF16), wbb_ref[...])
    mixed = _sigmoid(gates[:, :D_MODEL]) * branch_a + _sigmoid(gates[:, D_MODEL:]) * branch_b
    return h + _dot(mixed.astype(BF16), wout_ref[...])


def _combine_groups(os, ls):
    m = jnp.maximum(jnp.maximum(ls[0], ls[1]), ls[2])
    es = [jnp.exp(x - m) for x in ls]
    den = es[0] + es[1] + es[2]
    num = es[0] * os[0] + es[1] * os[1] + es[2] * os[2]
    return num * (1.0 / den)


def _merge_split(split_ref, field, scr, rb):
    d = split_ref.shape[1]
    if d == 1:
        return split_ref[0, 0, rb, field * KV_B_W:(field + 1) * KV_B_W]
    n = (rb.stop - rb.start) // d
    src = slice(rb.start // d, rb.start // d + n)
    for r in range(d):
        for c in range(scr.shape[0]):
            lanes = slice(field * KV_B_W + c * LANES, field * KV_B_W + (c + 1) * LANES)
            scr[c, pl.ds(rb.start + r, n, stride=d), :] = split_ref[0, r, src, lanes]
    return jnp.concatenate([scr[c, rb, :] for c in range(scr.shape[0])], axis=1)


def _softmax_rows(s):
    m = jnp.max(s, axis=-1, keepdims=True)
    e = jnp.exp(s - m)
    return e * (1.0 / jnp.sum(e, axis=-1, keepdims=True))


def _back_prompt_kernel(h_ref, oa_ref, ol0_ref, ol1_ref, ol2_ref, mk_ref, mv_ref,
                        gm_ref, wgate_ref, wba_ref, wbb_ref, wout_ref, gq_ref, wq_ref, wo_ref,
                        g2_ref, wg_ref, wu_ref, wd_ref, gf_ref, y_ref, o1_s, o2_s, l1_s, l2_s, ob_s):
    blocks = _row_blocks(h_ref, BACK_ROW_PARTS)
    h3s = []
    for rb in blocks:
        ob_s[rb, :] = _combine_groups(
            (_merge_split(ol0_ref, 0, None, rb), _merge_split(ol1_ref, 0, o1_s, rb),
             _merge_split(ol2_ref, 0, o2_s, rb)),
            (_merge_split(ol0_ref, 1, None, rb), _merge_split(ol1_ref, 1, l1_s, rb),
             _merge_split(ol2_ref, 1, l2_s, rb))).astype(BF16)
        h2 = _mix_out(h_ref[rb, :], oa_ref[rb, :], ob_s[rb, :], gm_ref, wgate_ref, wba_ref, wbb_ref, wout_ref)
        q = _dot(_rms(h2, gq_ref[...]).astype(BF16), wq_ref[...])
        heads = [None] * MEM_HEADS

        def head(hd):
            cols = slice(hd * MEM_HEAD_DIM, (hd + 1) * MEM_HEAD_DIM)
            s = _dot_nt(q[:, cols].astype(BF16), mk_ref[0, :, cols]) * MEM_SCALE
            yield
            e = jnp.exp(s - jnp.max(s, axis=-1, keepdims=True))
            inv = 1.0 / jnp.sum(e, axis=-1, keepdims=True)
            yield
            heads[hd] = _dot(e.astype(BF16), mv_ref[0, :, cols]) * inv
            yield

        _issue_skewed([head(hd) for hd in range(MEM_HEADS)])
        oc = jnp.concatenate(heads, axis=1).astype(BF16)
        h3s.append(h2 + _dot(oc, wo_ref[...]))
    _ffn2_and_norm(jnp.concatenate(h3s, axis=0) if len(h3s) > 1 else h3s[0], g2_ref, wg_ref, wu_ref, wd_ref, gf_ref,
                   y_ref)


def _back_prompt_call(h, oa, ols, mk, mv, w, seq):
    t = h.shape[0]
    tm = TOKEN_TILE
    tiles_per_batch = seq // tm
    row = lambda i: (i, 0)
    mem = lambda i: (i // tiles_per_batch, 0, 0)
    tok = lambda width: pl.BlockSpec((tm, width), row)
    split = lambda d: pl.BlockSpec((1, d, tm // d, 2 * KV_B_W),
                                   lambda i: (i // tiles_per_batch, 0, i % tiles_per_batch, 0))
    in_specs = ([tok(D_MODEL), tok(Q_A_W)] + [split(d) for d in DILATIONS]
                + [pl.BlockSpec((1, N_MEM, MEM_W), mem), pl.BlockSpec((1, N_MEM, MEM_W), mem)]
                + [_const_spec((1, D_MODEL)), _const_spec((D_MODEL, 2 * D_MODEL), (0, 1)), _const_spec((Q_A_W, D_MODEL)),
                   _const_spec((KV_B_W, D_MODEL)), _const_spec((D_MODEL, D_MODEL)),
                   _const_spec((1, D_MODEL)), _const_spec((D_MODEL, MEM_W)), _const_spec((MEM_W, D_MODEL)),
                   _const_spec((1, D_MODEL)), _const_spec((D_MODEL, D_FF)), _const_spec((D_MODEL, D_FF)),
                   _const_spec((D_FF, D_MODEL)), _const_spec((1, D_MODEL))])
    return pl.pallas_call(
        _back_prompt_kernel,
        grid=(t // tm,),
        in_specs=in_specs,
        out_specs=tok(D_MODEL),
        out_shape=jax.ShapeDtypeStruct((t, D_MODEL), F32),
        scratch_shapes=[pltpu.VMEM((KV_B_W // LANES, tm, LANES), F32)] * 4 + [pltpu.VMEM((tm, KV_B_W), BF16)],
        compiler_params=pltpu.CompilerParams(dimension_semantics=("parallel",), vmem_limit_bytes=VMEM_LIMIT),
        name="back_prompt",
    )(h, oa, *ols, mk, mv, w["mix_norm"], w["w_in"], w["w_branch_a"], w["w_branch_b"], w["w_out"],
      w["mem_q_norm"], w["w_mem_q"], w["w_mem_o"], w["ffn2_norm"], w["ffn2_wg"], w["ffn2_wu"], w["ffn2_wd"],
      w["final_norm"])


def _back_sample_a_kernel(h_ref, oa_ref, ob_ref, gm_ref, wgate_ref, wba_ref, wbb_ref, wout_ref, gq_ref, wq_ref,
                          h2_ref, q_ref):
    h2 = _mix_out(h_ref[...], oa_ref[...], ob_ref[...], gm_ref, wgate_ref, wba_ref, wbb_ref, wout_ref)
    h2_ref[...] = h2
    q_ref[...] = _dot(_rms(h2, gq_ref[...]).astype(BF16), wq_ref[...])


def _back_sample_a_call(h, oa, ob, w):
    t = h.shape[0]
    tm = TOKEN_TILE
    tok = lambda width: pl.BlockSpec((tm, width), lambda i: (i, 0))
    return pl.pallas_call(
        _back_sample_a_kernel,
        grid=(t // tm,),
        in_specs=[tok(D_MODEL), tok(Q_A_W), tok(KV_B_W),
                  _const_spec((1, D_MODEL)), _const_spec((D_MODEL, 2 * D_MODEL), (0, 1)), _const_spec((Q_A_W, D_MODEL)),
                  _const_spec((KV_B_W, D_MODEL)), _const_spec((D_MODEL, D_MODEL)),
                  _const_spec((1, D_MODEL)), _const_spec((D_MODEL, MEM_W))],
        out_specs=[tok(D_MODEL), tok(MEM_W)],
        out_shape=[jax.ShapeDtypeStruct((t, D_MODEL), F32), jax.ShapeDtypeStruct((t, MEM_W), F32)],
        compiler_params=pltpu.CompilerParams(dimension_semantics=("parallel",), vmem_limit_bytes=VMEM_LIMIT),
        name="back_sample_mix",
    )(h, oa, ob, w["mix_norm"], w["w_in"], w["w_branch_a"], w["w_branch_b"], w["w_out"],
      w["mem_q_norm"], w["w_mem_q"])


CROSS_BATCH_TILE = 16


def _cross_sample_kernel(q_ref, mk_ref, mv_ref, o_ref):
    nrow = MEM_HEADS * SAMPLE_T
    qhead = lax.broadcasted_iota(jnp.int32, (nrow, N_MEM * MEM_HEADS), 0) >> (SAMPLE_T.bit_length() - 1)
    khead = lax.broadcasted_iota(jnp.int32, (nrow, N_MEM * MEM_HEADS), 1) & (MEM_HEADS - 1)
    own_head = qhead == khead
    def element(bi):
        rows = slice(bi * SAMPLE_T, (bi + 1) * SAMPLE_T)
        qs = jnp.concatenate(
            [q_ref[rows, hd * MEM_HEAD_DIM:(hd + 1) * MEM_HEAD_DIM] for hd in range(MEM_HEADS)], axis=0)
        s = _dot_nt(qs, mk_ref[bi]) * MEM_SCALE
        yield
        p = _softmax_rows(jnp.where(own_head, s, NEG_INF))
        yield
        o = _dot(p, mv_ref[bi])
        for hd in range(MEM_HEADS):
            o_ref[rows, hd * MEM_HEAD_DIM:(hd + 1) * MEM_HEAD_DIM] = o[hd * SAMPLE_T:(hd + 1) * SAMPLE_T]
        yield

    _issue_skewed([element(bi) for bi in range(CROSS_BATCH_TILE)])


def _cross_sample_call(q, mk, mv):
    t = q.shape[0]
    bt = CROSS_BATCH_TILE
    tok = pl.BlockSpec((bt * SAMPLE_T, MEM_W), lambda i: (i, 0))
    mem = pl.BlockSpec((bt, N_MEM * MEM_HEADS, MEM_HEAD_DIM), lambda i: (i, 0, 0))
    return pl.pallas_call(
        _cross_sample_kernel,
        grid=(mk.shape[0] // bt,),
        in_specs=[tok, mem, mem],
        out_specs=tok,
        out_shape=jax.ShapeDtypeStruct((t, MEM_W), F32),
        compiler_params=pltpu.CompilerParams(dimension_semantics=("parallel",), vmem_limit_bytes=VMEM_LIMIT),
        name="cross_sample",
    )(q, mk, mv)


def _back_sample_b_kernel(h2_ref, oc_ref, wo_ref, g2_ref, wg_ref, wu_ref, wd_ref, gf_ref, y_ref):
    h3 = h2_ref[...] + _dot(oc_ref[...].astype(BF16), wo_ref[...])
    _ffn2_and_norm(h3, g2_ref, wg_ref, wu_ref, wd_ref, gf_ref, y_ref)


def _back_sample_b_call(h2, oc, w):
    t = h2.shape[0]
    tm = TOKEN_TILE
    tok = lambda width: pl.BlockSpec((tm, width), lambda i: (i, 0))
    return pl.pallas_call(
        _back_sample_b_kernel,
        grid=(t // tm,),
        in_specs=[tok(D_MODEL), tok(MEM_W), _const_spec((MEM_W, D_MODEL)),
                  _const_spec((1, D_MODEL)), _const_spec((D_MODEL, D_FF)), _const_spec((D_MODEL, D_FF)),
                  _const_spec((D_FF, D_MODEL)), _const_spec((1, D_MODEL))],
        out_specs=tok(D_MODEL),
        out_shape=jax.ShapeDtypeStruct((t, D_MODEL), F32),
        compiler_params=pltpu.CompilerParams(dimension_semantics=("parallel",), vmem_limit_bytes=VMEM_LIMIT),
        name="back_sample_ffn",
    )(h2, oc, w["w_mem_o"], w["ffn2_norm"], w["ffn2_wg"], w["ffn2_wu"], w["ffn2_wd"], w["final_norm"])


def _mem_kv_kernel(mem_ref, g_ref, wk_ref, wv_ref, k32_ref, v32_ref, k16_ref, v16_ref):
    u = _rms(mem_ref[...], g_ref[...]).astype(BF16)
    k = _dot(u, wk_ref[...])
    v = _dot(u, wv_ref[...])
    rows = mem_ref.shape[0]
    for hd in range(MEM_HEADS):
        cols = slice(hd * MEM_HEAD_DIM, (hd + 1) * MEM_HEAD_DIM)
        k32_ref[pl.ds(hd, rows, stride=MEM_HEADS), :] = k[:, cols]
        v32_ref[pl.ds(hd, rows, stride=MEM_HEADS), :] = v[:, cols]
    k16_ref[...] = k.astype(BF16)
    v16_ref[...] = v.astype(BF16)


def _mem_kv_call(mem, w):
    rows = mem.shape[0]
    full = lambda width: pl.BlockSpec((rows, width), lambda i: (0, 0))
    return pl.pallas_call(
        _mem_kv_kernel,
        grid=(1,),
        in_specs=[full(D_MODEL), _const_spec((1, D_MODEL)), _const_spec((D_MODEL, MEM_W)),
                  _const_spec((D_MODEL, MEM_W))],
        out_specs=[pl.BlockSpec((rows * MEM_HEADS, MEM_HEAD_DIM), lambda i: (0, 0))] * 2 + [full(MEM_W)] * 2,
        out_shape=([jax.ShapeDtypeStruct((rows * MEM_HEADS, MEM_HEAD_DIM), F32)] * 2
                   + [jax.ShapeDtypeStruct((rows, MEM_W), BF16)] * 2),
        compiler_params=pltpu.CompilerParams(dimension_semantics=("arbitrary",), vmem_limit_bytes=VMEM_LIMIT),
        name="mem_kv",
    )(mem, w["mem_kv_norm"], w["w_mem_k"], w["w_mem_v"])


def _rope_tables(pos):
    inv_freq = jnp.power(jnp.float32(ROPE_THETA), -jnp.arange(ROT_HALF, dtype=jnp.float32) / ROT_HALF)
    ang = pos.astype(jnp.float32)[:, None] * inv_freq[None, :]
    lane = np.arange(LANES)
    within = lane % HEAD_DIM
    first = within < ROT_HALF
    second = (within >= ROT_HALF) & (within < 2 * ROT_HALF)
    pick_cos = np.zeros((ROT_HALF, LANES), np.float32)
    pick_sin = np.zeros((ROT_HALF, LANES), np.float32)
    pick_cos[within[first | second] % ROT_HALF, lane[first | second]] = 1.0
    pick_sin[within[first] % ROT_HALF, lane[first]] = -1.0
    pick_sin[within[second] % ROT_HALF, lane[second]] = 1.0
    spread = functools.partial(jnp.dot, precision=lax.Precision.HIGHEST)
    return (spread(jnp.cos(ang), pick_cos) + np.where(first | second, 0.0, 1.0).astype(np.float32)[None, :],
            spread(jnp.sin(ang), pick_sin))


def _prep_weights(p):
    bf = lambda x: x.astype(BF16)
    vec = lambda x: x.reshape(1, -1)
    return {
        "ffn1_norm": vec(p["ffn1_norm"]), "ffn1_wg": bf(p["ffn1_w_gate"]), "ffn1_wu": bf(p["ffn1_w_up"]),
        "ffn1_wd": bf(p["ffn1_w_down"]),
        "mix_norm": vec(p["mix_norm"]), "w_in": bf(p["w_in"]),
        "w_branch_a": bf(p["w_branch_a"].reshape(N_HEADS_A, HEAD_DIM, D_MODEL)[np.array(HEAD_PERM_A)].reshape(
            Q_A_W, D_MODEL)),
        "w_branch_b": bf(p["w_branch_b"]), "w_out": bf(p["w_out"]),
        "mem_q_norm": vec(p["mem_q_norm"]), "mem_kv_norm": vec(p["mem_kv_norm"]),
        "w_mem_q": bf(p["w_mem_q"]), "w_mem_k": bf(p["w_mem_k"]), "w_mem_v": bf(p["w_mem_v"]),
        "w_mem_o": bf(p["w_mem_o"]),
        "ffn2_norm": vec(p["ffn2_norm"]), "ffn2_wg": bf(p["ffn2_w_gate"]), "ffn2_wu": bf(p["ffn2_w_up"]),
        "ffn2_wd": bf(p["ffn2_w_down"]), "final_norm": vec(p["final_norm"]),
    }


def kernel(x_prompt, x_sample, cache_swa_k, cache_swa_v, cache_dil_k, cache_dil_v, cache_mem_k, cache_mem_v, mem_prompt, ffn1_norm, ffn1_w_gate, ffn1_w_up, ffn1_w_down, mix_norm, w_in, attn_sink, w_branch_a, w_branch_b, w_out, mem_q_norm, mem_kv_norm, w_mem_q, w_mem_k, w_mem_v, w_mem_o, ffn2_norm, ffn2_w_gate, ffn2_w_up, ffn2_w_down, final_norm):
    depth = ffn1_norm.shape[0]
    assert depth == 1
    batch, seq, d = x_prompt.shape
    nb, t_new, _ = x_sample.shape
    assert d == D_MODEL and t_new == SAMPLE_T and seq % (TOKEN_TILE * DILATIONS[-1]) == 0 and seq % ATTN_CHUNK == 0
    assert cache_swa_k.shape[2] == CACHE_A and cache_dil_k.shape[2] == CACHE_B
    layer = lambda x: x[0]
    w = _prep_weights(dict(
        ffn1_norm=layer(ffn1_norm), ffn1_w_gate=layer(ffn1_w_gate), ffn1_w_up=layer(ffn1_w_up),
        ffn1_w_down=layer(ffn1_w_down), mix_norm=layer(mix_norm), w_in=layer(w_in),
        w_branch_a=layer(w_branch_a), w_branch_b=layer(w_branch_b), w_out=layer(w_out),
        mem_q_norm=layer(mem_q_norm), mem_kv_norm=layer(mem_kv_norm), w_mem_q=layer(w_mem_q),
        w_mem_k=layer(w_mem_k), w_mem_v=layer(w_mem_v), w_mem_o=layer(w_mem_o), ffn2_norm=layer(ffn2_norm),
        ffn2_w_gate=layer(ffn2_w_gate), ffn2_w_up=layer(ffn2_w_up), ffn2_w_down=layer(ffn2_w_down),
        final_norm=final_norm))
    sink = layer(attn_sink)

    ts = nb * t_new
    pos_s = PAST_LEN + (jnp.arange(TOKEN_TILE, dtype=jnp.int32) % t_new)
    tables_s = _rope_tables(pos_s)
    (h_s, q32, ka32_s, va32_s, kb32_s, vb32_s) = _front_call(x_sample.reshape(ts, d), tables_s, w, sample=True)
    by_pos = lambda c: jnp.transpose(layer(c), (0, 2, 3, 1)).reshape(nb, c.shape[3] * c.shape[4], c.shape[2])
    by_slot_head = lambda c: layer(c).reshape(nb, N_MEM * MEM_HEADS, MEM_HEAD_DIM)

    oa_s, ob_s = _attn_sample_call(
        sink, q32, ka32_s, va32_s, kb32_s, vb32_s,
        by_pos(cache_swa_k), by_pos(cache_swa_v), by_pos(cache_dil_k), by_pos(cache_dil_v))

    tp = batch * seq
    mk32, mv32, mk16, mv16 = _mem_kv_call(mem_prompt.reshape(batch * N_MEM, d), w)
    tables_p = _rope_tables(jnp.arange(seq, dtype=jnp.int32))
    h_p, main, split4, split16, tails_kva, tails_kb, tails_vb = _front_call(
        x_prompt.reshape(tp, d), tables_p, w, sample=False, batch=batch)
    oa = _attn_a_call(sink, main, batch, seq)
    groups = ((main.reshape(batch, 1, seq, MAIN_W), MAIN_QB_COL), (split4, 0), (split16, 0))
    ols = [_attn_b_call(qkv, qcol, dil) for (qkv, qcol), dil in zip(groups, DILATIONS)]
    y_p = _back_prompt_call(h_p, oa, ols, mk16.reshape(batch, N_MEM, MEM_W),
                            mv16.reshape(batch, N_MEM, MEM_W), w, seq)

    h2_s, qc_s = _back_sample_a_call(h_s, oa_s, ob_s, w)
    oc_s = _cross_sample_call(qc_s, by_slot_head(cache_mem_k), by_slot_head(cache_mem_v))
    y_s = _back_sample_b_call(h2_s, oc_s, w)

    keep_a, keep_b = min(CACHE_A, seq), min(CACHE_B, seq)
    assert max(keep_a, keep_b) <= TAIL_ROWS <= seq and TAIL_ROWS % TOKEN_TILE == 0

    def tail(x, col, keep, heads):
        part = x[:, col:col + heads * HEAD_DIM, TAIL_ROWS - keep:].reshape(batch, heads, HEAD_DIM, keep)
        return jnp.transpose(part, (0, 3, 1, 2))[None]

    new = lambda x, heads: x.reshape(1, nb, t_new, heads, HEAD_DIM)
    memo = lambda x: x.reshape(1, batch, N_MEM, MEM_HEADS, MEM_HEAD_DIM)
    return (y_p.reshape(batch, seq, d), y_s.reshape(nb, t_new, d),
            tail(tails_kva, 0, keep_a, N_KV_A), tail(tails_kva, KV_A_W, keep_a, N_KV_A),
            tail(tails_kb, 0, keep_b, N_KV_B), tail(tails_vb, 0, keep_b, N_KV_B),
            memo(mk32), memo(mv32),
            new(ka32_s, N_KV_A), new(va32_s, N_KV_A), new(kb32_s, N_KV_B), new(vb32_s, N_KV_B))
```
